```python
import jax
import jax.numpy as jnp
from jax import lax
import numpy as np

D_MODEL = 2048
BATCH = 2
SEQ = 4096
DEPTH = 2
DEC_BATCH = 16
DEC_SEQ = 32
PAST_LEN = 1024

CHUNK = 64
ATT_HEADS = 8
KV_HEADS = 2
HEAD_DIM = 128
IDX_HEADS = 16
IDX_DIM = 64
TOPK_MAX = 256
Q_BLOCK = 128
ROPE_THETA = 10000.0
SSD_HEADS = 16
SSD_HEADDIM = 64
SSD_GROUPS = 2
SSD_STATE = 128
SSD_INNER = SSD_HEADS * SSD_HEADDIM
SSD_CONV_DIM = SSD_INNER + 2 * SSD_GROUPS * SSD_STATE
CONV_W = 4
LRU_WIDTH = D_MODEL
LRU_BLOCKS = 16
LRU_BLOCK_DIM = LRU_WIDTH // LRU_BLOCKS
LRU_C = 8.0
FFN_DIM = ((8 * D_MODEL + 2) // 3 + 255) // 256 * 256
ATT_WIDTH = ATT_HEADS * HEAD_DIM
AB_SIZES = (ATT_WIDTH, KV_HEADS * HEAD_DIM, KV_HEADS * HEAD_DIM, IDX_HEADS * IDX_DIM, IDX_DIM, IDX_HEADS, SSD_INNER, SSD_CONV_DIM, SSD_HEADS)
AB_IN_DIM = sum(AB_SIZES)
MIX_WIDTH = ATT_WIDTH + SSD_INNER
N_ATT_LAYERS = (DEPTH + 1) // 2
N_LRU_LAYERS = DEPTH // 2
EPS = 1e-6

kernel_name = 'hybrid_dsa_ssd_rglru_stream_step'


def rmsnorm(x, w):
    xf = x.astype(jnp.float32)
    y = xf * lax.rsqrt(jnp.mean(xf * xf, axis=-1, keepdims=True) + EPS)
    return (y * w.astype(jnp.float32)).astype(x.dtype)


def rope(x, pos):
    half = x.shape[-1] // 2
    inv = jnp.power(ROPE_THETA, -jnp.arange(half, dtype=jnp.float32) / half)
    ang = pos.astype(jnp.float32)[:, None] * inv[None, :]
    cos = jnp.cos(ang)[:, None, :]
    sin = jnp.sin(ang)[:, None, :]
    xf = x.astype(jnp.float32)
    x1, x2 = xf[..., :half], xf[..., half:]
    return jnp.concatenate([x1 * cos - x2 * sin, x1 * sin + x2 * cos], axis=-1).astype(x.dtype)


def causal_dwconv(u, buf, w, b):
    t = u.shape[1]
    full = jnp.concatenate([buf.astype(u.dtype), u], axis=1)
    out = b
    for j in range(CONV_W):
        out = out + full[:, j:j + t] * w[j]
    return out, full[:, t:]


def lin_combine(e1, e2):
    a1, b1 = e1
    a2, b2 = e2
    return a1 * a2, a2 * b1 + b2


def swiglu(h, wg, wu, wd):
    return (jax.nn.silu(h @ wg) * (h @ wu)) @ wd


def dsa_attention(q, qi, wi, k, v, ki, q_pos, topk):
    b_, t = q.shape[:2]
    n_keys = k.shape[1]
    qb = Q_BLOCK if t % Q_BLOCK == 0 else t
    nb = t // qb
    k_chunk = jnp.arange(n_keys) // CHUNK

    def blocks(u):
        return jnp.moveaxis(u.reshape((b_, nb, qb) + u.shape[2:]), 1, 0)

    def one_block(args):
        q_b, qi_b, wi_b, pos_b = args
        q_chunk = pos_b // CHUNK
        s_idx = jnp.einsum('bqhd,bkd->bqhk', qi_b, ki) * IDX_DIM ** -0.5
        score = jnp.einsum('bqh,bqhk->bqk', wi_b, jax.nn.relu(s_idx)).astype(jnp.float32)
        score = jnp.where(k_chunk[None, None, :] <= q_chunk[None, :, None], score, -jnp.inf)
        _, sel = lax.top_k(score, topk)
        valid = (sel // CHUNK) <= q_chunk[None, :, None]
        kg = jax.vmap(lambda kb, ib: kb[ib])(k, sel)
        vg = jax.vmap(lambda vb, ib: vb[ib])(v, sel)
        qg = q_b.reshape(b_, qb, KV_HEADS, ATT_HEADS // KV_HEADS, HEAD_DIM)
        s = jnp.einsum('bqgrd,bqkgd->bqgrk', qg, kg).astype(jnp.float32) * HEAD_DIM ** -0.5
        s = jnp.where(valid[:, :, None, None, :], s, -jnp.inf)
        p = jax.nn.softmax(s, axis=-1).astype(v.dtype)
        o = jnp.einsum('bqgrk,bqkgd->bqgrd', p, vg)
        return o.reshape(b_, qb, ATT_WIDTH)

    out = lax.map(one_block, (blocks(q), blocks(qi), blocks(wi), q_pos.reshape(nb, qb)))
    return jnp.moveaxis(out, 0, 1).reshape(b_, t, ATT_WIDTH)


def ssd_scan(xh, dt, a_neg, bm, cm, h0):
    b_, t = xh.shape[:2]
    cl = min(CHUNK, t)
    nc = t // cl
    tri = jnp.tril(jnp.ones((cl, cl), dtype=bool))

    def to_chunks(u):
        return jnp.moveaxis(u.reshape((b_, nc, cl) + u.shape[2:]), 1, 0)

    def step(h, inp):
        x_c, dt_c, b_c, c_c = inp
        cum = jnp.cumsum(dt_c * a_neg, axis=1)
        seg = cum[:, :, None] - cum[:, None, :]
        decay = jnp.exp(jnp.where(tri[None, :, :, None, None], seg, -jnp.inf))
        cb = jnp.einsum('btgn,bsgn->btsg', c_c, b_c)
        wts = cb[..., None] * decay * dt_c[:, None]
        y = jnp.einsum('btsgh,bsghp->btghp', wts, x_c)
        y = y + jnp.einsum('btgn,bghpn->btghp', c_c, h) * jnp.exp(cum)[..., None]
        tail = jnp.exp(cum[:, -1:] - cum) * dt_c
        h_new = h * jnp.exp(cum[:, -1])[..., None, None] + jnp.einsum('bsgh,bsgn,bsghp->bghpn', tail, b_c, x_c)
        return h_new, y

    h_fin, ys = lax.scan(step, h0, (to_chunks(xh), to_chunks(dt), to_chunks(bm), to_chunks(cm)))
    y = jnp.moveaxis(ys, 0, 1).reshape(xh.shape)
    return y, h_fin


def mixer_ab(h, past_k, past_v, past_ki, ssm_h0, conv_buf, w_in, q_norm_w, k_norm_w,
             conv_w, conv_b, dt_bias, a_log, d_skip, norm_w, w_out):
    b_, t, _ = h.shape
    past = past_k.shape[1]
    proj = h @ w_in
    q, k, v, qi, ki, wi, z, xbc, dt_raw = jnp.split(proj, np.cumsum(AB_SIZES)[:-1].tolist(), axis=-1)
    q_pos = past + jnp.arange(t)
    q = rope(rmsnorm(q.reshape(b_, t, ATT_HEADS, HEAD_DIM), q_norm_w), q_pos)
    k = rope(rmsnorm(k.reshape(b_, t, KV_HEADS, HEAD_DIM), k_norm_w), q_pos)
    v = v.reshape(b_, t, KV_HEADS, HEAD_DIM)
    qi = rope(qi.reshape(b_, t, IDX_HEADS, IDX_DIM), q_pos)
    ki = rope(ki[:, :, None], q_pos)[:, :, 0]
    wi = wi * IDX_HEADS ** -0.5
    k_all = jnp.concatenate([past_k.astype(k.dtype), k], axis=1)
    v_all = jnp.concatenate([past_v.astype(v.dtype), v], axis=1)
    ki_all = jnp.concatenate([past_ki.astype(ki.dtype), ki], axis=1)
    topk = min(TOPK_MAX, (past + t) // 4)
    att = dsa_attention(q, qi, wi, k_all, v_all, ki_all, q_pos, topk)
    xbc, conv_new = causal_dwconv(xbc, conv_buf, conv_w, conv_b)
    xbc = jax.nn.silu(xbc)
    xs, bm, cm = jnp.split(xbc, [SSD_INNER, SSD_INNER + SSD_GROUPS * SSD_STATE], axis=-1)
    hg = SSD_HEADS // SSD_GROUPS
    xs_h = xs.reshape(b_, t, SSD_GROUPS, hg, SSD_HEADDIM).astype(jnp.float32)
    dt = jax.nn.softplus(dt_raw.astype(jnp.float32) + dt_bias.astype(jnp.float32)).reshape(b_, t, SSD_GROUPS, hg)
    a_neg = -jnp.exp(a_log.astype(jnp.float32)).reshape(SSD_GROUPS, hg)
    y, h_new = ssd_scan(xs_h, dt, a_neg,
                        bm.reshape(b_, t, SSD_GROUPS, SSD_STATE).astype(jnp.float32),
                        cm.reshape(b_, t, SSD_GROUPS, SSD_STATE).astype(jnp.float32),
                        ssm_h0.reshape(b_, SSD_GROUPS, hg, SSD_HEADDIM, SSD_STATE).astype(jnp.float32))
    y = y + d_skip.astype(jnp.float32).reshape(SSD_GROUPS, hg)[..., None] * xs_h
    y = y.reshape(b_, t, SSD_GROUPS, hg * SSD_HEADDIM).astype(h.dtype)
    y = y * jax.nn.silu(z.reshape(b_, t, SSD_GROUPS, hg * SSD_HEADDIM))
    y = rmsnorm(y, norm_w.reshape(SSD_GROUPS, hg * SSD_HEADDIM)).reshape(b_, t, SSD_INNER)
    out = jnp.concatenate([att, y], axis=-1) @ w_out
    h_new = h_new.reshape(b_, SSD_HEADS, SSD_HEADDIM, SSD_STATE).astype(ssm_h0.dtype)
    return out, k, v, ki, h_new, conv_new


def mixer_c(h, h0, conv_buf, w_in, conv_w, conv_b, w_a, b_a, w_x, b_x, lam, w_out):
    b_, t, _ = h.shape
    gate_in, x_in = jnp.split(h @ w_in, 2, axis=-1)
    xc, conv_new = causal_dwconv(x_in, conv_buf, conv_w, conv_b)
    xb = xc.reshape(b_, t, LRU_BLOCKS, LRU_BLOCK_DIM)
    r = jax.nn.sigmoid((jnp.einsum('btkd,kde->btke', xb, w_a).reshape(b_, t, LRU_WIDTH) + b_a).astype(jnp.float32))
    i = jax.nn.sigmoid((jnp.einsum('btkd,kde->btke', xb, w_x).reshape(b_, t, LRU_WIDTH) + b_x).astype(jnp.float32))
    log_a = -LRU_C * r * jax.nn.softplus(-lam.astype(jnp.float32))
    a = jnp.exp(log_a)
    u = jnp.sqrt(-jnp.expm1(2.0 * log_a)) * (i * xc.astype(jnp.float32))
    a_cum, u_cum = lax.associative_scan(lin_combine, (a, u), axis=1)
    hs = a_cum * h0.astype(jnp.float32)[:, None] + u_cum
    y = hs.astype(h.dtype) * jax.nn.gelu(gate_in)
    return y @ w_out, hs[:, -1].astype(h0.dtype), conv_new


def trunk(x, c, att_k, att_v, idx_k, ssm_h, ssm_conv, lru_h, lru_conv, p):
    k_l, v_l, ki_l, ssm_l, ssmc_l, lru_l, lruc_l = [], [], [], [], [], [], []
    for l in range(DEPTH):
        j = l // 2
        mod = jax.nn.silu(c) @ p['ada_w'][l] + p['ada_b'][l]
        sh_m, sc_m, g_m, sh_f, sc_f, g_f = jnp.split(mod, 6, axis=-1)
        h = rmsnorm(x, p['norm_mix_w'][l]) * (1.0 + sc_m[:, None]) + sh_m[:, None]
        if l % 2 == 0:
            mix, k, v, ki, hs, cs = mixer_ab(
                h, att_k[j], att_v[j], idx_k[j], ssm_h[j], ssm_conv[j],
                p['w_in_ab'][j], p['q_norm_w'][j], p['k_norm_w'][j], p['ssd_conv_w'][j], p['ssd_conv_b'][j],
                p['ssd_dt_bias'][j], p['ssd_a_log'][j], p['ssd_d'][j], p['ssd_norm_w'][j], p['w_out_ab'][j])
            k_l.append(k)
            v_l.append(v)
            ki_l.append(ki)
            ssm_l.append(hs)
            ssmc_l.append(cs)
        else:
            mix, hl, cl = mixer_c(
                h, lru_h[j], lru_conv[j], p['w_in_c'][j], p['lru_conv_w'][j], p['lru_conv_b'][j],
                p['lru_w_a'][j], p['lru_b_a'][j], p['lru_w_x'][j], p['lru_b_x'][j], p['lru_lambda'][j], p['w_out_c'][j])
            lru_l.append(hl)
            lruc_l.append(cl)
        x = x + g_m[:, None] * mix
        h = rmsnorm(x, p['norm_ffn_w'][l]) * (1.0 + sc_f[:, None]) + sh_f[:, None]
        x = x + g_f[:, None] * swiglu(h, p['ffn_w_gate'][l], p['ffn_w_up'][l], p['ffn_w_down'][l])
    return (x, jnp.stack(k_l), jnp.stack(v_l), jnp.stack(ki_l), jnp.stack(ssm_l), jnp.stack(ssmc_l),
            jnp.stack(lru_l), jnp.stack(lruc_l))


def setup_inputs(seed: int = 0) -> dict:
    key = jax.random.key(seed)
    keys = list(jax.random.split(key, 48))

    def nrm(shape, scale):
        return jax.random.normal(keys.pop(), shape, jnp.float32) * scale

    def unif(shape, lo, hi):
        return jax.random.uniform(keys.pop(), shape, jnp.float32, lo, hi)

    na, nl, d = N_ATT_LAYERS, N_LRU_LAYERS, D_MODEL
    dt0 = jnp.exp(unif((na, SSD_HEADS), float(np.log(1e-3)), float(np.log(1e-1))))
    s0 = unif((nl, LRU_WIDTH), 0.9, 0.999) ** (1.0 / LRU_C)
    return {
        'x_prompt': nrm((BATCH, SEQ, d), 1.0),
        'x_sample': nrm((DEC_BATCH, DEC_SEQ, d), 1.0),
        'cache_attn_k': nrm((na, DEC_BATCH, PAST_LEN, KV_HEADS, HEAD_DIM), 1.0),
        'cache_attn_v': nrm((na, DEC_BATCH, PAST_LEN, KV_HEADS, HEAD_DIM), 1.0),
        'cache_idx_k': nrm((na, DEC_BATCH, PAST_LEN, IDX_DIM), 1.0),
        'state_ssm': nrm((na, DEC_BATCH, SSD_HEADS, SSD_HEADDIM, SSD_STATE), 0.1),
        'state_ssm_conv': nrm((na, DEC_BATCH, CONV_W - 1, SSD_CONV_DIM), 1.0),
        'state_lru': nrm((nl, DEC_BATCH, LRU_WIDTH), 0.5),
        'state_lru_conv': nrm((nl, DEC_BATCH, CONV_W - 1, LRU_WIDTH), 1.0),
        'c_prompt': nrm((BATCH, d), 1.0),
        'c_sample': nrm((DEC_BATCH, d), 1.0),
        'ada_w': nrm((DEPTH, d, 6 * d), 0.5 * d ** -0.5),
        'ada_b': nrm((DEPTH, 6 * d), 0.02),
        'norm_mix_w': 1.0 + nrm((DEPTH, d), 0.02),
        'norm_ffn_w': 1.0 + nrm((DEPTH, d), 0.02),
        'w_in_ab': nrm((na, d, AB_IN_DIM), d ** -0.5),
        'q_norm_w': 1.0 + nrm((na, HEAD_DIM), 0.02),
        'k_norm_w': 1.0 + nrm((na, HEAD_DIM), 0.02),
        'ssd_conv_w': nrm((na, CONV_W, SSD_CONV_DIM), CONV_W ** -0.5),
        'ssd_conv_b': nrm((na, SSD_CONV_DIM), 0.02),
        'ssd_dt_bias': dt0 + jnp.log(-jnp.expm1(-dt0)),
        'ssd_a_log': jnp.log(unif((na, SSD_HEADS), 1.0, 16.0)),
        'ssd_d': 1.0 + nrm((na, SSD_HEADS), 0.02),
        'ssd_norm_w': 1.0 + nrm((na, SSD_INNER), 0.02),
        'w_out_ab': nrm((na, MIX_WIDTH, d), MIX_WIDTH ** -0.5),
        'w_in_c': nrm((nl, d, 2 * LRU_WIDTH), d ** -0.5),
        'lru_conv_w': nrm((nl, CONV_W, LRU_WIDTH), CONV_W ** -0.5),
        'lru_conv_b': nrm((nl, LRU_WIDTH), 0.02),
        'lru_w_a': nrm((nl, LRU_BLOCKS, LRU_BLOCK_DIM, LRU_BLOCK_DIM), LRU_BLOCK_DIM ** -0.5),
        'lru_b_a': nrm((nl, LRU_WIDTH), 0.02),
        'lru_w_x': nrm((nl, LRU_BLOCKS, LRU_BLOCK_DIM, LRU_BLOCK_DIM), LRU_BLOCK_DIM ** -0.5),
        'lru_b_x': nrm((nl, LRU_WIDTH), 0.02),
        'lru_lambda': jnp.log(s0) - jnp.log1p(-s0),
        'w_out_c': nrm((nl, LRU_WIDTH, d), LRU_WIDTH ** -0.5),
        'ffn_w_gate': nrm((DEPTH, d, FFN_DIM), d ** -0.5),
        'ffn_w_up': nrm((DEPTH, d, FFN_DIM), d ** -0.5),
        'ffn_w_down': nrm((DEPTH, FFN_DIM, d), FFN_DIM ** -0.5),
    }


def reference(x_prompt, x_sample, cache_attn_k, cache_attn_v, cache_idx_k, state_ssm, state_ssm_conv,
              state_lru, state_lru_conv, c_prompt, c_sample, ada_w, ada_b, norm_mix_w, norm_ffn_w,
              w_in_ab, q_norm_w, k_norm_w, ssd_conv_w, ssd_conv_b, ssd_dt_bias, ssd_a_log, ssd_d, ssd_norm_w,
              w_out_ab, w_in_c, lru_conv_w, lru_conv_b, lru_w_a, lru_b_a, lru_w_x, lru_b_x, lru_lambda, w_out_c,
              ffn_w_gate, ffn_w_up, ffn_w_down):
    p = dict(ada_w=ada_w, ada_b=ada_b, norm_mix_w=norm_mix_w, norm_ffn_w=norm_ffn_w,
             w_in_ab=w_in_ab, q_norm_w=q_norm_w, k_norm_w=k_norm_w, ssd_conv_w=ssd_conv_w, ssd_conv_b=ssd_conv_b,
             ssd_dt_bias=ssd_dt_bias, ssd_a_log=ssd_a_log, ssd_d=ssd_d, ssd_norm_w=ssd_norm_w, w_out_ab=w_out_ab,
             w_in_c=w_in_c, lru_conv_w=lru_conv_w, lru_conv_b=lru_conv_b, lru_w_a=lru_w_a, lru_b_a=lru_b_a,
             lru_w_x=lru_w_x, lru_b_x=lru_b_x, lru_lambda=lru_lambda, w_out_c=w_out_c,
             ffn_w_gate=ffn_w_gate, ffn_w_up=ffn_w_up, ffn_w_down=ffn_w_down)
    bp = x_prompt.shape[0]
    empty_k = jnp.zeros((N_ATT_LAYERS, bp, 0, KV_HEADS, HEAD_DIM), cache_attn_k.dtype)
    empty_v = jnp.zeros((N_ATT_LAYERS, bp, 0, KV_HEADS, HEAD_DIM), cache_attn_v.dtype)
    empty_ki = jnp.zeros((N_ATT_LAYERS, bp, 0, IDX_DIM), cache_idx_k.dtype)
    zero_ssm = jnp.zeros((N_ATT_LAYERS, bp) + state_ssm.shape[2:], state_ssm.dtype)
    zero_ssm_conv = jnp.zeros((N_ATT_LAYERS, bp) + state_ssm_conv.shape[2:], state_ssm_conv.dtype)
    zero_lru = jnp.zeros((N_LRU_LAYERS, bp) + state_lru.shape[2:], state_lru.dtype)
    zero_lru_conv = jnp.zeros((N_LRU_LAYERS, bp) + state_lru_conv.shape[2:], state_lru_conv.dtype)
    y_prompt, k_p, v_p, ki_p, ssm_p, ssmc_p, lru_p, lruc_p = trunk(
        x_prompt, c_prompt, empty_k, empty_v, empty_ki, zero_ssm, zero_ssm_conv, zero_lru, zero_lru_conv, p)
    y_sample, k_s, v_s, ki_s, ssm_s, ssmc_s, lru_s, lruc_s = trunk(
        x_sample, c_sample, cache_attn_k, cache_attn_v, cache_idx_k, state_ssm, state_ssm_conv,
        state_lru, state_lru_conv, p)
    return (y_prompt, y_sample, k_p, v_p, ki_p, ssm_p, ssmc_p, lru_p, lruc_p,
            k_s, v_s, ki_s, ssm_s, ssmc_s, lru_s, lruc_s)
```

```python
import functools

import numpy as np
import jax
import jax.numpy as jnp
from jax import lax
from jax.experimental import pallas as pl
from jax.experimental.pallas import tpu as pltpu

F32 = jnp.float32
BF16 = jnp.bfloat16
HIGHEST = lax.Precision.HIGHEST

D_MODEL = 2048
CHUNK = 64
CHUNK_SHIFT = 6
ATT_HEADS = 8
KV_HEADS = 2
HEAD_DIM = 128
IDX_HEADS = 16
IDX_DIM = 64
TOPK_MAX = 256
ROPE_THETA = 10000.0
SSD_HEADS = 16
SSD_HEADDIM = 64
SSD_GROUPS = 2
SSD_STATE = 128
SSD_INNER = SSD_HEADS * SSD_HEADDIM
SSD_CONV_DIM = SSD_INNER + 2 * SSD_GROUPS * SSD_STATE
CONV_W = 4
LRU_WIDTH = D_MODEL
LRU_BLOCKS = 16
LRU_BLOCK_DIM = LRU_WIDTH // LRU_BLOCKS
LRU_C = 8.0
EPS = 1e-6
ATT_WIDTH = ATT_HEADS * HEAD_DIM
KV_WIDTH = KV_HEADS * HEAD_DIM
QI_WIDTH = IDX_HEADS * IDX_DIM

LANES = 128
SUBLANES = 8
VMEM_LIMIT_BYTES = 56 * 1024 * 1024

OFF_Q = 0
OFF_QI = OFF_Q + ATT_WIDTH
OFF_Z = OFF_QI + QI_WIDTH
OFF_XBC = OFF_Z + SSD_INNER
OFF_K = OFF_XBC + SSD_CONV_DIM
OFF_V = OFF_K + KV_WIDTH
OFF_SMALL = OFF_V + KV_WIDTH
SM_WI = IDX_DIM
SM_DT = IDX_DIM + IDX_HEADS
AB_TILE_N = 768
AB_PAD_N = 5376

NEG_BIG = -1e30


def _params(n_axes):
    return pltpu.CompilerParams(dimension_semantics=("arbitrary",) * n_axes,
                                vmem_limit_bytes=VMEM_LIMIT_BYTES)


def _silu(x):
    return x * jax.nn.sigmoid(x)


def _softplus(x):
    return jnp.maximum(x, 0.0) + jnp.log1p(jnp.exp(-jnp.abs(x)))


def _mod_kernel(c_ref, w_ref, b_ref, o_ref):
    a = _silu(c_ref[...]).astype(BF16)
    o_ref[0] = jnp.dot(a, w_ref[0].astype(BF16), preferred_element_type=F32) + b_ref[0]


def _modulation(c_all, ada_w, ada_b):
    depth, d, n = ada_w.shape
    m = c_all.shape[0]
    tn = 1024
    return pl.pallas_call(
        _mod_kernel,
        grid=(depth, n // tn),
        in_specs=[pl.BlockSpec((m, d), lambda l, j: (0, 0)),
                  pl.BlockSpec((1, d, tn), lambda l, j: (l, 0, j)),
                  pl.BlockSpec((1, 1, tn), lambda l, j: (l, 0, j))],
        out_specs=pl.BlockSpec((1, m, tn), lambda l, j: (l, 0, j)),
        out_shape=jax.ShapeDtypeStruct((depth, m, n), F32),
        compiler_params=_params(2),
        name="adaln_mod",
    )(c_all, ada_w, ada_b.reshape(depth, 1, n))


def _nmm_kernel(x_ref, nw_ref, sc_ref, sh_ref, *rest, swiglu):
    n_w = 2 if swiglu else 1
    w_refs, o_ref, h_ref = rest[:n_w], rest[n_w], rest[n_w + 1]
    bb, tt, d = x_ref.shape

    @pl.when(pl.program_id(2) == 0)
    def _():
        x = x_ref[...]
        ms = jnp.mean(x * x, axis=-1, keepdims=True)
        y = x * lax.rsqrt(ms + EPS) * nw_ref[...]
        h = y * (1.0 + sc_ref[...]) + sh_ref[...]
        h_ref[...] = h.reshape(bb * tt, d).astype(BF16)

    h = h_ref[...]
    if swiglu:
        g = jnp.dot(h, w_refs[0][...].astype(BF16), preferred_element_type=F32)
        u = jnp.dot(h, w_refs[1][...].astype(BF16), preferred_element_type=F32)
        o = _silu(g) * u
    else:
        o = jnp.dot(h, w_refs[0][...].astype(BF16), preferred_element_type=F32)
    o_ref[...] = o.reshape(o_ref.shape).astype(o_ref.dtype)


def _norm_mod_matmul(x, nw, sc, sh, ws, *, bb, tt, tn, out_dtype, name):
    b, t, d = x.shape
    n = ws[0].shape[1]
    swiglu = len(ws) == 2
    xmap = lambda i, j, k: (i, j, 0)
    mmap = lambda i, j, k: (i, 0, 0)
    return pl.pallas_call(
        functools.partial(_nmm_kernel, swiglu=swiglu),
        grid=(b // bb, t // tt, n // tn),
        in_specs=[pl.BlockSpec((bb, tt, d), xmap),
                  pl.BlockSpec((1, 1, d), lambda i, j, k: (0, 0, 0)),
                  pl.BlockSpec((bb, 1, d), mmap),
                  pl.BlockSpec((bb, 1, d), mmap)]
                 + [pl.BlockSpec((d, tn), lambda i, j, k: (0, k))] * len(ws),
        out_specs=pl.BlockSpec((bb, tt, tn), lambda i, j, k: (i, j, k)),
        out_shape=jax.ShapeDtypeStruct((b, t, n), out_dtype),
        scratch_shapes=[pltpu.VMEM((bb * tt, d), BF16)],
        compiler_params=_params(3),
        name=name,
    )(x, nw.reshape(1, 1, d), sc, sh, *ws)


def _mmr_kernel(*refs, k_sizes):
    n_a = len(k_sizes)
    a_refs = refs[:n_a]
    w_ref, x_ref, g_ref, o_ref = refs[n_a:]
    bb, tt, tn = o_ref.shape
    acc = jnp.zeros((bb * tt, tn), F32)
    off = 0
    for a_ref, ks in zip(a_refs, k_sizes):
        a = a_ref[...].reshape(bb * tt, ks)
        acc = acc + jnp.dot(a, w_ref[off:off + ks, :].astype(BF16), preferred_element_type=F32)
        off += ks
    o_ref[...] = x_ref[...] + g_ref[...] * acc.reshape(bb, tt, tn)


def _matmul_residual(a_list, w, x, g, *, bb, tt, tn, name):
    b, t, d = x.shape
    k_sizes = tuple(a.shape[-1] for a in a_list)
    k_total = sum(k_sizes)
    return pl.pallas_call(
        functools.partial(_mmr_kernel, k_sizes=k_sizes),
        grid=(b // bb, t // tt, d // tn),
        in_specs=[pl.BlockSpec((bb, tt, ks), lambda i, j, k: (i, j, 0)) for ks in k_sizes]
                 + [pl.BlockSpec((k_total, tn), lambda i, j, k: (0, k)),
                    pl.BlockSpec((bb, tt, tn), lambda i, j, k: (i, j, k)),
                    pl.BlockSpec((bb, 1, tn), lambda i, j, k: (i, 0, k))],
        out_specs=pl.BlockSpec((bb, tt, tn), lambda i, j, k: (i, j, k)),
        out_shape=jax.ShapeDtypeStruct((b, t, d), F32),
        compiler_params=_params(3),
        name=name,
    )(*a_list, w, x, g)


def _swap_halves(x, half):
    w = x.shape[-1]
    lane = lax.broadcasted_iota(jnp.int32, x.shape, x.ndim - 1)
    first = (lane & half) == 0
    return jnp.where(first, pltpu.roll(x, w - half, x.ndim - 1), pltpu.roll(x, half, x.ndim - 1))


def _rope(x, cos, sin_signed, half):
    reps = x.shape[-1] // cos.shape[-1]
    if reps > 1:
        cos = jnp.concatenate([cos] * reps, axis=-1)
        sin_signed = jnp.concatenate([sin_signed] * reps, axis=-1)
    return x * cos + _swap_halves(x, half) * sin_signed


def _causal_conv(u, state_ref, w_ref, b_ref, cbuf_ref, new_ref, first, last):
    tt = u.shape[0]

    @pl.when(first)
    def _():
        cbuf_ref[0:SUBLANES, :] = jnp.zeros((SUBLANES, u.shape[1]), F32)
        cbuf_ref[SUBLANES - (CONV_W - 1):SUBLANES, :] = state_ref[0]

    cbuf_ref[SUBLANES:SUBLANES + tt, :] = u
    out = b_ref[...]
    for j in range(CONV_W):
        start = SUBLANES - (CONV_W - 1) + j
        out = out + cbuf_ref[start:start + tt, :] * w_ref[j:j + 1, :]
    tail = cbuf_ref[tt:tt + SUBLANES, :]
    cbuf_ref[0:SUBLANES, :] = tail

    @pl.when(last)
    def _():
        new_ref[0] = tail[SUBLANES - (CONV_W - 1):, :]

    return out


def _post0_kernel(p_ref, cq_ref, sq_ref, ci_ref, si_ref, qn_ref, kn_ref, cw_ref, cb_ref, dtb_ref,
                  cs_ref,
                  q_ref, qi_ref, k_ref, v_ref, ki_ref, sm_ref, xbc_ref, cnew_ref, kb_ref, vb_ref,
                  kib_ref, cbuf_ref):
    t = pl.program_id(1)
    nt = pl.num_programs(1)
    cq, sq, ci, si = cq_ref[...], sq_ref[...], ci_ref[...], si_ref[...]

    def head_norm_rope(x, w):
        ms = jnp.mean(x * x, axis=-1, keepdims=True)
        return _rope(x * lax.rsqrt(ms + EPS) * w, cq, sq, HEAD_DIM // 2)

    for h in range(ATT_HEADS):
        xh = p_ref[0, :, OFF_Q + h * HEAD_DIM:OFF_Q + (h + 1) * HEAD_DIM]
        q_ref[0, :, h * HEAD_DIM:(h + 1) * HEAD_DIM] = head_norm_rope(xh, qn_ref[...]).astype(BF16)
    for h in range(KV_HEADS):
        xh = p_ref[0, :, OFF_K + h * HEAD_DIM:OFF_K + (h + 1) * HEAD_DIM]
        kh = head_norm_rope(xh, kn_ref[...])
        k_ref[0, :, h * HEAD_DIM:(h + 1) * HEAD_DIM] = kh
        kb_ref[0, :, h * HEAD_DIM:(h + 1) * HEAD_DIM] = kh.astype(BF16)
    v = p_ref[0, :, OFF_V:OFF_V + KV_WIDTH]
    v_ref[0] = v
    vb_ref[0] = v.astype(BF16)

    qi = p_ref[0, :, OFF_QI:OFF_QI + QI_WIDTH]
    qi_ref[0] = _rope(qi, ci, si, IDX_DIM // 2).astype(BF16)

    sm = p_ref[0, :, OFF_SMALL:OFF_SMALL + LANES]
    lane = lax.broadcasted_iota(jnp.int32, sm.shape, 1)
    ki_part = _rope(sm, ci, si, IDX_DIM // 2)
    wi_part = sm * (IDX_HEADS ** -0.5 * IDX_DIM ** -0.5)
    dt_part = _softplus(sm + dtb_ref[...])
    sm_out = jnp.where(lane < SM_WI, ki_part,
                       jnp.where(lane < SM_DT, wi_part,
                                 jnp.where(lane < SM_DT + SSD_HEADS, dt_part, 0.0)))
    sm_ref[0] = sm_out
    ki_ref[0] = sm_out[:, :IDX_DIM]
    kib_ref[0] = sm_out[:, :IDX_DIM].astype(BF16)

    xbc = p_ref[0, :, OFF_XBC:OFF_XBC + SSD_CONV_DIM]
    conv = _causal_conv(xbc, cs_ref, cw_ref, cb_ref, cbuf_ref, cnew_ref, t == 0, t == nt - 1)
    xbc_ref[0] = _silu(conv)


def _post0(proj, tabs, q_norm_w, k_norm_w, conv_w, conv_b, dt_bias_pad, conv_state, *, tt):
    b, t, n = proj.shape
    tok = lambda w: pl.BlockSpec((1, tt, w), lambda i, j: (i, j, 0))
    tab = pl.BlockSpec((tt, LANES), lambda i, j: (j, 0))
    full2 = lambda a: pl.BlockSpec(a.shape, lambda i, j: (0, 0))
    state = lambda c: pl.BlockSpec((1, CONV_W - 1, c), lambda i, j: (i, 0, 0))
    out_shapes = [
        jax.ShapeDtypeStruct((b, t, ATT_WIDTH), BF16),
        jax.ShapeDtypeStruct((b, t, QI_WIDTH), BF16),
        jax.ShapeDtypeStruct((b, t, KV_WIDTH), F32),
        jax.ShapeDtypeStruct((b, t, KV_WIDTH), F32),
        jax.ShapeDtypeStruct((b, t, IDX_DIM), F32),
        jax.ShapeDtypeStruct((b, t, LANES), F32),
        jax.ShapeDtypeStruct((b, t, SSD_CONV_DIM), F32),
        jax.ShapeDtypeStruct((b, CONV_W - 1, SSD_CONV_DIM), F32),
        jax.ShapeDtypeStruct((b, t, KV_WIDTH), BF16),
        jax.ShapeDtypeStruct((b, t, KV_WIDTH), BF16),
        jax.ShapeDtypeStruct((b, t, IDX_DIM), BF16),
    ]
    out_specs = [tok(ATT_WIDTH), tok(QI_WIDTH), tok(KV_WIDTH), tok(KV_WIDTH), tok(IDX_DIM),
                 tok(LANES), tok(SSD_CONV_DIM), state(SSD_CONV_DIM), tok(KV_WIDTH), tok(KV_WIDTH),
                 tok(IDX_DIM)]
    qn = q_norm_w.reshape(1, HEAD_DIM)
    kn = k_norm_w.reshape(1, HEAD_DIM)
    cb = conv_b.reshape(1, SSD_CONV_DIM)
    return pl.pallas_call(
        _post0_kernel,
        grid=(b, t // tt),
        in_specs=[tok(n), tab, tab, tab, tab, full2(qn), full2(kn), full2(conv_w), full2(cb),
                  full2(dt_bias_pad), state(SSD_CONV_DIM)],
        out_specs=out_specs,
        out_shape=out_shapes,
        scratch_shapes=[pltpu.VMEM((tt + SUBLANES, SSD_CONV_DIM), F32)],
        compiler_params=_params(2),
        name="post_proj0",
    )(proj, *tabs, qn, kn, conv_w, cb, dt_bias_pad, conv_state)


def _dsa_kernel(q_ref, qi_ref, sm_ref, k_ref, v_ref, ki_ref, o_ref, sc_ref, *,
                qb, tk, n_keys, past, topk, n_bisect):
    j = pl.program_id(1)
    pos0 = past + j * qb
    rowi = lax.broadcasted_iota(jnp.int32, (qb, 1), 0)
    q_chunk = lax.shift_right_logical(pos0 + rowi, CHUNK_SHIFT)
    n_valid = jnp.minimum((q_chunk + 1) * CHUNK, n_keys)
    max_valid = jnp.minimum(((pos0 + qb - 1) // CHUNK + 1) * CHUNK, n_keys)
    nkt = (max_valid + tk - 1) // tk
    kf = float(topk)
    active_f = jnp.where(n_valid > topk, 1.0, 0.0)

    def key_index(kt):
        return kt * tk + lax.broadcasted_iota(jnp.int32, (qb, tk), 1)

    qi = qi_ref[0]
    wis = sm_ref[0][:, SM_WI:SM_WI + IDX_HEADS]

    def score_tile(kt, carry):
        kit = ki_ref[0, pl.ds(pl.multiple_of(kt * tk, tk), tk), :]
        acc = jnp.zeros((qb, tk), F32)
        for h in range(IDX_HEADS):
            s = lax.dot_general(qi[:, h * IDX_DIM:(h + 1) * IDX_DIM], kit,
                                (((1,), (1,)), ((), ())), preferred_element_type=F32)
            acc = acc + wis[:, h:h + 1] * jnp.maximum(s, 0.0)
        sc_ref[kt] = jnp.where(key_index(kt) < n_valid, acc, -jnp.inf)
        return carry

    lax.fori_loop(0, nkt, score_tile, 0)

    def lane_fold(m):
        part = m[:, 0:LANES]
        for c in range(1, tk // LANES):
            part = part + m[:, c * LANES:(c + 1) * LANES]
        return part

    def count(pred):
        def body(kt, part):
            return part + lane_fold(jnp.where(pred(sc_ref[kt], kt), 1.0, 0.0))
        part = lax.fori_loop(0, nkt, body, jnp.zeros((qb, LANES), F32))
        return jnp.sum(part, axis=-1, keepdims=True)

    def row_max(pred):
        def body(kt, part):
            t = sc_ref[kt]
            m = jnp.where(pred(t, kt), t, -jnp.inf)
            f = m[:, 0:LANES]
            for c in range(1, tk // LANES):
                f = jnp.maximum(f, m[:, c * LANES:(c + 1) * LANES])
            return jnp.maximum(part, f)
        part = lax.fori_loop(0, nkt, body, jnp.full((qb, LANES), -jnp.inf, F32))
        return jnp.max(part, axis=-1, keepdims=True)

    def row_min_valid():
        def body(kt, part):
            m = jnp.where(key_index(kt) < n_valid, sc_ref[kt], jnp.inf)
            f = m[:, 0:LANES]
            for c in range(1, tk // LANES):
                f = jnp.minimum(f, m[:, c * LANES:(c + 1) * LANES])
            return jnp.minimum(part, f)
        part = lax.fori_loop(0, nkt, body, jnp.full((qb, LANES), jnp.inf, F32))
        return jnp.min(part, axis=-1, keepdims=True)

    lo0 = row_min_valid()
    hi0 = row_max(lambda t, kt: t == t)

    def bisect(_, lh):
        lo, hi = lh
        mid = lo + (hi - lo) * 0.5
        ge = count(lambda t, kt: t >= mid) >= kf
        return jnp.where(ge, mid, lo), jnp.where(ge, hi, mid)

    lo, hi = lax.fori_loop(0, n_bisect, bisect, (lo0, hi0))
    w0 = row_max(lambda t, kt: t <= hi)
    c0 = count(lambda t, kt: t >= w0)

    def walk_cond(state):
        w, c = state
        return jnp.max(jnp.where(c < kf, active_f, 0.0)) > 0.0

    def walk_body(state):
        w, c = state
        w2 = row_max(lambda t, kt: t < w)
        c2 = count(lambda t, kt: t >= w2)
        upd = c < kf
        return jnp.where(upd, w2, w), jnp.where(upd, c2, c)

    w, c_ge = lax.while_loop(walk_cond, walk_body, (w0, c0))
    thr = jnp.where(active_f > 0.5, w, -jnp.inf)
    tied_f = jnp.where(c_ge > kf, active_f, 0.0)

    def write_bias(sel_fn):
        def body(kt, carry):
            kidx = key_index(kt)
            sel = sel_fn(sc_ref[kt], kidx) & (kidx < n_valid)
            sc_ref[kt] = jnp.where(sel, 0.0, NEG_BIG)
            return carry
        lax.fori_loop(0, nkt, body, 0)

    def no_ties():
        write_bias(lambda t, kidx: t >= thr)

    def with_ties():
        need = kf - count(lambda t, kt: t > thr)
        n_steps = max(1, int(np.ceil(np.log2(sc_ref.shape[0] * tk))) + 1)

        def step(_, lh):
            lo_i, hi_i = lh
            mid = lax.shift_right_arithmetic(lo_i + hi_i, 1)
            ge = count(lambda t, kt: (t == thr) & (key_index(kt) <= mid)) >= need
            return jnp.where(ge, lo_i, mid), jnp.where(ge, mid, hi_i)

        lo_i = jnp.full((qb, 1), -1, jnp.int32)
        hi_i = jnp.full((qb, 1), sc_ref.shape[0] * tk - 1, jnp.int32)
        _, last_tie = lax.fori_loop(0, n_steps, step, (lo_i, hi_i))
        write_bias(lambda t, kidx: (t > thr) | ((t == thr)
                                                & ((kidx <= last_tie) | (tied_f < 0.5))))

    any_tie = jnp.max(tied_f) > 0.0
    lax.cond(any_tie, with_ties, no_ties)

    rep = ATT_HEADS // KV_HEADS
    scale = HEAD_DIM ** -0.5
    for g in range(KV_HEADS):
        qg = jnp.concatenate(
            [q_ref[0, :, (g * rep + r) * HEAD_DIM:(g * rep + r + 1) * HEAD_DIM] for r in range(rep)],
            axis=0)

        def att_tile(kt, carry, g=g, qg=qg):
            m, l, acc = carry
            row0 = pl.multiple_of(kt * tk, tk)
            kg = k_ref[0, pl.ds(row0, tk), g * HEAD_DIM:(g + 1) * HEAD_DIM]
            vg = v_ref[0, pl.ds(row0, tk), g * HEAD_DIM:(g + 1) * HEAD_DIM]
            s = lax.dot_general(qg, kg, (((1,), (1,)), ((), ())), preferred_element_type=F32) * scale
            s = (s.reshape(rep, qb, tk) + sc_ref[kt][None]).reshape(rep * qb, tk)
            m_new = jnp.maximum(m, jnp.max(s, axis=-1, keepdims=True))
            alpha = jnp.exp(m - m_new)
            p = jnp.exp(s - m_new)
            l = alpha * l + jnp.sum(p, axis=-1, keepdims=True)
            acc = alpha * acc + jnp.dot(p.astype(BF16), vg, preferred_element_type=F32)
            return m_new, l, acc

        init = (jnp.full((rep * qb, 1), NEG_BIG, F32), jnp.zeros((rep * qb, 1), F32),
                jnp.zeros((rep * qb, HEAD_DIM), F32))
        _, l, acc = lax.fori_loop(0, nkt, att_tile, init)
        o = acc / l
        for r in range(rep):
            h = g * rep + r
            o_ref[0, :, h * HEAD_DIM:(h + 1) * HEAD_DIM] = o[r * qb:(r + 1) * qb].astype(BF16)


def _dsa_attention(q, qi, sm, k_all, v_all, ki_all, *, qb, tk, n_keys, past):
    b, t, _ = q.shape
    nk_pad = k_all.shape[1]
    topk = min(TOPK_MAX, n_keys // 4)
    tokq = lambda w: pl.BlockSpec((1, qb, w), lambda i, j: (i, j, 0))
    keys = lambda w: pl.BlockSpec((1, nk_pad, w), lambda i, j: (i, 0, 0))
    return pl.pallas_call(
        functools.partial(_dsa_kernel, qb=qb, tk=tk, n_keys=n_keys, past=past, topk=topk,
                          n_bisect=20),
        grid=(b, t // qb),
        in_specs=[tokq(ATT_WIDTH), tokq(QI_WIDTH), tokq(LANES), keys(KV_WIDTH), keys(KV_WIDTH),
                  keys(IDX_DIM)],
        out_specs=tokq(ATT_WIDTH),
        out_shape=jax.ShapeDtypeStruct((b, t, ATT_WIDTH), BF16),
        scratch_shapes=[pltpu.VMEM((nk_pad // tk, qb, tk), F32)],
        compiler_params=_params(2),
        name="dsa_attention",
    )(q, qi, sm, k_all, v_all, ki_all)


def _ssd_kernel(xbc_ref, z_ref, sm_ref, st_ref, alog_ref, dexp_ref, nw_ref, e_ref,
                y_ref, stout_ref, ht_ref, yi_ref):
    c = pl.program_id(1)
    nc = pl.num_programs(1)
    L = xbc_ref.shape[1]
    hp = SSD_INNER // SSD_GROUPS

    @pl.when(c == 0)
    def _():
        ht_ref[...] = st_ref[0].reshape(SSD_INNER, SSD_STATE).T

    xs = xbc_ref[0, :, 0:SSD_INNER]
    sm = sm_ref[0]
    lane = lax.broadcasted_iota(jnp.int32, (1, LANES), 1)
    is_dt = (lane >= SM_DT) & (lane < SM_DT + SSD_HEADS)
    a_neg = jnp.where(is_dt, -jnp.exp(alog_ref[...]), 0.0)
    dt = jnp.where(is_dt, sm, 0.0)
    rows = lax.broadcasted_iota(jnp.int32, (L, L), 0)
    cols = lax.broadcasted_iota(jnp.int32, (L, L), 1)
    tri = cols <= rows
    e = e_ref[...]
    cum = jnp.dot(tri.astype(F32), dt * a_neg, precision=HIGHEST, preferred_element_type=F32)
    eye = (lax.broadcasted_iota(jnp.int32, (LANES, LANES), 0)
           == lax.broadcasted_iota(jnp.int32, (LANES, LANES), 1)).astype(F32)
    cum_t = lax.dot_general(eye, cum, (((1,), (1,)), ((), ())), precision=HIGHEST,
                            preferred_element_type=F32)
    cum_e = jnp.dot(cum, e, precision=HIGHEST, preferred_element_type=F32)
    dt_e = jnp.dot(dt, e, precision=HIGHEST, preferred_element_type=F32)
    last = cum_e[L - 1:L, :]
    xdt = (xs * dt_e).astype(BF16)
    xw = (xs * (jnp.exp(last - cum_e) * dt_e)).astype(BF16)
    ht_old = ht_ref[...]

    for g in range(SSD_GROUPS):
        bg = xbc_ref[0, :, SSD_INNER + g * SSD_STATE:SSD_INNER + (g + 1) * SSD_STATE]
        cg = xbc_ref[0, :, SSD_INNER + (SSD_GROUPS + g) * SSD_STATE:
                     SSD_INNER + (SSD_GROUPS + g + 1) * SSD_STATE]
        bgb, cgb = bg.astype(BF16), cg.astype(BF16)
        cb = lax.dot_general(cgb, bgb, (((1,), (1,)), ((), ())), preferred_element_type=F32)
        for hh in range(SSD_HEADS // SSD_GROUPS):
            h = g * (SSD_HEADS // SSD_GROUPS) + hh
            seg = cum[:, SM_DT + h:SM_DT + h + 1] - cum_t[SM_DT + h:SM_DT + h + 1, :]
            decay = jnp.exp(jnp.where(tri, seg, -jnp.inf))
            wts = (cb * decay).astype(BF16)
            yi_ref[:, h * SSD_HEADDIM:(h + 1) * SSD_HEADDIM] = jnp.dot(
                wts, xdt[:, h * SSD_HEADDIM:(h + 1) * SSD_HEADDIM], preferred_element_type=F32)
        ht_g = ht_old[:, g * hp:(g + 1) * hp]
        y_inter = jnp.dot(cgb, ht_g.astype(BF16), preferred_element_type=F32)
        yi_ref[:, g * hp:(g + 1) * hp] = (yi_ref[:, g * hp:(g + 1) * hp]
                                          + y_inter * jnp.exp(cum_e[:, g * hp:(g + 1) * hp]))
        upd = jnp.dot(bg.T.astype(BF16), xw[:, g * hp:(g + 1) * hp],
                      preferred_element_type=F32)
        ht_ref[:, g * hp:(g + 1) * hp] = ht_g * jnp.exp(last[:, g * hp:(g + 1) * hp]) + upd

    y = yi_ref[...] + dexp_ref[...] * xs
    y = y * _silu(z_ref[0])
    for g in range(SSD_GROUPS):
        yg = y[:, g * hp:(g + 1) * hp]
        ms = jnp.mean(yg * yg, axis=-1, keepdims=True)
        y_ref[0, :, g * hp:(g + 1) * hp] = (yg * lax.rsqrt(ms + EPS)
                                            * nw_ref[:, g * hp:(g + 1) * hp]).astype(BF16)

    @pl.when(c == nc - 1)
    def _():
        stout_ref[0] = ht_ref[...].T.reshape(SSD_HEADS, SSD_HEADDIM, SSD_STATE)


def _ssd(xbc_act, proj, sm, state, a_log_pad, d_exp, norm_w, e_mat, *, chunk):
    b, t, _ = xbc_act.shape
    tok = lambda w, col=0: pl.BlockSpec((1, chunk, w), lambda i, j: (i, j, col))
    full2 = lambda a: pl.BlockSpec(a.shape, lambda i, j: (0, 0))
    st = pl.BlockSpec((1, SSD_HEADS, SSD_HEADDIM, SSD_STATE), lambda i, j: (i, 0, 0, 0))
    nw = norm_w.reshape(1, SSD_INNER)
    return pl.pallas_call(
        _ssd_kernel,
        grid=(b, t // chunk),
        in_specs=[tok(SSD_CONV_DIM), tok(SSD_INNER, OFF_Z // SSD_INNER), tok(LANES), st,
                  full2(a_log_pad), full2(d_exp), full2(nw), full2(e_mat)],
        out_specs=[tok(SSD_INNER), st],
        out_shape=[jax.ShapeDtypeStruct((b, t, SSD_INNER), BF16),
                   jax.ShapeDtypeStruct(state.shape, F32)],
        scratch_shapes=[pltpu.VMEM((SSD_STATE, SSD_INNER), F32),
                        pltpu.VMEM((chunk, SSD_INNER), F32)],
        compiler_params=_params(2),
        name="ssd_scan",
    )(xbc_act, proj, sm, state, a_log_pad, d_exp, nw, e_mat)


def _lru_kernel(gate_ref, xin_ref, cw_ref, cb_ref, wa_ref, wx_ref, ba_ref, bx_ref, lam_ref,
                h0_ref, cs_ref, y_ref, hout_ref, cnew_ref, cbuf_ref, a_ref, u_ref, hp_ref):
    t = pl.program_id(1)
    nt = pl.num_programs(1)
    tt = xin_ref.shape[1]
    xc = _causal_conv(xin_ref[0], cs_ref, cw_ref, cb_ref, cbuf_ref, cnew_ref, t == 0, t == nt - 1)

    @pl.when(t == 0)
    def _():
        hp_ref[...] = h0_ref[0]

    for kb in range(LRU_BLOCKS):
        sl = slice(kb * LRU_BLOCK_DIM, (kb + 1) * LRU_BLOCK_DIM)
        xb = xc[:, sl].astype(BF16)
        a_ref[:, sl] = jnp.dot(xb, wa_ref[kb].astype(BF16), preferred_element_type=F32)
        u_ref[:, sl] = jnp.dot(xb, wx_ref[kb].astype(BF16), preferred_element_type=F32)
    r = jax.nn.sigmoid(a_ref[...] + ba_ref[...])
    i = jax.nn.sigmoid(u_ref[...] + bx_ref[...])
    log_a = -LRU_C * r * _softplus(-lam_ref[...])
    a = jnp.exp(log_a)
    u = jnp.sqrt(-jnp.tanh(log_a) * (1.0 + a * a)) * (i * xc)

    sub = lax.broadcasted_iota(jnp.int32, a.shape, 0) & (SUBLANES - 1)
    for s in (1, 2, 4):
        m = sub >= s
        a_sh = pltpu.roll(a, s, 0)
        u_sh = pltpu.roll(u, s, 0)
        u = jnp.where(m, a * u_sh + u, u)
        a = jnp.where(m, a * a_sh, a)
    a_ref[...] = a
    u_ref[...] = u

    def group(gi, hprev):
        r0 = pl.multiple_of(gi * SUBLANES, SUBLANES)
        hs = a_ref[pl.ds(r0, SUBLANES), :] * hprev + u_ref[pl.ds(r0, SUBLANES), :]
        u_ref[pl.ds(r0, SUBLANES), :] = hs
        return hs[SUBLANES - 1:SUBLANES, :]

    h_last = lax.fori_loop(0, tt // SUBLANES, group, hp_ref[...])
    hp_ref[...] = h_last
    gate = gate_ref[0]
    gelu = 0.5 * gate * (1.0 + jnp.tanh(np.sqrt(2.0 / np.pi) * (gate + 0.044715 * (gate * gate * gate))))
    y_ref[0] = (u_ref[...] * gelu).astype(BF16)

    @pl.when(t == nt - 1)
    def _():
        hout_ref[0] = h_last


def _lru(proj, conv_w, conv_b, w_a, b_a, w_x, b_x, lam, h0, conv_state, *, tt):
    b, t, _ = proj.shape
    w = LRU_WIDTH
    tok = lambda col: pl.BlockSpec((1, tt, w), lambda i, j: (i, j, col))
    full = lambda a: pl.BlockSpec(a.shape, lambda i, j: (0,) * a.ndim)
    row = lambda v: v.reshape(1, w)
    state = pl.BlockSpec((1, CONV_W - 1, w), lambda i, j: (i, 0, 0))
    hspec = pl.BlockSpec((1, 1, w), lambda i, j: (i, 0, 0))
    args = (proj, proj, conv_w, row(conv_b), w_a, w_x, row(b_a), row(b_x), row(lam), h0, conv_state)
    return pl.pallas_call(
        _lru_kernel,
        grid=(b, t // tt),
        in_specs=[tok(0), tok(1)] + [full(a) for a in args[2:9]] + [hspec, state],
        out_specs=[pl.BlockSpec((1, tt, w), lambda i, j: (i, j, 0)), hspec, state],
        out_shape=[jax.ShapeDtypeStruct((b, t, w), BF16),
                   jax.ShapeDtypeStruct((b, 1, w), F32),
                   jax.ShapeDtypeStruct((b, CONV_W - 1, w), F32)],
        scratch_shapes=[pltpu.VMEM((tt + SUBLANES, w), F32), pltpu.VMEM((tt, w), F32),
                        pltpu.VMEM((tt, w), F32), pltpu.VMEM((1, w), F32)],
        compiler_params=_params(2),
        name="rg_lru",
    )(*args)


def _rope_tables(pos, dim):
    half = dim // 2
    inv = np.power(ROPE_THETA, -np.arange(half, dtype=np.float64) / half)
    ang = pos.astype(np.float64)[:, None] * inv[None, :]
    cos = np.concatenate([np.cos(ang), np.cos(ang)], axis=-1)
    sin = np.concatenate([-np.sin(ang), np.sin(ang)], axis=-1)
    reps = LANES // dim
    return (jnp.asarray(np.tile(cos, (1, reps)), F32), jnp.asarray(np.tile(sin, (1, reps)), F32))


def _head_expansion():
    e = np.zeros((LANES, SSD_INNER), np.float32)
    for h in range(SSD_HEADS):
        e[SM_DT + h, h * SSD_HEADDIM:(h + 1) * SSD_HEADDIM] = 1.0
    return jnp.asarray(e)


def _pad_lanes(v, offset):
    out = jnp.zeros((1, LANES), F32)
    return out.at[0, offset:offset + v.shape[0]].set(v)


def _reorder_w_in_ab(w):
    d = w.shape[0]
    q, k, v, qi, ki, wi, z, xbc, dt = (1024, 256, 256, 1024, 64, 16, 1024, 1536, 16)
    o = np.cumsum([0, q, k, v, qi, ki, wi, z, xbc, dt])
    seg = lambda i: w[:, o[i]:o[i + 1]]
    parts = [seg(0), seg(3), seg(6), seg(7), seg(1), seg(2), seg(4), seg(5), seg(8)]
    used = sum(p.shape[1] for p in parts)
    parts.append(jnp.zeros((d, AB_PAD_N - used), w.dtype))
    return jnp.concatenate(parts, axis=1)


def _trunk(x, mods, past_k, past_v, past_ki, ssm_h, ssm_conv, lru_h, lru_conv, p, *, cfg):
    b, t, d = x.shape
    bb, tt = cfg["bb"], cfg["tt"]
    past = 0 if past_k is None else past_k.shape[1]
    n_keys = past + t
    pos = np.arange(past, past + t)
    tabs = _rope_tables(pos, HEAD_DIM) + _rope_tables(pos, IDX_DIM)

    def ffn(x, l):
        sh_f, sc_f, g_f = mods[l][3], mods[l][4], mods[l][5]
        gu = _norm_mod_matmul(x, p["norm_ffn_w"][l], sc_f, sh_f,
                              [p["ffn_w_gate"][l], p["ffn_w_up"][l]],
                              bb=bb, tt=tt, tn=512, out_dtype=BF16, name="ffn_gate_up")
        return _matmul_residual([gu], p["ffn_w_down"][l], x, g_f, bb=bb, tt=tt, tn=256,
                                name="ffn_down")

    sh_m, sc_m, g_m = mods[0][0], mods[0][1], mods[0][2]
    proj = _norm_mod_matmul(x, p["norm_mix_w"][0], sc_m, sh_m, [p["w_in_ab_r"]],
                            bb=bb, tt=tt, tn=AB_TILE_N, out_dtype=F32, name="in_proj_ab")
    (q, qi, k, v, ki, sm, xbc_act, ssm_conv_new, kb, vb, kib) = _post0(
        proj, tabs, p["q_norm_w"][0], p["k_norm_w"][0], p["ssd_conv_w"][0], p["ssd_conv_b"][0],
        p["dt_bias_pad"], ssm_conv, tt=cfg["post_tt"])
    if past:
        nk_pad = -(-n_keys // cfg["tk"]) * cfg["tk"]
        cat = lambda old, new: jnp.pad(
            jnp.concatenate([old.reshape(b, past, -1).astype(BF16), new], axis=1),
            ((0, 0), (0, nk_pad - n_keys), (0, 0)))
        k_all, v_all, ki_all = cat(past_k, kb), cat(past_v, vb), cat(past_ki, kib)
    else:
        k_all, v_all, ki_all = kb, vb, kib
    att = _dsa_attention(q, qi, sm, k_all, v_all, ki_all, qb=cfg["qb"], tk=cfg["tk"],
                         n_keys=n_keys, past=past)
    y_ssd, ssm_new = _ssd(xbc_act, proj, sm, ssm_h, p["a_log_pad"], p["d_exp"], p["ssd_norm_w"][0],
                          p["e_mat"], chunk=cfg["ssd_chunk"])
    x = _matmul_residual([att, y_ssd], p["w_out_ab"][0], x, g_m, bb=bb, tt=tt, tn=512,
                         name="out_proj_ab")
    x = ffn(x, 0)

    sh_m, sc_m, g_m = mods[1][0], mods[1][1], mods[1][2]
    proj = _norm_mod_matmul(x, p["norm_mix_w"][1], sc_m, sh_m, [p["w_in_c"][0]],
                            bb=bb, tt=tt, tn=512, out_dtype=F32, name="in_proj_c")
    y_lru, lru_new, lru_conv_new = _lru(
        proj, p["lru_conv_w"][0], p["lru_conv_b"][0], p["lru_w_a"][0], p["lru_b_a"][0],
        p["lru_w_x"][0], p["lru_b_x"][0], p["lru_lambda"][0], lru_h.reshape(b, 1, LRU_WIDTH),
        lru_conv, tt=cfg["lru_tt"])
    x = _matmul_residual([y_lru], p["w_out_c"][0], x, g_m, bb=bb, tt=tt, tn=512, name="out_proj_c")
    x = ffn(x, 1)

    kv_shape = (1, b, t, KV_HEADS, HEAD_DIM)
    return (x, k.reshape(kv_shape), v.reshape(kv_shape), ki[None], ssm_new[None], ssm_conv_new[None],
            lru_new.reshape(1, b, LRU_WIDTH), lru_conv_new[None])


PROMPT_CFG = dict(bb=1, tt=1024, post_tt=256, qb=128, tk=512, ssd_chunk=128, lru_tt=256)
SAMPLE_CFG = dict(bb=16, tt=32, post_tt=32, qb=32, tk=384, ssd_chunk=32, lru_tt=32)


def kernel(x_prompt, x_sample, cache_attn_k, cache_attn_v, cache_idx_k, state_ssm, state_ssm_conv,
           state_lru, state_lru_conv, c_prompt, c_sample, ada_w, ada_b, norm_mix_w, norm_ffn_w,
           w_in_ab, q_norm_w, k_norm_w, ssd_conv_w, ssd_conv_b, ssd_dt_bias, ssd_a_log, ssd_d,
           ssd_norm_w, w_out_ab, w_in_c, lru_conv_w, lru_conv_b, lru_w_a, lru_b_a, lru_w_x, lru_b_x,
           lru_lambda, w_out_c, ffn_w_gate, ffn_w_up, ffn_w_down):
    bp, bs = x_prompt.shape[0], x_sample.shape[0]
    p = dict(norm_mix_w=norm_mix_w, norm_ffn_w=norm_ffn_w, q_norm_w=q_norm_w, k_norm_w=k_norm_w,
             ssd_conv_w=ssd_conv_w, ssd_conv_b=ssd_conv_b, ssd_norm_w=ssd_norm_w, w_out_ab=w_out_ab,
             w_in_c=w_in_c, lru_conv_w=lru_conv_w, lru_conv_b=lru_conv_b, lru_w_a=lru_w_a,
             lru_b_a=lru_b_a, lru_w_x=lru_w_x, lru_b_x=lru_b_x, lru_lambda=lru_lambda,
             w_out_c=w_out_c, ffn_w_gate=ffn_w_gate, ffn_w_up=ffn_w_up, ffn_w_down=ffn_w_down)
    p["w_in_ab_r"] = _reorder_w_in_ab(w_in_ab[0])
    p["dt_bias_pad"] = _pad_lanes(ssd_dt_bias[0], SM_DT)
    p["a_log_pad"] = _pad_lanes(ssd_a_log[0], SM_DT)
    p["d_exp"] = jnp.repeat(ssd_d[0], SSD_HEADDIM).reshape(1, SSD_INNER)
    p["e_mat"] = _head_expansion()

    m_rows = 32
    c_all = jnp.concatenate([c_prompt, c_sample, jnp.zeros((m_rows - bp - bs, D_MODEL), F32)], axis=0)
    mod = _modulation(c_all, ada_w, ada_b)

    def group_mods(r0, nb):
        return [[mod[l, r0:r0 + nb, i * D_MODEL:(i + 1) * D_MODEL].reshape(nb, 1, D_MODEL)
                 for i in range(6)] for l in range(mod.shape[0])]

    zeros = lambda *s: jnp.zeros(s, F32)
    out_p = _trunk(x_prompt, group_mods(0, bp), None, None, None,
                   zeros(bp, SSD_HEADS, SSD_HEADDIM, SSD_STATE), zeros(bp, CONV_W - 1, SSD_CONV_DIM),
                   zeros(bp, LRU_WIDTH), zeros(bp, CONV_W - 1, LRU_WIDTH), p, cfg=PROMPT_CFG)
    out_s = _trunk(x_sample, group_mods(bp, bs), cache_attn_k[0], cache_attn_v[0], cache_idx_k[0],
                   state_ssm[0], state_ssm_conv[0], state_lru[0], state_lru_conv[0], p, cfg=SAMPLE_CFG)
    return (out_p[0], out_s[0]) + out_p[1:] + out_s[1:]
```

```python
import functools

import numpy as np
import jax
import jax.numpy as jnp
from jax import lax
from jax.experimental import pallas as pl
from jax.experimental.pallas import tpu as pltpu

F32 = jnp.float32
BF16 = jnp.bfloat16
HIGHEST = lax.Precision.HIGHEST

D_MODEL = 2048
CHUNK = 64
CHUNK_SHIFT = 6
ATT_HEADS = 8
KV_HEADS = 2
HEAD_DIM = 128
IDX_HEADS = 16
IDX_DIM = 64
TOPK_MAX = 256
ROPE_THETA = 10000.0
SSD_HEADS = 16
SSD_HEADDIM = 64
SSD_GROUPS = 2
SSD_STATE = 128
SSD_INNER = SSD_HEADS * SSD_HEADDIM
SSD_CONV_DIM = SSD_INNER + 2 * SSD_GROUPS * SSD_STATE
CONV_W = 4
LRU_WIDTH = D_MODEL
LRU_BLOCKS = 16
LRU_BLOCK_DIM = LRU_WIDTH // LRU_BLOCKS
LRU_C = 8.0
EPS = 1e-6
ATT_WIDTH = ATT_HEADS * HEAD_DIM
KV_WIDTH = KV_HEADS * HEAD_DIM
VEXT_WIDTH = 2 * KV_WIDTH
QI_WIDTH = IDX_HEADS * IDX_DIM
Q_SCALE = HEAD_DIM ** -0.5 * float(np.log2(np.e))

LANES = 128
SUBLANES = 8
VMEM_LIMIT_BYTES = 56 * 1024 * 1024

OFF_Q = 0
OFF_QI = OFF_Q + ATT_WIDTH
OFF_Z = OFF_QI + QI_WIDTH
OFF_XBC = OFF_Z + SSD_INNER
OFF_K = OFF_XBC + SSD_CONV_DIM
OFF_V = OFF_K + KV_WIDTH
OFF_SMALL = OFF_V + KV_WIDTH
SM_WI = IDX_DIM
SM_DT = IDX_DIM + IDX_HEADS
AB_TILE_N = 768
AB_PAD_N = 5376

NEG_BIG = -1e30


def _params(n_axes):
    return pltpu.CompilerParams(dimension_semantics=("arbitrary",) * n_axes,
                                vmem_limit_bytes=VMEM_LIMIT_BYTES)


def _silu(x):
    return x * jax.nn.sigmoid(x)


def _softplus(x):
    return jnp.maximum(x, 0.0) + jnp.log1p(jnp.exp(-jnp.abs(x)))


def _mod_kernel(c_ref, w_ref, b_ref, o_ref):
    a = _silu(c_ref[...]).astype(BF16)
    o_ref[0] = jnp.dot(a, w_ref[0].astype(BF16), preferred_element_type=F32) + b_ref[0]


def _modulation(c_all, ada_w, ada_b):
    depth, d, n = ada_w.shape
    m = c_all.shape[0]
    tn = 1024
    return pl.pallas_call(
        _mod_kernel,
        grid=(depth, n // tn),
        in_specs=[pl.BlockSpec((m, d), lambda l, j: (0, 0)),
                  pl.BlockSpec((1, d, tn), lambda l, j: (l, 0, j)),
                  pl.BlockSpec((1, 1, tn), lambda l, j: (l, 0, j))],
        out_specs=pl.BlockSpec((1, m, tn), lambda l, j: (l, 0, j)),
        out_shape=jax.ShapeDtypeStruct((depth, m, n), F32),
        compiler_params=_params(2),
        name="adaln_mod",
    )(c_all, ada_w, ada_b.reshape(depth, 1, n))


def _nmm_kernel(x_ref, nw_ref, sc_ref, sh_ref, *rest, swiglu):
    n_w = 2 if swiglu else 1
    w_refs, o_ref, h_ref = rest[:n_w], rest[n_w], rest[n_w + 1]
    bb, tt, d = x_ref.shape

    @pl.when(pl.program_id(2) == 0)
    def _():
        x = x_ref[...]
        ms = jnp.mean(x * x, axis=-1, keepdims=True)
        y = x * lax.rsqrt(ms + EPS) * nw_ref[...]
        h = y * (1.0 + sc_ref[...]) + sh_ref[...]
        h_ref[...] = h.reshape(bb * tt, d).astype(BF16)

    h = h_ref[...]
    if swiglu:
        g = jnp.dot(h, w_refs[0][...].astype(BF16), preferred_element_type=F32)
        u = jnp.dot(h, w_refs[1][...].astype(BF16), preferred_element_type=F32)
        o = _silu(g) * u
    else:
        o = jnp.dot(h, w_refs[0][...].astype(BF16), preferred_element_type=F32)
    o_ref[...] = o.reshape(o_ref.shape).astype(o_ref.dtype)


def _norm_mod_matmul(x, nw, sc, sh, ws, layer, *, bb, tt, tn, out_dtype, name):
    b, t, d = x.shape
    n = ws[0].shape[2]
    swiglu = len(ws) == 2
    xmap = lambda i, j, k: (i, j, 0)
    mmap = lambda i, j, k: (i, 0, 0)
    return pl.pallas_call(
        functools.partial(_nmm_kernel, swiglu=swiglu),
        grid=(b // bb, t // tt, n // tn),
        in_specs=[pl.BlockSpec((bb, tt, d), xmap),
                  pl.BlockSpec((1, 1, d), lambda i, j, k: (0, 0, 0)),
                  pl.BlockSpec((bb, 1, d), mmap),
                  pl.BlockSpec((bb, 1, d), mmap)]
                 + [pl.BlockSpec((None, d, tn), lambda i, j, k: (layer, 0, k))] * len(ws),
        out_specs=pl.BlockSpec((bb, tt, tn), lambda i, j, k: (i, j, k)),
        out_shape=jax.ShapeDtypeStruct((b, t, n), out_dtype),
        scratch_shapes=[pltpu.VMEM((bb * tt, d), BF16)],
        compiler_params=_params(3),
        name=name,
    )(x, nw.reshape(1, 1, d), sc, sh, *ws)


def _mmr_kernel(*refs, k_sizes):
    n_a = len(k_sizes)
    a_refs = refs[:n_a]
    w_ref, x_ref, g_ref, o_ref = refs[n_a:]
    bb, tt, tn = o_ref.shape
    acc = jnp.zeros((bb * tt, tn), F32)
    off = 0
    for a_ref, ks in zip(a_refs, k_sizes):
        a = a_ref[...].reshape(bb * tt, ks)
        acc = acc + jnp.dot(a, w_ref[off:off + ks, :].astype(BF16), preferred_element_type=F32)
        off += ks
    o_ref[...] = x_ref[...] + g_ref[...] * acc.reshape(bb, tt, tn)


def _matmul_residual(a_list, w, layer, x, g, *, bb, tt, tn, name):
    b, t, d = x.shape
    k_sizes = tuple(a.shape[-1] for a in a_list)
    k_total = sum(k_sizes)
    return pl.pallas_call(
        functools.partial(_mmr_kernel, k_sizes=k_sizes),
        grid=(b // bb, t // tt, d // tn),
        in_specs=[pl.BlockSpec((bb, tt, ks), lambda i, j, k: (i, j, 0)) for ks in k_sizes]
                 + [pl.BlockSpec((None, k_total, tn), lambda i, j, k: (layer, 0, k)),
                    pl.BlockSpec((bb, tt, tn), lambda i, j, k: (i, j, k)),
                    pl.BlockSpec((bb, 1, tn), lambda i, j, k: (i, 0, k))],
        out_specs=pl.BlockSpec((bb, tt, tn), lambda i, j, k: (i, j, k)),
        out_shape=jax.ShapeDtypeStruct((b, t, d), F32),
        compiler_params=_params(3),
        name=name,
    )(*a_list, w, x, g)


def _swap_halves(x, half):
    w = x.shape[-1]
    lane = lax.broadcasted_iota(jnp.int32, x.shape, x.ndim - 1)
    first = (lane & half) == 0
    return jnp.where(first, pltpu.roll(x, w - half, x.ndim - 1), pltpu.roll(x, half, x.ndim - 1))


def _rope(x, cos, sin_signed, half):
    reps = x.shape[-1] // cos.shape[-1]
    if reps > 1:
        cos = jnp.concatenate([cos] * reps, axis=-1)
        sin_signed = jnp.concatenate([sin_signed] * reps, axis=-1)
    return x * cos + _swap_halves(x, half) * sin_signed


def _causal_conv(u, state_ref, w_ref, b_ref, cbuf_ref, new_ref, first, last):
    tt = u.shape[0]

    @pl.when(first)
    def _():
        cbuf_ref[0:SUBLANES, :] = jnp.zeros((SUBLANES, u.shape[1]), F32)
        cbuf_ref[SUBLANES - (CONV_W - 1):SUBLANES, :] = state_ref[0]

    cbuf_ref[SUBLANES:SUBLANES + tt, :] = u
    out = b_ref[...]
    for j in range(CONV_W):
        start = SUBLANES - (CONV_W - 1) + j
        out = out + cbuf_ref[start:start + tt, :] * w_ref[j:j + 1, :]
    tail = cbuf_ref[tt:tt + SUBLANES, :]
    cbuf_ref[0:SUBLANES, :] = tail

    @pl.when(last)
    def _():
        new_ref[0] = tail[SUBLANES - (CONV_W - 1):, :]

    return out


def _post0_kernel(p_ref, cq_ref, sq_ref, ci_ref, si_ref, qn_ref, kn_ref, cw_ref, cb_ref, dtb_ref,
                  cs_ref,
                  q_ref, qi_ref, k_ref, v_ref, ki_ref, sm_ref, xbc_ref, cnew_ref, kb_ref, vb_ref,
                  kib_ref, cbuf_ref):
    t = pl.program_id(1)
    nt = pl.num_programs(1)
    cq, sq, ci, si = cq_ref[...], sq_ref[...], ci_ref[...], si_ref[...]

    def head_norm_rope(x, w):
        ms = jnp.mean(x * x, axis=-1, keepdims=True)
        return _rope(x * lax.rsqrt(ms + EPS) * w, cq, sq, HEAD_DIM // 2)

    for h in range(ATT_HEADS):
        xh = p_ref[0, :, OFF_Q + h * HEAD_DIM:OFF_Q + (h + 1) * HEAD_DIM]
        q_ref[0, :, h * HEAD_DIM:(h + 1) * HEAD_DIM] = (head_norm_rope(xh, qn_ref[...])
                                                         * Q_SCALE).astype(BF16)
    ones = jnp.ones((p_ref.shape[1], HEAD_DIM), BF16)
    for h in range(KV_HEADS):
        xh = p_ref[0, :, OFF_K + h * HEAD_DIM:OFF_K + (h + 1) * HEAD_DIM]
        kh = head_norm_rope(xh, kn_ref[...])
        k_ref[0, :, h * HEAD_DIM:(h + 1) * HEAD_DIM] = kh
        kb_ref[0, :, h * HEAD_DIM:(h + 1) * HEAD_DIM] = kh.astype(BF16)
        vh = p_ref[0, :, OFF_V + h * HEAD_DIM:OFF_V + (h + 1) * HEAD_DIM]
        v_ref[0, :, h * HEAD_DIM:(h + 1) * HEAD_DIM] = vh
        vb_ref[0, :, 2 * h * HEAD_DIM:(2 * h + 1) * HEAD_DIM] = vh.astype(BF16)
        vb_ref[0, :, (2 * h + 1) * HEAD_DIM:(2 * h + 2) * HEAD_DIM] = ones

    qi = p_ref[0, :, OFF_QI:OFF_QI + QI_WIDTH]
    qi_ref[0] = _rope(qi, ci, si, IDX_DIM // 2).astype(BF16)

    sm = p_ref[0, :, OFF_SMALL:OFF_SMALL + LANES]
    lane = lax.broadcasted_iota(jnp.int32, sm.shape, 1)
    ki_part = _rope(sm, ci, si, IDX_DIM // 2)
    wi_part = sm * (IDX_HEADS ** -0.5 * IDX_DIM ** -0.5)
    dt_part = _softplus(sm + dtb_ref[...])
    sm_out = jnp.where(lane < SM_WI, ki_part,
                       jnp.where(lane < SM_DT, wi_part,
                                 jnp.where(lane < SM_DT + SSD_HEADS, dt_part, 0.0)))
    sm_ref[0] = sm_out
    ki_ref[0] = sm_out[:, :IDX_DIM]
    kib_ref[0] = sm_out[:, :IDX_DIM].astype(BF16)

    xbc = p_ref[0, :, OFF_XBC:OFF_XBC + SSD_CONV_DIM]
    conv = _causal_conv(xbc, cs_ref, cw_ref, cb_ref, cbuf_ref, cnew_ref, t == 0, t == nt - 1)
    xbc_ref[0] = _silu(conv)


def _post0(proj, tabs, q_norm_w, k_norm_w, conv_w, conv_b, dt_bias_pad, conv_state, *, tt):
    b, t, n = proj.shape
    tok = lambda w: pl.BlockSpec((1, tt, w), lambda i, j: (i, j, 0))
    tab = pl.BlockSpec((tt, LANES), lambda i, j: (j, 0))
    full2 = lambda a: pl.BlockSpec(a.shape, lambda i, j: (0, 0))
    state = lambda c: pl.BlockSpec((1, CONV_W - 1, c), lambda i, j: (i, 0, 0))
    out_shapes = [
        jax.ShapeDtypeStruct((b, t, ATT_WIDTH), BF16),
        jax.ShapeDtypeStruct((b, t, QI_WIDTH), BF16),
        jax.ShapeDtypeStruct((b, t, KV_WIDTH), F32),
        jax.ShapeDtypeStruct((b, t, KV_WIDTH), F32),
        jax.ShapeDtypeStruct((b, t, IDX_DIM), F32),
        jax.ShapeDtypeStruct((b, t, LANES), F32),
        jax.ShapeDtypeStruct((b, t, SSD_CONV_DIM), F32),
        jax.ShapeDtypeStruct((b, CONV_W - 1, SSD_CONV_DIM), F32),
        jax.ShapeDtypeStruct((b, t, KV_WIDTH), BF16),
        jax.ShapeDtypeStruct((b, t, VEXT_WIDTH), BF16),
        jax.ShapeDtypeStruct((b, t, IDX_DIM), BF16),
    ]
    out_specs = [tok(ATT_WIDTH), tok(QI_WIDTH), tok(KV_WIDTH), tok(KV_WIDTH), tok(IDX_DIM),
                 tok(LANES), tok(SSD_CONV_DIM), state(SSD_CONV_DIM), tok(KV_WIDTH), tok(VEXT_WIDTH),
                 tok(IDX_DIM)]
    qn = q_norm_w.reshape(1, HEAD_DIM)
    kn = k_norm_w.reshape(1, HEAD_DIM)
    cb = conv_b.reshape(1, SSD_CONV_DIM)
    return pl.pallas_call(
        _post0_kernel,
        grid=(b, t // tt),
        in_specs=[tok(n), tab, tab, tab, tab, full2(qn), full2(kn), full2(conv_w), full2(cb),
                  full2(dt_bias_pad), state(SSD_CONV_DIM)],
        out_specs=out_specs,
        out_shape=out_shapes,
        scratch_shapes=[pltpu.VMEM((tt + SUBLANES, SSD_CONV_DIM), F32)],
        compiler_params=_params(2),
        name="post_proj0",
    )(proj, *tabs, qn, kn, conv_w, cb, dt_bias_pad, conv_state)


def _dsa_kernel(q_ref, qi_ref, sm_ref, k_ref, v_ref, ki_ref, o_ref, sc_ref, bias_ref, acc_ref, qs_ref,
                s0_ref, s1_ref, mt0_ref, mt1_ref, m_ref, *, qb, tk, n_keys, past, topk, n_bisect,
                keys_on_sublanes):
    kax = 0 if keys_on_sublanes else 1
    tile_shape = (tk, qb) if keys_on_sublanes else (qb, tk)
    vec_shape = (1, qb) if keys_on_sublanes else (qb, 1)
    n_acc = 8
    part_shape = (n_acc, SUBLANES, qb) if keys_on_sublanes else (qb, LANES)
    j = pl.program_id(1)
    pos0 = past + j * qb
    q_chunk = lax.shift_right_logical(pos0 + lax.broadcasted_iota(jnp.int32, vec_shape, 1 - kax),
                                      CHUNK_SHIFT)
    n_valid = jnp.minimum((q_chunk + 1) * CHUNK, n_keys)
    max_valid = jnp.minimum(((pos0 + qb - 1) // CHUNK + 1) * CHUNK, n_keys)
    nkt = (max_valid + tk - 1) // tk
    kf = float(topk)
    active_f = jnp.where(n_valid > topk, 1.0, 0.0)

    def key_index(kt):
        return kt * tk + lax.broadcasted_iota(jnp.int32, tile_shape, kax)

    nt_dims = (((1,), (1,)), ((), ()))
    if keys_on_sublanes:
        for h in range(IDX_HEADS):
            qs_ref[h * qb:(h + 1) * qb, :] = qi_ref[0, :, h * IDX_DIM:(h + 1) * IDX_DIM]
        wi_t = sm_ref[0].T
        head_w = lambda h: wi_t[SM_WI + h:SM_WI + h + 1, :]
    else:
        qi = qi_ref[0]
        wis = sm_ref[0][:, SM_WI:SM_WI + IDX_HEADS]
        head_w = lambda h: wis[:, h:h + 1]

    def score_tile(kt, carry):
        kit = ki_ref[0, pl.ds(pl.multiple_of(kt * tk, tk), tk), :]
        acc = jnp.zeros(tile_shape, F32)
        if keys_on_sublanes:
            for pr in range(IDX_HEADS // 2):
                s2 = lax.dot_general(kit, qs_ref[2 * pr * qb:(2 * pr + 2) * qb, :], nt_dims,
                                     preferred_element_type=F32)
                for e in range(2):
                    acc = acc + head_w(2 * pr + e) * jnp.maximum(s2[:, e * qb:(e + 1) * qb], 0.0)
        else:
            for h in range(IDX_HEADS):
                s = lax.dot_general(qi[:, h * IDX_DIM:(h + 1) * IDX_DIM], kit, nt_dims,
                                    preferred_element_type=F32)
                acc = acc + head_w(h) * jnp.maximum(s, 0.0)
        sc_ref[kt] = jnp.where(key_index(kt) < n_valid, acc, -jnp.inf)
        return carry

    lax.fori_loop(0, nkt, score_tile, 0)

    def fold(m, op2, red):
        if keys_on_sublanes:
            return red(m.reshape(tk // (n_acc * SUBLANES), n_acc, SUBLANES, qb), axis=0)
        f = m[:, 0:LANES]
        for c in range(1, tk // LANES):
            f = op2(f, m[:, c * LANES:(c + 1) * LANES])
        return f

    def reduce_tiles(make, op2, red, init):
        def body(kt, part):
            return op2(part, fold(make(sc_ref[kt], kt), op2, red))
        part = lax.fori_loop(0, nkt, body, jnp.full(part_shape, init, F32))
        if keys_on_sublanes:
            part = red(part, axis=0)
        return red(part, axis=kax, keepdims=True)

    def count(pred):
        return reduce_tiles(lambda t, kt: jnp.where(pred(t, kt), 1.0, 0.0), jnp.add, jnp.sum, 0.0)

    def row_max(pred):
        return reduce_tiles(lambda t, kt: jnp.where(pred(t, kt), t, -jnp.inf), jnp.maximum, jnp.max,
                            -jnp.inf)

    def row_min_valid():
        return reduce_tiles(lambda t, kt: jnp.where(key_index(kt) < n_valid, t, jnp.inf), jnp.minimum,
                            jnp.min, jnp.inf)

    def any_set(flag_f):
        return jnp.max(flag_f) > 0.0

    def search_cond(state):
        it, lo, hi, c_lo = state
        return (it < n_bisect) & any_set(jnp.where(c_lo != kf, active_f, 0.0))

    def search_body(state):
        it, lo, hi, c_lo = state
        mid = lo + (hi - lo) * 0.5
        c = count(lambda t, kt: t >= mid)
        ge = c >= kf
        return it + 1, jnp.where(ge, mid, lo), jnp.where(ge, hi, mid), jnp.where(ge, c, c_lo)

    lo0 = row_min_valid()
    hi0 = row_max(lambda t, kt: t == t)
    _, lo, hi, c_lo = lax.while_loop(search_cond, search_body,
                                     (jnp.int32(0), lo0, hi0, n_valid.astype(F32)))
    found_f = jnp.where(c_lo == kf, active_f, 0.0)
    v_found = reduce_tiles(lambda t, kt: jnp.where(t >= lo, t, jnp.inf), jnp.minimum, jnp.min, jnp.inf)

    def walk_down():
        w0 = row_max(lambda t, kt: t <= hi)
        c0 = count(lambda t, kt: t >= w0)

        def walk_cond(state):
            w, c = state
            return any_set(jnp.where(c < kf, active_f, 0.0))

        def walk_body(state):
            w, c = state
            w2 = row_max(lambda t, kt: t < w)
            c2 = count(lambda t, kt: t >= w2)
            upd = c < kf
            return jnp.where(upd, w2, w), jnp.where(upd, c2, c)

        return lax.while_loop(walk_cond, walk_body,
                              (jnp.where(found_f > 0.5, v_found, w0),
                               jnp.where(found_f > 0.5, kf, c0)))

    w, c_ge = lax.cond(any_set(active_f - found_f), walk_down,
                       lambda: (v_found, jnp.full(vec_shape, kf, F32)))
    thr = jnp.where(active_f > 0.5, w, -jnp.inf)
    tied_f = jnp.where(c_ge > kf, active_f, 0.0)

    def write_bias(sel_fn):
        def body(kt, carry):
            kidx = key_index(kt)
            sel = sel_fn(sc_ref[kt], kidx) & (kidx < n_valid)
            bias = jnp.where(sel, 0.0, NEG_BIG)
            bias_ref[kt] = bias.T if keys_on_sublanes else bias
            return carry
        lax.fori_loop(0, nkt, body, 0)

    def no_ties():
        write_bias(lambda t, kidx: t >= thr)

    def with_ties():
        need = kf - count(lambda t, kt: t > thr)
        n_steps = max(1, int(np.ceil(np.log2(sc_ref.shape[0] * tk))) + 1)

        def step(_, lh):
            lo_i, hi_i = lh
            mid = lax.shift_right_arithmetic(lo_i + hi_i, 1)
            ge = count(lambda t, kt: (t == thr) & (key_index(kt) <= mid)) >= need
            return jnp.where(ge, lo_i, mid), jnp.where(ge, mid, hi_i)

        lo_i = jnp.full(vec_shape, -1, jnp.int32)
        hi_i = jnp.full(vec_shape, sc_ref.shape[0] * tk - 1, jnp.int32)
        _, last_tie = lax.fori_loop(0, n_steps, step, (lo_i, hi_i))
        write_bias(lambda t, kidx: (t > thr) | ((t == thr)
                                                & ((kidx <= last_tie) | (tied_f < 0.5))))

    any_tie = jnp.max(tied_f) > 0.0
    lax.cond(any_tie, with_ties, no_ties)

    rep = ATT_HEADS // KV_HEADS
    qgs = [jnp.concatenate(
        [q_ref[0, :, (g * rep + r) * HEAD_DIM:(g * rep + r + 1) * HEAD_DIM] for r in range(rep)],
        axis=0) for g in range(KV_HEADS)]
    acc_ref[...] = jnp.zeros(acc_ref.shape, F32)
    bufs = ((s0_ref, mt0_ref), (s1_ref, mt1_ref))

    def qk_tile(kt, buf):
        s_ref, mt_ref = bufs[buf]
        row0 = pl.multiple_of(kt * tk, tk)
        bias = bias_ref[kt]
        for g in range(KV_HEADS):
            kg = k_ref[0, pl.ds(row0, tk), g * HEAD_DIM:(g + 1) * HEAD_DIM]
            s = lax.dot_general(qgs[g], kg, nt_dims, preferred_element_type=F32)
            s = (s.reshape(rep, qb, tk) + bias[None]).reshape(rep * qb, tk)
            s_ref[g] = s
            mt_ref[g] = jnp.broadcast_to(jnp.max(s, axis=-1, keepdims=True), (rep * qb, LANES))

    def lane_tile(x, width):
        return jnp.concatenate([x] * (width // LANES), axis=1)

    def att_tile(kt, buf, prefetch):
        if prefetch:
            qk_tile(kt + 1, 1 - buf)
        s_ref, mt_ref = bufs[buf]
        row0 = pl.multiple_of(kt * tk, tk)
        for g in range(KV_HEADS):
            vg = v_ref[0, pl.ds(row0, tk), 2 * g * HEAD_DIM:(2 * g + 2) * HEAD_DIM]
            m_old = m_ref[g]
            m_new = jnp.maximum(m_old, mt_ref[g])
            m_ref[g] = m_new
            alpha = jnp.exp2(m_old - m_new)
            p = jnp.exp2(s_ref[g] - lane_tile(m_new, tk)).astype(BF16)
            acc_ref[g] = (lane_tile(alpha, 2 * HEAD_DIM) * acc_ref[g]
                          + jnp.dot(p, vg, preferred_element_type=F32))

    def tile_pair(pi, carry):
        att_tile(2 * pi, 0, True)
        att_tile(2 * pi + 1, 1, True)
        return carry

    def last_two():
        att_tile(nkt - 2, 0, True)
        att_tile(nkt - 1, 1, False)

    def last_one():
        att_tile(nkt - 1, 0, False)

    m_ref[...] = jnp.full(m_ref.shape, NEG_BIG, F32)
    qk_tile(0, 0)
    lax.fori_loop(0, (nkt - 1) // 2, tile_pair, 0)
    lax.cond((nkt & 1) == 0, last_two, last_one)
    for g in range(KV_HEADS):
        acc = acc_ref[g]
        o = acc[:, :HEAD_DIM] / acc[:, HEAD_DIM:]
        for r in range(rep):
            h = g * rep + r
            o_ref[0, :, h * HEAD_DIM:(h + 1) * HEAD_DIM] = o[r * qb:(r + 1) * qb].astype(BF16)


def _dsa_attention(q, qi, sm, k_all, v_all, ki_all, *, qb, tk, n_keys, past):
    b, t, _ = q.shape
    nk_pad = k_all.shape[1]
    topk = min(TOPK_MAX, n_keys // 4)
    keys_on_sublanes = qb == LANES
    rep = ATT_HEADS // KV_HEADS
    tokq = lambda w: pl.BlockSpec((1, qb, w), lambda i, j: (i, j, 0))
    keys = lambda w: pl.BlockSpec((1, nk_pad, w), lambda i, j: (i, 0, 0))
    tile_shape = (tk, qb) if keys_on_sublanes else (qb, tk)
    return pl.pallas_call(
        functools.partial(_dsa_kernel, qb=qb, tk=tk, n_keys=n_keys, past=past, topk=topk,
                          n_bisect=30, keys_on_sublanes=keys_on_sublanes),
        grid=(b, t // qb),
        in_specs=[tokq(ATT_WIDTH), tokq(QI_WIDTH), tokq(LANES), keys(KV_WIDTH), keys(VEXT_WIDTH),
                  keys(IDX_DIM)],
        out_specs=tokq(ATT_WIDTH),
        out_shape=jax.ShapeDtypeStruct((b, t, ATT_WIDTH), BF16),
        scratch_shapes=[pltpu.VMEM((nk_pad // tk,) + tile_shape, F32),
                        pltpu.VMEM((nk_pad // tk, qb, tk), F32),
                        pltpu.VMEM((KV_HEADS, rep * qb, 2 * HEAD_DIM), F32),
                        pltpu.VMEM((IDX_HEADS * qb, IDX_DIM), BF16),
                        pltpu.VMEM((KV_HEADS, rep * qb, tk), F32),
                        pltpu.VMEM((KV_HEADS, rep * qb, tk), F32),
                        pltpu.VMEM((KV_HEADS, rep * qb, LANES), F32),
                        pltpu.VMEM((KV_HEADS, rep * qb, LANES), F32),
                        pltpu.VMEM((KV_HEADS, rep * qb, LANES), F32)],
        compiler_params=_params(2),
        name="dsa_attention",
    )(q, qi, sm, k_all, v_all, ki_all)


def _ssd_kernel(xbc_ref, z_ref, sm_ref, st_ref, alog_ref, dexp_ref, nw_ref, e_ref,
                y_ref, stout_ref, ht_ref, yi_ref):
    c = pl.program_id(1)
    nc = pl.num_programs(1)
    L = xbc_ref.shape[1]
    hp = SSD_INNER // SSD_GROUPS

    @pl.when(c == 0)
    def _():
        ht_ref[...] = st_ref[0].reshape(SSD_INNER, SSD_STATE).T

    xs = xbc_ref[0, :, 0:SSD_INNER]
    sm = sm_ref[0]
    lane = lax.broadcasted_iota(jnp.int32, (1, LANES), 1)
    is_dt = (lane >= SM_DT) & (lane < SM_DT + SSD_HEADS)
    a_neg = jnp.where(is_dt, -jnp.exp(alog_ref[...]), 0.0)
    dt = jnp.where(is_dt, sm, 0.0)
    rows = lax.broadcasted_iota(jnp.int32, (L, L), 0)
    cols = lax.broadcasted_iota(jnp.int32, (L, L), 1)
    tri = cols <= rows
    e = e_ref[...]
    cum = jnp.dot(tri.astype(F32), dt * a_neg, precision=HIGHEST, preferred_element_type=F32)
    eye = (lax.broadcasted_iota(jnp.int32, (LANES, LANES), 0)
           == lax.broadcasted_iota(jnp.int32, (LANES, LANES), 1)).astype(F32)
    cum_t = lax.dot_general(eye, cum, (((1,), (1,)), ((), ())), precision=HIGHEST,
                            preferred_element_type=F32)
    cum_e = jnp.dot(cum, e, precision=HIGHEST, preferred_element_type=F32)
    dt_e = jnp.dot(dt, e, precision=HIGHEST, preferred_element_type=F32)
    last = cum_e[L - 1:L, :]
    xdt = (xs * dt_e).astype(BF16)
    xw = (xs * (jnp.exp(last - cum_e) * dt_e)).astype(BF16)
    ht_old = ht_ref[...]

    for g in range(SSD_GROUPS):
        bg = xbc_ref[0, :, SSD_INNER + g * SSD_STATE:SSD_INNER + (g + 1) * SSD_STATE]
        cg = xbc_ref[0, :, SSD_INNER + (SSD_GROUPS + g) * SSD_STATE:
                     SSD_INNER + (SSD_GROUPS + g + 1) * SSD_STATE]
        bgb, cgb = bg.astype(BF16), cg.astype(BF16)
        cb = lax.dot_general(cgb, bgb, (((1,), (1,)), ((), ())), preferred_element_type=F32)
        for hh in range(SSD_HEADS // SSD_GROUPS):
            h = g * (SSD_HEADS // SSD_GROUPS) + hh
            seg = cum[:, SM_DT + h:SM_DT + h + 1] - cum_t[SM_DT + h:SM_DT + h + 1, :]
            decay = jnp.exp(jnp.where(tri, seg, -jnp.inf))
            wts = (cb * decay).astype(BF16)
            yi_ref[:, h * SSD_HEADDIM:(h + 1) * SSD_HEADDIM] = jnp.dot(
                wts, xdt[:, h * SSD_HEADDIM:(h + 1) * SSD_HEADDIM], preferred_element_type=F32)
        ht_g = ht_old[:, g * hp:(g + 1) * hp]
        y_inter = jnp.dot(cgb, ht_g.astype(BF16), preferred_element_type=F32)
        yi_ref[:, g * hp:(g + 1) * hp] = (yi_ref[:, g * hp:(g + 1) * hp]
                                          + y_inter * jnp.exp(cum_e[:, g * hp:(g + 1) * hp]))
        upd = jnp.dot(bg.T.astype(BF16), xw[:, g * hp:(g + 1) * hp],
                      preferred_element_type=F32)
        ht_ref[:, g * hp:(g + 1) * hp] = ht_g * jnp.exp(last[:, g * hp:(g + 1) * hp]) + upd

    y = yi_ref[...] + dexp_ref[...] * xs
    y = y * _silu(z_ref[0])
    for g in range(SSD_GROUPS):
        yg = y[:, g * hp:(g + 1) * hp]
        ms = jnp.mean(yg * yg, axis=-1, keepdims=True)
        y_ref[0, :, g * hp:(g + 1) * hp] = (yg * lax.rsqrt(ms + EPS)
                                            * nw_ref[:, g * hp:(g + 1) * hp]).astype(BF16)

    @pl.when(c == nc - 1)
    def _():
        stout_ref[0] = ht_ref[...].T.reshape(SSD_HEADS, SSD_HEADDIM, SSD_STATE)


def _ssd(xbc_act, proj, sm, state, a_log_pad, d_exp, norm_w, e_mat, *, chunk):
    b, t, _ = xbc_act.shape
    tok = lambda w, col=0: pl.BlockSpec((1, chunk, w), lambda i, j: (i, j, col))
    full2 = lambda a: pl.BlockSpec(a.shape, lambda i, j: (0, 0))
    st = pl.BlockSpec((1, SSD_HEADS, SSD_HEADDIM, SSD_STATE), lambda i, j: (i, 0, 0, 0))
    nw = norm_w.reshape(1, SSD_INNER)
    return pl.pallas_call(
        _ssd_kernel,
        grid=(b, t // chunk),
        in_specs=[tok(SSD_CONV_DIM), tok(SSD_INNER, OFF_Z // SSD_INNER), tok(LANES), st,
                  full2(a_log_pad), full2(d_exp), full2(nw), full2(e_mat)],
        out_specs=[tok(SSD_INNER), st],
        out_shape=[jax.ShapeDtypeStruct((b, t, SSD_INNER), BF16),
                   jax.ShapeDtypeStruct(state.shape, F32)],
        scratch_shapes=[pltpu.VMEM((SSD_STATE, SSD_INNER), F32),
                        pltpu.VMEM((chunk, SSD_INNER), F32)],
        compiler_params=_params(2),
        name="ssd_scan",
    )(xbc_act, proj, sm, state, a_log_pad, d_exp, nw, e_mat)


def _lru_kernel(gate_ref, xin_ref, cw_ref, cb_ref, wa_ref, wx_ref, ba_ref, bx_ref, lam_ref,
                h0_ref, cs_ref, y_ref, hout_ref, cnew_ref, cbuf_ref, a_ref, u_ref, hp_ref):
    t = pl.program_id(1)
    nt = pl.num_programs(1)
    tt = xin_ref.shape[1]
    xc = _causal_conv(xin_ref[0], cs_ref, cw_ref, cb_ref, cbuf_ref, cnew_ref, t == 0, t == nt - 1)

    @pl.when(t == 0)
    def _():
        hp_ref[...] = h0_ref[0]

    for kb in range(LRU_BLOCKS):
        sl = slice(kb * LRU_BLOCK_DIM, (kb + 1) * LRU_BLOCK_DIM)
        xb = xc[:, sl].astype(BF16)
        a_ref[:, sl] = jnp.dot(xb, wa_ref[kb].astype(BF16), preferred_element_type=F32)
        u_ref[:, sl] = jnp.dot(xb, wx_ref[kb].astype(BF16), preferred_element_type=F32)
    r = jax.nn.sigmoid(a_ref[...] + ba_ref[...])
    i = jax.nn.sigmoid(u_ref[...] + bx_ref[...])
    log_a = -LRU_C * r * _softplus(-lam_ref[...])
    a = jnp.exp(log_a)
    u = jnp.sqrt(-jnp.tanh(log_a) * (1.0 + a * a)) * (i * xc)

    sub = lax.broadcasted_iota(jnp.int32, a.shape, 0) & (SUBLANES - 1)
    for s in (1, 2, 4):
        m = sub >= s
        a_sh = pltpu.roll(a, s, 0)
        u_sh = pltpu.roll(u, s, 0)
        u = jnp.where(m, a * u_sh + u, u)
        a = jnp.where(m, a * a_sh, a)
    a_ref[...] = a
    u_ref[...] = u

    def group(gi, hprev):
        r0 = pl.multiple_of(gi * SUBLANES, SUBLANES)
        hs = a_ref[pl.ds(r0, SUBLANES), :] * hprev + u_ref[pl.ds(r0, SUBLANES), :]
        u_ref[pl.ds(r0, SUBLANES), :] = hs
        return hs[SUBLANES - 1:SUBLANES, :]

    h_last = lax.fori_loop(0, tt // SUBLANES, group, hp_ref[...])
    hp_ref[...] = h_last
    gate = gate_ref[0]
    gelu = 0.5 * gate * (1.0 + jnp.tanh(np.sqrt(2.0 / np.pi) * (gate + 0.044715 * (gate * gate * gate))))
    y_ref[0] = (u_ref[...] * gelu).astype(BF16)

    @pl.when(t == nt - 1)
    def _():
        hout_ref[0] = h_last


def _lru(proj, conv_w, conv_b, w_a, b_a, w_x, b_x, lam, h0, conv_state, *, tt):
    b, t, _ = proj.shape
    w = LRU_WIDTH
    tok = lambda col: pl.BlockSpec((1, tt, w), lambda i, j: (i, j, col))
    full = lambda a: pl.BlockSpec(a.shape, lambda i, j: (0,) * a.ndim)
    row = lambda v: v.reshape(1, w)
    state = pl.BlockSpec((1, CONV_W - 1, w), lambda i, j: (i, 0, 0))
    hspec = pl.BlockSpec((1, 1, w), lambda i, j: (i, 0, 0))
    args = (proj, proj, conv_w, row(conv_b), w_a, w_x, row(b_a), row(b_x), row(lam), h0, conv_state)
    return pl.pallas_call(
        _lru_kernel,
        grid=(b, t // tt),
        in_specs=[tok(0), tok(1)] + [full(a) for a in args[2:9]] + [hspec, state],
        out_specs=[pl.BlockSpec((1, tt, w), lambda i, j: (i, j, 0)), hspec, state],
        out_shape=[jax.ShapeDtypeStruct((b, t, w), BF16),
                   jax.ShapeDtypeStruct((b, 1, w), F32),
                   jax.ShapeDtypeStruct((b, CONV_W - 1, w), F32)],
        scratch_shapes=[pltpu.VMEM((tt + SUBLANES, w), F32), pltpu.VMEM((tt, w), F32),
                        pltpu.VMEM((tt, w), F32), pltpu.VMEM((1, w), F32)],
        compiler_params=_params(2),
        name="rg_lru",
    )(*args)


def _rope_tables(pos, dim):
    half = dim // 2
    inv = np.power(ROPE_THETA, -np.arange(half, dtype=np.float64) / half)
    ang = pos.astype(np.float64)[:, None] * inv[None, :]
    cos = np.concatenate([np.cos(ang), np.cos(ang)], axis=-1)
    sin = np.concatenate([-np.sin(ang), np.sin(ang)], axis=-1)
    reps = LANES // dim
    return (jnp.asarray(np.tile(cos, (1, reps)), F32), jnp.asarray(np.tile(sin, (1, reps)), F32))


def _head_expansion():
    e = np.zeros((LANES, SSD_INNER), np.float32)
    for h in range(SSD_HEADS):
        e[SM_DT + h, h * SSD_HEADDIM:(h + 1) * SSD_HEADDIM] = 1.0
    return jnp.asarray(e)


def _pad_lanes(v, offset):
    out = jnp.zeros((1, LANES), F32)
    return out.at[0, offset:offset + v.shape[0]].set(v)


def _reorder_w_in_ab(w):
    d = w.shape[0]
    q, k, v, qi, ki, wi, z, xbc, dt = (1024, 256, 256, 1024, 64, 16, 1024, 1536, 16)
    o = np.cumsum([0, q, k, v, qi, ki, wi, z, xbc, dt])
    seg = lambda i: w[:, o[i]:o[i + 1]]
    parts = [seg(0), seg(3), seg(6), seg(7), seg(1), seg(2), seg(4), seg(5), seg(8)]
    used = sum(p.shape[1] for p in parts)
    parts.append(jnp.zeros((d, AB_PAD_N - used), w.dtype))
    return jnp.concatenate(parts, axis=1)


def _trunk(x, mods, past_k, past_v, past_ki, ssm_h, ssm_conv, lru_h, lru_conv, p, *, cfg):
    b, t, d = x.shape
    bb, tt = cfg["bb"], cfg["tt"]
    past = 0 if past_k is None else past_k.shape[1]
    n_keys = past + t
    pos = np.arange(past, past + t)
    tabs = _rope_tables(pos, HEAD_DIM) + _rope_tables(pos, IDX_DIM)

    def ffn(x, l):
        sh_f, sc_f, g_f = mods[l][3], mods[l][4], mods[l][5]
        gu = _norm_mod_matmul(x, p["norm_ffn_w"][l], sc_f, sh_f, [p["ffn_w_gate"], p["ffn_w_up"]], l,
                              bb=bb, tt=tt, tn=512, out_dtype=BF16, name="ffn_gate_up")
        return _matmul_residual([gu], p["ffn_w_down"], l, x, g_f, bb=bb, tt=tt, tn=256,
                                name="ffn_down")

    sh_m, sc_m, g_m = mods[0][0], mods[0][1], mods[0][2]
    proj = _norm_mod_matmul(x, p["norm_mix_w"][0], sc_m, sh_m, [p["w_in_ab_r"][None]], 0,
                            bb=bb, tt=tt, tn=AB_TILE_N, out_dtype=F32, name="in_proj_ab")
    (q, qi, k, v, ki, sm, xbc_act, ssm_conv_new, kb, vbe, kib) = _post0(
        proj, tabs, p["q_norm_w"][0], p["k_norm_w"][0], p["ssd_conv_w"][0], p["ssd_conv_b"][0],
        p["dt_bias_pad"], ssm_conv, tt=cfg["post_tt"])
    if past:
        nk_pad = -(-n_keys // cfg["tk"]) * cfg["tk"]
        cat = lambda old, new: jnp.pad(jnp.concatenate([old, new], axis=1),
                                       ((0, 0), (0, nk_pad - n_keys), (0, 0)))
        pv = past_v.astype(BF16)
        pv_ext = jnp.concatenate([pv, jnp.ones_like(pv)], axis=-1).reshape(b, past, VEXT_WIDTH)
        k_all = cat(past_k.reshape(b, past, KV_WIDTH).astype(BF16), kb)
        v_all = cat(pv_ext, vbe)
        ki_all = cat(past_ki.astype(BF16), kib)
    else:
        k_all, v_all, ki_all = kb, vbe, kib
    att = _dsa_attention(q, qi, sm, k_all, v_all, ki_all, qb=cfg["qb"], tk=cfg["tk"],
                         n_keys=n_keys, past=past)
    y_ssd, ssm_new = _ssd(xbc_act, proj, sm, ssm_h, p["a_log_pad"], p["d_exp"], p["ssd_norm_w"][0],
                          p["e_mat"], chunk=cfg["ssd_chunk"])
    x = _matmul_residual([att, y_ssd], p["w_out_ab"], 0, x, g_m, bb=bb, tt=tt, tn=512,
                         name="out_proj_ab")
    x = ffn(x, 0)

    sh_m, sc_m, g_m = mods[1][0], mods[1][1], mods[1][2]
    proj = _norm_mod_matmul(x, p["norm_mix_w"][1], sc_m, sh_m, [p["w_in_c"]], 0,
                            bb=bb, tt=tt, tn=512, out_dtype=F32, name="in_proj_c")
    y_lru, lru_new, lru_conv_new = _lru(
        proj, p["lru_conv_w"][0], p["lru_conv_b"][0], p["lru_w_a"][0], p["lru_b_a"][0],
        p["lru_w_x"][0], p["lru_b_x"][0], p["lru_lambda"][0], lru_h.reshape(b, 1, LRU_WIDTH),
        lru_conv, tt=cfg["lru_tt"])
    x = _matmul_residual([y_lru], p["w_out_c"], 0, x, g_m, bb=bb, tt=tt, tn=512, name="out_proj_c")
    x = ffn(x, 1)

    kv_shape = (1, b, t, KV_HEADS, HEAD_DIM)
    return (x, k.reshape(kv_shape), v.reshape(kv_shape), ki[None], ssm_new[None], ssm_conv_new[None],
            lru_new.reshape(1, b, LRU_WIDTH), lru_conv_new[None])


PROMPT_CFG = dict(bb=1, tt=1024, post_tt=256, qb=128, tk=512, ssd_chunk=128, lru_tt=256)
SAMPLE_CFG = dict(bb=16, tt=32, post_tt=32, qb=32, tk=384, ssd_chunk=32, lru_tt=32)


def kernel(x_prompt, x_sample, cache_attn_k, cache_attn_v, cache_idx_k, state_ssm, state_ssm_conv,
           state_lru, state_lru_conv, c_prompt, c_sample, ada_w, ada_b, norm_mix_w, norm_ffn_w,
           w_in_ab, q_norm_w, k_norm_w, ssd_conv_w, ssd_conv_b, ssd_dt_bias, ssd_a_log, ssd_d,
           ssd_norm_w, w_out_ab, w_in_c, lru_conv_w, lru_conv_b, lru_w_a, lru_b_a, lru_w_x, lru_b_x,
           lru_lambda, w_out_c, ffn_w_gate, ffn_w_up, ffn_w_down):
    bp, bs = x_prompt.shape[0], x_sample.shape[0]
    p = dict(norm_mix_w=norm_mix_w, norm_ffn_w=norm_ffn_w, q_norm_w=q_norm_w, k_norm_w=k_norm_w,
             ssd_conv_w=ssd_conv_w, ssd_conv_b=ssd_conv_b, ssd_norm_w=ssd_norm_w, w_out_ab=w_out_ab,
             w_in_c=w_in_c, lru_conv_w=lru_conv_w, lru_conv_b=lru_conv_b, lru_w_a=lru_w_a,
             lru_b_a=lru_b_a, lru_w_x=lru_w_x, lru_b_x=lru_b_x, lru_lambda=lru_lambda,
             w_out_c=w_out_c, ffn_w_gate=ffn_w_gate, ffn_w_up=ffn_w_up, ffn_w_down=ffn_w_down)
    p["w_in_ab_r"] = _reorder_w_in_ab(w_in_ab[0])
    p["dt_bias_pad"] = _pad_lanes(ssd_dt_bias[0], SM_DT)
    p["a_log_pad"] = _pad_lanes(ssd_a_log[0], SM_DT)
    p["d_exp"] = jnp.repeat(ssd_d[0], SSD_HEADDIM).reshape(1, SSD_INNER)
    p["e_mat"] = _head_expansion()

    m_rows = 32
    c_all = jnp.concatenate([c_prompt, c_sample, jnp.zeros((m_rows - bp - bs, D_MODEL), F32)], axis=0)
    mod = _modulation(c_all, ada_w, ada_b)

    def group_mods(r0, nb):
        return [[mod[l, r0:r0 + nb, i * D_MODEL:(i + 1) * D_MODEL].reshape(nb, 1, D_MODEL)
                 for i in range(6)] for l in range(mod.shape[0])]

    zeros = lambda *s: jnp.zeros(s, F32)
    out_p = _trunk(x_prompt, group_mods(0, bp), None, None, None,
                   zeros(bp, SSD_HEADS, SSD_HEADDIM, SSD_STATE), zeros(bp, CONV_W - 1, SSD_CONV_DIM),
                   zeros(bp, LRU_WIDTH), zeros(bp, CONV_W - 1, LRU_WIDTH), p, cfg=PROMPT_CFG)
    out_s = _trunk(x_sample, group_mods(bp, bs), cache_attn_k[0], cache_attn_v[0], cache_idx_k[0],
                   state_ssm[0], state_ssm_conv[0], state_lru[0], state_lru_conv[0], p, cfg=SAMPLE_CFG)
    return (out_p[0], out_s[0]) + out_p[1:] + out_s[1:]
```

```python
import functools

import numpy as np
import jax
import jax.numpy as jnp
from jax import lax
from jax.experimental import pallas as pl
from jax.experimental.pallas import tpu as pltpu

F32 = jnp.float32
BF16 = jnp.bfloat16
HIGHEST = lax.Precision.HIGHEST

D_MODEL = 2048
CHUNK = 64
CHUNK_SHIFT = 6
ATT_HEADS = 8
KV_HEADS = 2
HEAD_DIM = 128
IDX_HEADS = 16
IDX_DIM = 64
TOPK_MAX = 256
ROPE_THETA = 10000.0
SSD_HEADS = 16
SSD_HEADDIM = 64
SSD_GROUPS = 2
SSD_STATE = 128
SSD_INNER = SSD_HEADS * SSD_HEADDIM
SSD_CONV_DIM = SSD_INNER + 2 * SSD_GROUPS * SSD_STATE
CONV_W = 4
LRU_WIDTH = D_MODEL
LRU_BLOCKS = 16
LRU_BLOCK_DIM = LRU_WIDTH // LRU_BLOCKS
LRU_C = 8.0
EPS = 1e-6
ATT_WIDTH = ATT_HEADS * HEAD_DIM
KV_WIDTH = KV_HEADS * HEAD_DIM
VEXT_WIDTH = 2 * KV_WIDTH
QI_WIDTH = IDX_HEADS * IDX_DIM
Q_SCALE = HEAD_DIM ** -0.5 * float(np.log2(np.e))

LANES = 128
SUBLANES = 8
VMEM_LIMIT_BYTES = 56 * 1024 * 1024
SINGLE_BUFFER_ROWS = 2048

OFF_Q = 0
OFF_QI = OFF_Q + ATT_WIDTH
OFF_Z = OFF_QI + QI_WIDTH
OFF_XBC = OFF_Z + SSD_INNER
OFF_K = OFF_XBC + SSD_CONV_DIM
OFF_V = OFF_K + KV_WIDTH
OFF_SMALL = OFF_V + KV_WIDTH
SM_WI = IDX_DIM
SM_DT = IDX_DIM + IDX_HEADS
AB_TILE_N = 768
AB_PAD_N = 5376

NEG_BIG = -1e30


def _params(n_axes):
    return pltpu.CompilerParams(dimension_semantics=("arbitrary",) * n_axes,
                                vmem_limit_bytes=VMEM_LIMIT_BYTES)


def _silu(x):
    return x * jax.nn.sigmoid(x)


def _softplus(x):
    return jnp.maximum(x, 0.0) + jnp.log1p(jnp.exp(-jnp.abs(x)))


def _mod_kernel(c_ref, w_ref, b_ref, o_ref):
    a = _silu(c_ref[...]).astype(BF16)
    o_ref[0] = jnp.dot(a, w_ref[0].astype(BF16), preferred_element_type=F32) + b_ref[0]


def _modulation(c_all, ada_w, ada_b):
    depth, d, n = ada_w.shape
    m = c_all.shape[0]
    tn = 1024
    return pl.pallas_call(
        _mod_kernel,
        grid=(depth, n // tn),
        in_specs=[pl.BlockSpec((m, d), lambda l, j: (0, 0)),
                  pl.BlockSpec((1, d, tn), lambda l, j: (l, 0, j)),
                  pl.BlockSpec((1, 1, tn), lambda l, j: (l, 0, j))],
        out_specs=pl.BlockSpec((1, m, tn), lambda l, j: (l, 0, j)),
        out_shape=jax.ShapeDtypeStruct((depth, m, n), F32),
        compiler_params=_params(2),
        name="adaln_mod",
    )(c_all, ada_w, ada_b.reshape(depth, 1, n))


def _nmm_kernel(x_ref, nw_ref, sc_ref, sh_ref, *rest, swiglu):
    n_w = 2 if swiglu else 1
    w_refs, o_ref, h_ref = rest[:n_w], rest[n_w], rest[n_w + 1]
    bb, tt, d = x_ref.shape

    @pl.when(pl.program_id(2) == 0)
    def _():
        x = x_ref[...]
        ms = jnp.mean(x * x, axis=-1, keepdims=True)
        y = x * lax.rsqrt(ms + EPS) * nw_ref[...]
        h = y * (1.0 + sc_ref[...]) + sh_ref[...]
        h_ref[...] = h.reshape(bb * tt, d).astype(BF16)

    h = h_ref[...]
    if swiglu:
        g = jnp.dot(h, w_refs[0][...].astype(BF16), preferred_element_type=F32)
        u = jnp.dot(h, w_refs[1][...].astype(BF16), preferred_element_type=F32)
        o = _silu(g) * u
    else:
        o = jnp.dot(h, w_refs[0][...].astype(BF16), preferred_element_type=F32)
    o_ref[...] = o.reshape(o_ref.shape).astype(o_ref.dtype)


def _row_block_mode(rows):
    return dict(pipeline_mode=pl.Buffered(1)) if rows >= SINGLE_BUFFER_ROWS else {}


def _norm_mod_matmul(x, nw, sc, sh, ws, layer, *, bb, tt, tn, out_dtype, name):
    b, t, d = x.shape
    n = ws[0].shape[2]
    swiglu = len(ws) == 2
    xmap = lambda i, j, k: (i, j, 0)
    mmap = lambda i, j, k: (i, 0, 0)
    return pl.pallas_call(
        functools.partial(_nmm_kernel, swiglu=swiglu),
        grid=(b // bb, t // tt, n // tn),
        in_specs=[pl.BlockSpec((bb, tt, d), xmap, **_row_block_mode(bb * tt)),
                  pl.BlockSpec((1, 1, d), lambda i, j, k: (0, 0, 0)),
                  pl.BlockSpec((bb, 1, d), mmap),
                  pl.BlockSpec((bb, 1, d), mmap)]
                 + [pl.BlockSpec((None, d, tn), lambda i, j, k: (layer, 0, k))] * len(ws),
        out_specs=pl.BlockSpec((bb, tt, tn), lambda i, j, k: (i, j, k)),
        out_shape=jax.ShapeDtypeStruct((b, t, n), out_dtype),
        scratch_shapes=[pltpu.VMEM((bb * tt, d), BF16)],
        compiler_params=_params(3),
        name=name,
    )(x, nw.reshape(1, 1, d), sc, sh, *ws)


def _mmr_kernel(*refs, k_sizes):
    n_a = len(k_sizes)
    a_refs = refs[:n_a]
    w_ref, x_ref, g_ref, o_ref = refs[n_a:]
    bb, tt, tn = o_ref.shape
    acc = jnp.zeros((bb * tt, tn), F32)
    off = 0
    for a_ref, ks in zip(a_refs, k_sizes):
        a = a_ref[...].reshape(bb * tt, ks)
        acc = acc + jnp.dot(a, w_ref[off:off + ks, :].astype(BF16), preferred_element_type=F32)
        off += ks
    o_ref[...] = x_ref[...] + g_ref[...] * acc.reshape(bb, tt, tn)


def _matmul_residual(a_list, w, layer, x, g, *, bb, tt, tn, name):
    b, t, d = x.shape
    k_sizes = tuple(a.shape[-1] for a in a_list)
    k_total = sum(k_sizes)
    return pl.pallas_call(
        functools.partial(_mmr_kernel, k_sizes=k_sizes),
        grid=(b // bb, t // tt, d // tn),
        in_specs=[pl.BlockSpec((bb, tt, ks), lambda i, j, k: (i, j, 0), **_row_block_mode(bb * tt))
                  for ks in k_sizes]
                 + [pl.BlockSpec((None, k_total, tn), lambda i, j, k: (layer, 0, k)),
                    pl.BlockSpec((bb, tt, tn), lambda i, j, k: (i, j, k)),
                    pl.BlockSpec((bb, 1, tn), lambda i, j, k: (i, 0, k))],
        out_specs=pl.BlockSpec((bb, tt, tn), lambda i, j, k: (i, j, k)),
        out_shape=jax.ShapeDtypeStruct((b, t, d), F32),
        compiler_params=_params(3),
        name=name,
    )(*a_list, w, x, g)


def _swap_halves(x, half):
    w = x.shape[-1]
    lane = lax.broadcasted_iota(jnp.int32, x.shape, x.ndim - 1)
    first = (lane & half) == 0
    return jnp.where(first, pltpu.roll(x, w - half, x.ndim - 1), pltpu.roll(x, half, x.ndim - 1))


def _rope(x, cos, sin_signed, half):
    reps = x.shape[-1] // cos.shape[-1]
    if reps > 1:
        cos = jnp.concatenate([cos] * reps, axis=-1)
        sin_signed = jnp.concatenate([sin_signed] * reps, axis=-1)
    return x * cos + _swap_halves(x, half) * sin_signed


def _causal_conv(u, state_ref, w_ref, b_ref, cbuf_ref, new_ref, first, last):
    tt = u.shape[0]

    @pl.when(first)
    def _():
        cbuf_ref[0:SUBLANES, :] = jnp.zeros((SUBLANES, u.shape[1]), F32)
        cbuf_ref[SUBLANES - (CONV_W - 1):SUBLANES, :] = state_ref[0]

    cbuf_ref[SUBLANES:SUBLANES + tt, :] = u
    full = cbuf_ref[...]
    out = b_ref[...]
    for j in range(CONV_W):
        shift = CONV_W - 1 - j
        rows = pltpu.roll(full, shift, 0) if shift else full
        out = out + rows[SUBLANES:SUBLANES + tt, :] * w_ref[j:j + 1, :]
    tail = cbuf_ref[tt:tt + SUBLANES, :]
    cbuf_ref[0:SUBLANES, :] = tail

    @pl.when(last)
    def _():
        new_ref[0] = tail[SUBLANES - (CONV_W - 1):, :]

    return out


def _post0_kernel(p_ref, cq_ref, sq_ref, ci_ref, si_ref, qn_ref, kn_ref, cw_ref, cb_ref, dtb_ref,
                  cs_ref,
                  q_ref, qi_ref, k_ref, v_ref, ki_ref, sm_ref, xbc_ref, cnew_ref, kb_ref, vb_ref,
                  kib_ref, cbuf_ref):
    t = pl.program_id(1)
    nt = pl.num_programs(1)
    cq, sq, ci, si = cq_ref[...], sq_ref[...], ci_ref[...], si_ref[...]

    def head_norm_rope(x, w):
        ms = jnp.mean(x * x, axis=-1, keepdims=True)
        return _rope(x * lax.rsqrt(ms + EPS) * w, cq, sq, HEAD_DIM // 2)

    for h in range(ATT_HEADS):
        xh = p_ref[0, :, OFF_Q + h * HEAD_DIM:OFF_Q + (h + 1) * HEAD_DIM]
        q_ref[0, :, h * HEAD_DIM:(h + 1) * HEAD_DIM] = (head_norm_rope(xh, qn_ref[...])
                                                         * Q_SCALE).astype(BF16)
    ones = jnp.ones((p_ref.shape[1], HEAD_DIM), BF16)
    for h in range(KV_HEADS):
        xh = p_ref[0, :, OFF_K + h * HEAD_DIM:OFF_K + (h + 1) * HEAD_DIM]
        kh = head_norm_rope(xh, kn_ref[...])
        k_ref[0, :, h * HEAD_DIM:(h + 1) * HEAD_DIM] = kh
        kb_ref[0, :, h * HEAD_DIM:(h + 1) * HEAD_DIM] = kh.astype(BF16)
        vh = p_ref[0, :, OFF_V + h * HEAD_DIM:OFF_V + (h + 1) * HEAD_DIM]
        v_ref[0, :, h * HEAD_DIM:(h + 1) * HEAD_DIM] = vh
        vb_ref[0, :, 2 * h * HEAD_DIM:(2 * h + 1) * HEAD_DIM] = vh.astype(BF16)
        vb_ref[0, :, (2 * h + 1) * HEAD_DIM:(2 * h + 2) * HEAD_DIM] = ones

    qi = p_ref[0, :, OFF_QI:OFF_QI + QI_WIDTH]
    qi_ref[0] = _rope(qi, ci, si, IDX_DIM // 2).astype(BF16)

    sm = p_ref[0, :, OFF_SMALL:OFF_SMALL + LANES]
    lane = lax.broadcasted_iota(jnp.int32, sm.shape, 1)
    ki_part = _rope(sm, ci, si, IDX_DIM // 2)
    wi_part = sm * (IDX_HEADS ** -0.5 * IDX_DIM ** -0.5)
    dt_part = _softplus(sm + dtb_ref[...])
    sm_out = jnp.where(lane < SM_WI, ki_part,
                       jnp.where(lane < SM_DT, wi_part,
                                 jnp.where(lane < SM_DT + SSD_HEADS, dt_part, 0.0)))
    sm_ref[0] = sm_out
    ki_ref[0] = sm_out[:, :IDX_DIM]
    kib_ref[0] = sm_out[:, :IDX_DIM].astype(BF16)

    xbc = p_ref[0, :, OFF_XBC:OFF_XBC + SSD_CONV_DIM]
    conv = _causal_conv(xbc, cs_ref, cw_ref, cb_ref, cbuf_ref, cnew_ref, t == 0, t == nt - 1)
    xbc_ref[0] = _silu(conv)


def _post0(proj, tabs, q_norm_w, k_norm_w, conv_w, conv_b, dt_bias_pad, conv_state, *, tt):
    b, t, n = proj.shape
    tok = lambda w: pl.BlockSpec((1, tt, w), lambda i, j: (i, j, 0))
    tab = pl.BlockSpec((tt, LANES), lambda i, j: (j, 0))
    full2 = lambda a: pl.BlockSpec(a.shape, lambda i, j: (0, 0))
    state = lambda c: pl.BlockSpec((1, CONV_W - 1, c), lambda i, j: (i, 0, 0))
    out_shapes = [
        jax.ShapeDtypeStruct((b, t, ATT_WIDTH), BF16),
        jax.ShapeDtypeStruct((b, t, QI_WIDTH), BF16),
        jax.ShapeDtypeStruct((b, t, KV_WIDTH), F32),
        jax.ShapeDtypeStruct((b, t, KV_WIDTH), F32),
        jax.ShapeDtypeStruct((b, t, IDX_DIM), F32),
        jax.ShapeDtypeStruct((b, t, LANES), F32),
        jax.ShapeDtypeStruct((b, t, SSD_CONV_DIM), F32),
        jax.ShapeDtypeStruct((b, CONV_W - 1, SSD_CONV_DIM), F32),
        jax.ShapeDtypeStruct((b, t, KV_WIDTH), BF16),
        jax.ShapeDtypeStruct((b, t, VEXT_WIDTH), BF16),
        jax.ShapeDtypeStruct((b, t, IDX_DIM), BF16),
    ]
    out_specs = [tok(ATT_WIDTH), tok(QI_WIDTH), tok(KV_WIDTH), tok(KV_WIDTH), tok(IDX_DIM),
                 tok(LANES), tok(SSD_CONV_DIM), state(SSD_CONV_DIM), tok(KV_WIDTH), tok(VEXT_WIDTH),
                 tok(IDX_DIM)]
    qn = q_norm_w.reshape(1, HEAD_DIM)
    kn = k_norm_w.reshape(1, HEAD_DIM)
    cb = conv_b.reshape(1, SSD_CONV_DIM)
    return pl.pallas_call(
        _post0_kernel,
        grid=(b, t // tt),
        in_specs=[tok(n), tab, tab, tab, tab, full2(qn), full2(kn), full2(conv_w), full2(cb),
                  full2(dt_bias_pad), state(SSD_CONV_DIM)],
        out_specs=out_specs,
        out_shape=out_shapes,
        scratch_shapes=[pltpu.VMEM((tt + SUBLANES, SSD_CONV_DIM), F32)],
        compiler_params=_params(2),
        name="post_proj0",
    )(proj, *tabs, qn, kn, conv_w, cb, dt_bias_pad, conv_state)


def _dsa_kernel(q_ref, qi_ref, sm_ref, k_ref, v_ref, ki_ref, o_ref, sc_ref, bias_ref, acc_ref, qs_ref,
                s0_ref, s1_ref, mt0_ref, mt1_ref, m_ref, *, qb, tk, n_keys, past, topk, n_bisect,
                keys_on_sublanes):
    kax = 0 if keys_on_sublanes else 1
    tile_shape = (tk, qb) if keys_on_sublanes else (qb, tk)
    vec_shape = (1, qb) if keys_on_sublanes else (qb, 1)
    n_acc = 8
    part_shape = (n_acc, SUBLANES, qb) if keys_on_sublanes else (qb, LANES)
    j = pl.program_id(1)
    pos0 = past + j * qb
    q_chunk = lax.shift_right_logical(pos0 + lax.broadcasted_iota(jnp.int32, vec_shape, 1 - kax),
                                      CHUNK_SHIFT)
    n_valid = jnp.minimum((q_chunk + 1) * CHUNK, n_keys)
    max_valid = jnp.minimum(((pos0 + qb - 1) // CHUNK + 1) * CHUNK, n_keys)
    nkt = (max_valid + tk - 1) // tk
    kf = float(topk)
    active_f = jnp.where(n_valid > topk, 1.0, 0.0)

    def key_index(kt):
        return kt * tk + lax.broadcasted_iota(jnp.int32, tile_shape, kax)

    nt_dims = (((1,), (1,)), ((), ()))
    if keys_on_sublanes:
        for h in range(IDX_HEADS):
            qs_ref[h * qb:(h + 1) * qb, :] = qi_ref[0, :, h * IDX_DIM:(h + 1) * IDX_DIM]
        wi_t = sm_ref[0].T
        head_w = lambda h: wi_t[SM_WI + h:SM_WI + h + 1, :]
    else:
        qi = qi_ref[0]
        wis = sm_ref[0][:, SM_WI:SM_WI + IDX_HEADS]
        head_w = lambda h: wis[:, h:h + 1]

    def score_tile(kt, carry):
        kit = ki_ref[0, pl.ds(pl.multiple_of(kt * tk, tk), tk), :]
        acc = jnp.zeros(tile_shape, F32)
        if keys_on_sublanes:
            for pr in range(IDX_HEADS // 2):
                s2 = lax.dot_general(kit, qs_ref[2 * pr * qb:(2 * pr + 2) * qb, :], nt_dims,
                                     preferred_element_type=F32)
                for e in range(2):
                    acc = acc + head_w(2 * pr + e) * jnp.maximum(s2[:, e * qb:(e + 1) * qb], 0.0)
        else:
            for h in range(IDX_HEADS):
                s = lax.dot_general(qi[:, h * IDX_DIM:(h + 1) * IDX_DIM], kit, nt_dims,
                                    preferred_element_type=F32)
                acc = acc + head_w(h) * jnp.maximum(s, 0.0)
        sc_ref[kt] = jnp.where(key_index(kt) < n_valid, acc, -jnp.inf)
        return carry

    lax.fori_loop(0, nkt, score_tile, 0)

    def fold(m, op2, red):
        if keys_on_sublanes:
            return red(m.reshape(tk // (n_acc * SUBLANES), n_acc, SUBLANES, qb), axis=0)
        f = m[:, 0:LANES]
        for c in range(1, tk // LANES):
            f = op2(f, m[:, c * LANES:(c + 1) * LANES])
        return f

    def reduce_tiles(make, op2, red, init):
        def body(kt, part):
            return op2(part, fold(make(sc_ref[kt], kt), op2, red))
        part = lax.fori_loop(0, nkt, body, jnp.full(part_shape, init, F32))
        if keys_on_sublanes:
            part = red(part, axis=0)
        return red(part, axis=kax, keepdims=True)

    def count(pred):
        return reduce_tiles(lambda t, kt: jnp.where(pred(t, kt), 1.0, 0.0), jnp.add, jnp.sum, 0.0)

    def row_max(pred):
        return reduce_tiles(lambda t, kt: jnp.where(pred(t, kt), t, -jnp.inf), jnp.maximum, jnp.max,
                            -jnp.inf)

    def row_min_valid():
        return reduce_tiles(lambda t, kt: jnp.where(key_index(kt) < n_valid, t, jnp.inf), jnp.minimum,
                            jnp.min, jnp.inf)

    def any_set(flag_f):
        return jnp.max(flag_f) > 0.0

    def bisect(_, state):
        lo, hi, c_lo = state
        mid = lo + (hi - lo) * 0.5
        c = count(lambda t, kt: t >= mid)
        ge = c >= kf
        return jnp.where(ge, mid, lo), jnp.where(ge, hi, mid), jnp.where(ge, c, c_lo)

    lo0 = row_min_valid()
    hi0 = row_max(lambda t, kt: t == t)
    lo, hi, c_lo = lax.fori_loop(0, n_bisect, bisect, (lo0, hi0, n_valid.astype(F32)))
    found_f = jnp.where(c_lo == kf, active_f, 0.0)
    v_found = reduce_tiles(lambda t, kt: jnp.where(t >= lo, t, jnp.inf), jnp.minimum, jnp.min, jnp.inf)

    def walk_down():
        w0 = row_max(lambda t, kt: t <= hi)
        c0 = count(lambda t, kt: t >= w0)

        def walk_cond(state):
            w, c = state
            return any_set(jnp.where(c < kf, active_f, 0.0))

        def walk_body(state):
            w, c = state
            w2 = row_max(lambda t, kt: t < w)
            c2 = count(lambda t, kt: t >= w2)
            upd = c < kf
            return jnp.where(upd, w2, w), jnp.where(upd, c2, c)

        return lax.while_loop(walk_cond, walk_body,
                              (jnp.where(found_f > 0.5, v_found, w0),
                               jnp.where(found_f > 0.5, kf, c0)))

    w, c_ge = lax.cond(any_set(active_f - found_f), walk_down,
                       lambda: (v_found, jnp.full(vec_shape, kf, F32)))
    thr = jnp.where(active_f > 0.5, w, -jnp.inf)
    tied_f = jnp.where(c_ge > kf, active_f, 0.0)

    def write_bias(sel_fn):
        def body(kt, carry):
            kidx = key_index(kt)
            sel = sel_fn(sc_ref[kt], kidx) & (kidx < n_valid)
            bias = jnp.where(sel, 0.0, NEG_BIG)
            bias_ref[kt] = bias.T if keys_on_sublanes else bias
            return carry
        lax.fori_loop(0, nkt, body, 0)

    def no_ties():
        write_bias(lambda t, kidx: t >= thr)

    def with_ties():
        need = kf - count(lambda t, kt: t > thr)
        n_steps = max(1, int(np.ceil(np.log2(sc_ref.shape[0] * tk))) + 1)

        def step(_, lh):
            lo_i, hi_i = lh
            mid = lax.shift_right_arithmetic(lo_i + hi_i, 1)
            ge = count(lambda t, kt: (t == thr) & (key_index(kt) <= mid)) >= need
            return jnp.where(ge, lo_i, mid), jnp.where(ge, mid, hi_i)

        lo_i = jnp.full(vec_shape, -1, jnp.int32)
        hi_i = jnp.full(vec_shape, sc_ref.shape[0] * tk - 1, jnp.int32)
        _, last_tie = lax.fori_loop(0, n_steps, step, (lo_i, hi_i))
        write_bias(lambda t, kidx: (t > thr) | ((t == thr)
                                                & ((kidx <= last_tie) | (tied_f < 0.5))))

    any_tie = jnp.max(tied_f) > 0.0
    lax.cond(any_tie, with_ties, no_ties)

    rep = ATT_HEADS // KV_HEADS
    qgs = [jnp.concatenate(
        [q_ref[0, :, (g * rep + r) * HEAD_DIM:(g * rep + r + 1) * HEAD_DIM] for r in range(rep)],
        axis=0) for g in range(KV_HEADS)]
    acc_ref[...] = jnp.zeros(acc_ref.shape, F32)
    bufs = ((s0_ref, mt0_ref), (s1_ref, mt1_ref))

    def qk_tile(kt, buf):
        s_ref, mt_ref = bufs[buf]
        row0 = pl.multiple_of(kt * tk, tk)
        bias = bias_ref[kt]
        for g in range(KV_HEADS):
            kg = k_ref[0, pl.ds(row0, tk), g * HEAD_DIM:(g + 1) * HEAD_DIM]
            s = lax.dot_general(qgs[g], kg, nt_dims, preferred_element_type=F32)
            s = (s.reshape(rep, qb, tk) + bias[None]).reshape(rep * qb, tk)
            s_ref[g] = s
            mt_ref[g] = jnp.broadcast_to(jnp.max(s, axis=-1, keepdims=True), (rep * qb, LANES))

    def lane_tile(x, width):
        return jnp.concatenate([x] * (width // LANES), axis=1)

    def att_tile(kt, buf, prefetch):
        if prefetch:
            qk_tile(kt + 1, 1 - buf)
        s_ref, mt_ref = bufs[buf]
        row0 = pl.multiple_of(kt * tk, tk)
        for g in range(KV_HEADS):
            vg = v_ref[0, pl.ds(row0, tk), 2 * g * HEAD_DIM:(2 * g + 2) * HEAD_DIM]
            m_old = m_ref[g]
            m_new = jnp.maximum(m_old, mt_ref[g])
            m_ref[g] = m_new
            alpha = jnp.exp2(m_old - m_new)
            p = jnp.exp2(s_ref[g] - lane_tile(m_new, tk)).astype(BF16)
            acc_ref[g] = (lane_tile(alpha, 2 * HEAD_DIM) * acc_ref[g]
                          + jnp.dot(p, vg, preferred_element_type=F32))

    def tile_pair(pi, carry):
        att_tile(2 * pi, 0, True)
        att_tile(2 * pi + 1, 1, True)
        return carry

    def last_two():
        att_tile(nkt - 2, 0, True)
        att_tile(nkt - 1, 1, False)

    def last_one():
        att_tile(nkt - 1, 0, False)

    m_ref[...] = jnp.full(m_ref.shape, NEG_BIG, F32)
    qk_tile(0, 0)
    lax.fori_loop(0, (nkt - 1) // 2, tile_pair, 0)
    lax.cond((nkt & 1) == 0, last_two, last_one)
    for g in range(KV_HEADS):
        acc = acc_ref[g]
        o = acc[:, :HEAD_DIM] / acc[:, HEAD_DIM:]
        for r in range(rep):
            h = g * rep + r
            o_ref[0, :, h * HEAD_DIM:(h + 1) * HEAD_DIM] = o[r * qb:(r + 1) * qb].astype(BF16)


def _dsa_attention(q, qi, sm, k_all, v_all, ki_all, *, qb, tk, n_keys, past):
    b, t, _ = q.shape
    nk_pad = k_all.shape[1]
    topk = min(TOPK_MAX, n_keys // 4)
    keys_on_sublanes = qb == LANES
    rep = ATT_HEADS // KV_HEADS
    tokq = lambda w: pl.BlockSpec((1, qb, w), lambda i, j: (i, j, 0))
    keys = lambda w: pl.BlockSpec((1, nk_pad, w), lambda i, j: (i, 0, 0))
    tile_shape = (tk, qb) if keys_on_sublanes else (qb, tk)
    return pl.pallas_call(
        functools.partial(_dsa_kernel, qb=qb, tk=tk, n_keys=n_keys, past=past, topk=topk,
                          n_bisect=20, keys_on_sublanes=keys_on_sublanes),
        grid=(b, t // qb),
        in_specs=[tokq(ATT_WIDTH), tokq(QI_WIDTH), tokq(LANES), keys(KV_WIDTH), keys(VEXT_WIDTH),
                  keys(IDX_DIM)],
        out_specs=tokq(ATT_WIDTH),
        out_shape=jax.ShapeDtypeStruct((b, t, ATT_WIDTH), BF16),
        scratch_shapes=[pltpu.VMEM((nk_pad // tk,) + tile_shape, F32),
                        pltpu.VMEM((nk_pad // tk, qb, tk), F32),
                        pltpu.VMEM((KV_HEADS, rep * qb, 2 * HEAD_DIM), F32),
                        pltpu.VMEM((IDX_HEADS * qb, IDX_DIM), BF16),
                        pltpu.VMEM((KV_HEADS, rep * qb, tk), F32),
                        pltpu.VMEM((KV_HEADS, rep * qb, tk), F32),
                        pltpu.VMEM((KV_HEADS, rep * qb, LANES), F32),
                        pltpu.VMEM((KV_HEADS, rep * qb, LANES), F32),
                        pltpu.VMEM((KV_HEADS, rep * qb, LANES), F32)],
        compiler_params=_params(2),
        name="dsa_attention",
    )(q, qi, sm, k_all, v_all, ki_all)


def _ssd_kernel(xbc_ref, z_ref, sm_ref, st_ref, alog_ref, dexp_ref, nw_ref, e_ref,
                y_ref, stout_ref, ht_ref, yi_ref):
    c = pl.program_id(1)
    nc = pl.num_programs(1)
    L = xbc_ref.shape[1]
    hp = SSD_INNER // SSD_GROUPS

    @pl.when(c == 0)
    def _():
        ht_ref[...] = st_ref[0].reshape(SSD_INNER, SSD_STATE).T

    xs = xbc_ref[0, :, 0:SSD_INNER]
    sm = sm_ref[0]
    lane = lax.broadcasted_iota(jnp.int32, (1, LANES), 1)
    is_dt = (lane >= SM_DT) & (lane < SM_DT + SSD_HEADS)
    a_neg = jnp.where(is_dt, -jnp.exp(alog_ref[...]), 0.0)
    dt = jnp.where(is_dt, sm, 0.0)
    rows = lax.broadcasted_iota(jnp.int32, (L, L), 0)
    cols = lax.broadcasted_iota(jnp.int32, (L, L), 1)
    tri = cols <= rows
    e = e_ref[...]
    cum = jnp.dot(tri.astype(F32), dt * a_neg, precision=HIGHEST, preferred_element_type=F32)
    eye = (lax.broadcasted_iota(jnp.int32, (LANES, LANES), 0)
           == lax.broadcasted_iota(jnp.int32, (LANES, LANES), 1)).astype(F32)
    cum_t = lax.dot_general(eye, cum, (((1,), (1,)), ((), ())), precision=HIGHEST,
                            preferred_element_type=F32)
    cum_e = jnp.dot(cum, e, precision=HIGHEST, preferred_element_type=F32)
    dt_e = jnp.dot(dt, e, precision=HIGHEST, preferred_element_type=F32)
    last = cum_e[L - 1:L, :]
    xdt = (xs * dt_e).astype(BF16)
    xw = (xs * (jnp.exp(last - cum_e) * dt_e)).astype(BF16)
    ht_old = ht_ref[...]

    for g in range(SSD_GROUPS):
        bg = xbc_ref[0, :, SSD_INNER + g * SSD_STATE:SSD_INNER + (g + 1) * SSD_STATE]
        cg = xbc_ref[0, :, SSD_INNER + (SSD_GROUPS + g) * SSD_STATE:
                     SSD_INNER + (SSD_GROUPS + g + 1) * SSD_STATE]
        bgb, cgb = bg.astype(BF16), cg.astype(BF16)
        cb = lax.dot_general(cgb, bgb, (((1,), (1,)), ((), ())), preferred_element_type=F32)
        for hh in range(SSD_HEADS // SSD_GROUPS):
            h = g * (SSD_HEADS // SSD_GROUPS) + hh
            seg = cum[:, SM_DT + h:SM_DT + h + 1] - cum_t[SM_DT + h:SM_DT + h + 1, :]
            decay = jnp.exp(jnp.where(tri, seg, -jnp.inf))
            wts = (cb * decay).astype(BF16)
            yi_ref[:, h * SSD_HEADDIM:(h + 1) * SSD_HEADDIM] = jnp.dot(
                wts, xdt[:, h * SSD_HEADDIM:(h + 1) * SSD_HEADDIM], preferred_element_type=F32)
        ht_g = ht_old[:, g * hp:(g + 1) * hp]
        y_inter = jnp.dot(cgb, ht_g.astype(BF16), preferred_element_type=F32)
        yi_ref[:, g * hp:(g + 1) * hp] = (yi_ref[:, g * hp:(g + 1) * hp]
                                          + y_inter * jnp.exp(cum_e[:, g * hp:(g + 1) * hp]))
        upd = jnp.dot(bg.T.astype(BF16), xw[:, g * hp:(g + 1) * hp],
                      preferred_element_type=F32)
        ht_ref[:, g * hp:(g + 1) * hp] = ht_g * jnp.exp(last[:, g * hp:(g + 1) * hp]) + upd

    y = yi_ref[...] + dexp_ref[...] * xs
    y = y * _silu(z_ref[0])
    for g in range(SSD_GROUPS):
        yg = y[:, g * hp:(g + 1) * hp]
        ms = jnp.mean(yg * yg, axis=-1, keepdims=True)
        y_ref[0, :, g * hp:(g + 1) * hp] = (yg * lax.rsqrt(ms + EPS)
                                            * nw_ref[:, g * hp:(g + 1) * hp]).astype(BF16)

    @pl.when(c == nc - 1)
    def _():
        stout_ref[0] = ht_ref[...].T.reshape(SSD_HEADS, SSD_HEADDIM, SSD_STATE)


def _ssd(xbc_act, proj, sm, state, a_log_pad, d_exp, norm_w, e_mat, *, chunk):
    b, t, _ = xbc_act.shape
    tok = lambda w, col=0: pl.BlockSpec((1, chunk, w), lambda i, j: (i, j, col))
    full2 = lambda a: pl.BlockSpec(a.shape, lambda i, j: (0, 0))
    st = pl.BlockSpec((1, SSD_HEADS, SSD_HEADDIM, SSD_STATE), lambda i, j: (i, 0, 0, 0))
    nw = norm_w.reshape(1, SSD_INNER)
    return pl.pallas_call(
        _ssd_kernel,
        grid=(b, t // chunk),
        in_specs=[tok(SSD_CONV_DIM), tok(SSD_INNER, OFF_Z // SSD_INNER), tok(LANES), st,
                  full2(a_log_pad), full2(d_exp), full2(nw), full2(e_mat)],
        out_specs=[tok(SSD_INNER), st],
        out_shape=[jax.ShapeDtypeStruct((b, t, SSD_INNER), BF16),
                   jax.ShapeDtypeStruct(state.shape, F32)],
        scratch_shapes=[pltpu.VMEM((SSD_STATE, SSD_INNER), F32),
                        pltpu.VMEM((chunk, SSD_INNER), F32)],
        compiler_params=_params(2),
        name="ssd_scan",
    )(xbc_act, proj, sm, state, a_log_pad, d_exp, nw, e_mat)


def _lru_kernel(gate_ref, xin_ref, cw_ref, cb_ref, wa_ref, wx_ref, ba_ref, bx_ref, lam_ref,
                h0_ref, cs_ref, y_ref, hout_ref, cnew_ref, cbuf_ref, a_ref, u_ref, hp_ref):
    t = pl.program_id(1)
    nt = pl.num_programs(1)
    tt = xin_ref.shape[1]
    xc = _causal_conv(xin_ref[0], cs_ref, cw_ref, cb_ref, cbuf_ref, cnew_ref, t == 0, t == nt - 1)

    @pl.when(t == 0)
    def _():
        hp_ref[...] = h0_ref[0]

    for kb in range(LRU_BLOCKS):
        sl = slice(kb * LRU_BLOCK_DIM, (kb + 1) * LRU_BLOCK_DIM)
        xb = xc[:, sl].astype(BF16)
        a_ref[:, sl] = jnp.dot(xb, wa_ref[kb].astype(BF16), preferred_element_type=F32)
        u_ref[:, sl] = jnp.dot(xb, wx_ref[kb].astype(BF16), preferred_element_type=F32)
    r = jax.nn.sigmoid(a_ref[...] + ba_ref[...])
    i = jax.nn.sigmoid(u_ref[...] + bx_ref[...])
    log_a = -LRU_C * r * _softplus(-lam_ref[...])
    a = jnp.exp(log_a)
    u = jnp.sqrt(-jnp.tanh(log_a) * (1.0 + a * a)) * (i * xc)

    width = a.shape[1]
    a = a.reshape(tt // SUBLANES, SUBLANES, width)
    u = u.reshape(tt // SUBLANES, SUBLANES, width)
    sub = lax.broadcasted_iota(jnp.int32, (1, SUBLANES, width), 1)
    for s in (1, 2, 4):
        m = sub >= s
        a_sh = pltpu.roll(a, s, 1)
        u_sh = pltpu.roll(u, s, 1)
        u = jnp.where(m, a * u_sh + u, u)
        a = jnp.where(m, a * a_sh, a)
    a_ref[...] = a.reshape(tt, width)
    u_ref[...] = u.reshape(tt, width)

    def group(gi, hprev):
        r0 = pl.multiple_of(gi * SUBLANES, SUBLANES)
        hs = a_ref[pl.ds(r0, SUBLANES), :] * hprev + u_ref[pl.ds(r0, SUBLANES), :]
        u_ref[pl.ds(r0, SUBLANES), :] = hs
        return hs[SUBLANES - 1:SUBLANES, :]

    h_last = lax.fori_loop(0, tt // SUBLANES, group, hp_ref[...])
    hp_ref[...] = h_last
    gate = gate_ref[0]
    gelu = 0.5 * gate * (1.0 + jnp.tanh(np.sqrt(2.0 / np.pi) * (gate + 0.044715 * (gate * gate * gate))))
    y_ref[0] = (u_ref[...] * gelu).astype(BF16)

    @pl.when(t == nt - 1)
    def _():
        hout_ref[0] = h_last


def _lru(proj, conv_w, conv_b, w_a, b_a, w_x, b_x, lam, h0, conv_state, *, tt):
    b, t, _ = proj.shape
    w = LRU_WIDTH
    tok = lambda col: pl.BlockSpec((1, tt, w), lambda i, j: (i, j, col))
    full = lambda a: pl.BlockSpec(a.shape, lambda i, j: (0,) * a.ndim)
    row = lambda v: v.reshape(1, w)
    state = pl.BlockSpec((1, CONV_W - 1, w), lambda i, j: (i, 0, 0))
    hspec = pl.BlockSpec((1, 1, w), lambda i, j: (i, 0, 0))
    args = (proj, proj, conv_w, row(conv_b), w_a, w_x, row(b_a), row(b_x), row(lam), h0, conv_state)
    return pl.pallas_call(
        _lru_kernel,
        grid=(b, t // tt),
        in_specs=[tok(0), tok(1)] + [full(a) for a in args[2:9]] + [hspec, state],
        out_specs=[pl.BlockSpec((1, tt, w), lambda i, j: (i, j, 0)), hspec, state],
        out_shape=[jax.ShapeDtypeStruct((b, t, w), BF16),
                   jax.ShapeDtypeStruct((b, 1, w), F32),
                   jax.ShapeDtypeStruct((b, CONV_W - 1, w), F32)],
        scratch_shapes=[pltpu.VMEM((tt + SUBLANES, w), F32), pltpu.VMEM((tt, w), F32),
                        pltpu.VMEM((tt, w), F32), pltpu.VMEM((1, w), F32)],
        compiler_params=_params(2),
        name="rg_lru",
    )(*args)


def _rope_tables(pos, dim):
    half = dim // 2
    inv = np.power(ROPE_THETA, -np.arange(half, dtype=np.float64) / half)
    ang = pos.astype(np.float64)[:, None] * inv[None, :]
    cos = np.concatenate([np.cos(ang), np.cos(ang)], axis=-1)
    sin = np.concatenate([-np.sin(ang), np.sin(ang)], axis=-1)
    reps = LANES // dim
    return (jnp.asarray(np.tile(cos, (1, reps)), F32), jnp.asarray(np.tile(sin, (1, reps)), F32))


def _head_expansion():
    e = np.zeros((LANES, SSD_INNER), np.float32)
    for h in range(SSD_HEADS):
        e[SM_DT + h, h * SSD_HEADDIM:(h + 1) * SSD_HEADDIM] = 1.0
    return jnp.asarray(e)


def _pad_lanes(v, offset):
    out = jnp.zeros((1, LANES), F32)
    return out.at[0, offset:offset + v.shape[0]].set(v)


def _reorder_w_in_ab(w):
    d = w.shape[0]
    q, k, v, qi, ki, wi, z, xbc, dt = (1024, 256, 256, 1024, 64, 16, 1024, 1536, 16)
    o = np.cumsum([0, q, k, v, qi, ki, wi, z, xbc, dt])
    seg = lambda i: w[:, o[i]:o[i + 1]]
    parts = [seg(0), seg(3), seg(6), seg(7), seg(1), seg(2), seg(4), seg(5), seg(8)]
    used = sum(p.shape[1] for p in parts)
    parts.append(jnp.zeros((d, AB_PAD_N - used), w.dtype))
    return jnp.concatenate(parts, axis=1)


def _trunk(x, mods, past_k, past_v, past_ki, ssm_h, ssm_conv, lru_h, lru_conv, p, *, cfg):
    b, t, d = x.shape
    bb, tt = cfg["bb"], cfg["tt"]
    past = 0 if past_k is None else past_k.shape[1]
    n_keys = past + t
    pos = np.arange(past, past + t)
    tabs = _rope_tables(pos, HEAD_DIM) + _rope_tables(pos, IDX_DIM)

    def ffn(x, l):
        sh_f, sc_f, g_f = mods[l][3], mods[l][4], mods[l][5]
        gu = _norm_mod_matmul(x, p["norm_ffn_w"][l], sc_f, sh_f, [p["ffn_w_gate"], p["ffn_w_up"]], l,
                              bb=bb, tt=tt, tn=cfg["tn_gu"], out_dtype=BF16, name="ffn_gate_up")
        return _matmul_residual([gu], p["ffn_w_down"], l, x, g_f, bb=bb, tt=tt, tn=256,
                                name="ffn_down")

    sh_m, sc_m, g_m = mods[0][0], mods[0][1], mods[0][2]
    proj = _norm_mod_matmul(x, p["norm_mix_w"][0], sc_m, sh_m, [p["w_in_ab_r"][None]], 0,
                            bb=bb, tt=cfg["tt_ab"], tn=AB_TILE_N, out_dtype=F32, name="in_proj_ab")
    (q, qi, k, v, ki, sm, xbc_act, ssm_conv_new, kb, vbe, kib) = _post0(
        proj, tabs, p["q_norm_w"][0], p["k_norm_w"][0], p["ssd_conv_w"][0], p["ssd_conv_b"][0],
        p["dt_bias_pad"], ssm_conv, tt=cfg["post_tt"])
    if past:
        nk_pad = -(-n_keys // cfg["tk"]) * cfg["tk"]
        cat = lambda old, new: jnp.pad(jnp.concatenate([old, new], axis=1),
                                       ((0, 0), (0, nk_pad - n_keys), (0, 0)))
        pv = past_v.astype(BF16)
        pv_ext = jnp.concatenate([pv, jnp.ones_like(pv)], axis=-1).reshape(b, past, VEXT_WIDTH)
        k_all = cat(past_k.reshape(b, past, KV_WIDTH).astype(BF16), kb)
        v_all = cat(pv_ext, vbe)
        ki_all = cat(past_ki.astype(BF16), kib)
    else:
        k_all, v_all, ki_all = kb, vbe, kib
    att = _dsa_attention(q, qi, sm, k_all, v_all, ki_all, qb=cfg["qb"], tk=cfg["tk"],
                         n_keys=n_keys, past=past)
    y_ssd, ssm_new = _ssd(xbc_act, proj, sm, ssm_h, p["a_log_pad"], p["d_exp"], p["ssd_norm_w"][0],
                          p["e_mat"], chunk=cfg["ssd_chunk"])
    x = _matmul_residual([att, y_ssd], p["w_out_ab"], 0, x, g_m, bb=bb, tt=tt, tn=512,
                         name="out_proj_ab")
    x = ffn(x, 0)

    sh_m, sc_m, g_m = mods[1][0], mods[1][1], mods[1][2]
    proj = _norm_mod_matmul(x, p["norm_mix_w"][1], sc_m, sh_m, [p["w_in_c"]], 0,
                            bb=bb, tt=tt, tn=cfg["tn_gu"], out_dtype=F32, name="in_proj_c")
    y_lru, lru_new, lru_conv_new = _lru(
        proj, p["lru_conv_w"][0], p["lru_conv_b"][0], p["lru_w_a"][0], p["lru_b_a"][0],
        p["lru_w_x"][0], p["lru_b_x"][0], p["lru_lambda"][0], lru_h.reshape(b, 1, LRU_WIDTH),
        lru_conv, tt=cfg["lru_tt"])
    x = _matmul_residual([y_lru], p["w_out_c"], 0, x, g_m, bb=bb, tt=tt, tn=512, name="out_proj_c")
    x = ffn(x, 1)

    kv_shape = (1, b, t, KV_HEADS, HEAD_DIM)
    return (x, k.reshape(kv_shape), v.reshape(kv_shape), ki[None], ssm_new[None], ssm_conv_new[None],
            lru_new.reshape(1, b, LRU_WIDTH), lru_conv_new[None])


PROMPT_CFG = dict(bb=1, tt=2048, tt_ab=1024, tn_gu=256, post_tt=256, qb=128, tk=512, ssd_chunk=128,
                  lru_tt=256)
SAMPLE_CFG = dict(bb=16, tt=32, tt_ab=32, tn_gu=512, post_tt=32, qb=32, tk=384, ssd_chunk=32,
                  lru_tt=32)


def kernel(x_prompt, x_sample, cache_attn_k, cache_attn_v, cache_idx_k, state_ssm, state_ssm_conv,
           state_lru, state_lru_conv, c_prompt, c_sample, ada_w, ada_b, norm_mix_w, norm_ffn_w,
           w_in_ab, q_norm_w, k_norm_w, ssd_conv_w, ssd_conv_b, ssd_dt_bias, ssd_a_log, ssd_d,
           ssd_norm_w, w_out_ab, w_in_c, lru_conv_w, lru_conv_b, lru_w_a, lru_b_a, lru_w_x, lru_b_x,
           lru_lambda, w_out_c, ffn_w_gate, ffn_w_up, ffn_w_down):
    bp, bs = x_prompt.shape[0], x_sample.shape[0]
    p = dict(norm_mix_w=norm_mix_w, norm_ffn_w=norm_ffn_w, q_norm_w=q_norm_w, k_norm_w=k_norm_w,
             ssd_conv_w=ssd_conv_w, ssd_conv_b=ssd_conv_b, ssd_norm_w=ssd_norm_w, w_out_ab=w_out_ab,
             w_in_c=w_in_c, lru_conv_w=lru_conv_w, lru_conv_b=lru_conv_b, lru_w_a=lru_w_a,
             lru_b_a=lru_b_a, lru_w_x=lru_w_x, lru_b_x=lru_b_x, lru_lambda=lru_lambda,
             w_out_c=w_out_c, ffn_w_gate=ffn_w_gate, ffn_w_up=ffn_w_up, ffn_w_down=ffn_w_down)
    p["w_in_ab_r"] = _reorder_w_in_ab(w_in_ab[0])
    p["dt_bias_pad"] = _pad_lanes(ssd_dt_bias[0], SM_DT)
    p["a_log_pad"] = _pad_lanes(ssd_a_log[0], SM_DT)
    p["d_exp"] = jnp.repeat(ssd_d[0], SSD_HEADDIM).reshape(1, SSD_INNER)
    p["e_mat"] = _head_expansion()

    m_rows = 32
    c_all = jnp.concatenate([c_prompt, c_sample, jnp.zeros((m_rows - bp - bs, D_MODEL), F32)], axis=0)
    mod = _modulation(c_all, ada_w, ada_b)

    def group_mods(r0, nb):
        return [[mod[l, r0:r0 + nb, i * D_MODEL:(i + 1) * D_MODEL].reshape(nb, 1, D_MODEL)
                 for i in range(6)] for l in range(mod.shape[0])]

    zeros = lambda *s: jnp.zeros(s, F32)
    out_p = _trunk(x_prompt, group_mods(0, bp), None, None, None,
                   zeros(bp, SSD_HEADS, SSD_HEADDIM, SSD_STATE), zeros(bp, CONV_W - 1, SSD_CONV_DIM),
                   zeros(bp, LRU_WIDTH), zeros(bp, CONV_W - 1, LRU_WIDTH), p, cfg=PROMPT_CFG)
    out_s = _trunk(x_sample, group_mods(bp, bs), cache_attn_k[0], cache_attn_v[0], cache_idx_k[0],
                   state_ssm[0], state_ssm_conv[0], state_lru[0], state_lru_conv[0], p, cfg=SAMPLE_CFG)
    return (out_p[0], out_s[0]) + out_p[1:] + out_s[1:]
```

```python
import functools

import numpy as np
import jax
import jax.numpy as jnp
from jax import lax
from jax.experimental import pallas as pl
from jax.experimental.pallas import tpu as pltpu

F32 = jnp.float32
BF16 = jnp.bfloat16
HIGHEST = lax.Precision.HIGHEST

D_MODEL = 2048
CHUNK = 64
CHUNK_SHIFT = 6
ATT_HEADS = 8
KV_HEADS = 2
HEAD_DIM = 128
IDX_HEADS = 16
IDX_DIM = 64
TOPK_MAX = 256
ROPE_THETA = 10000.0
SSD_HEADS = 16
SSD_HEADDIM = 64
SSD_GROUPS = 2
SSD_STATE = 128
SSD_INNER = SSD_HEADS * SSD_HEADDIM
SSD_CONV_DIM = SSD_INNER + 2 * SSD_GROUPS * SSD_STATE
CONV_W = 4
LRU_WIDTH = D_MODEL
LRU_BLOCKS = 16
LRU_BLOCK_DIM = LRU_WIDTH // LRU_BLOCKS
LRU_C = 8.0
EPS = 1e-6
ATT_WIDTH = ATT_HEADS * HEAD_DIM
KV_WIDTH = KV_HEADS * HEAD_DIM
VEXT_WIDTH = 2 * KV_WIDTH
QI_WIDTH = IDX_HEADS * IDX_DIM
Q_SCALE = HEAD_DIM ** -0.5 * float(np.log2(np.e))

LANES = 128
SUBLANES = 8
VMEM_LIMIT_BYTES = 56 * 1024 * 1024
SINGLE_BUFFER_ROWS = 2048

OFF_Q = 0
OFF_K = OFF_Q + ATT_WIDTH
OFF_V = OFF_K + KV_WIDTH
OFF_QI = OFF_V + KV_WIDTH
OFF_SMALL = OFF_QI + QI_WIDTH
SM_WI = IDX_DIM
SM_DT = IDX_DIM + IDX_HEADS
OFF_Z = OFF_SMALL + SM_DT
OFF_XBC = OFF_Z + SSD_INNER
OFF_DT = OFF_XBC + SSD_CONV_DIM
AB_WIDTH = OFF_DT + SSD_HEADS
Z_BLOCK = 1280
AB_TILE_N = 768
AB_PAD_N = 5376
assert OFF_Z % LANES == SM_DT and OFF_XBC % LANES == SM_DT and OFF_DT % LANES == SM_DT
assert (OFF_Z - SM_DT) % Z_BLOCK == 0 and OFF_Z - SM_DT + Z_BLOCK >= OFF_Z + SSD_INNER

NEG_BIG = -1e30


def _params(n_axes):
    return pltpu.CompilerParams(dimension_semantics=("arbitrary",) * n_axes,
                                vmem_limit_bytes=VMEM_LIMIT_BYTES)


def _silu(x):
    return x * jax.nn.sigmoid(x)


def _softplus(x):
    return jnp.maximum(x, 0.0) + jnp.log1p(jnp.exp(-jnp.abs(x)))


def _mod_kernel(c_ref, w_ref, b_ref, o_ref):
    a = _silu(c_ref[...]).astype(BF16)
    o_ref[0] = jnp.dot(a, w_ref[0].astype(BF16), preferred_element_type=F32) + b_ref[0]


def _modulation(c_all, ada_w, ada_b):
    depth, d, n = ada_w.shape
    m = c_all.shape[0]
    tn = 1024
    return pl.pallas_call(
        _mod_kernel,
        grid=(depth, n // tn),
        in_specs=[pl.BlockSpec((m, d), lambda l, j: (0, 0)),
                  pl.BlockSpec((1, d, tn), lambda l, j: (l, 0, j)),
                  pl.BlockSpec((1, 1, tn), lambda l, j: (l, 0, j))],
        out_specs=pl.BlockSpec((1, m, tn), lambda l, j: (l, 0, j)),
        out_shape=jax.ShapeDtypeStruct((depth, m, n), F32),
        compiler_params=_params(2),
        name="adaln_mod",
    )(c_all, ada_w, ada_b.reshape(depth, 1, n))


def _nmm_kernel(x_ref, nw_ref, sc_ref, sh_ref, *rest, swiglu, tail):
    n_w = 2 if swiglu or tail else 1
    w_refs, o_ref, h_ref = rest[:n_w], rest[n_w], rest[n_w + 1]
    bb, tt, d = x_ref.shape

    @pl.when(pl.program_id(2) == 0)
    def _():
        x = x_ref[...]
        ms = jnp.mean(x * x, axis=-1, keepdims=True)
        y = x * lax.rsqrt(ms + EPS) * nw_ref[...]
        h = y * (1.0 + sc_ref[...]) + sh_ref[...]
        h_ref[...] = h.reshape(bb * tt, d).astype(BF16)

    h = h_ref[...]

    def project(w_ref):
        o = jnp.dot(h, w_ref[...].astype(BF16), preferred_element_type=F32)
        o_ref[...] = o.reshape(o_ref.shape).astype(o_ref.dtype)

    if swiglu:
        g = jnp.dot(h, w_refs[0][...].astype(BF16), preferred_element_type=F32)
        u = jnp.dot(h, w_refs[1][...].astype(BF16), preferred_element_type=F32)
        o_ref[...] = (_silu(g) * u).reshape(o_ref.shape).astype(o_ref.dtype)
    elif tail:
        last = pl.num_programs(2) - 1
        pl.when(pl.program_id(2) < last)(lambda: project(w_refs[0]))
        pl.when(pl.program_id(2) == last)(lambda: project(w_refs[1]))
    else:
        project(w_refs[0])


def _row_block_mode(rows):
    return dict(pipeline_mode=pl.Buffered(1)) if rows >= SINGLE_BUFFER_ROWS else {}


def _norm_mod_matmul(x, nw, sc, sh, ws, layer, *, bb, tt, tn, out_dtype, name, w_tail=None):
    b, t, d = x.shape
    swiglu = len(ws) == 2
    n_main = ws[0].shape[2] // tn
    n = (n_main + (w_tail is not None)) * tn
    xmap = lambda i, j, k: (i, j, 0)
    mmap = lambda i, j, k: (i, 0, 0)
    w_specs = [pl.BlockSpec((None, d, tn), lambda i, j, k: (layer, 0, jnp.minimum(k, n_main - 1)))
               ] * len(ws)
    operands = list(ws)
    if w_tail is not None:
        w_specs.append(pl.BlockSpec((d, tn), lambda i, j, k: (0, 0)))
        operands.append(w_tail)
    return pl.pallas_call(
        functools.partial(_nmm_kernel, swiglu=swiglu, tail=w_tail is not None),
        grid=(b // bb, t // tt, n // tn),
        in_specs=[pl.BlockSpec((bb, tt, d), xmap, **_row_block_mode(bb * tt)),
                  pl.BlockSpec((1, 1, d), lambda i, j, k: (0, 0, 0)),
                  pl.BlockSpec((bb, 1, d), mmap),
                  pl.BlockSpec((bb, 1, d), mmap)] + w_specs,
        out_specs=pl.BlockSpec((bb, tt, tn), lambda i, j, k: (i, j, k)),
        out_shape=jax.ShapeDtypeStruct((b, t, n), out_dtype),
        scratch_shapes=[pltpu.VMEM((bb * tt, d), BF16)],
        compiler_params=_params(3),
        name=name,
    )(x, nw.reshape(1, 1, d), sc, sh, *operands)


def _mmr_kernel(*refs, k_sizes):
    n_a = len(k_sizes)
    a_refs = refs[:n_a]
    w_ref, x_ref, g_ref, o_ref = refs[n_a:]
    bb, tt, tn = o_ref.shape
    acc = jnp.zeros((bb * tt, tn), F32)
    off = 0
    for a_ref, ks in zip(a_refs, k_sizes):
        a = a_ref[...].reshape(bb * tt, ks)
        acc = acc + jnp.dot(a, w_ref[off:off + ks, :].astype(BF16), preferred_element_type=F32)
        off += ks
    o_ref[...] = x_ref[...] + g_ref[...] * acc.reshape(bb, tt, tn)


def _matmul_residual(a_list, w, layer, x, g, *, bb, tt, tn, name):
    b, t, d = x.shape
    k_sizes = tuple(a.shape[-1] for a in a_list)
    k_total = sum(k_sizes)
    return pl.pallas_call(
        functools.partial(_mmr_kernel, k_sizes=k_sizes),
        grid=(b // bb, t // tt, d // tn),
        in_specs=[pl.BlockSpec((bb, tt, ks), lambda i, j, k: (i, j, 0), **_row_block_mode(bb * tt))
                  for ks in k_sizes]
                 + [pl.BlockSpec((None, k_total, tn), lambda i, j, k: (layer, 0, k)),
                    pl.BlockSpec((bb, tt, tn), lambda i, j, k: (i, j, k)),
                    pl.BlockSpec((bb, 1, tn), lambda i, j, k: (i, 0, k))],
        out_specs=pl.BlockSpec((bb, tt, tn), lambda i, j, k: (i, j, k)),
        out_shape=jax.ShapeDtypeStruct((b, t, d), F32),
        compiler_params=_params(3),
        name=name,
    )(*a_list, w, x, g)


def _swap_halves(x, half):
    w = x.shape[-1]
    lane = lax.broadcasted_iota(jnp.int32, x.shape, x.ndim - 1)
    first = (lane & half) == 0
    return jnp.where(first, pltpu.roll(x, w - half, x.ndim - 1), pltpu.roll(x, half, x.ndim - 1))


def _lane_aligned(slab, offset, width):
    return pltpu.roll(slab, slab.shape[-1] - offset, slab.ndim - 1)[:, :width]


def _rope(x, cos, sin_signed, half):
    reps = x.shape[-1] // cos.shape[-1]
    if reps > 1:
        cos = jnp.concatenate([cos] * reps, axis=-1)
        sin_signed = jnp.concatenate([sin_signed] * reps, axis=-1)
    return x * cos + _swap_halves(x, half) * sin_signed


def _causal_conv(u, state_ref, w_ref, b_ref, cbuf_ref, new_ref, first, last):
    tt = u.shape[0]

    @pl.when(first)
    def _():
        cbuf_ref[0:SUBLANES, :] = jnp.zeros((SUBLANES, u.shape[1]), F32)
        cbuf_ref[SUBLANES - (CONV_W - 1):SUBLANES, :] = state_ref[0]

    cbuf_ref[SUBLANES:SUBLANES + tt, :] = u
    full = cbuf_ref[...]
    out = b_ref[...]
    for j in range(CONV_W):
        shift = CONV_W - 1 - j
        rows = pltpu.roll(full, shift, 0) if shift else full
        out = out + rows[SUBLANES:SUBLANES + tt, :] * w_ref[j:j + 1, :]
    tail = cbuf_ref[tt:tt + SUBLANES, :]
    cbuf_ref[0:SUBLANES, :] = tail

    @pl.when(last)
    def _():
        new_ref[0] = tail[SUBLANES - (CONV_W - 1):, :]

    return out


def _post0_kernel(p_ref, cq_ref, sq_ref, ci_ref, si_ref, qn_ref, kn_ref, cw_ref, cb_ref, dtb_ref,
                  cs_ref,
                  q_ref, qi_ref, k_ref, v_ref, ki_ref, sm_ref, xbc_ref, cnew_ref, kb_ref, vb_ref,
                  kib_ref, cbuf_ref):
    t = pl.program_id(1)
    nt = pl.num_programs(1)
    cq, sq, ci, si = cq_ref[...], sq_ref[...], ci_ref[...], si_ref[...]

    def head_norm_rope(x, w):
        ms = jnp.mean(x * x, axis=-1, keepdims=True)
        return _rope(x * lax.rsqrt(ms + EPS) * w, cq, sq, HEAD_DIM // 2)

    for h in range(ATT_HEADS):
        xh = p_ref[0, :, OFF_Q + h * HEAD_DIM:OFF_Q + (h + 1) * HEAD_DIM]
        q_ref[0, :, h * HEAD_DIM:(h + 1) * HEAD_DIM] = (head_norm_rope(xh, qn_ref[...])
                                                         * Q_SCALE).astype(BF16)
    ones = jnp.ones((p_ref.shape[1], HEAD_DIM), BF16)
    for h in range(KV_HEADS):
        xh = p_ref[0, :, OFF_K + h * HEAD_DIM:OFF_K + (h + 1) * HEAD_DIM]
        kh = head_norm_rope(xh, kn_ref[...])
        k_ref[0, :, h * HEAD_DIM:(h + 1) * HEAD_DIM] = kh
        kb_ref[0, :, h * HEAD_DIM:(h + 1) * HEAD_DIM] = kh.astype(BF16)
        vh = p_ref[0, :, OFF_V + h * HEAD_DIM:OFF_V + (h + 1) * HEAD_DIM]
        v_ref[0, :, h * HEAD_DIM:(h + 1) * HEAD_DIM] = vh
        vb_ref[0, :, 2 * h * HEAD_DIM:(2 * h + 1) * HEAD_DIM] = vh.astype(BF16)
        vb_ref[0, :, (2 * h + 1) * HEAD_DIM:(2 * h + 2) * HEAD_DIM] = ones

    qi = p_ref[0, :, OFF_QI:OFF_QI + QI_WIDTH]
    qi_ref[0] = _rope(qi, ci, si, IDX_DIM // 2).astype(BF16)

    sm = p_ref[0, :, OFF_SMALL:OFF_SMALL + LANES]
    lane = lax.broadcasted_iota(jnp.int32, sm.shape, 1)
    ki_part = _rope(sm, ci, si, IDX_DIM // 2)
    wi_part = sm * (IDX_HEADS ** -0.5 * IDX_DIM ** -0.5)
    dt_col = OFF_DT - SM_DT
    dt_part = _softplus(p_ref[0, :, dt_col:dt_col + LANES] + dtb_ref[...])
    sm_out = jnp.where(lane < SM_WI, ki_part,
                       jnp.where(lane < SM_DT, wi_part,
                                 jnp.where(lane < SM_DT + SSD_HEADS, dt_part, 0.0)))
    sm_ref[0] = sm_out
    ki_ref[0] = sm_out[:, :IDX_DIM]
    kib_ref[0] = sm_out[:, :IDX_DIM].astype(BF16)

    xbc = _lane_aligned(p_ref[0, :, OFF_XBC - SM_DT:OFF_DT - SM_DT + LANES], SM_DT, SSD_CONV_DIM)
    conv = _causal_conv(xbc, cs_ref, cw_ref, cb_ref, cbuf_ref, cnew_ref, t == 0, t == nt - 1)
    xbc_ref[0] = _silu(conv)


def _post0(proj, tabs, q_norm_w, k_norm_w, conv_w, conv_b, dt_bias_pad, conv_state, *, tt):
    b, t, n = proj.shape
    tok = lambda w: pl.BlockSpec((1, tt, w), lambda i, j: (i, j, 0))
    tab = pl.BlockSpec((tt, LANES), lambda i, j: (j, 0))
    full2 = lambda a: pl.BlockSpec(a.shape, lambda i, j: (0, 0))
    state = lambda c: pl.BlockSpec((1, CONV_W - 1, c), lambda i, j: (i, 0, 0))
    out_shapes = [
        jax.ShapeDtypeStruct((b, t, ATT_WIDTH), BF16),
        jax.ShapeDtypeStruct((b, t, QI_WIDTH), BF16),
        jax.ShapeDtypeStruct((b, t, KV_WIDTH), F32),
        jax.ShapeDtypeStruct((b, t, KV_WIDTH), F32),
        jax.ShapeDtypeStruct((b, t, IDX_DIM), F32),
        jax.ShapeDtypeStruct((b, t, LANES), F32),
        jax.ShapeDtypeStruct((b, t, SSD_CONV_DIM), F32),
        jax.ShapeDtypeStruct((b, CONV_W - 1, SSD_CONV_DIM), F32),
        jax.ShapeDtypeStruct((b, t, KV_WIDTH), BF16),
        jax.ShapeDtypeStruct((b, t, VEXT_WIDTH), BF16),
        jax.ShapeDtypeStruct((b, t, IDX_DIM), BF16),
    ]
    out_specs = [tok(ATT_WIDTH), tok(QI_WIDTH), tok(KV_WIDTH), tok(KV_WIDTH), tok(IDX_DIM),
                 tok(LANES), tok(SSD_CONV_DIM), state(SSD_CONV_DIM), tok(KV_WIDTH), tok(VEXT_WIDTH),
                 tok(IDX_DIM)]
    qn = q_norm_w.reshape(1, HEAD_DIM)
    kn = k_norm_w.reshape(1, HEAD_DIM)
    cb = conv_b.reshape(1, SSD_CONV_DIM)
    return pl.pallas_call(
        _post0_kernel,
        grid=(b, t // tt),
        in_specs=[tok(n), tab, tab, tab, tab, full2(qn), full2(kn), full2(conv_w), full2(cb),
                  full2(dt_bias_pad), state(SSD_CONV_DIM)],
        out_specs=out_specs,
        out_shape=out_shapes,
        scratch_shapes=[pltpu.VMEM((tt + SUBLANES, SSD_CONV_DIM), F32)],
        compiler_params=_params(2),
        name="post_proj0",
    )(proj, *tabs, qn, kn, conv_w, cb, dt_bias_pad, conv_state)


def _dsa_kernel(q_ref, qi_ref, sm_ref, k_ref, v_ref, ki_ref, *rest, qb, tk, n_keys, past, topk,
                n_bisect, keys_on_sublanes):
    if past:
        pk_ref, pv_ref, pki_ref = rest[:3]
        rest = rest[3:]
    (o_ref, sc_ref, bias_ref, acc_ref, qs_ref, s0_ref, s1_ref, mt0_ref, mt1_ref, m_ref) = rest[:10]
    if past:
        kall_ref, vall_ref, kiall_ref = rest[10:]
        t_new = k_ref.shape[1]
        pad = kall_ref.shape[0] - n_keys
        for g in range(KV_HEADS):
            kall_ref[0:past, g * HEAD_DIM:(g + 1) * HEAD_DIM] = pk_ref[0, :, g, :].astype(BF16)
            vall_ref[0:past, 2 * g * HEAD_DIM:(2 * g + 1) * HEAD_DIM] = pv_ref[0, :, g, :].astype(BF16)
            vall_ref[0:past, (2 * g + 1) * HEAD_DIM:(2 * g + 2) * HEAD_DIM] = jnp.ones(
                (past, HEAD_DIM), BF16)
        kiall_ref[0:past, :] = pki_ref[0].astype(BF16)
        for dst, src in ((kall_ref, k_ref), (vall_ref, v_ref), (kiall_ref, ki_ref)):
            dst[past:past + t_new, :] = src[0]
            dst[past + t_new:, :] = jnp.zeros((pad, dst.shape[1]), BF16)
        k_src, v_src, ki_src = kall_ref, vall_ref, kiall_ref
    else:
        k_src, v_src, ki_src = k_ref.at[0], v_ref.at[0], ki_ref.at[0]
    kax = 0 if keys_on_sublanes else 1
    tile_shape = (tk, qb) if keys_on_sublanes else (qb, tk)
    vec_shape = (1, qb) if keys_on_sublanes else (qb, 1)
    n_acc = 8
    part_shape = (n_acc, SUBLANES, qb) if keys_on_sublanes else (qb, LANES)
    j = pl.program_id(1)
    pos0 = past + j * qb
    q_chunk = lax.shift_right_logical(pos0 + lax.broadcasted_iota(jnp.int32, vec_shape, 1 - kax),
                                      CHUNK_SHIFT)
    n_valid = jnp.minimum((q_chunk + 1) * CHUNK, n_keys)
    max_valid = jnp.minimum(((pos0 + qb - 1) // CHUNK + 1) * CHUNK, n_keys)
    nkt = (max_valid + tk - 1) // tk
    kf = float(topk)
    active_f = jnp.where(n_valid > topk, 1.0, 0.0)

    def key_index(kt):
        return kt * tk + lax.broadcasted_iota(jnp.int32, tile_shape, kax)

    nt_dims = (((1,), (1,)), ((), ()))
    if keys_on_sublanes:
        for h in range(IDX_HEADS):
            qs_ref[h * qb:(h + 1) * qb, :] = qi_ref[0, :, h * IDX_DIM:(h + 1) * IDX_DIM]
        wi_t = sm_ref[0].T
        head_w = lambda h: wi_t[SM_WI + h:SM_WI + h + 1, :]
    else:
        qi = qi_ref[0]
        wis = sm_ref[0][:, SM_WI:SM_WI + IDX_HEADS]
        head_w = lambda h: wis[:, h:h + 1]

    def score_tile(kt, carry):
        kit = ki_src[pl.ds(pl.multiple_of(kt * tk, tk), tk), :]
        acc = jnp.zeros(tile_shape, F32)
        if keys_on_sublanes:
            for pr in range(IDX_HEADS // 2):
                s2 = lax.dot_general(kit, qs_ref[2 * pr * qb:(2 * pr + 2) * qb, :], nt_dims,
                                     preferred_element_type=F32)
                for e in range(2):
                    acc = acc + head_w(2 * pr + e) * jnp.maximum(s2[:, e * qb:(e + 1) * qb], 0.0)
        else:
            for h in range(IDX_HEADS):
                s = lax.dot_general(qi[:, h * IDX_DIM:(h + 1) * IDX_DIM], kit, nt_dims,
                                    preferred_element_type=F32)
                acc = acc + head_w(h) * jnp.maximum(s, 0.0)
        sc_ref[kt] = jnp.where(key_index(kt) < n_valid, acc, -jnp.inf)
        return carry

    lax.fori_loop(0, nkt, score_tile, 0)

    def fold(m, op2, red):
        if keys_on_sublanes:
            return red(m.reshape(tk // (n_acc * SUBLANES), n_acc, SUBLANES, qb), axis=0)
        f = m[:, 0:LANES]
        for c in range(1, tk // LANES):
            f = op2(f, m[:, c * LANES:(c + 1) * LANES])
        return f

    def reduce_tiles(make, op2, red, init):
        def body(kt, part):
            return op2(part, fold(make(sc_ref[kt], kt), op2, red))
        part = lax.fori_loop(0, nkt, body, jnp.full(part_shape, init, F32))
        if keys_on_sublanes:
            part = red(part, axis=0)
        return red(part, axis=kax, keepdims=True)

    def count(pred):
        return reduce_tiles(lambda t, kt: jnp.where(pred(t, kt), 1.0, 0.0), jnp.add, jnp.sum, 0.0)

    def row_max(pred):
        return reduce_tiles(lambda t, kt: jnp.where(pred(t, kt), t, -jnp.inf), jnp.maximum, jnp.max,
                            -jnp.inf)

    def row_min_valid():
        return reduce_tiles(lambda t, kt: jnp.where(key_index(kt) < n_valid, t, jnp.inf), jnp.minimum,
                            jnp.min, jnp.inf)

    def any_set(flag_f):
        return jnp.max(flag_f) > 0.0

    def bisect(_, state):
        lo, hi, c_lo = state
        mid = lo + (hi - lo) * 0.5
        c = count(lambda t, kt: t >= mid)
        ge = c >= kf
        return jnp.where(ge, mid, lo), jnp.where(ge, hi, mid), jnp.where(ge, c, c_lo)

    lo0 = row_min_valid()
    hi0 = row_max(lambda t, kt: t == t)
    lo, hi, c_lo = lax.fori_loop(0, n_bisect, bisect, (lo0, hi0, n_valid.astype(F32)))
    found_f = jnp.where(c_lo == kf, active_f, 0.0)
    v_found = reduce_tiles(lambda t, kt: jnp.where(t >= lo, t, jnp.inf), jnp.minimum, jnp.min, jnp.inf)

    def walk_down():
        w0 = row_max(lambda t, kt: t <= hi)
        c0 = count(lambda t, kt: t >= w0)

        def walk_cond(state):
            w, c = state
            return any_set(jnp.where(c < kf, active_f, 0.0))

        def walk_body(state):
            w, c = state
            w2 = row_max(lambda t, kt: t < w)
            c2 = count(lambda t, kt: t >= w2)
            upd = c < kf
            return jnp.where(upd, w2, w), jnp.where(upd, c2, c)

        return lax.while_loop(walk_cond, walk_body,
                              (jnp.where(found_f > 0.5, v_found, w0),
                               jnp.where(found_f > 0.5, kf, c0)))

    w, c_ge = lax.cond(any_set(active_f - found_f), walk_down,
                       lambda: (v_found, jnp.full(vec_shape, kf, F32)))
    thr = jnp.where(active_f > 0.5, w, -jnp.inf)
    tied_f = jnp.where(c_ge > kf, active_f, 0.0)

    def write_bias(sel_fn):
        def body(kt, carry):
            kidx = key_index(kt)
            sel = sel_fn(sc_ref[kt], kidx) & (kidx < n_valid)
            bias = jnp.where(sel, 0.0, NEG_BIG)
            bias_ref[kt] = bias.T if keys_on_sublanes else bias
            return carry
        lax.fori_loop(0, nkt, body, 0)

    def no_ties():
        write_bias(lambda t, kidx: t >= thr)

    def with_ties():
        need = kf - count(lambda t, kt: t > thr)
        n_steps = max(1, int(np.ceil(np.log2(sc_ref.shape[0] * tk))) + 1)

        def step(_, lh):
            lo_i, hi_i = lh
            mid = lax.shift_right_arithmetic(lo_i + hi_i, 1)
            ge = count(lambda t, kt: (t == thr) & (key_index(kt) <= mid)) >= need
            return jnp.where(ge, lo_i, mid), jnp.where(ge, mid, hi_i)

        lo_i = jnp.full(vec_shape, -1, jnp.int32)
        hi_i = jnp.full(vec_shape, sc_ref.shape[0] * tk - 1, jnp.int32)
        _, last_tie = lax.fori_loop(0, n_steps, step, (lo_i, hi_i))
        write_bias(lambda t, kidx: (t > thr) | ((t == thr)
                                                & ((kidx <= last_tie) | (tied_f < 0.5))))

    any_tie = jnp.max(tied_f) > 0.0
    lax.cond(any_tie, with_ties, no_ties)

    rep = ATT_HEADS // KV_HEADS
    qgs = [jnp.concatenate(
        [q_ref[0, :, (g * rep + r) * HEAD_DIM:(g * rep + r + 1) * HEAD_DIM] for r in range(rep)],
        axis=0) for g in range(KV_HEADS)]
    acc_ref[...] = jnp.zeros(acc_ref.shape, F32)
    bufs = ((s0_ref, mt0_ref), (s1_ref, mt1_ref))

    def qk_tile(kt, buf):
        s_ref, mt_ref = bufs[buf]
        row0 = pl.multiple_of(kt * tk, tk)
        bias = bias_ref[kt]
        for g in range(KV_HEADS):
            kg = k_src[pl.ds(row0, tk), g * HEAD_DIM:(g + 1) * HEAD_DIM]
            s = lax.dot_general(qgs[g], kg, nt_dims, preferred_element_type=F32)
            s = (s.reshape(rep, qb, tk) + bias[None]).reshape(rep * qb, tk)
            s_ref[g] = s
            mt_ref[g] = jnp.broadcast_to(jnp.max(s, axis=-1, keepdims=True), (rep * qb, LANES))

    def lane_tile(x, width):
        return jnp.concatenate([x] * (width // LANES), axis=1)

    def att_tile(kt, buf, prefetch):
        if prefetch:
            qk_tile(kt + 1, 1 - buf)
        s_ref, mt_ref = bufs[buf]
        row0 = pl.multiple_of(kt * tk, tk)
        for g in range(KV_HEADS):
            vg = v_src[pl.ds(row0, tk), 2 * g * HEAD_DIM:(2 * g + 2) * HEAD_DIM]
            m_old = m_ref[g]
            m_new = jnp.maximum(m_old, mt_ref[g])
            m_ref[g] = m_new
            alpha = jnp.exp2(m_old - m_new)
            p = jnp.exp2(s_ref[g] - lane_tile(m_new, tk)).astype(BF16)
            acc_ref[g] = (lane_tile(alpha, 2 * HEAD_DIM) * acc_ref[g]
                          + jnp.dot(p, vg, preferred_element_type=F32))

    def tile_pair(pi, carry):
        att_tile(2 * pi, 0, True)
        att_tile(2 * pi + 1, 1, True)
        return carry

    def last_two():
        att_tile(nkt - 2, 0, True)
        att_tile(nkt - 1, 1, False)

    def last_one():
        att_tile(nkt - 1, 0, False)

    m_ref[...] = jnp.full(m_ref.shape, NEG_BIG, F32)
    qk_tile(0, 0)
    lax.fori_loop(0, (nkt - 1) // 2, tile_pair, 0)
    lax.cond((nkt & 1) == 0, last_two, last_one)
    for g in range(KV_HEADS):
        acc = acc_ref[g]
        o = acc[:, :HEAD_DIM] / acc[:, HEAD_DIM:]
        for r in range(rep):
            h = g * rep + r
            o_ref[0, :, h * HEAD_DIM:(h + 1) * HEAD_DIM] = o[r * qb:(r + 1) * qb].astype(BF16)


def _dsa_attention(q, qi, sm, k_new, v_new, ki_new, cache, *, qb, tk):
    b, t, _ = q.shape
    past = 0 if cache is None else cache[0].shape[1]
    n_keys = past + t
    nk_pad = -(-n_keys // tk) * tk
    topk = min(TOPK_MAX, n_keys // 4)
    keys_on_sublanes = qb == LANES
    rep = ATT_HEADS // KV_HEADS
    tokq = lambda w: pl.BlockSpec((1, qb, w), lambda i, j: (i, j, 0))
    keys = lambda w: pl.BlockSpec((1, t, w), lambda i, j: (i, 0, 0))
    tile_shape = (tk, qb) if keys_on_sublanes else (qb, tk)
    cache_specs, cache_scratch = [], []
    if cache is not None:
        kv = pl.BlockSpec((1, past, KV_HEADS, HEAD_DIM), lambda i, j: (i, 0, 0, 0))
        cache_specs = [kv, kv, pl.BlockSpec((1, past, IDX_DIM), lambda i, j: (i, 0, 0))]
        cache_scratch = [pltpu.VMEM((nk_pad, w), BF16) for w in (KV_WIDTH, VEXT_WIDTH, IDX_DIM)]
    return pl.pallas_call(
        functools.partial(_dsa_kernel, qb=qb, tk=tk, n_keys=n_keys, past=past, topk=topk,
                          n_bisect=20, keys_on_sublanes=keys_on_sublanes),
        grid=(b, t // qb),
        in_specs=[tokq(ATT_WIDTH), tokq(QI_WIDTH), tokq(LANES), keys(KV_WIDTH), keys(VEXT_WIDTH),
                  keys(IDX_DIM)] + cache_specs,
        out_specs=tokq(ATT_WIDTH),
        out_shape=jax.ShapeDtypeStruct((b, t, ATT_WIDTH), BF16),
        scratch_shapes=[pltpu.VMEM((nk_pad // tk,) + tile_shape, F32),
                        pltpu.VMEM((nk_pad // tk, qb, tk), F32),
                        pltpu.VMEM((KV_HEADS, rep * qb, 2 * HEAD_DIM), F32),
                        pltpu.VMEM((IDX_HEADS * qb, IDX_DIM), BF16),
                        pltpu.VMEM((KV_HEADS, rep * qb, tk), F32),
                        pltpu.VMEM((KV_HEADS, rep * qb, tk), F32),
                        pltpu.VMEM((KV_HEADS, rep * qb, LANES), F32),
                        pltpu.VMEM((KV_HEADS, rep * qb, LANES), F32),
                        pltpu.VMEM((KV_HEADS, rep * qb, LANES), F32)] + cache_scratch,
        compiler_params=_params(2),
        name="dsa_attention",
    )(q, qi, sm, k_new, v_new, ki_new, *(cache or ()))


def _ssd_kernel(xbc_ref, z_ref, sm_ref, st_ref, alog_ref, dexp_ref, nw_ref, e_ref,
                y_ref, stout_ref, ht_ref, yi_ref):
    c = pl.program_id(1)
    nc = pl.num_programs(1)
    L = xbc_ref.shape[1]
    hp = SSD_INNER // SSD_GROUPS

    @pl.when(c == 0)
    def _():
        ht_ref[...] = st_ref[0].reshape(SSD_INNER, SSD_STATE).T

    xs = xbc_ref[0, :, 0:SSD_INNER]
    sm = sm_ref[0]
    lane = lax.broadcasted_iota(jnp.int32, (1, LANES), 1)
    is_dt = (lane >= SM_DT) & (lane < SM_DT + SSD_HEADS)
    a_neg = jnp.where(is_dt, -jnp.exp(alog_ref[...]), 0.0)
    dt = jnp.where(is_dt, sm, 0.0)
    rows = lax.broadcasted_iota(jnp.int32, (L, L), 0)
    cols = lax.broadcasted_iota(jnp.int32, (L, L), 1)
    tri = cols <= rows
    e = e_ref[...]
    cum = jnp.dot(tri.astype(F32), dt * a_neg, precision=HIGHEST, preferred_element_type=F32)
    eye = (lax.broadcasted_iota(jnp.int32, (LANES, LANES), 0)
           == lax.broadcasted_iota(jnp.int32, (LANES, LANES), 1)).astype(F32)
    cum_t = lax.dot_general(eye, cum, (((1,), (1,)), ((), ())), precision=HIGHEST,
                            preferred_element_type=F32)
    cum_e = jnp.dot(cum, e, precision=HIGHEST, preferred_element_type=F32)
    dt_e = jnp.dot(dt, e, precision=HIGHEST, preferred_element_type=F32)
    last = cum_e[L - 1:L, :]
    xdt = (xs * dt_e).astype(BF16)
    xw = (xs * (jnp.exp(last - cum_e) * dt_e)).astype(BF16)
    ht_old = ht_ref[...]

    for g in range(SSD_GROUPS):
        bg = xbc_ref[0, :, SSD_INNER + g * SSD_STATE:SSD_INNER + (g + 1) * SSD_STATE]
        cg = xbc_ref[0, :, SSD_INNER + (SSD_GROUPS + g) * SSD_STATE:
                     SSD_INNER + (SSD_GROUPS + g + 1) * SSD_STATE]
        bgb, cgb = bg.astype(BF16), cg.astype(BF16)
        cb = lax.dot_general(cgb, bgb, (((1,), (1,)), ((), ())), preferred_element_type=F32)
        for hh in range(SSD_HEADS // SSD_GROUPS):
            h = g * (SSD_HEADS // SSD_GROUPS) + hh
            seg = cum[:, SM_DT + h:SM_DT + h + 1] - cum_t[SM_DT + h:SM_DT + h + 1, :]
            decay = jnp.exp(jnp.where(tri, seg, -jnp.inf))
            wts = (cb * decay).astype(BF16)
            yi_ref[:, h * SSD_HEADDIM:(h + 1) * SSD_HEADDIM] = jnp.dot(
                wts, xdt[:, h * SSD_HEADDIM:(h + 1) * SSD_HEADDIM], preferred_element_type=F32)
        ht_g = ht_old[:, g * hp:(g + 1) * hp]
        y_inter = jnp.dot(cgb, ht_g.astype(BF16), preferred_element_type=F32)
        yi_ref[:, g * hp:(g + 1) * hp] = (yi_ref[:, g * hp:(g + 1) * hp]
                                          + y_inter * jnp.exp(cum_e[:, g * hp:(g + 1) * hp]))
        upd = jnp.dot(bg.T.astype(BF16), xw[:, g * hp:(g + 1) * hp],
                      preferred_element_type=F32)
        ht_ref[:, g * hp:(g + 1) * hp] = ht_g * jnp.exp(last[:, g * hp:(g + 1) * hp]) + upd

    y = yi_ref[...] + dexp_ref[...] * xs
    y = y * _silu(_lane_aligned(z_ref[0], SM_DT, SSD_INNER))
    for g in range(SSD_GROUPS):
        yg = y[:, g * hp:(g + 1) * hp]
        ms = jnp.mean(yg * yg, axis=-1, keepdims=True)
        y_ref[0, :, g * hp:(g + 1) * hp] = (yg * lax.rsqrt(ms + EPS)
                                            * nw_ref[:, g * hp:(g + 1) * hp]).astype(BF16)

    @pl.when(c == nc - 1)
    def _():
        stout_ref[0] = ht_ref[...].T.reshape(SSD_HEADS, SSD_HEADDIM, SSD_STATE)


def _ssd(xbc_act, proj, sm, state, a_log_pad, d_exp, norm_w, e_mat, *, chunk):
    b, t, _ = xbc_act.shape
    tok = lambda w, col=0: pl.BlockSpec((1, chunk, w), lambda i, j: (i, j, col))
    full2 = lambda a: pl.BlockSpec(a.shape, lambda i, j: (0, 0))
    st = pl.BlockSpec((1, SSD_HEADS, SSD_HEADDIM, SSD_STATE), lambda i, j: (i, 0, 0, 0))
    nw = norm_w.reshape(1, SSD_INNER)
    return pl.pallas_call(
        _ssd_kernel,
        grid=(b, t // chunk),
        in_specs=[tok(SSD_CONV_DIM), tok(Z_BLOCK, (OFF_Z - SM_DT) // Z_BLOCK), tok(LANES), st,
                  full2(a_log_pad), full2(d_exp), full2(nw), full2(e_mat)],
        out_specs=[tok(SSD_INNER), st],
        out_shape=[jax.ShapeDtypeStruct((b, t, SSD_INNER), BF16),
                   jax.ShapeDtypeStruct(state.shape, F32)],
        scratch_shapes=[pltpu.VMEM((SSD_STATE, SSD_INNER), F32),
                        pltpu.VMEM((chunk, SSD_INNER), F32)],
        compiler_params=_params(2),
        name="ssd_scan",
    )(xbc_act, proj, sm, state, a_log_pad, d_exp, nw, e_mat)


def _lru_kernel(gate_ref, xin_ref, cw_ref, cb_ref, wa_ref, wx_ref, ba_ref, bx_ref, lam_ref,
                h0_ref, cs_ref, y_ref, hout_ref, cnew_ref, cbuf_ref, a_ref, u_ref, hp_ref):
    t = pl.program_id(1)
    nt = pl.num_programs(1)
    tt = xin_ref.shape[1]
    xc = _causal_conv(xin_ref[0], cs_ref, cw_ref, cb_ref, cbuf_ref, cnew_ref, t == 0, t == nt - 1)

    @pl.when(t == 0)
    def _():
        hp_ref[...] = h0_ref[0]

    for kb in range(LRU_BLOCKS):
        sl = slice(kb * LRU_BLOCK_DIM, (kb + 1) * LRU_BLOCK_DIM)
        xb = xc[:, sl].astype(BF16)
        a_ref[:, sl] = jnp.dot(xb, wa_ref[kb].astype(BF16), preferred_element_type=F32)
        u_ref[:, sl] = jnp.dot(xb, wx_ref[kb].astype(BF16), preferred_element_type=F32)
    r = jax.nn.sigmoid(a_ref[...] + ba_ref[...])
    i = jax.nn.sigmoid(u_ref[...] + bx_ref[...])
    log_a = -LRU_C * r * _softplus(-lam_ref[...])
    a = jnp.exp(log_a)
    u = jnp.sqrt(-jnp.tanh(log_a) * (1.0 + a * a)) * (i * xc)

    width = a.shape[1]
    a = a.reshape(tt // SUBLANES, SUBLANES, width)
    u = u.reshape(tt // SUBLANES, SUBLANES, width)
    sub = lax.broadcasted_iota(jnp.int32, (1, SUBLANES, width), 1)
    for s in (1, 2, 4):
        m = sub >= s
        a_sh = pltpu.roll(a, s, 1)
        u_sh = pltpu.roll(u, s, 1)
        u = jnp.where(m, a * u_sh + u, u)
        a = jnp.where(m, a * a_sh, a)
    a_ref[...] = a.reshape(tt, width)
    u_ref[...] = u.reshape(tt, width)

    def group(gi, hprev):
        r0 = pl.multiple_of(gi * SUBLANES, SUBLANES)
        hs = a_ref[pl.ds(r0, SUBLANES), :] * hprev + u_ref[pl.ds(r0, SUBLANES), :]
        u_ref[pl.ds(r0, SUBLANES), :] = hs
        return hs[SUBLANES - 1:SUBLANES, :]

    h_last = lax.fori_loop(0, tt // SUBLANES, group, hp_ref[...])
    hp_ref[...] = h_last
    gate = gate_ref[0]
    gelu = 0.5 * gate * (1.0 + jnp.tanh(np.sqrt(2.0 / np.pi) * (gate + 0.044715 * (gate * gate * gate))))
    y_ref[0] = (u_ref[...] * gelu).astype(BF16)

    @pl.when(t == nt - 1)
    def _():
        hout_ref[0] = h_last


def _lru(proj, conv_w, conv_b, w_a, b_a, w_x, b_x, lam, h0, conv_state, *, tt):
    b, t, _ = proj.shape
    w = LRU_WIDTH
    tok = lambda col: pl.BlockSpec((1, tt, w), lambda i, j: (i, j, col))
    full = lambda a: pl.BlockSpec(a.shape, lambda i, j: (0,) * a.ndim)
    row = lambda v: v.reshape(1, w)
    state = pl.BlockSpec((1, CONV_W - 1, w), lambda i, j: (i, 0, 0))
    hspec = pl.BlockSpec((1, 1, w), lambda i, j: (i, 0, 0))
    args = (proj, proj, conv_w, row(conv_b), w_a, w_x, row(b_a), row(b_x), row(lam), h0, conv_state)
    return pl.pallas_call(
        _lru_kernel,
        grid=(b, t // tt),
        in_specs=[tok(0), tok(1)] + [full(a) for a in args[2:9]] + [hspec, state],
        out_specs=[pl.BlockSpec((1, tt, w), lambda i, j: (i, j, 0)), hspec, state],
        out_shape=[jax.ShapeDtypeStruct((b, t, w), BF16),
                   jax.ShapeDtypeStruct((b, 1, w), F32),
                   jax.ShapeDtypeStruct((b, CONV_W - 1, w), F32)],
        scratch_shapes=[pltpu.VMEM((tt + SUBLANES, w), F32), pltpu.VMEM((tt, w), F32),
                        pltpu.VMEM((tt, w), F32), pltpu.VMEM((1, w), F32)],
        compiler_params=_params(2),
        name="rg_lru",
    )(*args)


def _rope_tables(pos, dim):
    half = dim // 2
    inv = np.power(ROPE_THETA, -np.arange(half, dtype=np.float64) / half)
    ang = pos.astype(np.float64)[:, None] * inv[None, :]
    cos = np.concatenate([np.cos(ang), np.cos(ang)], axis=-1)
    sin = np.concatenate([-np.sin(ang), np.sin(ang)], axis=-1)
    reps = LANES // dim
    return (jnp.asarray(np.tile(cos, (1, reps)), F32), jnp.asarray(np.tile(sin, (1, reps)), F32))


def _head_expansion():
    e = np.zeros((LANES, SSD_INNER), np.float32)
    for h in range(SSD_HEADS):
        e[SM_DT + h, h * SSD_HEADDIM:(h + 1) * SSD_HEADDIM] = 1.0
    return jnp.asarray(e)


def _pad_lanes(v, offset):
    out = jnp.zeros((1, LANES), F32)
    return out.at[0, offset:offset + v.shape[0]].set(v)


def _w_in_ab_tail(w):
    whole = AB_WIDTH // AB_TILE_N * AB_TILE_N
    return jnp.pad(w[:, whole:], ((0, 0), (0, AB_PAD_N - AB_WIDTH)))


def _trunk(x, mods, past_k, past_v, past_ki, ssm_h, ssm_conv, lru_h, lru_conv, p, *, cfg):
    b, t, d = x.shape
    bb, tt = cfg["bb"], cfg["tt"]
    past = 0 if past_k is None else past_k.shape[1]
    pos = np.arange(past, past + t)
    tabs = _rope_tables(pos, HEAD_DIM) + _rope_tables(pos, IDX_DIM)

    def ffn(x, l):
        sh_f, sc_f, g_f = mods[l][3], mods[l][4], mods[l][5]
        gu = _norm_mod_matmul(x, p["norm_ffn_w"][l], sc_f, sh_f, [p["ffn_w_gate"], p["ffn_w_up"]], l,
                              bb=bb, tt=cfg["tt_ffn"], tn=512, out_dtype=BF16, name="ffn_gate_up")
        return _matmul_residual([gu], p["ffn_w_down"], l, x, g_f, bb=bb, tt=cfg["tt_ffn"], tn=256,
                                name="ffn_down")

    sh_m, sc_m, g_m = mods[0][0], mods[0][1], mods[0][2]
    proj = _norm_mod_matmul(x, p["norm_mix_w"][0], sc_m, sh_m, [p["w_in_ab"]], 0, bb=bb,
                            tt=cfg["tt_ab"], tn=AB_TILE_N, out_dtype=F32, name="in_proj_ab",
                            w_tail=p["w_in_ab_tail"])
    (q, qi, k, v, ki, sm, xbc_act, ssm_conv_new, kb, vbe, kib) = _post0(
        proj, tabs, p["q_norm_w"][0], p["k_norm_w"][0], p["ssd_conv_w"][0], p["ssd_conv_b"][0],
        p["dt_bias_pad"], ssm_conv, tt=cfg["post_tt"])
    cache = None if past_k is None else (past_k, past_v, past_ki)
    att = _dsa_attention(q, qi, sm, kb, vbe, kib, cache, qb=cfg["qb"], tk=cfg["tk"])
    y_ssd, ssm_new = _ssd(xbc_act, proj, sm, ssm_h, p["a_log_pad"], p["d_exp"], p["ssd_norm_w"][0],
                          p["e_mat"], chunk=cfg["ssd_chunk"])
    x = _matmul_residual([att, y_ssd], p["w_out_ab"], 0, x, g_m, bb=bb, tt=tt, tn=512,
                         name="out_proj_ab")
    x = ffn(x, 0)

    sh_m, sc_m, g_m = mods[1][0], mods[1][1], mods[1][2]
    proj = _norm_mod_matmul(x, p["norm_mix_w"][1], sc_m, sh_m, [p["w_in_c"]], 0,
                            bb=bb, tt=tt, tn=cfg["tn_c"], out_dtype=F32, name="in_proj_c")
    y_lru, lru_new, lru_conv_new = _lru(
        proj, p["lru_conv_w"][0], p["lru_conv_b"][0], p["lru_w_a"][0], p["lru_b_a"][0],
        p["lru_w_x"][0], p["lru_b_x"][0], p["lru_lambda"][0], lru_h.reshape(b, 1, LRU_WIDTH),
        lru_conv, tt=cfg["lru_tt"])
    x = _matmul_residual([y_lru], p["w_out_c"], 0, x, g_m, bb=bb, tt=tt, tn=512, name="out_proj_c")
    x = ffn(x, 1)

    kv_shape = (1, b, t, KV_HEADS, HEAD_DIM)
    return (x, k.reshape(kv_shape), v.reshape(kv_shape), ki[None], ssm_new[None], ssm_conv_new[None],
            lru_new.reshape(1, b, LRU_WIDTH), lru_conv_new[None])


PROMPT_CFG = dict(bb=1, tt=2048, tt_ab=1024, tt_ffn=1024, tn_c=256, post_tt=256, qb=128, tk=512,
                  ssd_chunk=128, lru_tt=256)
SAMPLE_CFG = dict(bb=16, tt=32, tt_ab=32, tt_ffn=32, tn_c=512, post_tt=32, qb=32, tk=384,
                  ssd_chunk=32, lru_tt=32)


def kernel(x_prompt, x_sample, cache_attn_k, cache_attn_v, cache_idx_k, state_ssm, state_ssm_conv,
           state_lru, state_lru_conv, c_prompt, c_sample, ada_w, ada_b, norm_mix_w, norm_ffn_w,
           w_in_ab, q_norm_w, k_norm_w, ssd_conv_w, ssd_conv_b, ssd_dt_bias, ssd_a_log, ssd_d,
           ssd_norm_w, w_out_ab, w_in_c, lru_conv_w, lru_conv_b, lru_w_a, lru_b_a, lru_w_x, lru_b_x,
           lru_lambda, w_out_c, ffn_w_gate, ffn_w_up, ffn_w_down):
    bp, bs = x_prompt.shape[0], x_sample.shape[0]
    p = dict(norm_mix_w=norm_mix_w, norm_ffn_w=norm_ffn_w, q_norm_w=q_norm_w, k_norm_w=k_norm_w,
             ssd_conv_w=ssd_conv_w, ssd_conv_b=ssd_conv_b, ssd_norm_w=ssd_norm_w, w_out_ab=w_out_ab,
             w_in_c=w_in_c, lru_conv_w=lru_conv_w, lru_conv_b=lru_conv_b, lru_w_a=lru_w_a,
             lru_b_a=lru_b_a, lru_w_x=lru_w_x, lru_b_x=lru_b_x, lru_lambda=lru_lambda,
             w_out_c=w_out_c, ffn_w_gate=ffn_w_gate, ffn_w_up=ffn_w_up, ffn_w_down=ffn_w_down)
    p["w_in_ab"] = w_in_ab
    p["w_in_ab_tail"] = _w_in_ab_tail(w_in_ab[0])
    p["dt_bias_pad"] = _pad_lanes(ssd_dt_bias[0], SM_DT)
    p["a_log_pad"] = _pad_lanes(ssd_a_log[0], SM_DT)
    p["d_exp"] = jnp.repeat(ssd_d[0], SSD_HEADDIM).reshape(1, SSD_INNER)
    p["e_mat"] = _head_expansion()

    m_rows = 32
    c_all = jnp.concatenate([c_prompt, c_sample, jnp.zeros((m_rows - bp - bs, D_MODEL), F32)], axis=0)
    mod = _modulation(c_all, ada_w, ada_b)

    def group_mods(r0, nb):
        return [[mod[l, r0:r0 + nb, i * D_MODEL:(i + 1) * D_MODEL].reshape(nb, 1, D_MODEL)
                 for i in range(6)] for l in range(mod.shape[0])]

    zeros = lambda *s: jnp.zeros(s, F32)
    out_p = _trunk(x_prompt, group_mods(0, bp), None, None, None,
                   zeros(bp, SSD_HEADS, SSD_HEADDIM, SSD_STATE), zeros(bp, CONV_W - 1, SSD_CONV_DIM),
                   zeros(bp, LRU_WIDTH), zeros(bp, CONV_W - 1, LRU_WIDTH), p, cfg=PROMPT_CFG)
    out_s = _trunk(x_sample, group_mods(bp, bs), cache_attn_k[0], cache_attn_v[0], cache_idx_k[0],
                   state_ssm[0], state_ssm_conv[0], state_lru[0], state_lru_conv[0], p, cfg=SAMPLE_CFG)
    return (out_p[0], out_s[0]) + out_p[1:] + out_s[1:]
```

```python
import functools

import numpy as np
import jax
import jax.numpy as jnp
from jax import lax
from jax.experimental import pallas as pl
from jax.experimental.pallas import tpu as pltpu

F32 = jnp.float32
BF16 = jnp.bfloat16
HIGHEST = lax.Precision.HIGHEST

D_MODEL = 2048
CHUNK = 64
CHUNK_SHIFT = 6
ATT_HEADS = 8
KV_HEADS = 2
HEAD_DIM = 128
IDX_HEADS = 16
IDX_DIM = 64
TOPK_MAX = 256
ROPE_THETA = 10000.0
SSD_HEADS = 16
SSD_HEADDIM = 64
SSD_GROUPS = 2
SSD_STATE = 128
SSD_INNER = SSD_HEADS * SSD_HEADDIM
SSD_CONV_DIM = SSD_INNER + 2 * SSD_GROUPS * SSD_STATE
CONV_W = 4
LRU_WIDTH = D_MODEL
LRU_BLOCKS = 16
LRU_BLOCK_DIM = LRU_WIDTH // LRU_BLOCKS
LRU_C = 8.0
EPS = 1e-6
ATT_WIDTH = ATT_HEADS * HEAD_DIM
KV_WIDTH = KV_HEADS * HEAD_DIM
VEXT_WIDTH = 2 * KV_WIDTH
QI_WIDTH = IDX_HEADS * IDX_DIM
Q_SCALE = HEAD_DIM ** -0.5 * float(np.log2(np.e))

LANES = 128
SUBLANES = 8
VMEM_LIMIT_BYTES = 56 * 1024 * 1024
SINGLE_BUFFER_ROWS = 2048

OFF_Q = 0
OFF_K = OFF_Q + ATT_WIDTH
OFF_V = OFF_K + KV_WIDTH
OFF_QI = OFF_V + KV_WIDTH
OFF_SMALL = OFF_QI + QI_WIDTH
SM_WI = IDX_DIM
SM_DT = IDX_DIM + IDX_HEADS
OFF_Z = OFF_SMALL + SM_DT
OFF_XBC = OFF_Z + SSD_INNER
OFF_DT = OFF_XBC + SSD_CONV_DIM
AB_WIDTH = OFF_DT + SSD_HEADS
Z_BLOCK = 1280
AB_TILE_N = 768
AB_PAD_N = 5376
assert OFF_Z % LANES == SM_DT and OFF_XBC % LANES == SM_DT and OFF_DT % LANES == SM_DT
assert (OFF_Z - SM_DT) % Z_BLOCK == 0 and OFF_Z - SM_DT + Z_BLOCK >= OFF_Z + SSD_INNER

NEG_BIG = -1e30


def _params(n_axes):
    return pltpu.CompilerParams(dimension_semantics=("arbitrary",) * n_axes,
                                vmem_limit_bytes=VMEM_LIMIT_BYTES)


def _silu(x):
    return x * jax.nn.sigmoid(x)


def _softplus(x):
    return jnp.maximum(x, 0.0) + jnp.log1p(jnp.exp(-jnp.abs(x)))


def _mod_kernel(c_ref, w_ref, b_ref, o_ref):
    a = _silu(c_ref[...]).astype(BF16)
    o_ref[0] = jnp.dot(a, w_ref[0].astype(BF16), preferred_element_type=F32) + b_ref[0]


def _modulation(c_all, ada_w, ada_b):
    depth, d, n = ada_w.shape
    m = c_all.shape[0]
    tn = 1024
    return pl.pallas_call(
        _mod_kernel,
        grid=(depth, n // tn),
        in_specs=[pl.BlockSpec((m, d), lambda l, j: (0, 0)),
                  pl.BlockSpec((1, d, tn), lambda l, j: (l, 0, j)),
                  pl.BlockSpec((1, 1, tn), lambda l, j: (l, 0, j))],
        out_specs=pl.BlockSpec((1, m, tn), lambda l, j: (l, 0, j)),
        out_shape=jax.ShapeDtypeStruct((depth, m, n), F32),
        compiler_params=_params(2),
        name="adaln_mod",
    )(c_all, ada_w, ada_b.reshape(depth, 1, n))


def _nmm_kernel(x_ref, nw_ref, sc_ref, sh_ref, *rest, swiglu, tail, w_rows_out):
    n_w = 2 if swiglu or tail else 1
    w_refs, o_ref, h_ref = rest[:n_w], rest[n_w], rest[n_w + 1]
    bb, tt, d = x_ref.shape

    @pl.when(pl.program_id(2) == 0)
    def _():
        x = x_ref[...]
        ms = jnp.mean(x * x, axis=-1, keepdims=True)
        y = x * lax.rsqrt(ms + EPS) * nw_ref[...]
        h = y * (1.0 + sc_ref[...]) + sh_ref[...]
        h_ref[...] = h.reshape(bb * tt, d).astype(BF16)

    h = h_ref[...]

    def project(w_ref):
        dims = (((1,), (1,)), ((), ())) if w_rows_out else (((1,), (0,)), ((), ()))
        o = lax.dot_general(h, w_ref[...].astype(BF16), dims, preferred_element_type=F32)
        o_ref[...] = o.reshape(o_ref.shape).astype(o_ref.dtype)

    if swiglu:
        g = jnp.dot(h, w_refs[0][...].astype(BF16), preferred_element_type=F32)
        u = jnp.dot(h, w_refs[1][...].astype(BF16), preferred_element_type=F32)
        o_ref[...] = (_silu(g) * u).reshape(o_ref.shape).astype(o_ref.dtype)
    elif tail:
        last = pl.num_programs(2) - 1
        pl.when(pl.program_id(2) < last)(lambda: project(w_refs[0]))
        pl.when(pl.program_id(2) == last)(lambda: project(w_refs[1]))
    else:
        project(w_refs[0])


def _row_block_mode(rows):
    return dict(pipeline_mode=pl.Buffered(1)) if rows >= SINGLE_BUFFER_ROWS else {}


def _norm_mod_matmul(x, nw, sc, sh, ws, layer, *, bb, tt, tn, out_dtype, name, w_tail=None,
                     w_rows_out=False):
    b, t, d = x.shape
    swiglu = len(ws) == 2
    n_main = ws[0].shape[1 if w_rows_out else 2] // tn
    n = (n_main + (w_tail is not None)) * tn
    xmap = lambda i, j, k: (i, j, 0)
    mmap = lambda i, j, k: (i, 0, 0)
    tile = lambda k: jnp.minimum(k, n_main - 1)
    if w_rows_out:
        w_specs = [pl.BlockSpec((None, tn, d), lambda i, j, k: (layer, tile(k), 0))] * len(ws)
    else:
        w_specs = [pl.BlockSpec((None, d, tn), lambda i, j, k: (layer, 0, tile(k)))] * len(ws)
    operands = list(ws)
    if w_tail is not None:
        w_specs.append(pl.BlockSpec(w_tail.shape, lambda i, j, k: (0, 0)))
        operands.append(w_tail)
    return pl.pallas_call(
        functools.partial(_nmm_kernel, swiglu=swiglu, tail=w_tail is not None,
                          w_rows_out=w_rows_out),
        grid=(b // bb, t // tt, n // tn),
        in_specs=[pl.BlockSpec((bb, tt, d), xmap, **_row_block_mode(bb * tt)),
                  pl.BlockSpec((1, 1, d), lambda i, j, k: (0, 0, 0)),
                  pl.BlockSpec((bb, 1, d), mmap),
                  pl.BlockSpec((bb, 1, d), mmap)] + w_specs,
        out_specs=pl.BlockSpec((bb, tt, tn), lambda i, j, k: (i, j, k)),
        out_shape=jax.ShapeDtypeStruct((b, t, n), out_dtype),
        scratch_shapes=[pltpu.VMEM((bb * tt, d), BF16)],
        compiler_params=_params(3),
        name=name,
    )(x, nw.reshape(1, 1, d), sc, sh, *operands)


def _mmr_kernel(*refs, k_sizes):
    n_a = len(k_sizes)
    a_refs = refs[:n_a]
    w_ref, x_ref, g_ref, o_ref = refs[n_a:]
    bb, tt, tn = o_ref.shape
    acc = jnp.zeros((bb * tt, tn), F32)
    off = 0
    for a_ref, ks in zip(a_refs, k_sizes):
        a = a_ref[...].reshape(bb * tt, ks)
        acc = acc + jnp.dot(a, w_ref[off:off + ks, :].astype(BF16), preferred_element_type=F32)
        off += ks
    o_ref[...] = x_ref[...] + g_ref[...] * acc.reshape(bb, tt, tn)


def _matmul_residual(a_list, w, layer, x, g, *, bb, tt, tn, name):
    b, t, d = x.shape
    k_sizes = tuple(a.shape[-1] for a in a_list)
    k_total = sum(k_sizes)
    return pl.pallas_call(
        functools.partial(_mmr_kernel, k_sizes=k_sizes),
        grid=(b // bb, t // tt, d // tn),
        in_specs=[pl.BlockSpec((bb, tt, ks), lambda i, j, k: (i, j, 0), **_row_block_mode(bb * tt))
                  for ks in k_sizes]
                 + [pl.BlockSpec((None, k_total, tn), lambda i, j, k: (layer, 0, k)),
                    pl.BlockSpec((bb, tt, tn), lambda i, j, k: (i, j, k)),
                    pl.BlockSpec((bb, 1, tn), lambda i, j, k: (i, 0, k))],
        out_specs=pl.BlockSpec((bb, tt, tn), lambda i, j, k: (i, j, k)),
        out_shape=jax.ShapeDtypeStruct((b, t, d), F32),
        compiler_params=_params(3),
        name=name,
    )(*a_list, w, x, g)


def _swap_halves(x, half):
    w = x.shape[-1]
    lane = lax.broadcasted_iota(jnp.int32, x.shape, x.ndim - 1)
    first = (lane & half) == 0
    return jnp.where(first, pltpu.roll(x, w - half, x.ndim - 1), pltpu.roll(x, half, x.ndim - 1))


def _lane_aligned(slab, offset, width):
    return pltpu.roll(slab, slab.shape[-1] - offset, slab.ndim - 1)[:, :width]


def _rope(x, cos, sin_signed, half):
    reps = x.shape[-1] // cos.shape[-1]
    if reps > 1:
        cos = jnp.concatenate([cos] * reps, axis=-1)
        sin_signed = jnp.concatenate([sin_signed] * reps, axis=-1)
    return x * cos + _swap_halves(x, half) * sin_signed


def _causal_conv(u, state_ref, w_ref, b_ref, cbuf_ref, new_ref, first, last):
    tt = u.shape[0]

    @pl.when(first)
    def _():
        cbuf_ref[0:SUBLANES, :] = jnp.zeros((SUBLANES, u.shape[1]), F32)
        cbuf_ref[SUBLANES - (CONV_W - 1):SUBLANES, :] = state_ref[0]

    cbuf_ref[SUBLANES:SUBLANES + tt, :] = u
    full = cbuf_ref[...]
    out = b_ref[...]
    for j in range(CONV_W):
        shift = CONV_W - 1 - j
        rows = pltpu.roll(full, shift, 0) if shift else full
        out = out + rows[SUBLANES:SUBLANES + tt, :] * w_ref[j:j + 1, :]
    tail = cbuf_ref[tt:tt + SUBLANES, :]
    cbuf_ref[0:SUBLANES, :] = tail

    @pl.when(last)
    def _():
        new_ref[0] = tail[SUBLANES - (CONV_W - 1):, :]

    return out


def _post0_kernel(p_ref, cq_ref, sq_ref, ci_ref, si_ref, qn_ref, kn_ref, cw_ref, cb_ref, dtb_ref,
                  cs_ref,
                  q_ref, qi_ref, k_ref, v_ref, ki_ref, sm_ref, xbc_ref, cnew_ref, kb_ref, vb_ref,
                  kib_ref, cbuf_ref):
    t = pl.program_id(1)
    nt = pl.num_programs(1)
    cq, sq, ci, si = cq_ref[...], sq_ref[...], ci_ref[...], si_ref[...]

    def head_norm_rope(x, w):
        ms = jnp.mean(x * x, axis=-1, keepdims=True)
        return _rope(x * lax.rsqrt(ms + EPS) * w, cq, sq, HEAD_DIM // 2)

    for h in range(ATT_HEADS):
        xh = p_ref[0, :, OFF_Q + h * HEAD_DIM:OFF_Q + (h + 1) * HEAD_DIM]
        q_ref[0, :, h * HEAD_DIM:(h + 1) * HEAD_DIM] = (head_norm_rope(xh, qn_ref[...])
                                                         * Q_SCALE).astype(BF16)
    ones = jnp.ones((p_ref.shape[1], HEAD_DIM), BF16)
    for h in range(KV_HEADS):
        xh = p_ref[0, :, OFF_K + h * HEAD_DIM:OFF_K + (h + 1) * HEAD_DIM]
        kh = head_norm_rope(xh, kn_ref[...])
        vh = p_ref[0, :, OFF_V + h * HEAD_DIM:OFF_V + (h + 1) * HEAD_DIM]
        k_ref[0, pl.ds(h, xh.shape[0], stride=KV_HEADS), :] = kh
        v_ref[0, pl.ds(h, xh.shape[0], stride=KV_HEADS), :] = vh
        kb_ref[0, :, h * HEAD_DIM:(h + 1) * HEAD_DIM] = kh.astype(BF16)
        vb_ref[0, :, 2 * h * HEAD_DIM:(2 * h + 1) * HEAD_DIM] = vh.astype(BF16)
        vb_ref[0, :, (2 * h + 1) * HEAD_DIM:(2 * h + 2) * HEAD_DIM] = ones

    qi = p_ref[0, :, OFF_QI:OFF_QI + QI_WIDTH]
    qi_ref[0] = _rope(qi, ci, si, IDX_DIM // 2).astype(BF16)

    sm = p_ref[0, :, OFF_SMALL:OFF_SMALL + LANES]
    lane = lax.broadcasted_iota(jnp.int32, sm.shape, 1)
    ki_part = _rope(sm, ci, si, IDX_DIM // 2)
    wi_part = sm * (IDX_HEADS ** -0.5 * IDX_DIM ** -0.5)
    dt_col = OFF_DT - SM_DT
    dt_part = _softplus(p_ref[0, :, dt_col:dt_col + LANES] + dtb_ref[...])
    sm_out = jnp.where(lane < SM_WI, ki_part,
                       jnp.where(lane < SM_DT, wi_part,
                                 jnp.where(lane < SM_DT + SSD_HEADS, dt_part, 0.0)))
    sm_ref[0] = sm_out
    ki_ref[0] = sm_out[:, :IDX_DIM]
    kib_ref[0] = sm_out[:, :IDX_DIM].astype(BF16)

    xbc = _lane_aligned(p_ref[0, :, OFF_XBC - SM_DT:OFF_DT - SM_DT + LANES], SM_DT, SSD_CONV_DIM)
    conv = _causal_conv(xbc, cs_ref, cw_ref, cb_ref, cbuf_ref, cnew_ref, t == 0, t == nt - 1)
    xbc_ref[0] = _silu(conv)


def _post0(proj, tabs, q_norm_w, k_norm_w, conv_w, conv_b, dt_bias_pad, conv_state, *, tt):
    b, t, n = proj.shape
    tok = lambda w: pl.BlockSpec((1, tt, w), lambda i, j: (i, j, 0))
    tab = pl.BlockSpec((tt, LANES), lambda i, j: (j, 0))
    full2 = lambda a: pl.BlockSpec(a.shape, lambda i, j: (0, 0))
    state = lambda c: pl.BlockSpec((1, CONV_W - 1, c), lambda i, j: (i, 0, 0))
    out_shapes = [
        jax.ShapeDtypeStruct((b, t, ATT_WIDTH), BF16),
        jax.ShapeDtypeStruct((b, t, QI_WIDTH), BF16),
        jax.ShapeDtypeStruct((b, t * KV_HEADS, HEAD_DIM), F32),
        jax.ShapeDtypeStruct((b, t * KV_HEADS, HEAD_DIM), F32),
        jax.ShapeDtypeStruct((b, t, IDX_DIM), F32),
        jax.ShapeDtypeStruct((b, t, LANES), F32),
        jax.ShapeDtypeStruct((b, t, SSD_CONV_DIM), F32),
        jax.ShapeDtypeStruct((b, CONV_W - 1, SSD_CONV_DIM), F32),
        jax.ShapeDtypeStruct((b, t, KV_WIDTH), BF16),
        jax.ShapeDtypeStruct((b, t, VEXT_WIDTH), BF16),
        jax.ShapeDtypeStruct((b, t, IDX_DIM), BF16),
    ]
    kv_rows = pl.BlockSpec((1, tt * KV_HEADS, HEAD_DIM), lambda i, j: (i, j, 0))
    out_specs = [tok(ATT_WIDTH), tok(QI_WIDTH), kv_rows, kv_rows, tok(IDX_DIM),
                 tok(LANES), tok(SSD_CONV_DIM), state(SSD_CONV_DIM), tok(KV_WIDTH), tok(VEXT_WIDTH),
                 tok(IDX_DIM)]
    qn = q_norm_w.reshape(1, HEAD_DIM)
    kn = k_norm_w.reshape(1, HEAD_DIM)
    cb = conv_b.reshape(1, SSD_CONV_DIM)
    return pl.pallas_call(
        _post0_kernel,
        grid=(b, t // tt),
        in_specs=[tok(n), tab, tab, tab, tab, full2(qn), full2(kn), full2(conv_w), full2(cb),
                  full2(dt_bias_pad), state(SSD_CONV_DIM)],
        out_specs=out_specs,
        out_shape=out_shapes,
        scratch_shapes=[pltpu.VMEM((tt + SUBLANES, SSD_CONV_DIM), F32)],
        compiler_params=_params(2),
        name="post_proj0",
    )(proj, *tabs, qn, kn, conv_w, cb, dt_bias_pad, conv_state)


def _dsa_kernel(q_ref, qi_ref, sm_ref, k_ref, v_ref, ki_ref, *rest, qb, tk, n_keys, past, topk,
                n_bisect, keys_on_sublanes):
    if past:
        pk_ref, pv_ref, pki_ref = rest[:3]
        rest = rest[3:]
    (o_ref, sc_ref, bias_ref, acc_ref, qs_ref, s0_ref, s1_ref, mt0_ref, mt1_ref, m_ref) = rest[:10]
    if past:
        kall_ref, vall_ref, kiall_ref = rest[10:]
        t_new = k_ref.shape[1]
        pad = kall_ref.shape[0] - n_keys
        for g in range(KV_HEADS):
            head_rows = pl.ds(g, past, stride=KV_HEADS)
            kall_ref[0:past, g * HEAD_DIM:(g + 1) * HEAD_DIM] = pk_ref[0, head_rows, :].astype(BF16)
            vall_ref[0:past, 2 * g * HEAD_DIM:(2 * g + 1) * HEAD_DIM] = pv_ref[0, head_rows, :].astype(BF16)
            vall_ref[0:past, (2 * g + 1) * HEAD_DIM:(2 * g + 2) * HEAD_DIM] = jnp.ones(
                (past, HEAD_DIM), BF16)
        kiall_ref[0:past, :] = pki_ref[0].astype(BF16)
        for dst, src in ((kall_ref, k_ref), (vall_ref, v_ref), (kiall_ref, ki_ref)):
            dst[past:past + t_new, :] = src[0]
            dst[past + t_new:, :] = jnp.zeros((pad, dst.shape[1]), BF16)
        k_src, v_src, ki_src = kall_ref, vall_ref, kiall_ref
    else:
        k_src, v_src, ki_src = k_ref.at[0], v_ref.at[0], ki_ref.at[0]
    kax = 0 if keys_on_sublanes else 1
    tile_shape = (tk, qb) if keys_on_sublanes else (qb, tk)
    vec_shape = (1, qb) if keys_on_sublanes else (qb, 1)
    n_acc = 8
    part_shape = (n_acc, SUBLANES, qb) if keys_on_sublanes else (qb, LANES)
    j = pl.program_id(1)
    pos0 = past + j * qb
    q_chunk = lax.shift_right_logical(pos0 + lax.broadcasted_iota(jnp.int32, vec_shape, 1 - kax),
                                      CHUNK_SHIFT)
    n_valid = jnp.minimum((q_chunk + 1) * CHUNK, n_keys)
    max_valid = jnp.minimum(((pos0 + qb - 1) // CHUNK + 1) * CHUNK, n_keys)
    nkt = (max_valid + tk - 1) // tk
    kf = float(topk)
    active_f = jnp.where(n_valid > topk, 1.0, 0.0)

    def key_index(kt):
        return kt * tk + lax.broadcasted_iota(jnp.int32, tile_shape, kax)

    nt_dims = (((1,), (1,)), ((), ()))
    for h in range(IDX_HEADS):
        qs_ref[h * qb:(h + 1) * qb, :] = qi_ref[0, :, h * IDX_DIM:(h + 1) * IDX_DIM]
    if keys_on_sublanes:
        wi_t = sm_ref[0].T
        head_w = lambda h: wi_t[SM_WI + h:SM_WI + h + 1, :]
    else:
        wis = sm_ref[0][:, SM_WI:SM_WI + IDX_HEADS]
        head_w = lambda h: wis[:, h:h + 1]

    def score_tile(kt, carry):
        kit = ki_src[pl.ds(pl.multiple_of(kt * tk, tk), tk), :]
        acc = jnp.zeros(tile_shape, F32)
        if keys_on_sublanes:
            for pr in range(IDX_HEADS // 2):
                s2 = lax.dot_general(kit, qs_ref[2 * pr * qb:(2 * pr + 2) * qb, :], nt_dims,
                                     preferred_element_type=F32)
                for e in range(2):
                    acc = acc + head_w(2 * pr + e) * jnp.maximum(s2[:, e * qb:(e + 1) * qb], 0.0)
        else:
            s_all = lax.dot_general(qs_ref[...], kit, nt_dims, preferred_element_type=F32)
            for h in range(IDX_HEADS):
                acc = acc + head_w(h) * jnp.maximum(s_all[h * qb:(h + 1) * qb, :], 0.0)
        sc_ref[kt] = jnp.where(key_index(kt) < n_valid, acc, -jnp.inf)
        return carry

    lax.fori_loop(0, nkt, score_tile, 0)

    def fold(m, op2, red):
        if keys_on_sublanes:
            return red(m.reshape(tk // (n_acc * SUBLANES), n_acc, SUBLANES, qb), axis=0)
        f = m[:, 0:LANES]
        for c in range(1, tk // LANES):
            f = op2(f, m[:, c * LANES:(c + 1) * LANES])
        return f

    def reduce_tiles(make, op2, red, init):
        def body(kt, part):
            return op2(part, fold(make(sc_ref[kt], kt), op2, red))
        part = lax.fori_loop(0, nkt, body, jnp.full(part_shape, init, F32))
        if keys_on_sublanes:
            part = red(part, axis=0)
        return red(part, axis=kax, keepdims=True)

    def count(pred):
        return reduce_tiles(lambda t, kt: jnp.where(pred(t, kt), 1.0, 0.0), jnp.add, jnp.sum, 0.0)

    def row_max(pred):
        return reduce_tiles(lambda t, kt: jnp.where(pred(t, kt), t, -jnp.inf), jnp.maximum, jnp.max,
                            -jnp.inf)

    def row_min_valid():
        return reduce_tiles(lambda t, kt: jnp.where(key_index(kt) < n_valid, t, jnp.inf), jnp.minimum,
                            jnp.min, jnp.inf)

    def any_set(flag_f):
        return jnp.max(flag_f) > 0.0

    def bisect(_, state):
        lo, hi, c_lo = state
        mid = lo + (hi - lo) * 0.5
        c = count(lambda t, kt: t >= mid)
        ge = c >= kf
        return jnp.where(ge, mid, lo), jnp.where(ge, hi, mid), jnp.where(ge, c, c_lo)

    lo0 = row_min_valid()
    hi0 = row_max(lambda t, kt: t == t)
    lo, hi, c_lo = lax.fori_loop(0, n_bisect, bisect, (lo0, hi0, n_valid.astype(F32)))
    found_f = jnp.where(c_lo == kf, active_f, 0.0)
    v_found = reduce_tiles(lambda t, kt: jnp.where(t >= lo, t, jnp.inf), jnp.minimum, jnp.min, jnp.inf)

    def walk_down():
        w0 = row_max(lambda t, kt: t <= hi)
        c0 = count(lambda t, kt: t >= w0)

        def walk_cond(state):
            w, c = state
            return any_set(jnp.where(c < kf, active_f, 0.0))

        def walk_body(state):
            w, c = state
            w2 = row_max(lambda t, kt: t < w)
            c2 = count(lambda t, kt: t >= w2)
            upd = c < kf
            return jnp.where(upd, w2, w), jnp.where(upd, c2, c)

        return lax.while_loop(walk_cond, walk_body,
                              (jnp.where(found_f > 0.5, v_found, w0),
                               jnp.where(found_f > 0.5, kf, c0)))

    w, c_ge = lax.cond(any_set(active_f - found_f), walk_down,
                       lambda: (v_found, jnp.full(vec_shape, kf, F32)))
    thr = jnp.where(active_f > 0.5, w, -jnp.inf)
    tied_f = jnp.where(c_ge > kf, active_f, 0.0)

    def write_bias(sel_fn):
        def body(kt, carry):
            kidx = key_index(kt)
            sel = sel_fn(sc_ref[kt], kidx) & (kidx < n_valid)
            bias = jnp.where(sel, 0.0, NEG_BIG)
            bias_ref[kt] = bias.T if keys_on_sublanes else bias
            return carry
        lax.fori_loop(0, nkt, body, 0)

    def no_ties():
        write_bias(lambda t, kidx: t >= thr)

    def with_ties():
        need = kf - count(lambda t, kt: t > thr)
        n_steps = max(1, int(np.ceil(np.log2(sc_ref.shape[0] * tk))) + 1)

        def step(_, lh):
            lo_i, hi_i = lh
            mid = lax.shift_right_arithmetic(lo_i + hi_i, 1)
            ge = count(lambda t, kt: (t == thr) & (key_index(kt) <= mid)) >= need
            return jnp.where(ge, lo_i, mid), jnp.where(ge, mid, hi_i)

        lo_i = jnp.full(vec_shape, -1, jnp.int32)
        hi_i = jnp.full(vec_shape, sc_ref.shape[0] * tk - 1, jnp.int32)
        _, last_tie = lax.fori_loop(0, n_steps, step, (lo_i, hi_i))
        write_bias(lambda t, kidx: (t > thr) | ((t == thr)
                                                & ((kidx <= last_tie) | (tied_f < 0.5))))

    any_tie = jnp.max(tied_f) > 0.0
    lax.cond(any_tie, with_ties, no_ties)

    rep = ATT_HEADS // KV_HEADS
    qgs = [jnp.concatenate(
        [q_ref[0, :, (g * rep + r) * HEAD_DIM:(g * rep + r + 1) * HEAD_DIM] for r in range(rep)],
        axis=0) for g in range(KV_HEADS)]
    acc_ref[...] = jnp.zeros(acc_ref.shape, F32)
    bufs = ((s0_ref, mt0_ref), (s1_ref, mt1_ref))

    def qk_tile(kt, buf):
        s_ref, mt_ref = bufs[buf]
        row0 = pl.multiple_of(kt * tk, tk)
        bias = bias_ref[kt]
        for g in range(KV_HEADS):
            kg = k_src[pl.ds(row0, tk), g * HEAD_DIM:(g + 1) * HEAD_DIM]
            s = lax.dot_general(qgs[g], kg, nt_dims, preferred_element_type=F32)
            s = (s.reshape(rep, qb, tk) + bias[None]).reshape(rep * qb, tk)
            s_ref[g] = s
            mt_ref[g] = jnp.broadcast_to(jnp.max(s, axis=-1, keepdims=True), (rep * qb, LANES))

    def lane_tile(x, width):
        return jnp.concatenate([x] * (width // LANES), axis=1)

    def att_tile(kt, buf, prefetch):
        if prefetch:
            qk_tile(kt + 1, 1 - buf)
        s_ref, mt_ref = bufs[buf]
        row0 = pl.multiple_of(kt * tk, tk)
        for g in range(KV_HEADS):
            vg = v_src[pl.ds(row0, tk), 2 * g * HEAD_DIM:(2 * g + 2) * HEAD_DIM]
            m_old = m_ref[g]
            m_new = jnp.maximum(m_old, mt_ref[g])
            m_ref[g] = m_new
            alpha = jnp.exp2(m_old - m_new)
            p = jnp.exp2(s_ref[g] - lane_tile(m_new, tk)).astype(BF16)
            acc_ref[g] = (lane_tile(alpha, 2 * HEAD_DIM) * acc_ref[g]
                          + jnp.dot(p, vg, preferred_element_type=F32))

    def tile_pair(pi, carry):
        att_tile(2 * pi, 0, True)
        att_tile(2 * pi + 1, 1, True)
        return carry

    def last_two():
        att_tile(nkt - 2, 0, True)
        att_tile(nkt - 1, 1, False)

    def last_one():
        att_tile(nkt - 1, 0, False)

    m_ref[...] = jnp.full(m_ref.shape, NEG_BIG, F32)
    qk_tile(0, 0)
    lax.fori_loop(0, (nkt - 1) // 2, tile_pair, 0)
    lax.cond((nkt & 1) == 0, last_two, last_one)
    for g in range(KV_HEADS):
        acc = acc_ref[g]
        o = acc[:, :HEAD_DIM] / acc[:, HEAD_DIM:]
        for r in range(rep):
            h = g * rep + r
            o_ref[0, :, h * HEAD_DIM:(h + 1) * HEAD_DIM] = o[r * qb:(r + 1) * qb].astype(BF16)


def _dsa_attention(q, qi, sm, k_new, v_new, ki_new, cache, *, qb, tk):
    b, t, _ = q.shape
    past = 0 if cache is None else cache[2].shape[1]
    n_keys = past + t
    nk_pad = -(-n_keys // tk) * tk
    topk = min(TOPK_MAX, n_keys // 4)
    keys_on_sublanes = qb == LANES
    rep = ATT_HEADS // KV_HEADS
    tokq = lambda w: pl.BlockSpec((1, qb, w), lambda i, j: (i, j, 0))
    keys = lambda w: pl.BlockSpec((1, t, w), lambda i, j: (i, 0, 0))
    tile_shape = (tk, qb) if keys_on_sublanes else (qb, tk)
    cache_specs, cache_scratch = [], []
    if cache is not None:
        kv = pl.BlockSpec((1, past * KV_HEADS, HEAD_DIM), lambda i, j: (i, 0, 0))
        cache_specs = [kv, kv, pl.BlockSpec((1, past, IDX_DIM), lambda i, j: (i, 0, 0))]
        cache_scratch = [pltpu.VMEM((nk_pad, w), BF16) for w in (KV_WIDTH, VEXT_WIDTH, IDX_DIM)]
    return pl.pallas_call(
        functools.partial(_dsa_kernel, qb=qb, tk=tk, n_keys=n_keys, past=past, topk=topk,
                          n_bisect=20, keys_on_sublanes=keys_on_sublanes),
        grid=(b, t // qb),
        in_specs=[tokq(ATT_WIDTH), tokq(QI_WIDTH), tokq(LANES), keys(KV_WIDTH), keys(VEXT_WIDTH),
                  keys(IDX_DIM)] + cache_specs,
        out_specs=tokq(ATT_WIDTH),
        out_shape=jax.ShapeDtypeStruct((b, t, ATT_WIDTH), BF16),
        scratch_shapes=[pltpu.VMEM((nk_pad // tk,) + tile_shape, F32),
                        pltpu.VMEM((nk_pad // tk, qb, tk), F32),
                        pltpu.VMEM((KV_HEADS, rep * qb, 2 * HEAD_DIM), F32),
                        pltpu.VMEM((IDX_HEADS * qb, IDX_DIM), BF16),
                        pltpu.VMEM((KV_HEADS, rep * qb, tk), F32),
                        pltpu.VMEM((KV_HEADS, rep * qb, tk), F32),
                        pltpu.VMEM((KV_HEADS, rep * qb, LANES), F32),
                        pltpu.VMEM((KV_HEADS, rep * qb, LANES), F32),
                        pltpu.VMEM((KV_HEADS, rep * qb, LANES), F32)] + cache_scratch,
        compiler_params=_params(2),
        name="dsa_attention",
    )(q, qi, sm, k_new, v_new, ki_new, *(cache or ()))


def _ssd_kernel(xbc_ref, z_ref, sm_ref, st_ref, alog_ref, dexp_ref, nw_ref, e_ref,
                y_ref, stout_ref, ht_ref, yi_ref):
    c = pl.program_id(1)
    nc = pl.num_programs(1)
    L = xbc_ref.shape[1]
    hp = SSD_INNER // SSD_GROUPS

    @pl.when(c == 0)
    def _():
        ht_ref[...] = st_ref[0].reshape(SSD_INNER, SSD_STATE).T

    xs = xbc_ref[0, :, 0:SSD_INNER]
    sm = sm_ref[0]
    lane = lax.broadcasted_iota(jnp.int32, (1, LANES), 1)
    is_dt = (lane >= SM_DT) & (lane < SM_DT + SSD_HEADS)
    a_neg = jnp.where(is_dt, -jnp.exp(alog_ref[...]), 0.0)
    dt = jnp.where(is_dt, sm, 0.0)
    rows = lax.broadcasted_iota(jnp.int32, (L, L), 0)
    cols = lax.broadcasted_iota(jnp.int32, (L, L), 1)
    tri = cols <= rows
    e = e_ref[...]
    cum = jnp.dot(tri.astype(F32), dt * a_neg, precision=HIGHEST, preferred_element_type=F32)
    eye = (lax.broadcasted_iota(jnp.int32, (LANES, LANES), 0)
           == lax.broadcasted_iota(jnp.int32, (LANES, LANES), 1)).astype(F32)
    cum_t = lax.dot_general(eye, cum, (((1,), (1,)), ((), ())), precision=HIGHEST,
                            preferred_element_type=F32)
    cum_e = jnp.dot(cum, e, precision=HIGHEST, preferred_element_type=F32)
    dt_e = jnp.dot(dt, e, precision=HIGHEST, preferred_element_type=F32)
    last = cum_e[L - 1:L, :]
    xdt = (xs * dt_e).astype(BF16)
    xw = (xs * (jnp.exp(last - cum_e) * dt_e)).astype(BF16)
    ht_old = ht_ref[...]

    for g in range(SSD_GROUPS):
        bg = xbc_ref[0, :, SSD_INNER + g * SSD_STATE:SSD_INNER + (g + 1) * SSD_STATE]
        cg = xbc_ref[0, :, SSD_INNER + (SSD_GROUPS + g) * SSD_STATE:
                     SSD_INNER + (SSD_GROUPS + g + 1) * SSD_STATE]
        bgb, cgb = bg.astype(BF16), cg.astype(BF16)
        cb = lax.dot_general(cgb, bgb, (((1,), (1,)), ((), ())), preferred_element_type=F32)
        for hh in range(SSD_HEADS // SSD_GROUPS):
            h = g * (SSD_HEADS // SSD_GROUPS) + hh
            seg = cum[:, SM_DT + h:SM_DT + h + 1] - cum_t[SM_DT + h:SM_DT + h + 1, :]
            decay = jnp.exp(jnp.where(tri, seg, -jnp.inf))
            wts = (cb * decay).astype(BF16)
            yi_ref[:, h * SSD_HEADDIM:(h + 1) * SSD_HEADDIM] = jnp.dot(
                wts, xdt[:, h * SSD_HEADDIM:(h + 1) * SSD_HEADDIM], preferred_element_type=F32)
        ht_g = ht_old[:, g * hp:(g + 1) * hp]
        y_inter = jnp.dot(cgb, ht_g.astype(BF16), preferred_element_type=F32)
        yi_ref[:, g * hp:(g + 1) * hp] = (yi_ref[:, g * hp:(g + 1) * hp]
                                          + y_inter * jnp.exp(cum_e[:, g * hp:(g + 1) * hp]))
        upd = jnp.dot(bg.T.astype(BF16), xw[:, g * hp:(g + 1) * hp],
                      preferred_element_type=F32)
        ht_ref[:, g * hp:(g + 1) * hp] = ht_g * jnp.exp(last[:, g * hp:(g + 1) * hp]) + upd

    y = yi_ref[...] + dexp_ref[...] * xs
    y = y * _silu(_lane_aligned(z_ref[0], SM_DT, SSD_INNER))
    for g in range(SSD_GROUPS):
        yg = y[:, g * hp:(g + 1) * hp]
        ms = jnp.mean(yg * yg, axis=-1, keepdims=True)
        y_ref[0, :, g * hp:(g + 1) * hp] = (yg * lax.rsqrt(ms + EPS)
                                            * nw_ref[:, g * hp:(g + 1) * hp]).astype(BF16)

    @pl.when(c == nc - 1)
    def _():
        stout_ref[0] = ht_ref[...].T.reshape(SSD_HEADS, SSD_HEADDIM, SSD_STATE)


def _ssd(xbc_act, proj, sm, state, a_log_pad, d_exp, norm_w, e_mat, *, chunk):
    b, t, _ = xbc_act.shape
    tok = lambda w, col=0: pl.BlockSpec((1, chunk, w), lambda i, j: (i, j, col))
    full2 = lambda a: pl.BlockSpec(a.shape, lambda i, j: (0, 0))
    st = pl.BlockSpec((1, SSD_HEADS, SSD_HEADDIM, SSD_STATE), lambda i, j: (i, 0, 0, 0))
    nw = norm_w.reshape(1, SSD_INNER)
    return pl.pallas_call(
        _ssd_kernel,
        grid=(b, t // chunk),
        in_specs=[tok(SSD_CONV_DIM), tok(Z_BLOCK, (OFF_Z - SM_DT) // Z_BLOCK), tok(LANES), st,
                  full2(a_log_pad), full2(d_exp), full2(nw), full2(e_mat)],
        out_specs=[tok(SSD_INNER), st],
        out_shape=[jax.ShapeDtypeStruct((b, t, SSD_INNER), BF16),
                   jax.ShapeDtypeStruct(state.shape, F32)],
        scratch_shapes=[pltpu.VMEM((SSD_STATE, SSD_INNER), F32),
                        pltpu.VMEM((chunk, SSD_INNER), F32)],
        compiler_params=_params(2),
        name="ssd_scan",
    )(xbc_act, proj, sm, state, a_log_pad, d_exp, nw, e_mat)


def _lru_kernel(gate_ref, xin_ref, cw_ref, cb_ref, wa_ref, wx_ref, ba_ref, bx_ref, lam_ref,
                h0_ref, cs_ref, y_ref, hout_ref, cnew_ref, cbuf_ref, a_ref, u_ref, hp_ref):
    t = pl.program_id(1)
    nt = pl.num_programs(1)
    tt = xin_ref.shape[1]
    xc = _causal_conv(xin_ref[0], cs_ref, cw_ref, cb_ref, cbuf_ref, cnew_ref, t == 0, t == nt - 1)

    @pl.when(t == 0)
    def _():
        hp_ref[...] = h0_ref[0]

    for kb in range(LRU_BLOCKS):
        sl = slice(kb * LRU_BLOCK_DIM, (kb + 1) * LRU_BLOCK_DIM)
        xb = xc[:, sl].astype(BF16)
        a_ref[:, sl] = jnp.dot(xb, wa_ref[kb].astype(BF16), preferred_element_type=F32)
        u_ref[:, sl] = jnp.dot(xb, wx_ref[kb].astype(BF16), preferred_element_type=F32)
    r = jax.nn.sigmoid(a_ref[...] + ba_ref[...])
    i = jax.nn.sigmoid(u_ref[...] + bx_ref[...])
    log_a = -LRU_C * r * _softplus(-lam_ref[...])
    a = jnp.exp(log_a)
    u = jnp.sqrt(-jnp.tanh(log_a) * (1.0 + a * a)) * (i * xc)

    width = a.shape[1]
    a = a.reshape(tt // SUBLANES, SUBLANES, width)
    u = u.reshape(tt // SUBLANES, SUBLANES, width)
    sub = lax.broadcasted_iota(jnp.int32, (1, SUBLANES, width), 1)
    for s in (1, 2, 4):
        m = sub >= s
        a_sh = pltpu.roll(a, s, 1)
        u_sh = pltpu.roll(u, s, 1)
        u = jnp.where(m, a * u_sh + u, u)
        a = jnp.where(m, a * a_sh, a)
    a_ref[...] = a.reshape(tt, width)
    u_ref[...] = u.reshape(tt, width)

    def group(gi, hprev):
        r0 = pl.multiple_of(gi * SUBLANES, SUBLANES)
        hs = a_ref[pl.ds(r0, SUBLANES), :] * hprev + u_ref[pl.ds(r0, SUBLANES), :]
        u_ref[pl.ds(r0, SUBLANES), :] = hs
        return hs[SUBLANES - 1:SUBLANES, :]

    h_last = lax.fori_loop(0, tt // SUBLANES, group, hp_ref[...])
    hp_ref[...] = h_last
    gate = gate_ref[0]
    gelu = 0.5 * gate * (1.0 + jnp.tanh(np.sqrt(2.0 / np.pi) * (gate + 0.044715 * (gate * gate * gate))))
    y_ref[0] = (u_ref[...] * gelu).astype(BF16)

    @pl.when(t == nt - 1)
    def _():
        hout_ref[0] = h_last


def _lru(proj, conv_w, conv_b, w_a, b_a, w_x, b_x, lam, h0, conv_state, *, tt):
    b, t, _ = proj.shape
    w = LRU_WIDTH
    tok = lambda col: pl.BlockSpec((1, tt, w), lambda i, j: (i, j, col))
    full = lambda a: pl.BlockSpec(a.shape, lambda i, j: (0,) * a.ndim)
    row = lambda v: v.reshape(1, w)
    state = pl.BlockSpec((1, CONV_W - 1, w), lambda i, j: (i, 0, 0))
    hspec = pl.BlockSpec((1, 1, w), lambda i, j: (i, 0, 0))
    args = (proj, proj, conv_w, row(conv_b), w_a, w_x, row(b_a), row(b_x), row(lam), h0, conv_state)
    return pl.pallas_call(
        _lru_kernel,
        grid=(b, t // tt),
        in_specs=[tok(0), tok(1)] + [full(a) for a in args[2:9]] + [hspec, state],
        out_specs=[pl.BlockSpec((1, tt, w), lambda i, j: (i, j, 0)), hspec, state],
        out_shape=[jax.ShapeDtypeStruct((b, t, w), BF16),
                   jax.ShapeDtypeStruct((b, 1, w), F32),
                   jax.ShapeDtypeStruct((b, CONV_W - 1, w), F32)],
        scratch_shapes=[pltpu.VMEM((tt + SUBLANES, w), F32), pltpu.VMEM((tt, w), F32),
                        pltpu.VMEM((tt, w), F32), pltpu.VMEM((1, w), F32)],
        compiler_params=_params(2),
        name="rg_lru",
    )(*args)


def _rope_tables(pos, dim):
    half = dim // 2
    inv = np.power(ROPE_THETA, -np.arange(half, dtype=np.float64) / half)
    ang = pos.astype(np.float64)[:, None] * inv[None, :]
    cos = np.concatenate([np.cos(ang), np.cos(ang)], axis=-1)
    sin = np.concatenate([-np.sin(ang), np.sin(ang)], axis=-1)
    reps = LANES // dim
    return (jnp.asarray(np.tile(cos, (1, reps)), F32), jnp.asarray(np.tile(sin, (1, reps)), F32))


def _head_expansion():
    e = np.zeros((LANES, SSD_INNER), np.float32)
    for h in range(SSD_HEADS):
        e[SM_DT + h, h * SSD_HEADDIM:(h + 1) * SSD_HEADDIM] = 1.0
    return jnp.asarray(e)


def _pad_lanes(v, offset):
    out = jnp.zeros((1, LANES), F32)
    return out.at[0, offset:offset + v.shape[0]].set(v)


def _w_in_ab_tail(w_t):
    whole = AB_WIDTH // AB_TILE_N * AB_TILE_N
    return jnp.pad(w_t[whole:], ((0, AB_PAD_N - AB_WIDTH), (0, 0)))


def _trunk(x, mods, past_k, past_v, past_ki, ssm_h, ssm_conv, lru_h, lru_conv, p, *, cfg):
    b, t, d = x.shape
    bb, tt = cfg["bb"], cfg["tt"]
    past = 0 if past_k is None else past_k.shape[1]
    pos = np.arange(past, past + t)
    tabs = _rope_tables(pos, HEAD_DIM) + _rope_tables(pos, IDX_DIM)

    def ffn(x, l):
        sh_f, sc_f, g_f = mods[l][3], mods[l][4], mods[l][5]
        gu = _norm_mod_matmul(x, p["norm_ffn_w"][l], sc_f, sh_f, [p["ffn_w_gate"], p["ffn_w_up"]], l,
                              bb=bb, tt=cfg["tt_ffn"], tn=512, out_dtype=BF16, name="ffn_gate_up")
        return _matmul_residual([gu], p["ffn_w_down"], l, x, g_f, bb=bb, tt=cfg["tt_ffn"], tn=256,
                                name="ffn_down")

    sh_m, sc_m, g_m = mods[0][0], mods[0][1], mods[0][2]
    proj = _norm_mod_matmul(x, p["norm_mix_w"][0], sc_m, sh_m, [p["w_in_ab_t"]], 0, bb=bb,
                            tt=cfg["tt_ab"], tn=AB_TILE_N, out_dtype=F32, name="in_proj_ab",
                            w_tail=p["w_in_ab_tail"], w_rows_out=True)
    (q, qi, k, v, ki, sm, xbc_act, ssm_conv_new, kb, vbe, kib) = _post0(
        proj, tabs, p["q_norm_w"][0], p["k_norm_w"][0], p["ssd_conv_w"][0], p["ssd_conv_b"][0],
        p["dt_bias_pad"], ssm_conv, tt=cfg["post_tt"])
    cache = None if past_k is None else (past_k.reshape(b, past * KV_HEADS, HEAD_DIM),
                                         past_v.reshape(b, past * KV_HEADS, HEAD_DIM), past_ki)
    att = _dsa_attention(q, qi, sm, kb, vbe, kib, cache, qb=cfg["qb"], tk=cfg["tk"])
    y_ssd, ssm_new = _ssd(xbc_act, proj, sm, ssm_h, p["a_log_pad"], p["d_exp"], p["ssd_norm_w"][0],
                          p["e_mat"], chunk=cfg["ssd_chunk"])
    x = _matmul_residual([att, y_ssd], p["w_out_ab"], 0, x, g_m, bb=bb, tt=tt, tn=512,
                         name="out_proj_ab")
    x = ffn(x, 0)

    sh_m, sc_m, g_m = mods[1][0], mods[1][1], mods[1][2]
    proj = _norm_mod_matmul(x, p["norm_mix_w"][1], sc_m, sh_m, [p["w_in_c"]], 0,
                            bb=bb, tt=tt, tn=cfg["tn_c"], out_dtype=F32, name="in_proj_c")
    y_lru, lru_new, lru_conv_new = _lru(
        proj, p["lru_conv_w"][0], p["lru_conv_b"][0], p["lru_w_a"][0], p["lru_b_a"][0],
        p["lru_w_x"][0], p["lru_b_x"][0], p["lru_lambda"][0], lru_h.reshape(b, 1, LRU_WIDTH),
        lru_conv, tt=cfg["lru_tt"])
    x = _matmul_residual([y_lru], p["w_out_c"], 0, x, g_m, bb=bb, tt=tt, tn=512, name="out_proj_c")
    x = ffn(x, 1)

    kv_shape = (1, b, t, KV_HEADS, HEAD_DIM)
    return (x, k.reshape(kv_shape), v.reshape(kv_shape), ki[None], ssm_new[None], ssm_conv_new[None],
            lru_new.reshape(1, b, LRU_WIDTH), lru_conv_new[None])


PROMPT_CFG = dict(bb=1, tt=2048, tt_ab=1024, tt_ffn=1024, tn_c=256, post_tt=256, qb=128, tk=512,
                  ssd_chunk=128, lru_tt=256)
SAMPLE_CFG = dict(bb=16, tt=32, tt_ab=32, tt_ffn=32, tn_c=512, post_tt=32, qb=32, tk=384,
                  ssd_chunk=32, lru_tt=32)


def kernel(x_prompt, x_sample, cache_attn_k, cache_attn_v, cache_idx_k, state_ssm, state_ssm_conv,
           state_lru, state_lru_conv, c_prompt, c_sample, ada_w, ada_b, norm_mix_w, norm_ffn_w,
           w_in_ab, q_norm_w, k_norm_w, ssd_conv_w, ssd_conv_b, ssd_dt_bias, ssd_a_log, ssd_d,
           ssd_norm_w, w_out_ab, w_in_c, lru_conv_w, lru_conv_b, lru_w_a, lru_b_a, lru_w_x, lru_b_x,
           lru_lambda, w_out_c, ffn_w_gate, ffn_w_up, ffn_w_down):
    bp, bs = x_prompt.shape[0], x_sample.shape[0]
    p = dict(norm_mix_w=norm_mix_w, norm_ffn_w=norm_ffn_w, q_norm_w=q_norm_w, k_norm_w=k_norm_w,
             ssd_conv_w=ssd_conv_w, ssd_conv_b=ssd_conv_b, ssd_norm_w=ssd_norm_w, w_out_ab=w_out_ab,
             w_in_c=w_in_c, lru_conv_w=lru_conv_w, lru_conv_b=lru_conv_b, lru_w_a=lru_w_a,
             lru_b_a=lru_b_a, lru_w_x=lru_w_x, lru_b_x=lru_b_x, lru_lambda=lru_lambda,
             w_out_c=w_out_c, ffn_w_gate=ffn_w_gate, ffn_w_up=ffn_w_up, ffn_w_down=ffn_w_down)
    p["w_in_ab_t"] = jnp.swapaxes(w_in_ab, 1, 2)
    p["w_in_ab_tail"] = _w_in_ab_tail(p["w_in_ab_t"][0])
    p["dt_bias_pad"] = _pad_lanes(ssd_dt_bias[0], SM_DT)
    p["a_log_pad"] = _pad_lanes(ssd_a_log[0], SM_DT)
    p["d_exp"] = jnp.repeat(ssd_d[0], SSD_HEADDIM).reshape(1, SSD_INNER)
    p["e_mat"] = _head_expansion()

    m_rows = 32
    c_all = jnp.concatenate([c_prompt, c_sample, jnp.zeros((m_rows - bp - bs, D_MODEL), F32)], axis=0)
    mod = _modulation(c_all, ada_w, ada_b)

    def group_mods(r0, nb):
        return [[mod[l, r0:r0 + nb, i * D_MODEL:(i + 1) * D_MODEL].reshape(nb, 1, D_MODEL)
                 for i in range(6)] for l in range(mod.shape[0])]

    zeros = lambda *s: jnp.zeros(s, F32)
    out_p = _trunk(x_prompt, group_mods(0, bp), None, None, None,
                   zeros(bp, SSD_HEADS, SSD_HEADDIM, SSD_STATE), zeros(bp, CONV_W - 1, SSD_CONV_DIM),
                   zeros(bp, LRU_WIDTH), zeros(bp, CONV_W - 1, LRU_WIDTH), p, cfg=PROMPT_CFG)
    out_s = _trunk(x_sample, group_mods(bp, bs), cache_attn_k[0], cache_attn_v[0], cache_idx_k[0],
                   state_ssm[0], state_ssm_conv[0], state_lru[0], state_lru_conv[0], p, cfg=SAMPLE_CFG)
    return (out_p[0], out_s[0]) + out_p[1:] + out_s[1:]
```

```python
import functools

import numpy as np
import jax
import jax.numpy as jnp
from jax import lax
from jax.experimental import pallas as pl
from jax.experimental.pallas import tpu as pltpu

F32 = jnp.float32
BF16 = jnp.bfloat16
HIGHEST = lax.Precision.HIGHEST

D_MODEL = 2048
CHUNK = 64
CHUNK_SHIFT = 6
ATT_HEADS = 8
KV_HEADS = 2
HEAD_DIM = 128
IDX_HEADS = 16
IDX_DIM = 64
TOPK_MAX = 256
ROPE_THETA = 10000.0
SSD_HEADS = 16
SSD_HEADDIM = 64
SSD_GROUPS = 2
SSD_STATE = 128
SSD_INNER = SSD_HEADS * SSD_HEADDIM
SSD_CONV_DIM = SSD_INNER + 2 * SSD_GROUPS * SSD_STATE
CONV_W = 4
LRU_WIDTH = D_MODEL
LRU_BLOCKS = 16
LRU_BLOCK_DIM = LRU_WIDTH // LRU_BLOCKS
LRU_C = 8.0
EPS = 1e-6
ATT_WIDTH = ATT_HEADS * HEAD_DIM
KV_WIDTH = KV_HEADS * HEAD_DIM
VEXT_WIDTH = 2 * KV_WIDTH
QI_WIDTH = IDX_HEADS * IDX_DIM
Q_SCALE = HEAD_DIM ** -0.5 * float(np.log2(np.e))

LANES = 128
SUBLANES = 8
VMEM_LIMIT_BYTES = 60 * 1024 * 1024
SINGLE_BUFFER_ROWS = 2048

OFF_Q = 0
OFF_K = OFF_Q + ATT_WIDTH
OFF_V = OFF_K + KV_WIDTH
OFF_QI = OFF_V + KV_WIDTH
OFF_SMALL = OFF_QI + QI_WIDTH
SM_WI = IDX_DIM
SM_DT = IDX_DIM + IDX_HEADS
OFF_Z = OFF_SMALL + SM_DT
OFF_XBC = OFF_Z + SSD_INNER
OFF_DT = OFF_XBC + SSD_CONV_DIM
AB_WIDTH = OFF_DT + SSD_HEADS
Z_BLOCK = 1280
AB_TILE_N = 256
AB_PAD_N = 5376
assert OFF_Z % LANES == SM_DT and OFF_XBC % LANES == SM_DT and OFF_DT % LANES == SM_DT
assert (OFF_Z - SM_DT) % Z_BLOCK == 0 and OFF_Z - SM_DT + Z_BLOCK >= OFF_Z + SSD_INNER

NEG_BIG = -1e30


def _params(n_axes):
    return pltpu.CompilerParams(dimension_semantics=("arbitrary",) * n_axes,
                                vmem_limit_bytes=VMEM_LIMIT_BYTES)


def _silu(x):
    return x * jax.nn.sigmoid(x)


def _softplus(x):
    return jnp.maximum(x, 0.0) + jnp.log1p(jnp.exp(-jnp.abs(x)))


def _mod_kernel(c_ref, w_ref, b_ref, o_ref):
    a = _silu(c_ref[...]).astype(BF16)
    o_ref[0] = jnp.dot(a, w_ref[0].astype(BF16), preferred_element_type=F32) + b_ref[0]


def _modulation(c_all, ada_w, ada_b):
    depth, d, n = ada_w.shape
    m = c_all.shape[0]
    tn = 1024
    return pl.pallas_call(
        _mod_kernel,
        grid=(depth, n // tn),
        in_specs=[pl.BlockSpec((m, d), lambda l, j: (0, 0)),
                  pl.BlockSpec((1, d, tn), lambda l, j: (l, 0, j)),
                  pl.BlockSpec((1, 1, tn), lambda l, j: (l, 0, j))],
        out_specs=pl.BlockSpec((1, m, tn), lambda l, j: (l, 0, j)),
        out_shape=jax.ShapeDtypeStruct((depth, m, n), F32),
        compiler_params=_params(2),
        name="adaln_mod",
    )(c_all, ada_w, ada_b.reshape(depth, 1, n))


def _nmm_kernel(x_ref, nw_ref, sc_ref, sh_ref, *rest, swiglu, tail, w_rows_out):
    n_w = 2 if swiglu or tail else 1
    w_refs, o_ref, h_ref = rest[:n_w], rest[n_w], rest[n_w + 1]
    bb, tt, d = x_ref.shape

    @pl.when(pl.program_id(2) == 0)
    def _():
        x = x_ref[...]
        ms = jnp.mean(x * x, axis=-1, keepdims=True)
        y = x * lax.rsqrt(ms + EPS) * nw_ref[...]
        h = y * (1.0 + sc_ref[...]) + sh_ref[...]
        h_ref[...] = h.reshape(bb * tt, d).astype(BF16)

    h = h_ref[...]

    def project(w_ref):
        dims = (((1,), (1,)), ((), ())) if w_rows_out else (((1,), (0,)), ((), ()))
        o = lax.dot_general(h, w_ref[...].astype(BF16), dims, preferred_element_type=F32)
        o_ref[...] = o.reshape(o_ref.shape).astype(o_ref.dtype)

    if swiglu:
        g = jnp.dot(h, w_refs[0][...].astype(BF16), preferred_element_type=F32)
        u = jnp.dot(h, w_refs[1][...].astype(BF16), preferred_element_type=F32)
        o_ref[...] = (_silu(g) * u).reshape(o_ref.shape).astype(o_ref.dtype)
    elif tail:
        last = pl.num_programs(2) - 1
        pl.when(pl.program_id(2) < last)(lambda: project(w_refs[0]))
        pl.when(pl.program_id(2) == last)(lambda: project(w_refs[1]))
    else:
        project(w_refs[0])


def _row_block_mode(rows):
    return dict(pipeline_mode=pl.Buffered(1)) if rows >= SINGLE_BUFFER_ROWS else {}


def _norm_mod_matmul(x, nw, sc, sh, ws, layer, *, bb, tt, tn, out_dtype, name, w_tail=None,
                     w_rows_out=False):
    b, t, d = x.shape
    swiglu = len(ws) == 2
    n_main = ws[0].shape[1 if w_rows_out else 2] // tn
    n = (n_main + (w_tail is not None)) * tn
    xmap = lambda i, j, k: (i, j, 0)
    mmap = lambda i, j, k: (i, 0, 0)
    tile = lambda k: jnp.minimum(k, n_main - 1)
    if w_rows_out:
        w_specs = [pl.BlockSpec((None, tn, d), lambda i, j, k: (layer, tile(k), 0))] * len(ws)
    else:
        w_specs = [pl.BlockSpec((None, d, tn), lambda i, j, k: (layer, 0, tile(k)))] * len(ws)
    operands = list(ws)
    if w_tail is not None:
        w_specs.append(pl.BlockSpec(w_tail.shape, lambda i, j, k: (0, 0)))
        operands.append(w_tail)
    return pl.pallas_call(
        functools.partial(_nmm_kernel, swiglu=swiglu, tail=w_tail is not None,
                          w_rows_out=w_rows_out),
        grid=(b // bb, t // tt, n // tn),
        in_specs=[pl.BlockSpec((bb, tt, d), xmap, **_row_block_mode(bb * tt)),
                  pl.BlockSpec((1, 1, d), lambda i, j, k: (0, 0, 0)),
                  pl.BlockSpec((bb, 1, d), mmap),
                  pl.BlockSpec((bb, 1, d), mmap)] + w_specs,
        out_specs=pl.BlockSpec((bb, tt, tn), lambda i, j, k: (i, j, k)),
        out_shape=jax.ShapeDtypeStruct((b, t, n), out_dtype),
        scratch_shapes=[pltpu.VMEM((bb * tt, d), BF16)],
        compiler_params=_params(3),
        name=name,
    )(x, nw.reshape(1, 1, d), sc, sh, *operands)


def _mmr_kernel(*refs, k_sizes):
    n_a = len(k_sizes)
    a_refs = refs[:n_a]
    w_ref, x_ref, g_ref, o_ref = refs[n_a:]
    bb, tt, tn = o_ref.shape
    acc = jnp.zeros((bb * tt, tn), F32)
    off = 0
    for a_ref, ks in zip(a_refs, k_sizes):
        a = a_ref[...].reshape(bb * tt, ks)
        acc = acc + jnp.dot(a, w_ref[off:off + ks, :].astype(BF16), preferred_element_type=F32)
        off += ks
    o_ref[...] = x_ref[...] + g_ref[...] * acc.reshape(bb, tt, tn)


def _matmul_residual(a_list, w, layer, x, g, *, bb, tt, tn, name):
    b, t, d = x.shape
    k_sizes = tuple(a.shape[-1] for a in a_list)
    k_total = sum(k_sizes)
    return pl.pallas_call(
        functools.partial(_mmr_kernel, k_sizes=k_sizes),
        grid=(b // bb, t // tt, d // tn),
        in_specs=[pl.BlockSpec((bb, tt, ks), lambda i, j, k: (i, j, 0), **_row_block_mode(bb * tt))
                  for ks in k_sizes]
                 + [pl.BlockSpec((None, k_total, tn), lambda i, j, k: (layer, 0, k)),
                    pl.BlockSpec((bb, tt, tn), lambda i, j, k: (i, j, k)),
                    pl.BlockSpec((bb, 1, tn), lambda i, j, k: (i, 0, k))],
        out_specs=pl.BlockSpec((bb, tt, tn), lambda i, j, k: (i, j, k)),
        out_shape=jax.ShapeDtypeStruct((b, t, d), F32),
        compiler_params=_params(3),
        name=name,
    )(*a_list, w, x, g)


def _swap_halves(x, half):
    w = x.shape[-1]
    lane = lax.broadcasted_iota(jnp.int32, x.shape, x.ndim - 1)
    first = (lane & half) == 0
    return jnp.where(first, pltpu.roll(x, w - half, x.ndim - 1), pltpu.roll(x, half, x.ndim - 1))


def _lane_aligned(slab, offset, width):
    return pltpu.roll(slab, slab.shape[-1] - offset, slab.ndim - 1)[:, :width]


def _rope(x, cos, sin_signed, half):
    reps = x.shape[-1] // cos.shape[-1]
    if reps > 1:
        cos = jnp.concatenate([cos] * reps, axis=-1)
        sin_signed = jnp.concatenate([sin_signed] * reps, axis=-1)
    return x * cos + _swap_halves(x, half) * sin_signed


def _causal_conv(u, state_ref, w_ref, b_ref, cbuf_ref, new_ref, first, last):
    tt = u.shape[0]

    @pl.when(first)
    def _():
        cbuf_ref[0:SUBLANES, :] = jnp.zeros((SUBLANES, u.shape[1]), F32)
        cbuf_ref[SUBLANES - (CONV_W - 1):SUBLANES, :] = state_ref[0]

    cbuf_ref[SUBLANES:SUBLANES + tt, :] = u
    full = cbuf_ref[...]
    out = b_ref[...]
    for j in range(CONV_W):
        shift = CONV_W - 1 - j
        rows = pltpu.roll(full, shift, 0) if shift else full
        out = out + rows[SUBLANES:SUBLANES + tt, :] * w_ref[j:j + 1, :]
    tail = cbuf_ref[tt:tt + SUBLANES, :]
    cbuf_ref[0:SUBLANES, :] = tail

    @pl.when(last)
    def _():
        new_ref[0] = tail[SUBLANES - (CONV_W - 1):, :]

    return out


def _post0_kernel(p_ref, cq_ref, sq_ref, ci_ref, si_ref, qn_ref, kn_ref, cw_ref, cb_ref, dtb_ref,
                  cs_ref,
                  q_ref, qi_ref, k_ref, v_ref, ki_ref, sm_ref, xbc_ref, cnew_ref, kb_ref, vb_ref,
                  kib_ref, cbuf_ref):
    t = pl.program_id(1)
    nt = pl.num_programs(1)
    cq, sq, ci, si = cq_ref[...], sq_ref[...], ci_ref[...], si_ref[...]

    def head_norm_rope(x, w):
        ms = jnp.mean(x * x, axis=-1, keepdims=True)
        return _rope(x * lax.rsqrt(ms + EPS) * w, cq, sq, HEAD_DIM // 2)

    for h in range(ATT_HEADS):
        xh = p_ref[0, :, OFF_Q + h * HEAD_DIM:OFF_Q + (h + 1) * HEAD_DIM]
        q_ref[0, :, h * HEAD_DIM:(h + 1) * HEAD_DIM] = (head_norm_rope(xh, qn_ref[...])
                                                         * Q_SCALE).astype(BF16)
    ones = jnp.ones((p_ref.shape[1], HEAD_DIM), BF16)
    for h in range(KV_HEADS):
        xh = p_ref[0, :, OFF_K + h * HEAD_DIM:OFF_K + (h + 1) * HEAD_DIM]
        kh = head_norm_rope(xh, kn_ref[...])
        vh = p_ref[0, :, OFF_V + h * HEAD_DIM:OFF_V + (h + 1) * HEAD_DIM]
        k_ref[0, pl.ds(h, xh.shape[0], stride=KV_HEADS), :] = kh
        v_ref[0, pl.ds(h, xh.shape[0], stride=KV_HEADS), :] = vh
        kb_ref[0, :, h * HEAD_DIM:(h + 1) * HEAD_DIM] = kh.astype(BF16)
        vb_ref[0, :, 2 * h * HEAD_DIM:(2 * h + 1) * HEAD_DIM] = vh.astype(BF16)
        vb_ref[0, :, (2 * h + 1) * HEAD_DIM:(2 * h + 2) * HEAD_DIM] = ones

    qi = p_ref[0, :, OFF_QI:OFF_QI + QI_WIDTH]
    qi_ref[0] = _rope(qi, ci, si, IDX_DIM // 2).astype(BF16)

    sm = p_ref[0, :, OFF_SMALL:OFF_SMALL + LANES]
    lane = lax.broadcasted_iota(jnp.int32, sm.shape, 1)
    ki_part = _rope(sm, ci, si, IDX_DIM // 2)
    wi_part = sm * (IDX_HEADS ** -0.5 * IDX_DIM ** -0.5)
    dt_col = OFF_DT - SM_DT
    dt_part = _softplus(p_ref[0, :, dt_col:dt_col + LANES] + dtb_ref[...])
    sm_out = jnp.where(lane < SM_WI, ki_part,
                       jnp.where(lane < SM_DT, wi_part,
                                 jnp.where(lane < SM_DT + SSD_HEADS, dt_part, 0.0)))
    sm_ref[0] = sm_out
    ki_ref[0] = sm_out[:, :IDX_DIM]
    kib_ref[0] = sm_out[:, :IDX_DIM].astype(BF16)

    xbc = _lane_aligned(p_ref[0, :, OFF_XBC - SM_DT:OFF_DT - SM_DT + LANES], SM_DT, SSD_CONV_DIM)
    conv = _causal_conv(xbc, cs_ref, cw_ref, cb_ref, cbuf_ref, cnew_ref, t == 0, t == nt - 1)
    xbc_ref[0] = _silu(conv)


def _post0(proj, tabs, q_norm_w, k_norm_w, conv_w, conv_b, dt_bias_pad, conv_state, *, tt):
    b, t, n = proj.shape
    tok = lambda w: pl.BlockSpec((1, tt, w), lambda i, j: (i, j, 0))
    tab = pl.BlockSpec((tt, LANES), lambda i, j: (j, 0))
    full2 = lambda a: pl.BlockSpec(a.shape, lambda i, j: (0, 0))
    state = lambda c: pl.BlockSpec((1, CONV_W - 1, c), lambda i, j: (i, 0, 0))
    out_shapes = [
        jax.ShapeDtypeStruct((b, t, ATT_WIDTH), BF16),
        jax.ShapeDtypeStruct((b, t, QI_WIDTH), BF16),
        jax.ShapeDtypeStruct((b, t * KV_HEADS, HEAD_DIM), F32),
        jax.ShapeDtypeStruct((b, t * KV_HEADS, HEAD_DIM), F32),
        jax.ShapeDtypeStruct((b, t, IDX_DIM), F32),
        jax.ShapeDtypeStruct((b, t, LANES), F32),
        jax.ShapeDtypeStruct((b, t, SSD_CONV_DIM), F32),
        jax.ShapeDtypeStruct((b, CONV_W - 1, SSD_CONV_DIM), F32),
        jax.ShapeDtypeStruct((b, t, KV_WIDTH), BF16),
        jax.ShapeDtypeStruct((b, t, VEXT_WIDTH), BF16),
        jax.ShapeDtypeStruct((b, t, IDX_DIM), BF16),
    ]
    kv_rows = pl.BlockSpec((1, tt * KV_HEADS, HEAD_DIM), lambda i, j: (i, j, 0))
    out_specs = [tok(ATT_WIDTH), tok(QI_WIDTH), kv_rows, kv_rows, tok(IDX_DIM),
                 tok(LANES), tok(SSD_CONV_DIM), state(SSD_CONV_DIM), tok(KV_WIDTH), tok(VEXT_WIDTH),
                 tok(IDX_DIM)]
    qn = q_norm_w.reshape(1, HEAD_DIM)
    kn = k_norm_w.reshape(1, HEAD_DIM)
    cb = conv_b.reshape(1, SSD_CONV_DIM)
    return pl.pallas_call(
        _post0_kernel,
        grid=(b, t // tt),
        in_specs=[tok(n), tab, tab, tab, tab, full2(qn), full2(kn), full2(conv_w), full2(cb),
                  full2(dt_bias_pad), state(SSD_CONV_DIM)],
        out_specs=out_specs,
        out_shape=out_shapes,
        scratch_shapes=[pltpu.VMEM((tt + SUBLANES, SSD_CONV_DIM), F32)],
        compiler_params=_params(2),
        name="post_proj0",
    )(proj, *tabs, qn, kn, conv_w, cb, dt_bias_pad, conv_state)


def _dsa_kernel(q_ref, qi_ref, sm_ref, k_ref, v_ref, ki_ref, *rest, qb, tk, n_keys, past, topk,
                n_bisect, keys_on_sublanes):
    if past:
        pk_ref, pv_ref, pki_ref = rest[:3]
        rest = rest[3:]
    (o_ref, sc_ref, bias_ref, acc_ref, qs_ref, s0_ref, s1_ref, mt0_ref, mt1_ref, m_ref) = rest[:10]
    if past:
        kall_ref, vall_ref, kiall_ref = rest[10:]
        t_new = k_ref.shape[1]
        pad = kall_ref.shape[0] - n_keys
        for g in range(KV_HEADS):
            head_rows = pl.ds(g, past, stride=KV_HEADS)
            kall_ref[0:past, g * HEAD_DIM:(g + 1) * HEAD_DIM] = pk_ref[0, head_rows, :].astype(BF16)
            vall_ref[0:past, 2 * g * HEAD_DIM:(2 * g + 1) * HEAD_DIM] = pv_ref[0, head_rows, :].astype(BF16)
            vall_ref[0:past, (2 * g + 1) * HEAD_DIM:(2 * g + 2) * HEAD_DIM] = jnp.ones(
                (past, HEAD_DIM), BF16)
        kiall_ref[0:past, :] = pki_ref[0].astype(BF16)
        for dst, src in ((kall_ref, k_ref), (vall_ref, v_ref), (kiall_ref, ki_ref)):
            dst[past:past + t_new, :] = src[0]
            dst[past + t_new:, :] = jnp.zeros((pad, dst.shape[1]), BF16)
        k_src, v_src, ki_src = kall_ref, vall_ref, kiall_ref
    else:
        k_src, v_src, ki_src = k_ref.at[0], v_ref.at[0], ki_ref.at[0]
    kax = 0 if keys_on_sublanes else 1
    tile_shape = (tk, qb) if keys_on_sublanes else (qb, tk)
    vec_shape = (1, qb) if keys_on_sublanes else (qb, 1)
    n_acc = 8
    part_shape = (n_acc, SUBLANES, qb) if keys_on_sublanes else (qb, LANES)
    j = pl.program_id(1)
    pos0 = past + j * qb
    q_chunk = lax.shift_right_logical(pos0 + lax.broadcasted_iota(jnp.int32, vec_shape, 1 - kax),
                                      CHUNK_SHIFT)
    n_valid = jnp.minimum((q_chunk + 1) * CHUNK, n_keys)
    max_valid = jnp.minimum(((pos0 + qb - 1) // CHUNK + 1) * CHUNK, n_keys)
    nkt = (max_valid + tk - 1) // tk
    kf = float(topk)
    active_f = jnp.where(n_valid > topk, 1.0, 0.0)

    def key_index(kt):
        return kt * tk + lax.broadcasted_iota(jnp.int32, tile_shape, kax)

    nt_dims = (((1,), (1,)), ((), ()))
    for h in range(IDX_HEADS):
        qs_ref[h * qb:(h + 1) * qb, :] = qi_ref[0, :, h * IDX_DIM:(h + 1) * IDX_DIM]
    if keys_on_sublanes:
        wi_t = sm_ref[0].T
        head_w = lambda h: wi_t[SM_WI + h:SM_WI + h + 1, :]
    else:
        wis = sm_ref[0][:, SM_WI:SM_WI + IDX_HEADS]
        head_w = lambda h: wis[:, h:h + 1]

    def score_tile(kt, carry):
        kit = ki_src[pl.ds(pl.multiple_of(kt * tk, tk), tk), :]
        acc = jnp.zeros(tile_shape, F32)
        if keys_on_sublanes:
            for pr in range(IDX_HEADS // 2):
                s2 = lax.dot_general(kit, qs_ref[2 * pr * qb:(2 * pr + 2) * qb, :], nt_dims,
                                     preferred_element_type=F32)
                for e in range(2):
                    acc = acc + head_w(2 * pr + e) * jnp.maximum(s2[:, e * qb:(e + 1) * qb], 0.0)
        else:
            s_all = lax.dot_general(qs_ref[...], kit, nt_dims, preferred_element_type=F32)
            for h in range(IDX_HEADS):
                acc = acc + head_w(h) * jnp.maximum(s_all[h * qb:(h + 1) * qb, :], 0.0)
        sc_ref[kt] = jnp.where(key_index(kt) < n_valid, acc, -jnp.inf)
        return carry

    lax.fori_loop(0, nkt, score_tile, 0)

    def fold(m, op2, red):
        if keys_on_sublanes:
            return red(m.reshape(tk // (n_acc * SUBLANES), n_acc, SUBLANES, qb), axis=0)
        f = m[:, 0:LANES]
        for c in range(1, tk // LANES):
            f = op2(f, m[:, c * LANES:(c + 1) * LANES])
        return f

    def reduce_tiles(make, op2, red, init):
        def body(kt, part):
            return op2(part, fold(make(sc_ref[kt], kt), op2, red))
        part = lax.fori_loop(0, nkt, body, jnp.full(part_shape, init, F32))
        if keys_on_sublanes:
            part = red(part, axis=0)
        return red(part, axis=kax, keepdims=True)

    def count(pred):
        return reduce_tiles(lambda t, kt: jnp.where(pred(t, kt), 1.0, 0.0), jnp.add, jnp.sum, 0.0)

    def row_max(pred):
        return reduce_tiles(lambda t, kt: jnp.where(pred(t, kt), t, -jnp.inf), jnp.maximum, jnp.max,
                            -jnp.inf)

    def row_min_valid():
        return reduce_tiles(lambda t, kt: jnp.where(key_index(kt) < n_valid, t, jnp.inf), jnp.minimum,
                            jnp.min, jnp.inf)

    def any_set(flag_f):
        return jnp.max(flag_f) > 0.0

    def bisect(_, state):
        lo, hi, c_lo = state
        mid = lo + (hi - lo) * 0.5
        c = count(lambda t, kt: t >= mid)
        ge = c >= kf
        return jnp.where(ge, mid, lo), jnp.where(ge, hi, mid), jnp.where(ge, c, c_lo)

    lo0 = row_min_valid()
    hi0 = row_max(lambda t, kt: t == t)
    lo, hi, c_lo = lax.fori_loop(0, n_bisect, bisect, (lo0, hi0, n_valid.astype(F32)))
    found_f = jnp.where(c_lo == kf, active_f, 0.0)
    v_found = reduce_tiles(lambda t, kt: jnp.where(t >= lo, t, jnp.inf), jnp.minimum, jnp.min, jnp.inf)

    def walk_down():
        w0 = row_max(lambda t, kt: t <= hi)
        c0 = count(lambda t, kt: t >= w0)

        def walk_cond(state):
            w, c = state
            return any_set(jnp.where(c < kf, active_f, 0.0))

        def walk_body(state):
            w, c = state
            w2 = row_max(lambda t, kt: t < w)
            c2 = count(lambda t, kt: t >= w2)
            upd = c < kf
            return jnp.where(upd, w2, w), jnp.where(upd, c2, c)

        return lax.while_loop(walk_cond, walk_body,
                              (jnp.where(found_f > 0.5, v_found, w0),
                               jnp.where(found_f > 0.5, kf, c0)))

    w, c_ge = lax.cond(any_set(active_f - found_f), walk_down,
                       lambda: (v_found, jnp.full(vec_shape, kf, F32)))
    thr = jnp.where(active_f > 0.5, w, -jnp.inf)
    tied_f = jnp.where(c_ge > kf, active_f, 0.0)

    def write_bias(sel_fn):
        def body(kt, carry):
            kidx = key_index(kt)
            sel = sel_fn(sc_ref[kt], kidx) & (kidx < n_valid)
            bias = jnp.where(sel, 0.0, NEG_BIG)
            bias_ref[kt] = bias.T if keys_on_sublanes else bias
            return carry
        lax.fori_loop(0, nkt, body, 0)

    def no_ties():
        write_bias(lambda t, kidx: t >= thr)

    def with_ties():
        need = kf - count(lambda t, kt: t > thr)
        n_steps = max(1, int(np.ceil(np.log2(sc_ref.shape[0] * tk))) + 1)

        def step(_, lh):
            lo_i, hi_i = lh
            mid = lax.shift_right_arithmetic(lo_i + hi_i, 1)
            ge = count(lambda t, kt: (t == thr) & (key_index(kt) <= mid)) >= need
            return jnp.where(ge, lo_i, mid), jnp.where(ge, mid, hi_i)

        lo_i = jnp.full(vec_shape, -1, jnp.int32)
        hi_i = jnp.full(vec_shape, sc_ref.shape[0] * tk - 1, jnp.int32)
        _, last_tie = lax.fori_loop(0, n_steps, step, (lo_i, hi_i))
        write_bias(lambda t, kidx: (t > thr) | ((t == thr)
                                                & ((kidx <= last_tie) | (tied_f < 0.5))))

    any_tie = jnp.max(tied_f) > 0.0
    lax.cond(any_tie, with_ties, no_ties)

    rep = ATT_HEADS // KV_HEADS
    qgs = [jnp.concatenate(
        [q_ref[0, :, (g * rep + r) * HEAD_DIM:(g * rep + r + 1) * HEAD_DIM] for r in range(rep)],
        axis=0) for g in range(KV_HEADS)]
    acc_ref[...] = jnp.zeros(acc_ref.shape, F32)
    bufs = ((s0_ref, mt0_ref), (s1_ref, mt1_ref))

    def qk_tile(kt, buf):
        s_ref, mt_ref = bufs[buf]
        row0 = pl.multiple_of(kt * tk, tk)
        bias = bias_ref[kt]
        for g in range(KV_HEADS):
            kg = k_src[pl.ds(row0, tk), g * HEAD_DIM:(g + 1) * HEAD_DIM]
            s = lax.dot_general(qgs[g], kg, nt_dims, preferred_element_type=F32)
            s = (s.reshape(rep, qb, tk) + bias[None]).reshape(rep * qb, tk)
            s_ref[g] = s
            mt_ref[g] = jnp.broadcast_to(jnp.max(s, axis=-1, keepdims=True), (rep * qb, LANES))

    def lane_tile(x, width):
        return jnp.concatenate([x] * (width // LANES), axis=1)

    def att_tile(kt, buf, prefetch):
        if prefetch:
            qk_tile(kt + 1, 1 - buf)
        s_ref, mt_ref = bufs[buf]
        row0 = pl.multiple_of(kt * tk, tk)
        for g in range(KV_HEADS):
            vg = v_src[pl.ds(row0, tk), 2 * g * HEAD_DIM:(2 * g + 2) * HEAD_DIM]
            m_old = m_ref[g]
            m_new = jnp.maximum(m_old, mt_ref[g])
            m_ref[g] = m_new
            alpha = jnp.exp2(m_old - m_new)
            p = jnp.exp2(s_ref[g] - lane_tile(m_new, tk)).astype(BF16)
            acc_ref[g] = (lane_tile(alpha, 2 * HEAD_DIM) * acc_ref[g]
                          + jnp.dot(p, vg, preferred_element_type=F32))

    def tile_pair(pi, carry):
        att_tile(2 * pi, 0, True)
        att_tile(2 * pi + 1, 1, True)
        return carry

    def last_two():
        att_tile(nkt - 2, 0, True)
        att_tile(nkt - 1, 1, False)

    def last_one():
        att_tile(nkt - 1, 0, False)

    m_ref[...] = jnp.full(m_ref.shape, NEG_BIG, F32)
    qk_tile(0, 0)
    lax.fori_loop(0, (nkt - 1) // 2, tile_pair, 0)
    lax.cond((nkt & 1) == 0, last_two, last_one)
    for g in range(KV_HEADS):
        acc = acc_ref[g]
        o = acc[:, :HEAD_DIM] / acc[:, HEAD_DIM:]
        for r in range(rep):
            h = g * rep + r
            o_ref[0, :, h * HEAD_DIM:(h + 1) * HEAD_DIM] = o[r * qb:(r + 1) * qb].astype(BF16)


def _dsa_attention(q, qi, sm, k_new, v_new, ki_new, cache, *, qb, tk):
    b, t, _ = q.shape
    past = 0 if cache is None else cache[2].shape[1]
    n_keys = past + t
    nk_pad = -(-n_keys // tk) * tk
    topk = min(TOPK_MAX, n_keys // 4)
    keys_on_sublanes = qb == LANES
    rep = ATT_HEADS // KV_HEADS
    tokq = lambda w: pl.BlockSpec((1, qb, w), lambda i, j: (i, j, 0))
    keys = lambda w: pl.BlockSpec((1, t, w), lambda i, j: (i, 0, 0))
    tile_shape = (tk, qb) if keys_on_sublanes else (qb, tk)
    cache_specs, cache_scratch = [], []
    if cache is not None:
        kv = pl.BlockSpec((1, past * KV_HEADS, HEAD_DIM), lambda i, j: (i, 0, 0))
        cache_specs = [kv, kv, pl.BlockSpec((1, past, IDX_DIM), lambda i, j: (i, 0, 0))]
        cache_scratch = [pltpu.VMEM((nk_pad, w), BF16) for w in (KV_WIDTH, VEXT_WIDTH, IDX_DIM)]
    return pl.pallas_call(
        functools.partial(_dsa_kernel, qb=qb, tk=tk, n_keys=n_keys, past=past, topk=topk,
                          n_bisect=20, keys_on_sublanes=keys_on_sublanes),
        grid=(b, t // qb),
        in_specs=[tokq(ATT_WIDTH), tokq(QI_WIDTH), tokq(LANES), keys(KV_WIDTH), keys(VEXT_WIDTH),
                  keys(IDX_DIM)] + cache_specs,
        out_specs=tokq(ATT_WIDTH),
        out_shape=jax.ShapeDtypeStruct((b, t, ATT_WIDTH), BF16),
        scratch_shapes=[pltpu.VMEM((nk_pad // tk,) + tile_shape, F32),
                        pltpu.VMEM((nk_pad // tk, qb, tk), F32),
                        pltpu.VMEM((KV_HEADS, rep * qb, 2 * HEAD_DIM), F32),
                        pltpu.VMEM((IDX_HEADS * qb, IDX_DIM), BF16),
                        pltpu.VMEM((KV_HEADS, rep * qb, tk), F32),
                        pltpu.VMEM((KV_HEADS, rep * qb, tk), F32),
                        pltpu.VMEM((KV_HEADS, rep * qb, LANES), F32),
                        pltpu.VMEM((KV_HEADS, rep * qb, LANES), F32),
                        pltpu.VMEM((KV_HEADS, rep * qb, LANES), F32)] + cache_scratch,
        compiler_params=_params(2),
        name="dsa_attention",
    )(q, qi, sm, k_new, v_new, ki_new, *(cache or ()))


def _ssd_kernel(xbc_ref, z_ref, sm_ref, st_ref, alog_ref, dexp_ref, nw_ref, e_ref,
                y_ref, stout_ref, ht_ref, yi_ref):
    c = pl.program_id(1)
    nc = pl.num_programs(1)
    L = xbc_ref.shape[1]
    hp = SSD_INNER // SSD_GROUPS

    @pl.when(c == 0)
    def _():
        ht_ref[...] = st_ref[0].reshape(SSD_INNER, SSD_STATE).T

    xs = xbc_ref[0, :, 0:SSD_INNER]
    sm = sm_ref[0]
    lane = lax.broadcasted_iota(jnp.int32, (1, LANES), 1)
    is_dt = (lane >= SM_DT) & (lane < SM_DT + SSD_HEADS)
    a_neg = jnp.where(is_dt, -jnp.exp(alog_ref[...]), 0.0)
    dt = jnp.where(is_dt, sm, 0.0)
    rows = lax.broadcasted_iota(jnp.int32, (L, L), 0)
    cols = lax.broadcasted_iota(jnp.int32, (L, L), 1)
    tri = cols <= rows
    e = e_ref[...]
    cum = jnp.dot(tri.astype(F32), dt * a_neg, precision=HIGHEST, preferred_element_type=F32)
    eye = (lax.broadcasted_iota(jnp.int32, (LANES, LANES), 0)
           == lax.broadcasted_iota(jnp.int32, (LANES, LANES), 1)).astype(F32)
    cum_t = lax.dot_general(eye, cum, (((1,), (1,)), ((), ())), precision=HIGHEST,
                            preferred_element_type=F32)
    both = jnp.concatenate([cum, dt], axis=0)
    hi = both.astype(BF16)
    rest = both - hi.astype(F32)
    mid = rest.astype(BF16)
    low = (rest - mid.astype(F32)).astype(BF16)
    both_e = (jnp.dot(hi, e, preferred_element_type=F32) + jnp.dot(mid, e, preferred_element_type=F32)
              + jnp.dot(low, e, preferred_element_type=F32))
    cum_e, dt_e = both_e[:L], both_e[L:]
    last = cum_e[L - 1:L, :]
    xdt = (xs * dt_e).astype(BF16)
    xw = (xs * (jnp.exp(last - cum_e) * dt_e)).astype(BF16)
    ht_old = ht_ref[...]

    for g in range(SSD_GROUPS):
        bg = xbc_ref[0, :, SSD_INNER + g * SSD_STATE:SSD_INNER + (g + 1) * SSD_STATE]
        cg = xbc_ref[0, :, SSD_INNER + (SSD_GROUPS + g) * SSD_STATE:
                     SSD_INNER + (SSD_GROUPS + g + 1) * SSD_STATE]
        bgb, cgb = bg.astype(BF16), cg.astype(BF16)
        cb = lax.dot_general(cgb, bgb, (((1,), (1,)), ((), ())), preferred_element_type=F32)
        for hh in range(SSD_HEADS // SSD_GROUPS):
            h = g * (SSD_HEADS // SSD_GROUPS) + hh
            seg = cum[:, SM_DT + h:SM_DT + h + 1] - cum_t[SM_DT + h:SM_DT + h + 1, :]
            decay = jnp.exp(jnp.where(tri, seg, -jnp.inf))
            wts = (cb * decay).astype(BF16)
            yi_ref[:, h * SSD_HEADDIM:(h + 1) * SSD_HEADDIM] = jnp.dot(
                wts, xdt[:, h * SSD_HEADDIM:(h + 1) * SSD_HEADDIM], preferred_element_type=F32)
        ht_g = ht_old[:, g * hp:(g + 1) * hp]
        y_inter = jnp.dot(cgb, ht_g.astype(BF16), preferred_element_type=F32)
        yi_ref[:, g * hp:(g + 1) * hp] = (yi_ref[:, g * hp:(g + 1) * hp]
                                          + y_inter * jnp.exp(cum_e[:, g * hp:(g + 1) * hp]))
        upd = jnp.dot(bg.T.astype(BF16), xw[:, g * hp:(g + 1) * hp],
                      preferred_element_type=F32)
        ht_ref[:, g * hp:(g + 1) * hp] = ht_g * jnp.exp(last[:, g * hp:(g + 1) * hp]) + upd

    y = yi_ref[...] + dexp_ref[...] * xs
    y = y * _silu(_lane_aligned(z_ref[0], SM_DT, SSD_INNER))
    for g in range(SSD_GROUPS):
        yg = y[:, g * hp:(g + 1) * hp]
        ms = jnp.mean(yg * yg, axis=-1, keepdims=True)
        y_ref[0, :, g * hp:(g + 1) * hp] = (yg * lax.rsqrt(ms + EPS)
                                            * nw_ref[:, g * hp:(g + 1) * hp]).astype(BF16)

    @pl.when(c == nc - 1)
    def _():
        stout_ref[0] = ht_ref[...].T.reshape(SSD_HEADS, SSD_HEADDIM, SSD_STATE)


def _ssd(xbc_act, proj, sm, state, a_log_pad, d_exp, norm_w, e_mat, *, chunk):
    b, t, _ = xbc_act.shape
    tok = lambda w, col=0: pl.BlockSpec((1, chunk, w), lambda i, j: (i, j, col))
    full2 = lambda a: pl.BlockSpec(a.shape, lambda i, j: (0, 0))
    st = pl.BlockSpec((1, SSD_HEADS, SSD_HEADDIM, SSD_STATE), lambda i, j: (i, 0, 0, 0))
    nw = norm_w.reshape(1, SSD_INNER)
    return pl.pallas_call(
        _ssd_kernel,
        grid=(b, t // chunk),
        in_specs=[tok(SSD_CONV_DIM), tok(Z_BLOCK, (OFF_Z - SM_DT) // Z_BLOCK), tok(LANES), st,
                  full2(a_log_pad), full2(d_exp), full2(nw), full2(e_mat)],
        out_specs=[tok(SSD_INNER), st],
        out_shape=[jax.ShapeDtypeStruct((b, t, SSD_INNER), BF16),
                   jax.ShapeDtypeStruct(state.shape, F32)],
        scratch_shapes=[pltpu.VMEM((SSD_STATE, SSD_INNER), F32),
                        pltpu.VMEM((chunk, SSD_INNER), F32)],
        compiler_params=_params(2),
        name="ssd_scan",
    )(xbc_act, proj, sm, state, a_log_pad, d_exp, nw, e_mat)


def _lru_kernel(gate_ref, xin_ref, cw_ref, cb_ref, wa_ref, wx_ref, ba_ref, bx_ref, lam_ref,
                h0_ref, cs_ref, y_ref, hout_ref, cnew_ref, cbuf_ref, a_ref, u_ref, hp_ref):
    t = pl.program_id(1)
    nt = pl.num_programs(1)
    tt = xin_ref.shape[1]
    xc = _causal_conv(xin_ref[0], cs_ref, cw_ref, cb_ref, cbuf_ref, cnew_ref, t == 0, t == nt - 1)

    @pl.when(t == 0)
    def _():
        hp_ref[...] = h0_ref[0]

    for kb in range(LRU_BLOCKS):
        sl = slice(kb * LRU_BLOCK_DIM, (kb + 1) * LRU_BLOCK_DIM)
        xb = xc[:, sl].astype(BF16)
        a_ref[:, sl] = jnp.dot(xb, wa_ref[kb].astype(BF16), preferred_element_type=F32)
        u_ref[:, sl] = jnp.dot(xb, wx_ref[kb].astype(BF16), preferred_element_type=F32)
    r = jax.nn.sigmoid(a_ref[...] + ba_ref[...])
    i = jax.nn.sigmoid(u_ref[...] + bx_ref[...])
    log_a = -LRU_C * r * _softplus(-lam_ref[...])
    a = jnp.exp(log_a)
    u = jnp.sqrt(-jnp.tanh(log_a) * (1.0 + a * a)) * (i * xc)

    width = a.shape[1]
    a = a.reshape(tt // SUBLANES, SUBLANES, width)
    u = u.reshape(tt // SUBLANES, SUBLANES, width)
    sub = lax.broadcasted_iota(jnp.int32, (1, SUBLANES, width), 1)
    for s in (1, 2, 4):
        m = sub >= s
        a_sh = pltpu.roll(a, s, 1)
        u_sh = pltpu.roll(u, s, 1)
        u = jnp.where(m, a * u_sh + u, u)
        a = jnp.where(m, a * a_sh, a)
    a_ref[...] = a.reshape(tt, width)
    u_ref[...] = u.reshape(tt, width)

    def group(gi, hprev):
        r0 = pl.multiple_of(gi * SUBLANES, SUBLANES)
        hs = a_ref[pl.ds(r0, SUBLANES), :] * hprev + u_ref[pl.ds(r0, SUBLANES), :]
        u_ref[pl.ds(r0, SUBLANES), :] = hs
        return hs[SUBLANES - 1:SUBLANES, :]

    h_last = lax.fori_loop(0, tt // SUBLANES, group, hp_ref[...])
    hp_ref[...] = h_last
    gate = gate_ref[0]
    gelu = 0.5 * gate * (1.0 + jnp.tanh(np.sqrt(2.0 / np.pi) * (gate + 0.044715 * (gate * gate * gate))))
    y_ref[0] = (u_ref[...] * gelu).astype(BF16)

    @pl.when(t == nt - 1)
    def _():
        hout_ref[0] = h_last


def _lru(proj, conv_w, conv_b, w_a, b_a, w_x, b_x, lam, h0, conv_state, *, tt):
    b, t, _ = proj.shape
    w = LRU_WIDTH
    tok = lambda col: pl.BlockSpec((1, tt, w), lambda i, j: (i, j, col))
    full = lambda a: pl.BlockSpec(a.shape, lambda i, j: (0,) * a.ndim)
    row = lambda v: v.reshape(1, w)
    state = pl.BlockSpec((1, CONV_W - 1, w), lambda i, j: (i, 0, 0))
    hspec = pl.BlockSpec((1, 1, w), lambda i, j: (i, 0, 0))
    args = (proj, proj, conv_w, row(conv_b), w_a, w_x, row(b_a), row(b_x), row(lam), h0, conv_state)
    return pl.pallas_call(
        _lru_kernel,
        grid=(b, t // tt),
        in_specs=[tok(0), tok(1)] + [full(a) for a in args[2:9]] + [hspec, state],
        out_specs=[pl.BlockSpec((1, tt, w), lambda i, j: (i, j, 0)), hspec, state],
        out_shape=[jax.ShapeDtypeStruct((b, t, w), BF16),
                   jax.ShapeDtypeStruct((b, 1, w), F32),
                   jax.ShapeDtypeStruct((b, CONV_W - 1, w), F32)],
        scratch_shapes=[pltpu.VMEM((tt + SUBLANES, w), F32), pltpu.VMEM((tt, w), F32),
                        pltpu.VMEM((tt, w), F32), pltpu.VMEM((1, w), F32)],
        compiler_params=_params(2),
        name="rg_lru",
    )(*args)


def _rope_tables(pos, dim):
    half = dim // 2
    inv = np.power(ROPE_THETA, -np.arange(half, dtype=np.float64) / half)
    ang = pos.astype(np.float64)[:, None] * inv[None, :]
    cos = np.concatenate([np.cos(ang), np.cos(ang)], axis=-1)
    sin = np.concatenate([-np.sin(ang), np.sin(ang)], axis=-1)
    reps = LANES // dim
    return (jnp.asarray(np.tile(cos, (1, reps)), F32), jnp.asarray(np.tile(sin, (1, reps)), F32))


def _head_expansion():
    e = np.zeros((LANES, SSD_INNER), np.float32)
    for h in range(SSD_HEADS):
        e[SM_DT + h, h * SSD_HEADDIM:(h + 1) * SSD_HEADDIM] = 1.0
    return jnp.asarray(e, BF16)


def _pad_lanes(v, offset):
    out = jnp.zeros((1, LANES), F32)
    return out.at[0, offset:offset + v.shape[0]].set(v)


def _w_in_ab_tail(w_t):
    whole = AB_WIDTH // AB_TILE_N * AB_TILE_N
    return jnp.pad(w_t[whole:], ((0, AB_PAD_N - AB_WIDTH), (0, 0)))


def _trunk(x, mods, past_k, past_v, past_ki, ssm_h, ssm_conv, lru_h, lru_conv, p, *, cfg):
    b, t, d = x.shape
    bb, tt = cfg["bb"], cfg["tt"]
    past = 0 if past_k is None else past_k.shape[1]
    pos = np.arange(past, past + t)
    tabs = _rope_tables(pos, HEAD_DIM) + _rope_tables(pos, IDX_DIM)

    def ffn(x, l):
        sh_f, sc_f, g_f = mods[l][3], mods[l][4], mods[l][5]
        gu = _norm_mod_matmul(x, p["norm_ffn_w"][l], sc_f, sh_f, [p["ffn_w_gate"], p["ffn_w_up"]], l,
                              bb=bb, tt=cfg["tt_ffn"], tn=512, out_dtype=BF16, name="ffn_gate_up")
        return _matmul_residual([gu], p["ffn_w_down"], l, x, g_f, bb=bb, tt=cfg["tt_ffn"], tn=256,
                                name="ffn_down")

    sh_m, sc_m, g_m = mods[0][0], mods[0][1], mods[0][2]
    proj = _norm_mod_matmul(x, p["norm_mix_w"][0], sc_m, sh_m, [p["w_in_ab_t"]], 0, bb=bb,
                            tt=cfg["tt_ab"], tn=AB_TILE_N, out_dtype=F32, name="in_proj_ab",
                            w_tail=p["w_in_ab_tail"], w_rows_out=True)
    (q, qi, k, v, ki, sm, xbc_act, ssm_conv_new, kb, vbe, kib) = _post0(
        proj, tabs, p["q_norm_w"][0], p["k_norm_w"][0], p["ssd_conv_w"][0], p["ssd_conv_b"][0],
        p["dt_bias_pad"], ssm_conv, tt=cfg["post_tt"])
    cache = None if past_k is None else (past_k.reshape(b, past * KV_HEADS, HEAD_DIM),
                                         past_v.reshape(b, past * KV_HEADS, HEAD_DIM), past_ki)
    att = _dsa_attention(q, qi, sm, kb, vbe, kib, cache, qb=cfg["qb"], tk=cfg["tk"])
    y_ssd, ssm_new = _ssd(xbc_act, proj, sm, ssm_h, p["a_log_pad"], p["d_exp"], p["ssd_norm_w"][0],
                          p["e_mat"], chunk=cfg["ssd_chunk"])
    x = _matmul_residual([att, y_ssd], p["w_out_ab"], 0, x, g_m, bb=bb, tt=tt, tn=512,
                         name="out_proj_ab")
    x = ffn(x, 0)

    sh_m, sc_m, g_m = mods[1][0], mods[1][1], mods[1][2]
    proj = _norm_mod_matmul(x, p["norm_mix_w"][1], sc_m, sh_m, [p["w_in_c"]], 0,
                            bb=bb, tt=tt, tn=cfg["tn_c"], out_dtype=F32, name="in_proj_c")
    y_lru, lru_new, lru_conv_new = _lru(
        proj, p["lru_conv_w"][0], p["lru_conv_b"][0], p["lru_w_a"][0], p["lru_b_a"][0],
        p["lru_w_x"][0], p["lru_b_x"][0], p["lru_lambda"][0], lru_h.reshape(b, 1, LRU_WIDTH),
        lru_conv, tt=cfg["lru_tt"])
    x = _matmul_residual([y_lru], p["w_out_c"], 0, x, g_m, bb=bb, tt=tt, tn=512, name="out_proj_c")
    x = ffn(x, 1)

    kv_shape = (1, b, t, KV_HEADS, HEAD_DIM)
    return (x, k.reshape(kv_shape), v.reshape(kv_shape), ki[None], ssm_new[None], ssm_conv_new[None],
            lru_new.reshape(1, b, LRU_WIDTH), lru_conv_new[None])


PROMPT_CFG = dict(bb=1, tt=2048, tt_ab=2048, tt_ffn=1024, tn_c=256, post_tt=512, qb=128, tk=512,
                  ssd_chunk=128, lru_tt=512)
SAMPLE_CFG = dict(bb=16, tt=32, tt_ab=32, tt_ffn=32, tn_c=512, post_tt=32, qb=32, tk=384,
                  ssd_chunk=32, lru_tt=32)


def kernel(x_prompt, x_sample, cache_attn_k, cache_attn_v, cache_idx_k, state_ssm, state_ssm_conv,
           state_lru, state_lru_conv, c_prompt, c_sample, ada_w, ada_b, norm_mix_w, norm_ffn_w,
           w_in_ab, q_norm_w, k_norm_w, ssd_conv_w, ssd_conv_b, ssd_dt_bias, ssd_a_log, ssd_d,
           ssd_norm_w, w_out_ab, w_in_c, lru_conv_w, lru_conv_b, lru_w_a, lru_b_a, lru_w_x, lru_b_x,
           lru_lambda, w_out_c, ffn_w_gate, ffn_w_up, ffn_w_down):
    bp, bs = x_prompt.shape[0], x_sample.shape[0]
    p = dict(norm_mix_w=norm_mix_w, norm_ffn_w=norm_ffn_w, q_norm_w=q_norm_w, k_norm_w=k_norm_w,
             ssd_conv_w=ssd_conv_w, ssd_conv_b=ssd_conv_b, ssd_norm_w=ssd_norm_w, w_out_ab=w_out_ab,
             w_in_c=w_in_c, lru_conv_w=lru_conv_w, lru_conv_b=lru_conv_b, lru_w_a=lru_w_a,
             lru_b_a=lru_b_a, lru_w_x=lru_w_x, lru_b_x=lru_b_x, lru_lambda=lru_lambda,
             w_out_c=w_out_c, ffn_w_gate=ffn_w_gate, ffn_w_up=ffn_w_up, ffn_w_down=ffn_w_down)
    p["w_in_ab_t"] = jnp.swapaxes(w_in_ab, 1, 2)
    p["w_in_ab_tail"] = _w_in_ab_tail(p["w_in_ab_t"][0])
    p["dt_bias_pad"] = _pad_lanes(ssd_dt_bias[0], SM_DT)
    p["a_log_pad"] = _pad_lanes(ssd_a_log[0], SM_DT)
    p["d_exp"] = jnp.repeat(ssd_d[0], SSD_HEADDIM).reshape(1, SSD_INNER)
    p["e_mat"] = _head_expansion()

    m_rows = 32
    c_all = jnp.concatenate([c_prompt, c_sample, jnp.zeros((m_rows - bp - bs, D_MODEL), F32)], axis=0)
    mod = _modulation(c_all, ada_w, ada_b)

    def group_mods(r0, nb):
        return [[mod[l, r0:r0 + nb, i * D_MODEL:(i + 1) * D_MODEL].reshape(nb, 1, D_MODEL)
                 for i in range(6)] for l in range(mod.shape[0])]

    zeros = lambda *s: jnp.zeros(s, F32)
    out_p = _trunk(x_prompt, group_mods(0, bp), None, None, None,
                   zeros(bp, SSD_HEADS, SSD_HEADDIM, SSD_STATE), zeros(bp, CONV_W - 1, SSD_CONV_DIM),
                   zeros(bp, LRU_WIDTH), zeros(bp, CONV_W - 1, LRU_WIDTH), p, cfg=PROMPT_CFG)
    out_s = _trunk(x_sample, group_mods(bp, bs), cache_attn_k[0], cache_attn_v[0], cache_idx_k[0],
                   state_ssm[0], state_ssm_conv[0], state_lru[0], state_lru_conv[0], p, cfg=SAMPLE_CFG)
    return (out_p[0], out_s[0]) + out_p[1:] + out_s[1:]
```

```python
import functools

import numpy as np
import jax
import jax.numpy as jnp
from jax import lax
from jax.experimental import pallas as pl
from jax.experimental.pallas import tpu as pltpu

F32 = jnp.float32
BF16 = jnp.bfloat16
HIGHEST = lax.Precision.HIGHEST

D_MODEL = 2048
CHUNK = 64
CHUNK_SHIFT = 6
ATT_HEADS = 8
KV_HEADS = 2
HEAD_DIM = 128
IDX_HEADS = 16
IDX_DIM = 64
TOPK_MAX = 256
ROPE_THETA = 10000.0
SSD_HEADS = 16
SSD_HEADDIM = 64
SSD_GROUPS = 2
SSD_STATE = 128
SSD_INNER = SSD_HEADS * SSD_HEADDIM
SSD_CONV_DIM = SSD_INNER + 2 * SSD_GROUPS * SSD_STATE
CONV_W = 4
LRU_WIDTH = D_MODEL
LRU_BLOCKS = 16
LRU_BLOCK_DIM = LRU_WIDTH // LRU_BLOCKS
LRU_C = 8.0
EPS = 1e-6
ATT_WIDTH = ATT_HEADS * HEAD_DIM
KV_WIDTH = KV_HEADS * HEAD_DIM
VEXT_WIDTH = 2 * KV_WIDTH
QI_WIDTH = IDX_HEADS * IDX_DIM
Q_SCALE = HEAD_DIM ** -0.5 * float(np.log2(np.e))

LANES = 128
SUBLANES = 8
VMEM_LIMIT_BYTES = 62 * 1024 * 1024
SINGLE_BUFFER_ROWS = 2048

OFF_Q = 0
OFF_K = OFF_Q + ATT_WIDTH
OFF_V = OFF_K + KV_WIDTH
OFF_QI = OFF_V + KV_WIDTH
OFF_SMALL = OFF_QI + QI_WIDTH
SM_WI = IDX_DIM
SM_DT = IDX_DIM + IDX_HEADS
OFF_Z = OFF_SMALL + SM_DT
OFF_XBC = OFF_Z + SSD_INNER
OFF_DT = OFF_XBC + SSD_CONV_DIM
AB_WIDTH = OFF_DT + SSD_HEADS
Z_BLOCK = 1280
AB_TILE_N = 512
AB_PAD_N = 5632
assert OFF_Z % LANES == SM_DT and OFF_XBC % LANES == SM_DT and OFF_DT % LANES == SM_DT
assert (OFF_Z - SM_DT) % Z_BLOCK == 0 and OFF_Z - SM_DT + Z_BLOCK >= OFF_Z + SSD_INNER

NEG_BIG = -1e30


def _params(n_axes):
    return pltpu.CompilerParams(dimension_semantics=("arbitrary",) * n_axes,
                                vmem_limit_bytes=VMEM_LIMIT_BYTES)


def _silu(x):
    return x * jax.nn.sigmoid(x)


def _softplus(x):
    return jnp.maximum(x, 0.0) + jnp.log1p(jnp.exp(-jnp.abs(x)))


def _mod_kernel(c_ref, w_ref, b_ref, o_ref):
    a = _silu(c_ref[...]).astype(BF16)
    o_ref[0] = jnp.dot(a, w_ref[0].astype(BF16), preferred_element_type=F32) + b_ref[0]


def _modulation(c_all, ada_w, ada_b):
    depth, d, n = ada_w.shape
    m = c_all.shape[0]
    tn = 1024
    return pl.pallas_call(
        _mod_kernel,
        grid=(depth, n // tn),
        in_specs=[pl.BlockSpec((m, d), lambda l, j: (0, 0)),
                  pl.BlockSpec((1, d, tn), lambda l, j: (l, 0, j)),
                  pl.BlockSpec((1, 1, tn), lambda l, j: (l, 0, j))],
        out_specs=pl.BlockSpec((1, m, tn), lambda l, j: (l, 0, j)),
        out_shape=jax.ShapeDtypeStruct((depth, m, n), F32),
        compiler_params=_params(2),
        name="adaln_mod",
    )(c_all, ada_w, ada_b.reshape(depth, 1, n))


def _first_row_tile():
    return (pl.program_id(0) == 0) & (pl.program_id(1) == 0)


def _nmm_kernel(*refs, swiglu, tail, w_rows_out, ride):
    refs = list(refs)
    x_ref, nw_ref, sc_ref, sh_ref = refs[:4]
    del refs[:4]
    if ride:
        xr_ref, scr_ref, shr_ref = refs[:3]
        del refs[:3]
    n_w = 2 if swiglu or tail else 1
    w_refs = refs[:n_w]
    del refs[:n_w]
    if ride:
        o_ref, or_ref, h_ref, hr_ref = refs
    else:
        o_ref, h_ref = refs
    k = pl.program_id(2)

    def norm_mod(src_ref, scale_ref, shift_ref, dst_ref):
        x = src_ref[...]
        ms = jnp.mean(x * x, axis=-1, keepdims=True)
        y = x * lax.rsqrt(ms + EPS) * nw_ref[...]
        h = y * (1.0 + scale_ref[...]) + shift_ref[...]
        dst_ref[...] = h.reshape(dst_ref.shape).astype(BF16)

    pl.when(k == 0)(lambda: norm_mod(x_ref, sc_ref, sh_ref, h_ref))
    if ride:
        pl.when(_first_row_tile() & (k == 0))(lambda: norm_mod(xr_ref, scr_ref, shr_ref, hr_ref))

    def project(ws, rows_ref, out_ref):
        h = rows_ref[...]
        if swiglu:
            g = jnp.dot(h, ws[0], preferred_element_type=F32)
            u = jnp.dot(h, ws[1], preferred_element_type=F32)
            o = _silu(g) * u
        else:
            dims = (((1,), (1,)), ((), ())) if w_rows_out else (((1,), (0,)), ((), ()))
            o = lax.dot_general(h, ws[0], dims, preferred_element_type=F32)
        out_ref[...] = o.reshape(out_ref.shape).astype(out_ref.dtype)

    def step(tile_refs):
        ws = [w[...].astype(BF16) for w in tile_refs]
        project(ws, h_ref, o_ref)
        if ride:
            pl.when(_first_row_tile())(lambda: project(ws, hr_ref, or_ref))

    if tail:
        last = pl.num_programs(2) - 1
        pl.when(k < last)(lambda: step(w_refs[:1]))
        pl.when(k == last)(lambda: step(w_refs[1:]))
    else:
        step(w_refs)


def _row_block_mode(rows):
    return dict(pipeline_mode=pl.Buffered(1)) if rows >= SINGLE_BUFFER_ROWS else {}


def _ride_col_tile(n_tiles):
    return lambda i, j, k: (0, 0, jnp.where((i == 0) & (j == 0), k, n_tiles - 1))


def _norm_mod_matmul(x, nw, sc, sh, ws, layer, *, bb, tt, tn, out_dtype, name, w_tail=None,
                     w_rows_out=False, ride=None):
    b, t, d = x.shape
    swiglu = len(ws) == 2
    n_main = ws[0].shape[1 if w_rows_out else 2] // tn
    n_tiles = n_main + (w_tail is not None)
    n = n_tiles * tn
    xmap = lambda i, j, k: (i, j, 0)
    mmap = lambda i, j, k: (i, 0, 0)
    const3 = lambda i, j, k: (0, 0, 0)
    tile = lambda k: jnp.minimum(k, n_main - 1)
    if w_rows_out:
        w_specs = [pl.BlockSpec((None, tn, d), lambda i, j, k: (layer, tile(k), 0))] * len(ws)
    else:
        w_specs = [pl.BlockSpec((None, d, tn), lambda i, j, k: (layer, 0, tile(k)))] * len(ws)
    operands = list(ws)
    if w_tail is not None:
        w_specs.append(pl.BlockSpec(w_tail.shape, lambda i, j, k: (0, 0)))
        operands.append(w_tail)
    in_specs = [pl.BlockSpec((bb, tt, d), xmap, **_row_block_mode(bb * tt)),
                pl.BlockSpec((1, 1, d), const3),
                pl.BlockSpec((bb, 1, d), mmap),
                pl.BlockSpec((bb, 1, d), mmap)]
    out_specs = [pl.BlockSpec((bb, tt, tn), lambda i, j, k: (i, j, k))]
    out_shape = [jax.ShapeDtypeStruct((b, t, n), out_dtype)]
    scratch = [pltpu.VMEM((bb * tt, d), BF16)]
    ride_ops = ()
    if ride is not None:
        ride_ops = ride
        rb, rt, _ = ride[0].shape
        in_specs += [pl.BlockSpec((rb, rt, d), const3, pipeline_mode=pl.Buffered(1)),
                     pl.BlockSpec((rb, 1, d), const3), pl.BlockSpec((rb, 1, d), const3)]
        out_specs.append(pl.BlockSpec((rb, rt, tn), _ride_col_tile(n_tiles)))
        out_shape.append(jax.ShapeDtypeStruct((rb, rt, n), out_dtype))
        scratch.append(pltpu.VMEM((rb * rt, d), BF16))
    outs = pl.pallas_call(
        functools.partial(_nmm_kernel, swiglu=swiglu, tail=w_tail is not None,
                          w_rows_out=w_rows_out, ride=ride is not None),
        grid=(b // bb, t // tt, n_tiles),
        in_specs=in_specs + w_specs,
        out_specs=out_specs,
        out_shape=out_shape,
        scratch_shapes=scratch,
        compiler_params=_params(3),
        name=name,
    )(x, nw.reshape(1, 1, d), sc, sh, *ride_ops, *operands)
    return outs if ride is not None else outs[0]


def _mmr_kernel(*refs, k_sizes, ride):
    refs = list(refs)
    n_a = len(k_sizes)
    groups = [refs[:n_a + 2]]
    del refs[:n_a + 2]
    if ride:
        groups.append(refs[:n_a + 2])
        del refs[:n_a + 2]
    w_ref = refs[0]
    out_refs = refs[1:]
    offs = np.cumsum((0,) + tuple(k_sizes))
    ws = [w_ref[offs[i]:offs[i + 1], :].astype(BF16) for i in range(n_a)]

    def apply(group, o_ref):
        *a_refs, x_ref, g_ref = group
        bb, tt, tn = o_ref.shape
        acc = jnp.zeros((bb * tt, tn), F32)
        for a_ref, w, ks in zip(a_refs, ws, k_sizes):
            acc = acc + jnp.dot(a_ref[...].reshape(bb * tt, ks), w, preferred_element_type=F32)
        o_ref[...] = x_ref[...] + g_ref[...] * acc.reshape(bb, tt, tn)

    apply(groups[0], out_refs[0])
    if ride:
        pl.when(_first_row_tile())(lambda: apply(groups[1], out_refs[1]))


def _matmul_residual(a_list, w, layer, x, g, *, bb, tt, tn, name, ride=None):
    b, t, d = x.shape
    k_sizes = tuple(a.shape[-1] for a in a_list)
    k_total = sum(k_sizes)
    n_tiles = d // tn
    in_specs = ([pl.BlockSpec((bb, tt, ks), lambda i, j, k: (i, j, 0), **_row_block_mode(bb * tt))
                 for ks in k_sizes]
                + [pl.BlockSpec((bb, tt, tn), lambda i, j, k: (i, j, k)),
                   pl.BlockSpec((bb, 1, tn), lambda i, j, k: (i, 0, k))])
    out_specs = [pl.BlockSpec((bb, tt, tn), lambda i, j, k: (i, j, k))]
    out_shape = [jax.ShapeDtypeStruct((b, t, d), F32)]
    ride_ops = ()
    if ride is not None:
        ra_list, rx, rg = ride
        rb, rt, _ = rx.shape
        ride_ops = (*ra_list, rx, rg)
        col = _ride_col_tile(n_tiles)
        in_specs += ([pl.BlockSpec((rb, rt, ks), lambda i, j, k: (0, 0, 0),
                                   pipeline_mode=pl.Buffered(1)) for ks in k_sizes]
                     + [pl.BlockSpec((rb, rt, tn), col), pl.BlockSpec((rb, 1, tn), col)])
        out_specs.append(pl.BlockSpec((rb, rt, tn), col))
        out_shape.append(jax.ShapeDtypeStruct(rx.shape, F32))
    outs = pl.pallas_call(
        functools.partial(_mmr_kernel, k_sizes=k_sizes, ride=ride is not None),
        grid=(b // bb, t // tt, n_tiles),
        in_specs=in_specs + [pl.BlockSpec((None, k_total, tn), lambda i, j, k: (layer, 0, k))],
        out_specs=out_specs,
        out_shape=out_shape,
        compiler_params=_params(3),
        name=name,
    )(*a_list, x, g, *ride_ops, w)
    return outs if ride is not None else outs[0]


def _swap_halves(x, half):
    w = x.shape[-1]
    lane = lax.broadcasted_iota(jnp.int32, x.shape, x.ndim - 1)
    first = (lane & half) == 0
    return jnp.where(first, pltpu.roll(x, w - half, x.ndim - 1), pltpu.roll(x, half, x.ndim - 1))


def _lane_aligned(slab, offset, width):
    return pltpu.roll(slab, slab.shape[-1] - offset, slab.ndim - 1)[:, :width]


def _rope(x, cos, sin_signed, half):
    reps = x.shape[-1] // cos.shape[-1]
    if reps > 1:
        cos = jnp.concatenate([cos] * reps, axis=-1)
        sin_signed = jnp.concatenate([sin_signed] * reps, axis=-1)
    return x * cos + _swap_halves(x, half) * sin_signed


def _causal_conv(u, state_ref, w_ref, b_ref, cbuf_ref, new_ref, first, last):
    tt = u.shape[0]

    @pl.when(first)
    def _():
        cbuf_ref[0:SUBLANES, :] = jnp.zeros((SUBLANES, u.shape[1]), F32)
        cbuf_ref[SUBLANES - (CONV_W - 1):SUBLANES, :] = state_ref[0]

    cbuf_ref[SUBLANES:SUBLANES + tt, :] = u
    full = cbuf_ref[...]
    out = b_ref[...]
    for j in range(CONV_W):
        shift = CONV_W - 1 - j
        rows = pltpu.roll(full, shift, 0) if shift else full
        out = out + rows[SUBLANES:SUBLANES + tt, :] * w_ref[j:j + 1, :]
    tail = cbuf_ref[tt:tt + SUBLANES, :]
    cbuf_ref[0:SUBLANES, :] = tail

    @pl.when(last)
    def _():
        new_ref[0] = tail[SUBLANES - (CONV_W - 1):, :]

    return out


def _post0_kernel(p_ref, cq_ref, sq_ref, ci_ref, si_ref, qn_ref, kn_ref, cw_ref, cb_ref, dtb_ref,
                  cs_ref,
                  q_ref, qi_ref, k_ref, v_ref, ki_ref, sm_ref, xbc_ref, cnew_ref, kb_ref, vb_ref,
                  kib_ref, cbuf_ref):
    t = pl.program_id(1)
    nt = pl.num_programs(1)
    cq, sq, ci, si = cq_ref[...], sq_ref[...], ci_ref[...], si_ref[...]

    def head_norm_rope(x, w):
        ms = jnp.mean(x * x, axis=-1, keepdims=True)
        return _rope(x * lax.rsqrt(ms + EPS) * w, cq, sq, HEAD_DIM // 2)

    for h in range(ATT_HEADS):
        xh = p_ref[0, :, OFF_Q + h * HEAD_DIM:OFF_Q + (h + 1) * HEAD_DIM]
        q_ref[0, :, h * HEAD_DIM:(h + 1) * HEAD_DIM] = (head_norm_rope(xh, qn_ref[...])
                                                         * Q_SCALE).astype(BF16)
    ones = jnp.ones((p_ref.shape[1], HEAD_DIM), BF16)
    for h in range(KV_HEADS):
        xh = p_ref[0, :, OFF_K + h * HEAD_DIM:OFF_K + (h + 1) * HEAD_DIM]
        kh = head_norm_rope(xh, kn_ref[...])
        vh = p_ref[0, :, OFF_V + h * HEAD_DIM:OFF_V + (h + 1) * HEAD_DIM]
        k_ref[0, pl.ds(h, xh.shape[0], stride=KV_HEADS), :] = kh
        v_ref[0, pl.ds(h, xh.shape[0], stride=KV_HEADS), :] = vh
        kb_ref[0, :, h * HEAD_DIM:(h + 1) * HEAD_DIM] = kh.astype(BF16)
        vb_ref[0, :, 2 * h * HEAD_DIM:(2 * h + 1) * HEAD_DIM] = vh.astype(BF16)
        vb_ref[0, :, (2 * h + 1) * HEAD_DIM:(2 * h + 2) * HEAD_DIM] = ones

    qi = p_ref[0, :, OFF_QI:OFF_QI + QI_WIDTH]
    qi_ref[0] = _rope(qi, ci, si, IDX_DIM // 2).astype(BF16)

    sm = p_ref[0, :, OFF_SMALL:OFF_SMALL + LANES]
    lane = lax.broadcasted_iota(jnp.int32, sm.shape, 1)
    ki_part = _rope(sm, ci, si, IDX_DIM // 2)
    wi_part = sm * (IDX_HEADS ** -0.5 * IDX_DIM ** -0.5)
    dt_col = OFF_DT - SM_DT
    dt_part = _softplus(p_ref[0, :, dt_col:dt_col + LANES] + dtb_ref[...])
    sm_out = jnp.where(lane < SM_WI, ki_part,
                       jnp.where(lane < SM_DT, wi_part,
                                 jnp.where(lane < SM_DT + SSD_HEADS, dt_part, 0.0)))
    sm_ref[0] = sm_out
    ki_ref[0] = sm_out[:, :IDX_DIM]
    kib_ref[0] = sm_out[:, :IDX_DIM].astype(BF16)

    xbc = _lane_aligned(p_ref[0, :, OFF_XBC - SM_DT:OFF_DT - SM_DT + LANES], SM_DT, SSD_CONV_DIM)
    conv = _causal_conv(xbc, cs_ref, cw_ref, cb_ref, cbuf_ref, cnew_ref, t == 0, t == nt - 1)
    xbc_ref[0] = _silu(conv)


def _post0(proj, tabs, q_norm_w, k_norm_w, conv_w, conv_b, dt_bias_pad, conv_state, *, tt):
    b, t, n = proj.shape
    tok = lambda w: pl.BlockSpec((1, tt, w), lambda i, j: (i, j, 0))
    tab = pl.BlockSpec((tt, LANES), lambda i, j: (j, 0))
    full2 = lambda a: pl.BlockSpec(a.shape, lambda i, j: (0, 0))
    state = lambda c: pl.BlockSpec((1, CONV_W - 1, c), lambda i, j: (i, 0, 0))
    out_shapes = [
        jax.ShapeDtypeStruct((b, t, ATT_WIDTH), BF16),
        jax.ShapeDtypeStruct((b, t, QI_WIDTH), BF16),
        jax.ShapeDtypeStruct((b, t * KV_HEADS, HEAD_DIM), F32),
        jax.ShapeDtypeStruct((b, t * KV_HEADS, HEAD_DIM), F32),
        jax.ShapeDtypeStruct((b, t, IDX_DIM), F32),
        jax.ShapeDtypeStruct((b, t, LANES), F32),
        jax.ShapeDtypeStruct((b, t, SSD_CONV_DIM), F32),
        jax.ShapeDtypeStruct((b, CONV_W - 1, SSD_CONV_DIM), F32),
        jax.ShapeDtypeStruct((b, t, KV_WIDTH), BF16),
        jax.ShapeDtypeStruct((b, t, VEXT_WIDTH), BF16),
        jax.ShapeDtypeStruct((b, t, IDX_DIM), BF16),
    ]
    kv_rows = pl.BlockSpec((1, tt * KV_HEADS, HEAD_DIM), lambda i, j: (i, j, 0))
    out_specs = [tok(ATT_WIDTH), tok(QI_WIDTH), kv_rows, kv_rows, tok(IDX_DIM),
                 tok(LANES), tok(SSD_CONV_DIM), state(SSD_CONV_DIM), tok(KV_WIDTH), tok(VEXT_WIDTH),
                 tok(IDX_DIM)]
    qn = q_norm_w.reshape(1, HEAD_DIM)
    kn = k_norm_w.reshape(1, HEAD_DIM)
    cb = conv_b.reshape(1, SSD_CONV_DIM)
    return pl.pallas_call(
        _post0_kernel,
        grid=(b, t // tt),
        in_specs=[tok(n), tab, tab, tab, tab, full2(qn), full2(kn), full2(conv_w), full2(cb),
                  full2(dt_bias_pad), state(SSD_CONV_DIM)],
        out_specs=out_specs,
        out_shape=out_shapes,
        scratch_shapes=[pltpu.VMEM((tt + SUBLANES, SSD_CONV_DIM), F32)],
        compiler_params=_params(2),
        name="post_proj0",
    )(proj, *tabs, qn, kn, conv_w, cb, dt_bias_pad, conv_state)


def _dsa_kernel(q_ref, qi_ref, sm_ref, k_ref, v_ref, ki_ref, *rest, qb, tk, n_keys, past, topk,
                n_bisect, keys_on_sublanes):
    if past:
        pk_ref, pv_ref, pki_ref = rest[:3]
        rest = rest[3:]
    (o_ref, sc_ref, bias_ref, acc_ref, qs_ref, s0_ref, s1_ref, mt0_ref, mt1_ref, m_ref) = rest[:10]
    if past:
        kall_ref, vall_ref, kiall_ref = rest[10:]
        t_new = k_ref.shape[1]
        pad = kall_ref.shape[0] - n_keys
        for g in range(KV_HEADS):
            head_rows = pl.ds(g, past, stride=KV_HEADS)
            kall_ref[0:past, g * HEAD_DIM:(g + 1) * HEAD_DIM] = pk_ref[0, head_rows, :].astype(BF16)
            vall_ref[0:past, 2 * g * HEAD_DIM:(2 * g + 1) * HEAD_DIM] = pv_ref[0, head_rows, :].astype(BF16)
            vall_ref[0:past, (2 * g + 1) * HEAD_DIM:(2 * g + 2) * HEAD_DIM] = jnp.ones(
                (past, HEAD_DIM), BF16)
        kiall_ref[0:past, :] = pki_ref[0].astype(BF16)
        for dst, src in ((kall_ref, k_ref), (vall_ref, v_ref), (kiall_ref, ki_ref)):
            dst[past:past + t_new, :] = src[0]
            dst[past + t_new:, :] = jnp.zeros((pad, dst.shape[1]), BF16)
        k_src, v_src, ki_src = kall_ref, vall_ref, kiall_ref
    else:
        k_src, v_src, ki_src = k_ref.at[0], v_ref.at[0], ki_ref.at[0]
    kax = 0 if keys_on_sublanes else 1
    tile_shape = (tk, qb) if keys_on_sublanes else (qb, tk)
    vec_shape = (1, qb) if keys_on_sublanes else (qb, 1)
    n_acc = 8
    part_shape = (n_acc, SUBLANES, qb) if keys_on_sublanes else (qb, LANES)
    j = pl.program_id(1)
    pos0 = past + j * qb
    q_chunk = lax.shift_right_logical(pos0 + lax.broadcasted_iota(jnp.int32, vec_shape, 1 - kax),
                                      CHUNK_SHIFT)
    n_valid = jnp.minimum((q_chunk + 1) * CHUNK, n_keys)
    max_valid = jnp.minimum(((pos0 + qb - 1) // CHUNK + 1) * CHUNK, n_keys)
    nkt = (max_valid + tk - 1) // tk
    kf = float(topk)
    active_f = jnp.where(n_valid > topk, 1.0, 0.0)

    def key_index(kt):
        return kt * tk + lax.broadcasted_iota(jnp.int32, tile_shape, kax)

    nt_dims = (((1,), (1,)), ((), ()))
    for h in range(IDX_HEADS):
        qs_ref[h * qb:(h + 1) * qb, :] = qi_ref[0, :, h * IDX_DIM:(h + 1) * IDX_DIM]
    if keys_on_sublanes:
        wi_t = sm_ref[0].T
        head_w = lambda h: wi_t[SM_WI + h:SM_WI + h + 1, :]
    else:
        wis = sm_ref[0][:, SM_WI:SM_WI + IDX_HEADS]
        head_w = lambda h: wis[:, h:h + 1]

    def score_tile(kt, carry):
        kit = ki_src[pl.ds(pl.multiple_of(kt * tk, tk), tk), :]
        acc = jnp.zeros(tile_shape, F32)
        if keys_on_sublanes:
            for pr in range(IDX_HEADS // 2):
                s2 = lax.dot_general(kit, qs_ref[2 * pr * qb:(2 * pr + 2) * qb, :], nt_dims,
                                     preferred_element_type=F32)
                for e in range(2):
                    acc = acc + head_w(2 * pr + e) * jnp.maximum(s2[:, e * qb:(e + 1) * qb], 0.0)
        else:
            s_all = lax.dot_general(qs_ref[...], kit, nt_dims, preferred_element_type=F32)
            for h in range(IDX_HEADS):
                acc = acc + head_w(h) * jnp.maximum(s_all[h * qb:(h + 1) * qb, :], 0.0)
        sc_ref[kt] = jnp.where(key_index(kt) < n_valid, acc, -jnp.inf)
        return carry

    lax.fori_loop(0, nkt, score_tile, 0)

    def fold(m, op2, red):
        if keys_on_sublanes:
            return red(m.reshape(tk // (n_acc * SUBLANES), n_acc, SUBLANES, qb), axis=0)
        f = m[:, 0:LANES]
        for c in range(1, tk // LANES):
            f = op2(f, m[:, c * LANES:(c + 1) * LANES])
        return f

    def reduce_tiles(make, op2, red, init):
        def body(kt, part):
            return op2(part, fold(make(sc_ref[kt], kt), op2, red))
        part = lax.fori_loop(0, nkt, body, jnp.full(part_shape, init, F32))
        if keys_on_sublanes:
            part = red(part, axis=0)
        return red(part, axis=kax, keepdims=True)

    def count(pred):
        return reduce_tiles(lambda t, kt: jnp.where(pred(t, kt), 1.0, 0.0), jnp.add, jnp.sum, 0.0)

    def row_max(pred):
        return reduce_tiles(lambda t, kt: jnp.where(pred(t, kt), t, -jnp.inf), jnp.maximum, jnp.max,
                            -jnp.inf)

    def row_min_valid():
        return reduce_tiles(lambda t, kt: jnp.where(key_index(kt) < n_valid, t, jnp.inf), jnp.minimum,
                            jnp.min, jnp.inf)

    def any_set(flag_f):
        return jnp.max(flag_f) > 0.0

    def bisect(_, state):
        lo, hi, c_lo = state
        mid = lo + (hi - lo) * 0.5
        c = count(lambda t, kt: t >= mid)
        ge = c >= kf
        return jnp.where(ge, mid, lo), jnp.where(ge, hi, mid), jnp.where(ge, c, c_lo)

    lo0 = row_min_valid()
    hi0 = row_max(lambda t, kt: t == t)
    lo, hi, c_lo = lax.fori_loop(0, n_bisect, bisect, (lo0, hi0, n_valid.astype(F32)))
    found_f = jnp.where(c_lo == kf, active_f, 0.0)
    v_found = reduce_tiles(lambda t, kt: jnp.where(t >= lo, t, jnp.inf), jnp.minimum, jnp.min, jnp.inf)

    def walk_down():
        w0 = row_max(lambda t, kt: t <= hi)
        c0 = count(lambda t, kt: t >= w0)

        def walk_cond(state):
            w, c = state
            return any_set(jnp.where(c < kf, active_f, 0.0))

        def walk_body(state):
            w, c = state
            w2 = row_max(lambda t, kt: t < w)
            c2 = count(lambda t, kt: t >= w2)
            upd = c < kf
            return jnp.where(upd, w2, w), jnp.where(upd, c2, c)

        return lax.while_loop(walk_cond, walk_body,
                              (jnp.where(found_f > 0.5, v_found, w0),
                               jnp.where(found_f > 0.5, kf, c0)))

    w, c_ge = lax.cond(any_set(active_f - found_f), walk_down,
                       lambda: (v_found, jnp.full(vec_shape, kf, F32)))
    thr = jnp.where(active_f > 0.5, w, -jnp.inf)
    tied_f = jnp.where(c_ge > kf, active_f, 0.0)

    def write_bias(sel_fn):
        def body(kt, carry):
            kidx = key_index(kt)
            sel = sel_fn(sc_ref[kt], kidx) & (kidx < n_valid)
            bias = jnp.where(sel, 0.0, NEG_BIG)
            bias_ref[kt] = bias.T if keys_on_sublanes else bias
            return carry
        lax.fori_loop(0, nkt, body, 0)

    def no_ties():
        write_bias(lambda t, kidx: t >= thr)

    def with_ties():
        need = kf - count(lambda t, kt: t > thr)
        n_steps = max(1, int(np.ceil(np.log2(sc_ref.shape[0] * tk))) + 1)

        def step(_, lh):
            lo_i, hi_i = lh
            mid = lax.shift_right_arithmetic(lo_i + hi_i, 1)
            ge = count(lambda t, kt: (t == thr) & (key_index(kt) <= mid)) >= need
            return jnp.where(ge, lo_i, mid), jnp.where(ge, mid, hi_i)

        lo_i = jnp.full(vec_shape, -1, jnp.int32)
        hi_i = jnp.full(vec_shape, sc_ref.shape[0] * tk - 1, jnp.int32)
        _, last_tie = lax.fori_loop(0, n_steps, step, (lo_i, hi_i))
        write_bias(lambda t, kidx: (t > thr) | ((t == thr)
                                                & ((kidx <= last_tie) | (tied_f < 0.5))))

    any_tie = jnp.max(tied_f) > 0.0
    lax.cond(any_tie, with_ties, no_ties)

    rep = ATT_HEADS // KV_HEADS
    qgs = [jnp.concatenate(
        [q_ref[0, :, (g * rep + r) * HEAD_DIM:(g * rep + r + 1) * HEAD_DIM] for r in range(rep)],
        axis=0) for g in range(KV_HEADS)]
    acc_ref[...] = jnp.zeros(acc_ref.shape, F32)
    bufs = ((s0_ref, mt0_ref), (s1_ref, mt1_ref))

    def qk_tile(kt, buf):
        s_ref, mt_ref = bufs[buf]
        row0 = pl.multiple_of(kt * tk, tk)
        bias = bias_ref[kt]
        for g in range(KV_HEADS):
            kg = k_src[pl.ds(row0, tk), g * HEAD_DIM:(g + 1) * HEAD_DIM]
            s = lax.dot_general(qgs[g], kg, nt_dims, preferred_element_type=F32)
            s = (s.reshape(rep, qb, tk) + bias[None]).reshape(rep * qb, tk)
            s_ref[g] = s
            mt_ref[g] = jnp.broadcast_to(jnp.max(s, axis=-1, keepdims=True), (rep * qb, LANES))

    def lane_tile(x, width):
        return jnp.concatenate([x] * (width // LANES), axis=1)

    def att_tile(kt, buf, prefetch):
        if prefetch:
            qk_tile(kt + 1, 1 - buf)
        s_ref, mt_ref = bufs[buf]
        row0 = pl.multiple_of(kt * tk, tk)
        for g in range(KV_HEADS):
            vg = v_src[pl.ds(row0, tk), 2 * g * HEAD_DIM:(2 * g + 2) * HEAD_DIM]
            m_old = m_ref[g]
            m_new = jnp.maximum(m_old, mt_ref[g])
            m_ref[g] = m_new
            alpha = jnp.exp2(m_old - m_new)
            p = jnp.exp2(s_ref[g] - lane_tile(m_new, tk)).astype(BF16)
            acc_ref[g] = (lane_tile(alpha, 2 * HEAD_DIM) * acc_ref[g]
                          + jnp.dot(p, vg, preferred_element_type=F32))

    def tile_pair(pi, carry):
        att_tile(2 * pi, 0, True)
        att_tile(2 * pi + 1, 1, True)
        return carry

    def last_two():
        att_tile(nkt - 2, 0, True)
        att_tile(nkt - 1, 1, False)

    def last_one():
        att_tile(nkt - 1, 0, False)

    m_ref[...] = jnp.full(m_ref.shape, NEG_BIG, F32)
    qk_tile(0, 0)
    lax.fori_loop(0, (nkt - 1) // 2, tile_pair, 0)
    lax.cond((nkt & 1) == 0, last_two, last_one)
    for g in range(KV_HEADS):
        acc = acc_ref[g]
        o = acc[:, :HEAD_DIM] / acc[:, HEAD_DIM:]
        for r in range(rep):
            h = g * rep + r
            o_ref[0, :, h * HEAD_DIM:(h + 1) * HEAD_DIM] = o[r * qb:(r + 1) * qb].astype(BF16)


def _dsa_attention(q, qi, sm, k_new, v_new, ki_new, cache, *, qb, tk):
    b, t, _ = q.shape
    past = 0 if cache is None else cache[2].shape[1]
    n_keys = past + t
    nk_pad = -(-n_keys // tk) * tk
    topk = min(TOPK_MAX, n_keys // 4)
    keys_on_sublanes = qb == LANES
    rep = ATT_HEADS // KV_HEADS
    tokq = lambda w: pl.BlockSpec((1, qb, w), lambda i, j: (i, j, 0))
    keys = lambda w: pl.BlockSpec((1, t, w), lambda i, j: (i, 0, 0))
    tile_shape = (tk, qb) if keys_on_sublanes else (qb, tk)
    cache_specs, cache_scratch = [], []
    if cache is not None:
        kv = pl.BlockSpec((1, past * KV_HEADS, HEAD_DIM), lambda i, j: (i, 0, 0))
        cache_specs = [kv, kv, pl.BlockSpec((1, past, IDX_DIM), lambda i, j: (i, 0, 0))]
        cache_scratch = [pltpu.VMEM((nk_pad, w), BF16) for w in (KV_WIDTH, VEXT_WIDTH, IDX_DIM)]
    return pl.pallas_call(
        functools.partial(_dsa_kernel, qb=qb, tk=tk, n_keys=n_keys, past=past, topk=topk,
                          n_bisect=20, keys_on_sublanes=keys_on_sublanes),
        grid=(b, t // qb),
        in_specs=[tokq(ATT_WIDTH), tokq(QI_WIDTH), tokq(LANES), keys(KV_WIDTH), keys(VEXT_WIDTH),
                  keys(IDX_DIM)] + cache_specs,
        out_specs=tokq(ATT_WIDTH),
        out_shape=jax.ShapeDtypeStruct((b, t, ATT_WIDTH), BF16),
        scratch_shapes=[pltpu.VMEM((nk_pad // tk,) + tile_shape, F32),
                        pltpu.VMEM((nk_pad // tk, qb, tk), F32),
                        pltpu.VMEM((KV_HEADS, rep * qb, 2 * HEAD_DIM), F32),
                        pltpu.VMEM((IDX_HEADS * qb, IDX_DIM), BF16),
                        pltpu.VMEM((KV_HEADS, rep * qb, tk), F32),
                        pltpu.VMEM((KV_HEADS, rep * qb, tk), F32),
                        pltpu.VMEM((KV_HEADS, rep * qb, LANES), F32),
                        pltpu.VMEM((KV_HEADS, rep * qb, LANES), F32),
                        pltpu.VMEM((KV_HEADS, rep * qb, LANES), F32)] + cache_scratch,
        compiler_params=_params(2),
        name="dsa_attention",
    )(q, qi, sm, k_new, v_new, ki_new, *(cache or ()))


def _ssd_kernel(xbc_ref, z_ref, sm_ref, st_ref, alog_ref, dexp_ref, nw_ref, e_ref,
                y_ref, stout_ref, ht_ref, yi_ref):
    c = pl.program_id(1)
    nc = pl.num_programs(1)
    L = xbc_ref.shape[1]
    hp = SSD_INNER // SSD_GROUPS

    @pl.when(c == 0)
    def _():
        ht_ref[...] = st_ref[0].reshape(SSD_INNER, SSD_STATE).T

    xs = xbc_ref[0, :, 0:SSD_INNER]
    sm = sm_ref[0]
    lane = lax.broadcasted_iota(jnp.int32, (1, LANES), 1)
    is_dt = (lane >= SM_DT) & (lane < SM_DT + SSD_HEADS)
    a_neg = jnp.where(is_dt, -jnp.exp(alog_ref[...]), 0.0)
    dt = jnp.where(is_dt, sm, 0.0)
    rows = lax.broadcasted_iota(jnp.int32, (L, L), 0)
    cols = lax.broadcasted_iota(jnp.int32, (L, L), 1)
    tri = cols <= rows
    e = e_ref[...]
    cum = jnp.dot(tri.astype(F32), dt * a_neg, precision=HIGHEST, preferred_element_type=F32)
    eye = (lax.broadcasted_iota(jnp.int32, (LANES, LANES), 0)
           == lax.broadcasted_iota(jnp.int32, (LANES, LANES), 1)).astype(F32)
    cum_t = lax.dot_general(eye, cum, (((1,), (1,)), ((), ())), precision=HIGHEST,
                            preferred_element_type=F32)
    both = jnp.concatenate([cum, dt], axis=0)
    hi = both.astype(BF16)
    rest = both - hi.astype(F32)
    mid = rest.astype(BF16)
    low = (rest - mid.astype(F32)).astype(BF16)
    both_e = (jnp.dot(hi, e, preferred_element_type=F32) + jnp.dot(mid, e, preferred_element_type=F32)
              + jnp.dot(low, e, preferred_element_type=F32))
    cum_e, dt_e = both_e[:L], both_e[L:]
    last = cum_e[L - 1:L, :]
    xdt = (xs * dt_e).astype(BF16)
    xw = (xs * (jnp.exp(last - cum_e) * dt_e)).astype(BF16)
    ht_old = ht_ref[...]

    for g in range(SSD_GROUPS):
        bg = xbc_ref[0, :, SSD_INNER + g * SSD_STATE:SSD_INNER + (g + 1) * SSD_STATE]
        cg = xbc_ref[0, :, SSD_INNER + (SSD_GROUPS + g) * SSD_STATE:
                     SSD_INNER + (SSD_GROUPS + g + 1) * SSD_STATE]
        bgb, cgb = bg.astype(BF16), cg.astype(BF16)
        cb = lax.dot_general(cgb, bgb, (((1,), (1,)), ((), ())), preferred_element_type=F32)
        for hh in range(SSD_HEADS // SSD_GROUPS):
            h = g * (SSD_HEADS // SSD_GROUPS) + hh
            seg = cum[:, SM_DT + h:SM_DT + h + 1] - cum_t[SM_DT + h:SM_DT + h + 1, :]
            decay = jnp.exp(jnp.where(tri, seg, -jnp.inf))
            wts = (cb * decay).astype(BF16)
            yi_ref[:, h * SSD_HEADDIM:(h + 1) * SSD_HEADDIM] = jnp.dot(
                wts, xdt[:, h * SSD_HEADDIM:(h + 1) * SSD_HEADDIM], preferred_element_type=F32)
        ht_g = ht_old[:, g * hp:(g + 1) * hp]
        y_inter = jnp.dot(cgb, ht_g.astype(BF16), preferred_element_type=F32)
        yi_ref[:, g * hp:(g + 1) * hp] = (yi_ref[:, g * hp:(g + 1) * hp]
                                          + y_inter * jnp.exp(cum_e[:, g * hp:(g + 1) * hp]))
        upd = jnp.dot(bg.T.astype(BF16), xw[:, g * hp:(g + 1) * hp],
                      preferred_element_type=F32)
        ht_ref[:, g * hp:(g + 1) * hp] = ht_g * jnp.exp(last[:, g * hp:(g + 1) * hp]) + upd

    y = yi_ref[...] + dexp_ref[...] * xs
    y = y * _silu(_lane_aligned(z_ref[0], SM_DT, SSD_INNER))
    for g in range(SSD_GROUPS):
        yg = y[:, g * hp:(g + 1) * hp]
        ms = jnp.mean(yg * yg, axis=-1, keepdims=True)
        y_ref[0, :, g * hp:(g + 1) * hp] = (yg * lax.rsqrt(ms + EPS)
                                            * nw_ref[:, g * hp:(g + 1) * hp]).astype(BF16)

    @pl.when(c == nc - 1)
    def _():
        stout_ref[0] = ht_ref[...].T.reshape(SSD_HEADS, SSD_HEADDIM, SSD_STATE)


def _ssd(xbc_act, proj, sm, state, a_log_pad, d_exp, norm_w, e_mat, *, chunk):
    b, t, _ = xbc_act.shape
    tok = lambda w, col=0: pl.BlockSpec((1, chunk, w), lambda i, j: (i, j, col))
    full2 = lambda a: pl.BlockSpec(a.shape, lambda i, j: (0, 0))
    st = pl.BlockSpec((1, SSD_HEADS, SSD_HEADDIM, SSD_STATE), lambda i, j: (i, 0, 0, 0))
    nw = norm_w.reshape(1, SSD_INNER)
    return pl.pallas_call(
        _ssd_kernel,
        grid=(b, t // chunk),
        in_specs=[tok(SSD_CONV_DIM), tok(Z_BLOCK, (OFF_Z - SM_DT) // Z_BLOCK), tok(LANES), st,
                  full2(a_log_pad), full2(d_exp), full2(nw), full2(e_mat)],
        out_specs=[tok(SSD_INNER), st],
        out_shape=[jax.ShapeDtypeStruct((b, t, SSD_INNER), BF16),
                   jax.ShapeDtypeStruct(state.shape, F32)],
        scratch_shapes=[pltpu.VMEM((SSD_STATE, SSD_INNER), F32),
                        pltpu.VMEM((chunk, SSD_INNER), F32)],
        compiler_params=_params(2),
        name="ssd_scan",
    )(xbc_act, proj, sm, state, a_log_pad, d_exp, nw, e_mat)


def _lru_kernel(gate_ref, xin_ref, cw_ref, cb_ref, wa_ref, wx_ref, ba_ref, bx_ref, lam_ref,
                h0_ref, cs_ref, y_ref, hout_ref, cnew_ref, cbuf_ref, a_ref, u_ref, hp_ref):
    t = pl.program_id(1)
    nt = pl.num_programs(1)
    tt = xin_ref.shape[1]
    xc = _causal_conv(xin_ref[0], cs_ref, cw_ref, cb_ref, cbuf_ref, cnew_ref, t == 0, t == nt - 1)

    @pl.when(t == 0)
    def _():
        hp_ref[...] = h0_ref[0]

    for kb in range(LRU_BLOCKS):
        sl = slice(kb * LRU_BLOCK_DIM, (kb + 1) * LRU_BLOCK_DIM)
        xb = xc[:, sl].astype(BF16)
        a_ref[:, sl] = jnp.dot(xb, wa_ref[kb].astype(BF16), preferred_element_type=F32)
        u_ref[:, sl] = jnp.dot(xb, wx_ref[kb].astype(BF16), preferred_element_type=F32)
    r = jax.nn.sigmoid(a_ref[...] + ba_ref[...])
    i = jax.nn.sigmoid(u_ref[...] + bx_ref[...])
    log_a = -LRU_C * r * _softplus(-lam_ref[...])
    a = jnp.exp(log_a)
    u = jnp.sqrt(-jnp.tanh(log_a) * (1.0 + a * a)) * (i * xc)

    width = a.shape[1]
    a = a.reshape(tt // SUBLANES, SUBLANES, width)
    u = u.reshape(tt // SUBLANES, SUBLANES, width)
    sub = lax.broadcasted_iota(jnp.int32, (1, SUBLANES, width), 1)
    for s in (1, 2, 4):
        m = sub >= s
        a_sh = pltpu.roll(a, s, 1)
        u_sh = pltpu.roll(u, s, 1)
        u = jnp.where(m, a * u_sh + u, u)
        a = jnp.where(m, a * a_sh, a)
    a_ref[...] = a.reshape(tt, width)
    u_ref[...] = u.reshape(tt, width)

    def group(gi, hprev):
        r0 = pl.multiple_of(gi * SUBLANES, SUBLANES)
        hs = a_ref[pl.ds(r0, SUBLANES), :] * hprev + u_ref[pl.ds(r0, SUBLANES), :]
        u_ref[pl.ds(r0, SUBLANES), :] = hs
        return hs[SUBLANES - 1:SUBLANES, :]

    h_last = lax.fori_loop(0, tt // SUBLANES, group, hp_ref[...])
    hp_ref[...] = h_last
    gate = gate_ref[0]
    gelu = 0.5 * gate * (1.0 + jnp.tanh(np.sqrt(2.0 / np.pi) * (gate + 0.044715 * (gate * gate * gate))))
    y_ref[0] = (u_ref[...] * gelu).astype(BF16)

    @pl.when(t == nt - 1)
    def _():
        hout_ref[0] = h_last


def _lru(proj, conv_w, conv_b, w_a, b_a, w_x, b_x, lam, h0, conv_state, *, tt):
    b, t, _ = proj.shape
    w = LRU_WIDTH
    tok = lambda col: pl.BlockSpec((1, tt, w), lambda i, j: (i, j, col))
    full = lambda a: pl.BlockSpec(a.shape, lambda i, j: (0,) * a.ndim)
    row = lambda v: v.reshape(1, w)
    state = pl.BlockSpec((1, CONV_W - 1, w), lambda i, j: (i, 0, 0))
    hspec = pl.BlockSpec((1, 1, w), lambda i, j: (i, 0, 0))
    args = (proj, proj, conv_w, row(conv_b), w_a, w_x, row(b_a), row(b_x), row(lam), h0, conv_state)
    return pl.pallas_call(
        _lru_kernel,
        grid=(b, t // tt),
        in_specs=[tok(0), tok(1)] + [full(a) for a in args[2:9]] + [hspec, state],
        out_specs=[pl.BlockSpec((1, tt, w), lambda i, j: (i, j, 0)), hspec, state],
        out_shape=[jax.ShapeDtypeStruct((b, t, w), BF16),
                   jax.ShapeDtypeStruct((b, 1, w), F32),
                   jax.ShapeDtypeStruct((b, CONV_W - 1, w), F32)],
        scratch_shapes=[pltpu.VMEM((tt + SUBLANES, w), F32), pltpu.VMEM((tt, w), F32),
                        pltpu.VMEM((tt, w), F32), pltpu.VMEM((1, w), F32)],
        compiler_params=_params(2),
        name="rg_lru",
    )(*args)


def _rope_tables(pos, dim):
    half = dim // 2
    inv = np.power(ROPE_THETA, -np.arange(half, dtype=np.float64) / half)
    ang = pos.astype(np.float64)[:, None] * inv[None, :]
    cos = np.concatenate([np.cos(ang), np.cos(ang)], axis=-1)
    sin = np.concatenate([-np.sin(ang), np.sin(ang)], axis=-1)
    reps = LANES // dim
    return (jnp.asarray(np.tile(cos, (1, reps)), F32), jnp.asarray(np.tile(sin, (1, reps)), F32))


def _head_expansion():
    e = np.zeros((LANES, SSD_INNER), np.float32)
    for h in range(SSD_HEADS):
        e[SM_DT + h, h * SSD_HEADDIM:(h + 1) * SSD_HEADDIM] = 1.0
    return jnp.asarray(e, BF16)


def _pad_lanes(v, offset):
    out = jnp.zeros((1, LANES), F32)
    return out.at[0, offset:offset + v.shape[0]].set(v)


def _w_in_ab_tail(w_t):
    whole = AB_WIDTH // AB_TILE_N * AB_TILE_N
    return jnp.pad(w_t[whole:], ((0, AB_PAD_N - AB_WIDTH), (0, 0)))


def _layer0_mixers(proj, st, p, cfg):
    b, t, _ = proj.shape
    past = 0 if st["past_k"] is None else st["past_k"].shape[1]
    pos = np.arange(past, past + t)
    tabs = _rope_tables(pos, HEAD_DIM) + _rope_tables(pos, IDX_DIM)
    (q, qi, k, v, ki, sm, xbc_act, ssm_conv_new, kb, vbe, kib) = _post0(
        proj, tabs, p["q_norm_w"][0], p["k_norm_w"][0], p["ssd_conv_w"][0], p["ssd_conv_b"][0],
        p["dt_bias_pad"], st["ssm_conv"], tt=cfg["post_tt"])
    cache = None if not past else (st["past_k"].reshape(b, past * KV_HEADS, HEAD_DIM),
                                   st["past_v"].reshape(b, past * KV_HEADS, HEAD_DIM), st["past_ki"])
    att = _dsa_attention(q, qi, sm, kb, vbe, kib, cache, qb=cfg["qb"], tk=cfg["tk"])
    y_ssd, ssm_new = _ssd(xbc_act, proj, sm, st["ssm_h"], p["a_log_pad"], p["d_exp"],
                          p["ssd_norm_w"][0], p["e_mat"], chunk=cfg["ssd_chunk"])
    kv_shape = (1, b, t, KV_HEADS, HEAD_DIM)
    return att, y_ssd, (k.reshape(kv_shape), v.reshape(kv_shape), ki[None], ssm_new[None],
                        ssm_conv_new[None])


def _layer1_mixer(proj, st, p, cfg):
    b = proj.shape[0]
    y_lru, lru_new, lru_conv_new = _lru(
        proj, p["lru_conv_w"][0], p["lru_conv_b"][0], p["lru_w_a"][0], p["lru_b_a"][0],
        p["lru_w_x"][0], p["lru_b_x"][0], p["lru_lambda"][0], st["lru_h"].reshape(b, 1, LRU_WIDTH),
        st["lru_conv"], tt=cfg["lru_tt"])
    return y_lru, (lru_new.reshape(1, b, LRU_WIDTH), lru_conv_new[None])


def _forward(xp, xs, mods_p, mods_s, st_p, st_s, p, cfg_p, cfg_s):
    bb, tt = cfg_p["bb"], cfg_p["tt"]

    def ffn(xp, xs, l):
        (_, _, _, sh_p, sc_p, g_p), (_, _, _, sh_s, sc_s, g_s) = mods_p[l], mods_s[l]
        gu_p, gu_s = _norm_mod_matmul(
            xp, p["norm_ffn_w"][l], sc_p, sh_p, [p["ffn_w_gate"], p["ffn_w_up"]], l, bb=bb,
            tt=cfg_p["tt_ffn"], tn=512, out_dtype=BF16, name="ffn_gate_up", ride=(xs, sc_s, sh_s))
        return _matmul_residual([gu_p], p["ffn_w_down"], l, xp, g_p, bb=bb, tt=cfg_p["tt_ffn"],
                                tn=256, name="ffn_down", ride=([gu_s], xs, g_s))

    (sh_p, sc_p, g_p, *_), (sh_s, sc_s, g_s, *_) = mods_p[0], mods_s[0]
    proj_p, proj_s = _norm_mod_matmul(
        xp, p["norm_mix_w"][0], sc_p, sh_p, [p["w_in_ab_t"]], 0, bb=bb, tt=cfg_p["tt_ab"],
        tn=AB_TILE_N, out_dtype=F32, name="in_proj_ab", w_tail=p["w_in_ab_tail"], w_rows_out=True,
        ride=(xs, sc_s, sh_s))
    att_p, y_p, outs0_p = _layer0_mixers(proj_p, st_p, p, cfg_p)
    att_s, y_s, outs0_s = _layer0_mixers(proj_s, st_s, p, cfg_s)
    xp, xs = _matmul_residual([att_p, y_p], p["w_out_ab"], 0, xp, g_p, bb=bb, tt=tt, tn=512,
                              name="out_proj_ab", ride=([att_s, y_s], xs, g_s))
    xp, xs = ffn(xp, xs, 0)

    (sh_p, sc_p, g_p, *_), (sh_s, sc_s, g_s, *_) = mods_p[1], mods_s[1]
    proj_p, proj_s = _norm_mod_matmul(
        xp, p["norm_mix_w"][1], sc_p, sh_p, [p["w_in_c"]], 0, bb=bb, tt=cfg_p["tt_ab"], tn=512,
        out_dtype=F32, name="in_proj_c", ride=(xs, sc_s, sh_s))
    y_p, outs1_p = _layer1_mixer(proj_p, st_p, p, cfg_p)
    y_s, outs1_s = _layer1_mixer(proj_s, st_s, p, cfg_s)
    xp, xs = _matmul_residual([y_p], p["w_out_c"], 0, xp, g_p, bb=bb, tt=tt, tn=512,
                              name="out_proj_c", ride=([y_s], xs, g_s))
    xp, xs = ffn(xp, xs, 1)
    return (xp,) + outs0_p + outs1_p, (xs,) + outs0_s + outs1_s


PROMPT_CFG = dict(bb=1, tt=2048, tt_ab=1024, tt_ffn=1024, post_tt=512, qb=128, tk=512,
                  ssd_chunk=128, lru_tt=512)
SAMPLE_CFG = dict(post_tt=32, qb=32, tk=384, ssd_chunk=32, lru_tt=32)


def kernel(x_prompt, x_sample, cache_attn_k, cache_attn_v, cache_idx_k, state_ssm, state_ssm_conv,
           state_lru, state_lru_conv, c_prompt, c_sample, ada_w, ada_b, norm_mix_w, norm_ffn_w,
           w_in_ab, q_norm_w, k_norm_w, ssd_conv_w, ssd_conv_b, ssd_dt_bias, ssd_a_log, ssd_d,
           ssd_norm_w, w_out_ab, w_in_c, lru_conv_w, lru_conv_b, lru_w_a, lru_b_a, lru_w_x, lru_b_x,
           lru_lambda, w_out_c, ffn_w_gate, ffn_w_up, ffn_w_down):
    bp, bs = x_prompt.shape[0], x_sample.shape[0]
    p = dict(norm_mix_w=norm_mix_w, norm_ffn_w=norm_ffn_w, q_norm_w=q_norm_w, k_norm_w=k_norm_w,
             ssd_conv_w=ssd_conv_w, ssd_conv_b=ssd_conv_b, ssd_norm_w=ssd_norm_w, w_out_ab=w_out_ab,
             w_in_c=w_in_c, lru_conv_w=lru_conv_w, lru_conv_b=lru_conv_b, lru_w_a=lru_w_a,
             lru_b_a=lru_b_a, lru_w_x=lru_w_x, lru_b_x=lru_b_x, lru_lambda=lru_lambda,
             w_out_c=w_out_c, ffn_w_gate=ffn_w_gate, ffn_w_up=ffn_w_up, ffn_w_down=ffn_w_down)
    p["w_in_ab_t"] = jnp.swapaxes(w_in_ab, 1, 2)
    p["w_in_ab_tail"] = _w_in_ab_tail(p["w_in_ab_t"][0])
    p["dt_bias_pad"] = _pad_lanes(ssd_dt_bias[0], SM_DT)
    p["a_log_pad"] = _pad_lanes(ssd_a_log[0], SM_DT)
    p["d_exp"] = jnp.repeat(ssd_d[0], SSD_HEADDIM).reshape(1, SSD_INNER)
    p["e_mat"] = _head_expansion()

    m_rows = 32
    c_all = jnp.concatenate([c_prompt, c_sample, jnp.zeros((m_rows - bp - bs, D_MODEL), F32)], axis=0)
    mod = _modulation(c_all, ada_w, ada_b)

    def group_mods(r0, nb):
        return [[mod[l, r0:r0 + nb, i * D_MODEL:(i + 1) * D_MODEL].reshape(nb, 1, D_MODEL)
                 for i in range(6)] for l in range(mod.shape[0])]

    zeros = lambda *s: jnp.zeros(s, F32)
    st_p = dict(past_k=None, past_v=None, past_ki=None,
                ssm_h=zeros(bp, SSD_HEADS, SSD_HEADDIM, SSD_STATE),
                ssm_conv=zeros(bp, CONV_W - 1, SSD_CONV_DIM), lru_h=zeros(bp, LRU_WIDTH),
                lru_conv=zeros(bp, CONV_W - 1, LRU_WIDTH))
    st_s = dict(past_k=cache_attn_k[0], past_v=cache_attn_v[0], past_ki=cache_idx_k[0],
                ssm_h=state_ssm[0], ssm_conv=state_ssm_conv[0], lru_h=state_lru[0],
                lru_conv=state_lru_conv[0])
    out_p, out_s = _forward(x_prompt, x_sample, group_mods(0, bp), group_mods(bp, bs), st_p, st_s, p,
                            PROMPT_CFG, SAMPLE_CFG)
    return (out_p[0], out_s[0]) + out_p[1:] + out_s[1:]
```

```python
import functools

import numpy as np
import jax
import jax.numpy as jnp
from jax import lax
from jax.experimental import pallas as pl
from jax.experimental.pallas import tpu as pltpu

F32 = jnp.float32
BF16 = jnp.bfloat16
HIGHEST = lax.Precision.HIGHEST

D_MODEL = 2048
CHUNK = 64
CHUNK_SHIFT = 6
ATT_HEADS = 8
KV_HEADS = 2
HEAD_DIM = 128
IDX_HEADS = 16
IDX_DIM = 64
TOPK_MAX = 256
ROPE_THETA = 10000.0
SSD_HEADS = 16
SSD_HEADDIM = 64
SSD_GROUPS = 2
SSD_STATE = 128
SSD_INNER = SSD_HEADS * SSD_HEADDIM
SSD_CONV_DIM = SSD_INNER + 2 * SSD_GROUPS * SSD_STATE
CONV_W = 4
LRU_WIDTH = D_MODEL
LRU_BLOCKS = 16
LRU_BLOCK_DIM = LRU_WIDTH // LRU_BLOCKS
LRU_C = 8.0
EPS = 1e-6
ATT_WIDTH = ATT_HEADS * HEAD_DIM
KV_WIDTH = KV_HEADS * HEAD_DIM
VEXT_WIDTH = 2 * KV_WIDTH
QI_WIDTH = IDX_HEADS * IDX_DIM
Q_SCALE = HEAD_DIM ** -0.5 * float(np.log2(np.e))

LANES = 128
SUBLANES = 8
VMEM_LIMIT_BYTES = 62 * 1024 * 1024
SINGLE_BUFFER_ROWS = 2048
NORM_CHUNKS = 4

OFF_Q = 0
OFF_K = OFF_Q + ATT_WIDTH
OFF_V = OFF_K + KV_WIDTH
OFF_QI = OFF_V + KV_WIDTH
OFF_SMALL = OFF_QI + QI_WIDTH
SM_WI = IDX_DIM
SM_DT = IDX_DIM + IDX_HEADS
OFF_Z = OFF_SMALL + SM_DT
OFF_XBC = OFF_Z + SSD_INNER
OFF_DT = OFF_XBC + SSD_CONV_DIM
AB_WIDTH = OFF_DT + SSD_HEADS
Z_BLOCK = 1280
AB_TILE_N = 512
AB_PAD_N = 5632
assert OFF_Z % LANES == SM_DT and OFF_XBC % LANES == SM_DT and OFF_DT % LANES == SM_DT
assert (OFF_Z - SM_DT) % Z_BLOCK == 0 and OFF_Z - SM_DT + Z_BLOCK >= OFF_Z + SSD_INNER

NEG_BIG = -1e30


def _params(n_axes):
    return pltpu.CompilerParams(dimension_semantics=("arbitrary",) * n_axes,
                                vmem_limit_bytes=VMEM_LIMIT_BYTES)


def _silu(x):
    return x * jax.nn.sigmoid(x)


def _softplus(x):
    return jnp.maximum(x, 0.0) + jnp.log1p(jnp.exp(-jnp.abs(x)))


def _mod_kernel(c_ref, w_ref, b_ref, o_ref):
    a = _silu(c_ref[...]).astype(BF16)
    o_ref[0] = jnp.dot(a, w_ref[0].astype(BF16), preferred_element_type=F32) + b_ref[0]


def _modulation(c_all, ada_w, ada_b):
    depth, d, n = ada_w.shape
    m = c_all.shape[0]
    tn = 1024
    return pl.pallas_call(
        _mod_kernel,
        grid=(depth, n // tn),
        in_specs=[pl.BlockSpec((m, d), lambda l, j: (0, 0)),
                  pl.BlockSpec((1, d, tn), lambda l, j: (l, 0, j)),
                  pl.BlockSpec((1, 1, tn), lambda l, j: (l, 0, j))],
        out_specs=pl.BlockSpec((1, m, tn), lambda l, j: (l, 0, j)),
        out_shape=jax.ShapeDtypeStruct((depth, m, n), F32),
        compiler_params=_params(2),
        name="adaln_mod",
    )(c_all, ada_w, ada_b.reshape(depth, 1, n))


def _first_row_tile():
    return (pl.program_id(0) == 0) & (pl.program_id(1) == 0)


def _nmm_kernel(*refs, swiglu, tail, w_rows_out, ride):
    refs = list(refs)
    x_ref, nw_ref, sc_ref, sh_ref = refs[:4]
    del refs[:4]
    if ride:
        xr_ref, scr_ref, shr_ref = refs[:3]
        del refs[:3]
    n_w = 2 if swiglu or tail else 1
    w_refs = refs[:n_w]
    del refs[:n_w]
    if ride:
        o_ref, or_ref, h_ref, hr_ref = refs
    else:
        o_ref, h_ref = refs
    k = pl.program_id(2)

    def matmul(h, w):
        dims = (((1,), (1,)), ((), ())) if w_rows_out else (((1,), (0,)), ((), ()))
        return lax.dot_general(h, w, dims, preferred_element_type=F32)

    def finish(accs, out_ref):
        o = _silu(accs[0]) * accs[1] if swiglu else accs[0]
        out_ref[...] = o.reshape(out_ref.shape).astype(out_ref.dtype)

    def project(ws, rows_ref, out_ref):
        h = rows_ref[...]
        finish([matmul(h, w) for w in ws], out_ref)

    def norm_project(ws, src_ref, scale_ref, shift_ref, rows_ref, out_ref):
        x = src_ref[...]
        d = x.shape[-1]
        rs = lax.rsqrt(jnp.mean(x * x, axis=-1, keepdims=True) + EPS)
        kc = d // NORM_CHUNKS
        accs = None
        for c in range(NORM_CHUNKS):
            cols = slice(c * kc, (c + 1) * kc)
            hc = (x[..., cols] * rs * nw_ref[..., cols] * (1.0 + scale_ref[..., cols])
                  + shift_ref[..., cols])
            hc = hc.reshape(rows_ref.shape[0], kc).astype(BF16)
            rows_ref[:, cols] = hc
            part = [matmul(hc, w[:, cols] if w_rows_out else w[cols, :]) for w in ws]
            accs = part if accs is None else [a + b for a, b in zip(accs, part)]
        finish(accs, out_ref)

    def step(tile_refs, first):
        ws = [w[...].astype(BF16) for w in tile_refs]
        if first:
            norm_project(ws, x_ref, sc_ref, sh_ref, h_ref, o_ref)
            if ride:
                pl.when(_first_row_tile())(
                    lambda: norm_project(ws, xr_ref, scr_ref, shr_ref, hr_ref, or_ref))
        else:
            project(ws, h_ref, o_ref)
            if ride:
                pl.when(_first_row_tile())(lambda: project(ws, hr_ref, or_ref))

    last = pl.num_programs(2) - 1 if tail else pl.num_programs(2)
    pl.when(k == 0)(lambda: step(w_refs[:n_w - tail], True))
    pl.when((k > 0) & (k < last))(lambda: step(w_refs[:n_w - tail], False))
    if tail:
        pl.when(k == last)(lambda: step(w_refs[1:], False))


def _row_block_mode(rows):
    return dict(pipeline_mode=pl.Buffered(1)) if rows >= SINGLE_BUFFER_ROWS else {}


def _ride_col_tile(n_tiles):
    return lambda i, j, k: (0, 0, jnp.where((i == 0) & (j == 0), k, n_tiles - 1))


def _norm_mod_matmul(x, nw, sc, sh, ws, layer, *, bb, tt, tn, out_dtype, name, w_tail=None,
                     w_rows_out=False, ride=None):
    b, t, d = x.shape
    swiglu = len(ws) == 2
    n_main = ws[0].shape[1 if w_rows_out else 2] // tn
    n_tiles = n_main + (w_tail is not None)
    n = n_tiles * tn
    xmap = lambda i, j, k: (i, j, 0)
    mmap = lambda i, j, k: (i, 0, 0)
    const3 = lambda i, j, k: (0, 0, 0)
    tile = lambda k: jnp.minimum(k, n_main - 1)
    if w_rows_out:
        w_specs = [pl.BlockSpec((None, tn, d), lambda i, j, k: (layer, tile(k), 0))] * len(ws)
    else:
        w_specs = [pl.BlockSpec((None, d, tn), lambda i, j, k: (layer, 0, tile(k)))] * len(ws)
    operands = list(ws)
    if w_tail is not None:
        w_specs.append(pl.BlockSpec(w_tail.shape, lambda i, j, k: (0, 0)))
        operands.append(w_tail)
    in_specs = [pl.BlockSpec((bb, tt, d), xmap, **_row_block_mode(bb * tt)),
                pl.BlockSpec((1, 1, d), const3),
                pl.BlockSpec((bb, 1, d), mmap),
                pl.BlockSpec((bb, 1, d), mmap)]
    out_specs = [pl.BlockSpec((bb, tt, tn), lambda i, j, k: (i, j, k))]
    out_shape = [jax.ShapeDtypeStruct((b, t, n), out_dtype)]
    scratch = [pltpu.VMEM((bb * tt, d), BF16)]
    ride_ops = ()
    if ride is not None:
        ride_ops = ride
        rb, rt, _ = ride[0].shape
        in_specs += [pl.BlockSpec((rb, rt, d), const3, pipeline_mode=pl.Buffered(1)),
                     pl.BlockSpec((rb, 1, d), const3), pl.BlockSpec((rb, 1, d), const3)]
        out_specs.append(pl.BlockSpec((rb, rt, tn), _ride_col_tile(n_tiles)))
        out_shape.append(jax.ShapeDtypeStruct((rb, rt, n), out_dtype))
        scratch.append(pltpu.VMEM((rb * rt, d), BF16))
    outs = pl.pallas_call(
        functools.partial(_nmm_kernel, swiglu=swiglu, tail=w_tail is not None,
                          w_rows_out=w_rows_out, ride=ride is not None),
        grid=(b // bb, t // tt, n_tiles),
        in_specs=in_specs + w_specs,
        out_specs=out_specs,
        out_shape=out_shape,
        scratch_shapes=scratch,
        compiler_params=_params(3),
        name=name,
    )(x, nw.reshape(1, 1, d), sc, sh, *ride_ops, *operands)
    return outs if ride is not None else outs[0]


def _mmr_kernel(*refs, k_sizes, ride):
    refs = list(refs)
    n_a = len(k_sizes)
    groups = [refs[:n_a + 2]]
    del refs[:n_a + 2]
    if ride:
        groups.append(refs[:n_a + 2])
        del refs[:n_a + 2]
    w_ref = refs[0]
    out_refs = refs[1:]
    offs = np.cumsum((0,) + tuple(k_sizes))
    ws = [w_ref[offs[i]:offs[i + 1], :].astype(BF16) for i in range(n_a)]

    def apply(group, o_ref):
        *a_refs, x_ref, g_ref = group
        bb, tt, tn = o_ref.shape
        acc = jnp.zeros((bb * tt, tn), F32)
        for a_ref, w, ks in zip(a_refs, ws, k_sizes):
            acc = acc + jnp.dot(a_ref[...].reshape(bb * tt, ks), w, preferred_element_type=F32)
        o_ref[...] = x_ref[...] + g_ref[...] * acc.reshape(bb, tt, tn)

    apply(groups[0], out_refs[0])
    if ride:
        pl.when(_first_row_tile())(lambda: apply(groups[1], out_refs[1]))


def _matmul_residual(a_list, w, layer, x, g, *, bb, tt, tn, name, ride=None):
    b, t, d = x.shape
    k_sizes = tuple(a.shape[-1] for a in a_list)
    k_total = sum(k_sizes)
    n_tiles = d // tn
    in_specs = ([pl.BlockSpec((bb, tt, ks), lambda i, j, k: (i, j, 0), **_row_block_mode(bb * tt))
                 for ks in k_sizes]
                + [pl.BlockSpec((bb, tt, tn), lambda i, j, k: (i, j, k)),
                   pl.BlockSpec((bb, 1, tn), lambda i, j, k: (i, 0, k))])
    out_specs = [pl.BlockSpec((bb, tt, tn), lambda i, j, k: (i, j, k))]
    out_shape = [jax.ShapeDtypeStruct((b, t, d), F32)]
    ride_ops = ()
    if ride is not None:
        ra_list, rx, rg = ride
        rb, rt, _ = rx.shape
        ride_ops = (*ra_list, rx, rg)
        col = _ride_col_tile(n_tiles)
        in_specs += ([pl.BlockSpec((rb, rt, ks), lambda i, j, k: (0, 0, 0),
                                   pipeline_mode=pl.Buffered(1)) for ks in k_sizes]
                     + [pl.BlockSpec((rb, rt, tn), col), pl.BlockSpec((rb, 1, tn), col)])
        out_specs.append(pl.BlockSpec((rb, rt, tn), col))
        out_shape.append(jax.ShapeDtypeStruct(rx.shape, F32))
    outs = pl.pallas_call(
        functools.partial(_mmr_kernel, k_sizes=k_sizes, ride=ride is not None),
        grid=(b // bb, t // tt, n_tiles),
        in_specs=in_specs + [pl.BlockSpec((None, k_total, tn), lambda i, j, k: (layer, 0, k))],
        out_specs=out_specs,
        out_shape=out_shape,
        compiler_params=_params(3),
        name=name,
    )(*a_list, x, g, *ride_ops, w)
    return outs if ride is not None else outs[0]


def _swap_halves(x, half):
    w = x.shape[-1]
    lane = lax.broadcasted_iota(jnp.int32, x.shape, x.ndim - 1)
    first = (lane & half) == 0
    return jnp.where(first, pltpu.roll(x, w - half, x.ndim - 1), pltpu.roll(x, half, x.ndim - 1))


def _lane_aligned(slab, offset, width):
    return pltpu.roll(slab, slab.shape[-1] - offset, slab.ndim - 1)[:, :width]


def _rope(x, cos, sin_signed, half):
    reps = x.shape[-1] // cos.shape[-1]
    if reps > 1:
        cos = jnp.concatenate([cos] * reps, axis=-1)
        sin_signed = jnp.concatenate([sin_signed] * reps, axis=-1)
    return x * cos + _swap_halves(x, half) * sin_signed


def _causal_conv(u, state_ref, w_ref, b_ref, cbuf_ref, new_ref, first, last):
    tt = u.shape[0]

    @pl.when(first)
    def _():
        cbuf_ref[0:SUBLANES, :] = jnp.zeros((SUBLANES, u.shape[1]), F32)
        cbuf_ref[SUBLANES - (CONV_W - 1):SUBLANES, :] = state_ref[0]

    cbuf_ref[SUBLANES:SUBLANES + tt, :] = u
    full = cbuf_ref[...]
    out = b_ref[...]
    for j in range(CONV_W):
        shift = CONV_W - 1 - j
        rows = pltpu.roll(full, shift, 0) if shift else full
        out = out + rows[SUBLANES:SUBLANES + tt, :] * w_ref[j:j + 1, :]
    tail = cbuf_ref[tt:tt + SUBLANES, :]
    cbuf_ref[0:SUBLANES, :] = tail

    @pl.when(last)
    def _():
        new_ref[0] = tail[SUBLANES - (CONV_W - 1):, :]

    return out


def _post0_kernel(p_ref, cq_ref, sq_ref, ci_ref, si_ref, qn_ref, kn_ref, cw_ref, cb_ref, dtb_ref,
                  cs_ref,
                  q_ref, qi_ref, k_ref, v_ref, ki_ref, sm_ref, xbc_ref, cnew_ref, kb_ref, vb_ref,
                  kib_ref, cbuf_ref):
    t = pl.program_id(1)
    nt = pl.num_programs(1)
    cq, sq, ci, si = cq_ref[...], sq_ref[...], ci_ref[...], si_ref[...]

    def head_norm_rope(x, w):
        ms = jnp.mean(x * x, axis=-1, keepdims=True)
        return _rope(x * lax.rsqrt(ms + EPS) * w, cq, sq, HEAD_DIM // 2)

    for h in range(ATT_HEADS):
        xh = p_ref[0, :, OFF_Q + h * HEAD_DIM:OFF_Q + (h + 1) * HEAD_DIM]
        q_ref[0, :, h * HEAD_DIM:(h + 1) * HEAD_DIM] = (head_norm_rope(xh, qn_ref[...])
                                                         * Q_SCALE).astype(BF16)
    ones = jnp.ones((p_ref.shape[1], HEAD_DIM), BF16)
    for h in range(KV_HEADS):
        xh = p_ref[0, :, OFF_K + h * HEAD_DIM:OFF_K + (h + 1) * HEAD_DIM]
        kh = head_norm_rope(xh, kn_ref[...])
        vh = p_ref[0, :, OFF_V + h * HEAD_DIM:OFF_V + (h + 1) * HEAD_DIM]
        k_ref[0, pl.ds(h, xh.shape[0], stride=KV_HEADS), :] = kh
        v_ref[0, pl.ds(h, xh.shape[0], stride=KV_HEADS), :] = vh
        kb_ref[0, :, h * HEAD_DIM:(h + 1) * HEAD_DIM] = kh.astype(BF16)
        vb_ref[0, :, 2 * h * HEAD_DIM:(2 * h + 1) * HEAD_DIM] = vh.astype(BF16)
        vb_ref[0, :, (2 * h + 1) * HEAD_DIM:(2 * h + 2) * HEAD_DIM] = ones

    qi = p_ref[0, :, OFF_QI:OFF_QI + QI_WIDTH]
    qi_ref[0] = _rope(qi, ci, si, IDX_DIM // 2).astype(BF16)

    sm = p_ref[0, :, OFF_SMALL:OFF_SMALL + LANES]
    lane = lax.broadcasted_iota(jnp.int32, sm.shape, 1)
    ki_part = _rope(sm, ci, si, IDX_DIM // 2)
    wi_part = sm * (IDX_HEADS ** -0.5 * IDX_DIM ** -0.5)
    dt_col = OFF_DT - SM_DT
    dt_part = _softplus(p_ref[0, :, dt_col:dt_col + LANES] + dtb_ref[...])
    sm_out = jnp.where(lane < SM_WI, ki_part,
                       jnp.where(lane < SM_DT, wi_part,
                                 jnp.where(lane < SM_DT + SSD_HEADS, dt_part, 0.0)))
    sm_ref[0] = sm_out
    ki_ref[0] = sm_out[:, :IDX_DIM]
    kib_ref[0] = sm_out[:, :IDX_DIM].astype(BF16)

    xbc = _lane_aligned(p_ref[0, :, OFF_XBC - SM_DT:OFF_DT - SM_DT + LANES], SM_DT, SSD_CONV_DIM)
    conv = _causal_conv(xbc, cs_ref, cw_ref, cb_ref, cbuf_ref, cnew_ref, t == 0, t == nt - 1)
    xbc_ref[0] = _silu(conv)


def _post0(proj, tabs, q_norm_w, k_norm_w, conv_w, conv_b, dt_bias_pad, conv_state, *, tt):
    b, t, n = proj.shape
    tok = lambda w: pl.BlockSpec((1, tt, w), lambda i, j: (i, j, 0))
    tab = pl.BlockSpec((tt, LANES), lambda i, j: (j, 0))
    full2 = lambda a: pl.BlockSpec(a.shape, lambda i, j: (0, 0))
    state = lambda c: pl.BlockSpec((1, CONV_W - 1, c), lambda i, j: (i, 0, 0))
    out_shapes = [
        jax.ShapeDtypeStruct((b, t, ATT_WIDTH), BF16),
        jax.ShapeDtypeStruct((b, t, QI_WIDTH), BF16),
        jax.ShapeDtypeStruct((b, t * KV_HEADS, HEAD_DIM), F32),
        jax.ShapeDtypeStruct((b, t * KV_HEADS, HEAD_DIM), F32),
        jax.ShapeDtypeStruct((b, t, IDX_DIM), F32),
        jax.ShapeDtypeStruct((b, t, LANES), F32),
        jax.ShapeDtypeStruct((b, t, SSD_CONV_DIM), F32),
        jax.ShapeDtypeStruct((b, CONV_W - 1, SSD_CONV_DIM), F32),
        jax.ShapeDtypeStruct((b, t, KV_WIDTH), BF16),
        jax.ShapeDtypeStruct((b, t, VEXT_WIDTH), BF16),
        jax.ShapeDtypeStruct((b, t, IDX_DIM), BF16),
    ]
    kv_rows = pl.BlockSpec((1, tt * KV_HEADS, HEAD_DIM), lambda i, j: (i, j, 0))
    out_specs = [tok(ATT_WIDTH), tok(QI_WIDTH), kv_rows, kv_rows, tok(IDX_DIM),
                 tok(LANES), tok(SSD_CONV_DIM), state(SSD_CONV_DIM), tok(KV_WIDTH), tok(VEXT_WIDTH),
                 tok(IDX_DIM)]
    qn = q_norm_w.reshape(1, HEAD_DIM)
    kn = k_norm_w.reshape(1, HEAD_DIM)
    cb = conv_b.reshape(1, SSD_CONV_DIM)
    return pl.pallas_call(
        _post0_kernel,
        grid=(b, t // tt),
        in_specs=[tok(n), tab, tab, tab, tab, full2(qn), full2(kn), full2(conv_w), full2(cb),
                  full2(dt_bias_pad), state(SSD_CONV_DIM)],
        out_specs=out_specs,
        out_shape=out_shapes,
        scratch_shapes=[pltpu.VMEM((tt + SUBLANES, SSD_CONV_DIM), F32)],
        compiler_params=_params(2),
        name="post_proj0",
    )(proj, *tabs, qn, kn, conv_w, cb, dt_bias_pad, conv_state)


def _dsa_kernel(q_ref, qi_ref, sm_ref, k_ref, v_ref, ki_ref, *rest, qb, tk, n_keys, past, topk,
                n_bisect, keys_on_sublanes):
    if past:
        pk_ref, pv_ref, pki_ref = rest[:3]
        rest = rest[3:]
    (o_ref, sc_ref, bias_ref, acc_ref, qs_ref, s0_ref, s1_ref, mt0_ref, mt1_ref, m_ref) = rest[:10]
    if past:
        kall_ref, vall_ref, kiall_ref = rest[10:]
        t_new = k_ref.shape[1]
        pad = kall_ref.shape[0] - n_keys
        for g in range(KV_HEADS):
            head_rows = pl.ds(g, past, stride=KV_HEADS)
            kall_ref[0:past, g * HEAD_DIM:(g + 1) * HEAD_DIM] = pk_ref[0, head_rows, :].astype(BF16)
            vall_ref[0:past, 2 * g * HEAD_DIM:(2 * g + 1) * HEAD_DIM] = pv_ref[0, head_rows, :].astype(BF16)
            vall_ref[0:past, (2 * g + 1) * HEAD_DIM:(2 * g + 2) * HEAD_DIM] = jnp.ones(
                (past, HEAD_DIM), BF16)
        kiall_ref[0:past, :] = pki_ref[0].astype(BF16)
        for dst, src in ((kall_ref, k_ref), (vall_ref, v_ref), (kiall_ref, ki_ref)):
            dst[past:past + t_new, :] = src[0]
            dst[past + t_new:, :] = jnp.zeros((pad, dst.shape[1]), BF16)
        k_src, v_src, ki_src = kall_ref, vall_ref, kiall_ref
    else:
        k_src, v_src, ki_src = k_ref.at[0], v_ref.at[0], ki_ref.at[0]
    kax = 0 if keys_on_sublanes else 1
    tile_shape = (tk, qb) if keys_on_sublanes else (qb, tk)
    vec_shape = (1, qb) if keys_on_sublanes else (qb, 1)
    n_acc = 8
    part_shape = (n_acc, SUBLANES, qb) if keys_on_sublanes else (qb, LANES)
    j = pl.program_id(1)
    pos0 = past + j * qb
    q_chunk = lax.shift_right_logical(pos0 + lax.broadcasted_iota(jnp.int32, vec_shape, 1 - kax),
                                      CHUNK_SHIFT)
    n_valid = jnp.minimum((q_chunk + 1) * CHUNK, n_keys)
    max_valid = jnp.minimum(((pos0 + qb - 1) // CHUNK + 1) * CHUNK, n_keys)
    nkt = (max_valid + tk - 1) // tk
    kf = float(topk)
    active_f = jnp.where(n_valid > topk, 1.0, 0.0)

    def key_index(kt):
        return kt * tk + lax.broadcasted_iota(jnp.int32, tile_shape, kax)

    nt_dims = (((1,), (1,)), ((), ()))
    for h in range(IDX_HEADS):
        qs_ref[h * qb:(h + 1) * qb, :] = qi_ref[0, :, h * IDX_DIM:(h + 1) * IDX_DIM]
    if keys_on_sublanes:
        wi_t = sm_ref[0].T
        head_w = lambda h: wi_t[SM_WI + h:SM_WI + h + 1, :]
    else:
        wis = sm_ref[0][:, SM_WI:SM_WI + IDX_HEADS]
        head_w = lambda h: wis[:, h:h + 1]

    def score_tile(kt, carry):
        kit = ki_src[pl.ds(pl.multiple_of(kt * tk, tk), tk), :]
        acc = jnp.zeros(tile_shape, F32)
        if keys_on_sublanes:
            for pr in range(IDX_HEADS // 2):
                s2 = lax.dot_general(kit, qs_ref[2 * pr * qb:(2 * pr + 2) * qb, :], nt_dims,
                                     preferred_element_type=F32)
                for e in range(2):
                    acc = acc + head_w(2 * pr + e) * jnp.maximum(s2[:, e * qb:(e + 1) * qb], 0.0)
        else:
            s_all = lax.dot_general(qs_ref[...], kit, nt_dims, preferred_element_type=F32)
            for h in range(IDX_HEADS):
                acc = acc + head_w(h) * jnp.maximum(s_all[h * qb:(h + 1) * qb, :], 0.0)
        sc_ref[kt] = jnp.where(key_index(kt) < n_valid, acc, -jnp.inf)
        return carry

    lax.fori_loop(0, nkt, score_tile, 0)

    def fold(m, op2, red):
        if keys_on_sublanes:
            return red(m.reshape(tk // (n_acc * SUBLANES), n_acc, SUBLANES, qb), axis=0)
        f = m[:, 0:LANES]
        for c in range(1, tk // LANES):
            f = op2(f, m[:, c * LANES:(c + 1) * LANES])
        return f

    def reduce_tiles(make, op2, red, init):
        def body(kt, part):
            return op2(part, fold(make(sc_ref[kt], kt), op2, red))
        part = lax.fori_loop(0, nkt, body, jnp.full(part_shape, init, F32))
        if keys_on_sublanes:
            part = red(part, axis=0)
        return red(part, axis=kax, keepdims=True)

    def count(pred):
        return reduce_tiles(lambda t, kt: jnp.where(pred(t, kt), 1.0, 0.0), jnp.add, jnp.sum, 0.0)

    def row_max(pred):
        return reduce_tiles(lambda t, kt: jnp.where(pred(t, kt), t, -jnp.inf), jnp.maximum, jnp.max,
                            -jnp.inf)

    def row_min_valid():
        return reduce_tiles(lambda t, kt: jnp.where(key_index(kt) < n_valid, t, jnp.inf), jnp.minimum,
                            jnp.min, jnp.inf)

    def any_set(flag_f):
        return jnp.max(flag_f) > 0.0

    def bisect(_, state):
        lo, hi, c_lo = state
        mid = lo + (hi - lo) * 0.5
        c = count(lambda t, kt: t >= mid)
        ge = c >= kf
        return jnp.where(ge, mid, lo), jnp.where(ge, hi, mid), jnp.where(ge, c, c_lo)

    lo0 = row_min_valid()
    hi0 = row_max(lambda t, kt: t == t)
    lo, hi, c_lo = lax.fori_loop(0, n_bisect, bisect, (lo0, hi0, n_valid.astype(F32)))
    found_f = jnp.where(c_lo == kf, active_f, 0.0)
    v_found = reduce_tiles(lambda t, kt: jnp.where(t >= lo, t, jnp.inf), jnp.minimum, jnp.min, jnp.inf)

    def walk_down():
        w0 = row_max(lambda t, kt: t <= hi)
        c0 = count(lambda t, kt: t >= w0)

        def walk_cond(state):
            w, c = state
            return any_set(jnp.where(c < kf, active_f, 0.0))

        def walk_body(state):
            w, c = state
            w2 = row_max(lambda t, kt: t < w)
            c2 = count(lambda t, kt: t >= w2)
            upd = c < kf
            return jnp.where(upd, w2, w), jnp.where(upd, c2, c)

        return lax.while_loop(walk_cond, walk_body,
                              (jnp.where(found_f > 0.5, v_found, w0),
                               jnp.where(found_f > 0.5, kf, c0)))

    w, c_ge = lax.cond(any_set(active_f - found_f), walk_down,
                       lambda: (v_found, jnp.full(vec_shape, kf, F32)))
    thr = jnp.where(active_f > 0.5, w, -jnp.inf)
    tied_f = jnp.where(c_ge > kf, active_f, 0.0)

    def write_bias(sel_fn):
        def body(kt, carry):
            kidx = key_index(kt)
            sel = sel_fn(sc_ref[kt], kidx) & (kidx < n_valid)
            bias = jnp.where(sel, 0.0, NEG_BIG)
            bias_ref[kt] = bias.T if keys_on_sublanes else bias
            return carry
        lax.fori_loop(0, nkt, body, 0)

    def no_ties():
        write_bias(lambda t, kidx: t >= thr)

    def with_ties():
        need = kf - count(lambda t, kt: t > thr)
        n_steps = max(1, int(np.ceil(np.log2(sc_ref.shape[0] * tk))) + 1)

        def step(_, lh):
            lo_i, hi_i = lh
            mid = lax.shift_right_arithmetic(lo_i + hi_i, 1)
            ge = count(lambda t, kt: (t == thr) & (key_index(kt) <= mid)) >= need
            return jnp.where(ge, lo_i, mid), jnp.where(ge, mid, hi_i)

        lo_i = jnp.full(vec_shape, -1, jnp.int32)
        hi_i = jnp.full(vec_shape, sc_ref.shape[0] * tk - 1, jnp.int32)
        _, last_tie = lax.fori_loop(0, n_steps, step, (lo_i, hi_i))
        write_bias(lambda t, kidx: (t > thr) | ((t == thr)
                                                & ((kidx <= last_tie) | (tied_f < 0.5))))

    any_tie = jnp.max(tied_f) > 0.0
    lax.cond(any_tie, with_ties, no_ties)

    rep = ATT_HEADS // KV_HEADS
    qgs = [jnp.concatenate(
        [q_ref[0, :, (g * rep + r) * HEAD_DIM:(g * rep + r + 1) * HEAD_DIM] for r in range(rep)],
        axis=0) for g in range(KV_HEADS)]
    acc_ref[...] = jnp.zeros(acc_ref.shape, F32)
    bufs = ((s0_ref, mt0_ref), (s1_ref, mt1_ref))

    def qk_tile(kt, buf):
        s_ref, mt_ref = bufs[buf]
        row0 = pl.multiple_of(kt * tk, tk)
        bias = bias_ref[kt]
        for g in range(KV_HEADS):
            kg = k_src[pl.ds(row0, tk), g * HEAD_DIM:(g + 1) * HEAD_DIM]
            s = lax.dot_general(qgs[g], kg, nt_dims, preferred_element_type=F32)
            s = (s.reshape(rep, qb, tk) + bias[None]).reshape(rep * qb, tk)
            s_ref[g] = s
            mt_ref[g] = jnp.broadcast_to(jnp.max(s, axis=-1, keepdims=True), (rep * qb, LANES))

    def lane_tile(x, width):
        return jnp.concatenate([x] * (width // LANES), axis=1)

    def att_tile(kt, buf, prefetch):
        if prefetch:
            qk_tile(kt + 1, 1 - buf)
        s_ref, mt_ref = bufs[buf]
        row0 = pl.multiple_of(kt * tk, tk)
        for g in range(KV_HEADS):
            vg = v_src[pl.ds(row0, tk), 2 * g * HEAD_DIM:(2 * g + 2) * HEAD_DIM]
            m_old = m_ref[g]
            m_new = jnp.maximum(m_old, mt_ref[g])
            m_ref[g] = m_new
            alpha = jnp.exp2(m_old - m_new)
            p = jnp.exp2(s_ref[g] - lane_tile(m_new, tk)).astype(BF16)
            acc_ref[g] = (lane_tile(alpha, 2 * HEAD_DIM) * acc_ref[g]
                          + jnp.dot(p, vg, preferred_element_type=F32))

    def tile_pair(pi, carry):
        att_tile(2 * pi, 0, True)
        att_tile(2 * pi + 1, 1, True)
        return carry

    def last_two():
        att_tile(nkt - 2, 0, True)
        att_tile(nkt - 1, 1, False)

    def last_one():
        att_tile(nkt - 1, 0, False)

    m_ref[...] = jnp.full(m_ref.shape, NEG_BIG, F32)
    qk_tile(0, 0)
    lax.fori_loop(0, (nkt - 1) // 2, tile_pair, 0)
    lax.cond((nkt & 1) == 0, last_two, last_one)
    for g in range(KV_HEADS):
        acc = acc_ref[g]
        o = acc[:, :HEAD_DIM] / acc[:, HEAD_DIM:]
        for r in range(rep):
            h = g * rep + r
            o_ref[0, :, h * HEAD_DIM:(h + 1) * HEAD_DIM] = o[r * qb:(r + 1) * qb].astype(BF16)


def _dsa_attention(q, qi, sm, k_new, v_new, ki_new, cache, *, qb, tk):
    b, t, _ = q.shape
    past = 0 if cache is None else cache[2].shape[1]
    n_keys = past + t
    nk_pad = -(-n_keys // tk) * tk
    topk = min(TOPK_MAX, n_keys // 4)
    keys_on_sublanes = qb == LANES
    rep = ATT_HEADS // KV_HEADS
    tokq = lambda w: pl.BlockSpec((1, qb, w), lambda i, j: (i, j, 0))
    keys = lambda w: pl.BlockSpec((1, t, w), lambda i, j: (i, 0, 0))
    tile_shape = (tk, qb) if keys_on_sublanes else (qb, tk)
    cache_specs, cache_scratch = [], []
    if cache is not None:
        kv = pl.BlockSpec((1, past * KV_HEADS, HEAD_DIM), lambda i, j: (i, 0, 0))
        cache_specs = [kv, kv, pl.BlockSpec((1, past, IDX_DIM), lambda i, j: (i, 0, 0))]
        cache_scratch = [pltpu.VMEM((nk_pad, w), BF16) for w in (KV_WIDTH, VEXT_WIDTH, IDX_DIM)]
    return pl.pallas_call(
        functools.partial(_dsa_kernel, qb=qb, tk=tk, n_keys=n_keys, past=past, topk=topk,
                          n_bisect=20, keys_on_sublanes=keys_on_sublanes),
        grid=(b, t // qb),
        in_specs=[tokq(ATT_WIDTH), tokq(QI_WIDTH), tokq(LANES), keys(KV_WIDTH), keys(VEXT_WIDTH),
                  keys(IDX_DIM)] + cache_specs,
        out_specs=tokq(ATT_WIDTH),
        out_shape=jax.ShapeDtypeStruct((b, t, ATT_WIDTH), BF16),
        scratch_shapes=[pltpu.VMEM((nk_pad // tk,) + tile_shape, F32),
                        pltpu.VMEM((nk_pad // tk, qb, tk), F32),
                        pltpu.VMEM((KV_HEADS, rep * qb, 2 * HEAD_DIM), F32),
                        pltpu.VMEM((IDX_HEADS * qb, IDX_DIM), BF16),
                        pltpu.VMEM((KV_HEADS, rep * qb, tk), F32),
                        pltpu.VMEM((KV_HEADS, rep * qb, tk), F32),
                        pltpu.VMEM((KV_HEADS, rep * qb, LANES), F32),
                        pltpu.VMEM((KV_HEADS, rep * qb, LANES), F32),
                        pltpu.VMEM((KV_HEADS, rep * qb, LANES), F32)] + cache_scratch,
        compiler_params=_params(2),
        name="dsa_attention",
    )(q, qi, sm, k_new, v_new, ki_new, *(cache or ()))


def _ssd_kernel(xbc_ref, z_ref, sm_ref, st_ref, alog_ref, dexp_ref, nw_ref, e_ref,
                y_ref, stout_ref, ht_ref, yi_ref):
    c = pl.program_id(1)
    nc = pl.num_programs(1)
    L = xbc_ref.shape[1]
    hp = SSD_INNER // SSD_GROUPS

    @pl.when(c == 0)
    def _():
        ht_ref[...] = st_ref[0].reshape(SSD_INNER, SSD_STATE).T

    xs = xbc_ref[0, :, 0:SSD_INNER]
    sm = sm_ref[0]
    lane = lax.broadcasted_iota(jnp.int32, (1, LANES), 1)
    is_dt = (lane >= SM_DT) & (lane < SM_DT + SSD_HEADS)
    a_neg = jnp.where(is_dt, -jnp.exp(alog_ref[...]), 0.0)
    dt = jnp.where(is_dt, sm, 0.0)
    rows = lax.broadcasted_iota(jnp.int32, (L, L), 0)
    cols = lax.broadcasted_iota(jnp.int32, (L, L), 1)
    tri = cols <= rows
    e = e_ref[...]
    cum = jnp.dot(tri.astype(F32), dt * a_neg, precision=HIGHEST, preferred_element_type=F32)
    eye = (lax.broadcasted_iota(jnp.int32, (LANES, LANES), 0)
           == lax.broadcasted_iota(jnp.int32, (LANES, LANES), 1)).astype(F32)
    cum_t = lax.dot_general(eye, cum, (((1,), (1,)), ((), ())), precision=HIGHEST,
                            preferred_element_type=F32)
    both = jnp.concatenate([cum, dt], axis=0)
    hi = both.astype(BF16)
    rest = both - hi.astype(F32)
    mid = rest.astype(BF16)
    low = (rest - mid.astype(F32)).astype(BF16)
    both_e = (jnp.dot(hi, e, preferred_element_type=F32) + jnp.dot(mid, e, preferred_element_type=F32)
              + jnp.dot(low, e, preferred_element_type=F32))
    cum_e, dt_e = both_e[:L], both_e[L:]
    last = cum_e[L - 1:L, :]
    xdt = (xs * dt_e).astype(BF16)
    xw = (xs * (jnp.exp(last - cum_e) * dt_e)).astype(BF16)
    ht_old = ht_ref[...]

    for g in range(SSD_GROUPS):
        bg = xbc_ref[0, :, SSD_INNER + g * SSD_STATE:SSD_INNER + (g + 1) * SSD_STATE]
        cg = xbc_ref[0, :, SSD_INNER + (SSD_GROUPS + g) * SSD_STATE:
                     SSD_INNER + (SSD_GROUPS + g + 1) * SSD_STATE]
        bgb, cgb = bg.astype(BF16), cg.astype(BF16)
        cb = lax.dot_general(cgb, bgb, (((1,), (1,)), ((), ())), preferred_element_type=F32)
        for hh in range(SSD_HEADS // SSD_GROUPS):
            h = g * (SSD_HEADS // SSD_GROUPS) + hh
            seg = cum[:, SM_DT + h:SM_DT + h + 1] - cum_t[SM_DT + h:SM_DT + h + 1, :]
            decay = jnp.exp(jnp.where(tri, seg, -jnp.inf))
            wts = (cb * decay).astype(BF16)
            yi_ref[:, h * SSD_HEADDIM:(h + 1) * SSD_HEADDIM] = jnp.dot(
                wts, xdt[:, h * SSD_HEADDIM:(h + 1) * SSD_HEADDIM], preferred_element_type=F32)
        ht_g = ht_old[:, g * hp:(g + 1) * hp]
        y_inter = jnp.dot(cgb, ht_g.astype(BF16), preferred_element_type=F32)
        yi_ref[:, g * hp:(g + 1) * hp] = (yi_ref[:, g * hp:(g + 1) * hp]
                                          + y_inter * jnp.exp(cum_e[:, g * hp:(g + 1) * hp]))
        upd = jnp.dot(bg.T.astype(BF16), xw[:, g * hp:(g + 1) * hp],
                      preferred_element_type=F32)
        ht_ref[:, g * hp:(g + 1) * hp] = ht_g * jnp.exp(last[:, g * hp:(g + 1) * hp]) + upd

    y = yi_ref[...] + dexp_ref[...] * xs
    y = y * _silu(_lane_aligned(z_ref[0], SM_DT, SSD_INNER))
    for g in range(SSD_GROUPS):
        yg = y[:, g * hp:(g + 1) * hp]
        ms = jnp.mean(yg * yg, axis=-1, keepdims=True)
        y_ref[0, :, g * hp:(g + 1) * hp] = (yg * lax.rsqrt(ms + EPS)
                                            * nw_ref[:, g * hp:(g + 1) * hp]).astype(BF16)

    @pl.when(c == nc - 1)
    def _():
        stout_ref[0] = ht_ref[...].T.reshape(SSD_HEADS, SSD_HEADDIM, SSD_STATE)


def _ssd(xbc_act, proj, sm, state, a_log_pad, d_exp, norm_w, e_mat, *, chunk):
    b, t, _ = xbc_act.shape
    tok = lambda w, col=0: pl.BlockSpec((1, chunk, w), lambda i, j: (i, j, col))
    full2 = lambda a: pl.BlockSpec(a.shape, lambda i, j: (0, 0))
    st = pl.BlockSpec((1, SSD_HEADS, SSD_HEADDIM, SSD_STATE), lambda i, j: (i, 0, 0, 0))
    nw = norm_w.reshape(1, SSD_INNER)
    return pl.pallas_call(
        _ssd_kernel,
        grid=(b, t // chunk),
        in_specs=[tok(SSD_CONV_DIM), tok(Z_BLOCK, (OFF_Z - SM_DT) // Z_BLOCK), tok(LANES), st,
                  full2(a_log_pad), full2(d_exp), full2(nw), full2(e_mat)],
        out_specs=[tok(SSD_INNER), st],
        out_shape=[jax.ShapeDtypeStruct((b, t, SSD_INNER), BF16),
                   jax.ShapeDtypeStruct(state.shape, F32)],
        scratch_shapes=[pltpu.VMEM((SSD_STATE, SSD_INNER), F32),
                        pltpu.VMEM((chunk, SSD_INNER), F32)],
        compiler_params=_params(2),
        name="ssd_scan",
    )(xbc_act, proj, sm, state, a_log_pad, d_exp, nw, e_mat)


def _lru_kernel(gate_ref, xin_ref, cw_ref, cb_ref, wa_ref, wx_ref, ba_ref, bx_ref, lam_ref,
                h0_ref, cs_ref, y_ref, hout_ref, cnew_ref, cbuf_ref, a_ref, u_ref, hp_ref):
    t = pl.program_id(1)
    nt = pl.num_programs(1)
    tt = xin_ref.shape[1]
    xc = _causal_conv(xin_ref[0], cs_ref, cw_ref, cb_ref, cbuf_ref, cnew_ref, t == 0, t == nt - 1)

    @pl.when(t == 0)
    def _():
        hp_ref[...] = h0_ref[0]

    for kb in range(LRU_BLOCKS):
        sl = slice(kb * LRU_BLOCK_DIM, (kb + 1) * LRU_BLOCK_DIM)
        xb = xc[:, sl].astype(BF16)
        a_ref[:, sl] = jnp.dot(xb, wa_ref[kb].astype(BF16), preferred_element_type=F32)
        u_ref[:, sl] = jnp.dot(xb, wx_ref[kb].astype(BF16), preferred_element_type=F32)
    r = jax.nn.sigmoid(a_ref[...] + ba_ref[...])
    i = jax.nn.sigmoid(u_ref[...] + bx_ref[...])
    log_a = -LRU_C * r * _softplus(-lam_ref[...])
    a = jnp.exp(log_a)
    u = jnp.sqrt(-jnp.tanh(log_a) * (1.0 + a * a)) * (i * xc)

    width = a.shape[1]
    a = a.reshape(tt // SUBLANES, SUBLANES, width)
    u = u.reshape(tt // SUBLANES, SUBLANES, width)
    sub = lax.broadcasted_iota(jnp.int32, (1, SUBLANES, width), 1)
    for s in (1, 2, 4):
        m = sub >= s
        a_sh = pltpu.roll(a, s, 1)
        u_sh = pltpu.roll(u, s, 1)
        u = jnp.where(m, a * u_sh + u, u)
        a = jnp.where(m, a * a_sh, a)
    a_ref[...] = a.reshape(tt, width)
    u_ref[...] = u.reshape(tt, width)

    def group(gi, hprev):
        r0 = pl.multiple_of(gi * SUBLANES, SUBLANES)
        hs = a_ref[pl.ds(r0, SUBLANES), :] * hprev + u_ref[pl.ds(r0, SUBLANES), :]
        u_ref[pl.ds(r0, SUBLANES), :] = hs
        return hs[SUBLANES - 1:SUBLANES, :]

    h_last = lax.fori_loop(0, tt // SUBLANES, group, hp_ref[...])
    hp_ref[...] = h_last
    gate = gate_ref[0]
    gelu = 0.5 * gate * (1.0 + jnp.tanh(np.sqrt(2.0 / np.pi) * (gate + 0.044715 * (gate * gate * gate))))
    y_ref[0] = (u_ref[...] * gelu).astype(BF16)

    @pl.when(t == nt - 1)
    def _():
        hout_ref[0] = h_last


def _lru(proj, conv_w, conv_b, w_a, b_a, w_x, b_x, lam, h0, conv_state, *, tt):
    b, t, _ = proj.shape
    w = LRU_WIDTH
    tok = lambda col: pl.BlockSpec((1, tt, w), lambda i, j: (i, j, col))
    full = lambda a: pl.BlockSpec(a.shape, lambda i, j: (0,) * a.ndim)
    row = lambda v: v.reshape(1, w)
    state = pl.BlockSpec((1, CONV_W - 1, w), lambda i, j: (i, 0, 0))
    hspec = pl.BlockSpec((1, 1, w), lambda i, j: (i, 0, 0))
    args = (proj, proj, conv_w, row(conv_b), w_a, w_x, row(b_a), row(b_x), row(lam), h0, conv_state)
    return pl.pallas_call(
        _lru_kernel,
        grid=(b, t // tt),
        in_specs=[tok(0), tok(1)] + [full(a) for a in args[2:9]] + [hspec, state],
        out_specs=[pl.BlockSpec((1, tt, w), lambda i, j: (i, j, 0)), hspec, state],
        out_shape=[jax.ShapeDtypeStruct((b, t, w), BF16),
                   jax.ShapeDtypeStruct((b, 1, w), F32),
                   jax.ShapeDtypeStruct((b, CONV_W - 1, w), F32)],
        scratch_shapes=[pltpu.VMEM((tt + SUBLANES, w), F32), pltpu.VMEM((tt, w), F32),
                        pltpu.VMEM((tt, w), F32), pltpu.VMEM((1, w), F32)],
        compiler_params=_params(2),
        name="rg_lru",
    )(*args)


def _rope_tables(pos, dim):
    half = dim // 2
    inv = np.power(ROPE_THETA, -np.arange(half, dtype=np.float64) / half)
    ang = pos.astype(np.float64)[:, None] * inv[None, :]
    cos = np.concatenate([np.cos(ang), np.cos(ang)], axis=-1)
    sin = np.concatenate([-np.sin(ang), np.sin(ang)], axis=-1)
    reps = LANES // dim
    return (jnp.asarray(np.tile(cos, (1, reps)), F32), jnp.asarray(np.tile(sin, (1, reps)), F32))


def _head_expansion():
    e = np.zeros((LANES, SSD_INNER), np.float32)
    for h in range(SSD_HEADS):
        e[SM_DT + h, h * SSD_HEADDIM:(h + 1) * SSD_HEADDIM] = 1.0
    return jnp.asarray(e, BF16)


def _pad_lanes(v, offset):
    out = jnp.zeros((1, LANES), F32)
    return out.at[0, offset:offset + v.shape[0]].set(v)


def _w_in_ab_tail(w_t):
    whole = AB_WIDTH // AB_TILE_N * AB_TILE_N
    return jnp.pad(w_t[whole:], ((0, AB_PAD_N - AB_WIDTH), (0, 0)))


def _layer0_mixers(proj, st, p, cfg):
    b, t, _ = proj.shape
    past = 0 if st["past_k"] is None else st["past_k"].shape[1]
    pos = np.arange(past, past + t)
    tabs = _rope_tables(pos, HEAD_DIM) + _rope_tables(pos, IDX_DIM)
    (q, qi, k, v, ki, sm, xbc_act, ssm_conv_new, kb, vbe, kib) = _post0(
        proj, tabs, p["q_norm_w"][0], p["k_norm_w"][0], p["ssd_conv_w"][0], p["ssd_conv_b"][0],
        p["dt_bias_pad"], st["ssm_conv"], tt=cfg["post_tt"])
    cache = None if not past else (st["past_k"].reshape(b, past * KV_HEADS, HEAD_DIM),
                                   st["past_v"].reshape(b, past * KV_HEADS, HEAD_DIM), st["past_ki"])
    att = _dsa_attention(q, qi, sm, kb, vbe, kib, cache, qb=cfg["qb"], tk=cfg["tk"])
    y_ssd, ssm_new = _ssd(xbc_act, proj, sm, st["ssm_h"], p["a_log_pad"], p["d_exp"],
                          p["ssd_norm_w"][0], p["e_mat"], chunk=cfg["ssd_chunk"])
    kv_shape = (1, b, t, KV_HEADS, HEAD_DIM)
    return att, y_ssd, (k.reshape(kv_shape), v.reshape(kv_shape), ki[None], ssm_new[None],
                        ssm_conv_new[None])


def _layer1_mixer(proj, st, p, cfg):
    b = proj.shape[0]
    y_lru, lru_new, lru_conv_new = _lru(
        proj, p["lru_conv_w"][0], p["lru_conv_b"][0], p["lru_w_a"][0], p["lru_b_a"][0],
        p["lru_w_x"][0], p["lru_b_x"][0], p["lru_lambda"][0], st["lru_h"].reshape(b, 1, LRU_WIDTH),
        st["lru_conv"], tt=cfg["lru_tt"])
    return y_lru, (lru_new.reshape(1, b, LRU_WIDTH), lru_conv_new[None])


def _forward(xp, xs, mods_p, mods_s, st_p, st_s, p, cfg_p, cfg_s):
    bb, tt = cfg_p["bb"], cfg_p["tt"]

    def ffn(xp, xs, l):
        (_, _, _, sh_p, sc_p, g_p), (_, _, _, sh_s, sc_s, g_s) = mods_p[l], mods_s[l]
        gu_p, gu_s = _norm_mod_matmul(
            xp, p["norm_ffn_w"][l], sc_p, sh_p, [p["ffn_w_gate"], p["ffn_w_up"]], l, bb=bb,
            tt=cfg_p["tt_ffn"], tn=512, out_dtype=BF16, name="ffn_gate_up", ride=(xs, sc_s, sh_s))
        return _matmul_residual([gu_p], p["ffn_w_down"], l, xp, g_p, bb=bb, tt=cfg_p["tt_ffn"],
                                tn=256, name="ffn_down", ride=([gu_s], xs, g_s))

    (sh_p, sc_p, g_p, *_), (sh_s, sc_s, g_s, *_) = mods_p[0], mods_s[0]
    proj_p, proj_s = _norm_mod_matmul(
        xp, p["norm_mix_w"][0], sc_p, sh_p, [p["w_in_ab_t"]], 0, bb=bb, tt=cfg_p["tt_ab"],
        tn=AB_TILE_N, out_dtype=F32, name="in_proj_ab", w_tail=p["w_in_ab_tail"], w_rows_out=True,
        ride=(xs, sc_s, sh_s))
    att_p, y_p, outs0_p = _layer0_mixers(proj_p, st_p, p, cfg_p)
    att_s, y_s, outs0_s = _layer0_mixers(proj_s, st_s, p, cfg_s)
    xp, xs = _matmul_residual([att_p, y_p], p["w_out_ab"], 0, xp, g_p, bb=bb, tt=tt, tn=512,
                              name="out_proj_ab", ride=([att_s, y_s], xs, g_s))
    xp, xs = ffn(xp, xs, 0)

    (sh_p, sc_p, g_p, *_), (sh_s, sc_s, g_s, *_) = mods_p[1], mods_s[1]
    proj_p, proj_s = _norm_mod_matmul(
        xp, p["norm_mix_w"][1], sc_p, sh_p, [p["w_in_c"]], 0, bb=bb, tt=cfg_p["tt_ab"], tn=512,
        out_dtype=F32, name="in_proj_c", ride=(xs, sc_s, sh_s))
    y_p, outs1_p = _layer1_mixer(proj_p, st_p, p, cfg_p)
    y_s, outs1_s = _layer1_mixer(proj_s, st_s, p, cfg_s)
    xp, xs = _matmul_residual([y_p], p["w_out_c"], 0, xp, g_p, bb=bb, tt=tt, tn=512,
                              name="out_proj_c", ride=([y_s], xs, g_s))
    xp, xs = ffn(xp, xs, 1)
    return (xp,) + outs0_p + outs1_p, (xs,) + outs0_s + outs1_s


PROMPT_CFG = dict(bb=1, tt=2048, tt_ab=1024, tt_ffn=1024, post_tt=512, qb=128, tk=512,
                  ssd_chunk=128, lru_tt=512)
SAMPLE_CFG = dict(post_tt=32, qb=32, tk=384, ssd_chunk=32, lru_tt=32)


def kernel(x_prompt, x_sample, cache_attn_k, cache_attn_v, cache_idx_k, state_ssm, state_ssm_conv,
           state_lru, state_lru_conv, c_prompt, c_sample, ada_w, ada_b, norm_mix_w, norm_ffn_w,
           w_in_ab, q_norm_w, k_norm_w, ssd_conv_w, ssd_conv_b, ssd_dt_bias, ssd_a_log, ssd_d,
           ssd_norm_w, w_out_ab, w_in_c, lru_conv_w, lru_conv_b, lru_w_a, lru_b_a, lru_w_x, lru_b_x,
           lru_lambda, w_out_c, ffn_w_gate, ffn_w_up, ffn_w_down):
    bp, bs = x_prompt.shape[0], x_sample.shape[0]
    p = dict(norm_mix_w=norm_mix_w, norm_ffn_w=norm_ffn_w, q_norm_w=q_norm_w, k_norm_w=k_norm_w,
             ssd_conv_w=ssd_conv_w, ssd_conv_b=ssd_conv_b, ssd_norm_w=ssd_norm_w, w_out_ab=w_out_ab,
             w_in_c=w_in_c, lru_conv_w=lru_conv_w, lru_conv_b=lru_conv_b, lru_w_a=lru_w_a,
             lru_b_a=lru_b_a, lru_w_x=lru_w_x, lru_b_x=lru_b_x, lru_lambda=lru_lambda,
             w_out_c=w_out_c, ffn_w_gate=ffn_w_gate, ffn_w_up=ffn_w_up, ffn_w_down=ffn_w_down)
    p["w_in_ab_t"] = jnp.swapaxes(w_in_ab, 1, 2)
    p["w_in_ab_tail"] = _w_in_ab_tail(p["w_in_ab_t"][0])
    p["dt_bias_pad"] = _pad_lanes(ssd_dt_bias[0], SM_DT)
    p["a_log_pad"] = _pad_lanes(ssd_a_log[0], SM_DT)
    p["d_exp"] = jnp.repeat(ssd_d[0], SSD_HEADDIM).reshape(1, SSD_INNER)
    p["e_mat"] = _head_expansion()

    m_rows = 32
    c_all = jnp.concatenate([c_prompt, c_sample, jnp.zeros((m_rows - bp - bs, D_MODEL), F32)], axis=0)
    mod = _modulation(c_all, ada_w, ada_b)

    def group_mods(r0, nb):
        return [[mod[l, r0:r0 + nb, i * D_MODEL:(i + 1) * D_MODEL].reshape(nb, 1, D_MODEL)
                 for i in range(6)] for l in range(mod.shape[0])]

    zeros = lambda *s: jnp.zeros(s, F32)
    st_p = dict(past_k=None, past_v=None, past_ki=None,
                ssm_h=zeros(bp, SSD_HEADS, SSD_HEADDIM, SSD_STATE),
                ssm_conv=zeros(bp, CONV_W - 1, SSD_CONV_DIM), lru_h=zeros(bp, LRU_WIDTH),
                lru_conv=zeros(bp, CONV_W - 1, LRU_WIDTH))
    st_s = dict(past_k=cache_attn_k[0], past_v=cache_attn_v[0], past_ki=cache_idx_k[0],
                ssm_h=state_ssm[0], ssm_conv=state_ssm_conv[0], lru_h=state_lru[0],
                lru_conv=state_lru_conv[0])
    out_p, out_s = _forward(x_prompt, x_sample, group_mods(0, bp), group_mods(bp, bs), st_p, st_s, p,
                            PROMPT_CFG, SAMPLE_CFG)
    return (out_p[0], out_s[0]) + out_p[1:] + out_s[1:]
```

```python
import functools

import numpy as np
import jax
import jax.numpy as jnp
from jax import lax
from jax.experimental import pallas as pl
from jax.experimental.pallas import tpu as pltpu

F32 = jnp.float32
BF16 = jnp.bfloat16
HIGHEST = lax.Precision.HIGHEST

D_MODEL = 2048
CHUNK = 64
CHUNK_SHIFT = 6
ATT_HEADS = 8
KV_HEADS = 2
HEAD_DIM = 128
IDX_HEADS = 16
IDX_DIM = 64
TOPK_MAX = 256
ROPE_THETA = 10000.0
SSD_HEADS = 16
SSD_HEADDIM = 64
SSD_GROUPS = 2
SSD_STATE = 128
SSD_INNER = SSD_HEADS * SSD_HEADDIM
SSD_CONV_DIM = SSD_INNER + 2 * SSD_GROUPS * SSD_STATE
CONV_W = 4
LRU_WIDTH = D_MODEL
LRU_BLOCKS = 16
LRU_BLOCK_DIM = LRU_WIDTH // LRU_BLOCKS
LRU_C = 8.0
EPS = 1e-6
ATT_WIDTH = ATT_HEADS * HEAD_DIM
KV_WIDTH = KV_HEADS * HEAD_DIM
VEXT_WIDTH = 2 * KV_WIDTH
QI_WIDTH = IDX_HEADS * IDX_DIM
Q_SCALE = HEAD_DIM ** -0.5 * float(np.log2(np.e))

LANES = 128
SUBLANES = 8
VMEM_LIMIT_BYTES = 62 * 1024 * 1024
SINGLE_BUFFER_ROWS = 2048
NORM_CHUNKS = 4

OFF_Q = 0
OFF_K = OFF_Q + ATT_WIDTH
OFF_V = OFF_K + KV_WIDTH
OFF_QI = OFF_V + KV_WIDTH
OFF_SMALL = OFF_QI + QI_WIDTH
SM_WI = IDX_DIM
SM_DT = IDX_DIM + IDX_HEADS
OFF_Z = OFF_SMALL + SM_DT
OFF_XBC = OFF_Z + SSD_INNER
OFF_DT = OFF_XBC + SSD_CONV_DIM
AB_WIDTH = OFF_DT + SSD_HEADS
Z_BLOCK = 1280
XBC_BLOCK = 1792
AB_TILE_N = 512
AB_PAD_N = 5632
assert OFF_Z % LANES == SM_DT and OFF_XBC % LANES == SM_DT and OFF_DT % LANES == SM_DT
assert (OFF_Z - SM_DT) % Z_BLOCK == 0 and OFF_Z - SM_DT + Z_BLOCK >= OFF_Z + SSD_INNER
assert (OFF_XBC - SM_DT) % XBC_BLOCK == 0 and XBC_BLOCK >= OFF_DT - OFF_XBC + LANES
assert OFF_XBC - SM_DT + XBC_BLOCK <= AB_PAD_N

NEG_BIG = -1e30


def _params(n_axes):
    return pltpu.CompilerParams(dimension_semantics=("arbitrary",) * n_axes,
                                vmem_limit_bytes=VMEM_LIMIT_BYTES)


def _silu(x):
    return x * jax.nn.sigmoid(x)


def _softplus(x):
    return jnp.maximum(x, 0.0) + jnp.log1p(jnp.exp(-jnp.abs(x)))


def _mod_kernel(c_ref, w_ref, b_ref, o_ref):
    a = _silu(c_ref[...]).astype(BF16)
    o_ref[0] = jnp.dot(a, w_ref[0].astype(BF16), preferred_element_type=F32) + b_ref[0]


def _modulation(c_all, ada_w, ada_b):
    depth, d, n = ada_w.shape
    m = c_all.shape[0]
    tn = 1024
    return pl.pallas_call(
        _mod_kernel,
        grid=(depth, n // tn),
        in_specs=[pl.BlockSpec((m, d), lambda l, j: (0, 0)),
                  pl.BlockSpec((1, d, tn), lambda l, j: (l, 0, j)),
                  pl.BlockSpec((1, 1, tn), lambda l, j: (l, 0, j))],
        out_specs=pl.BlockSpec((1, m, tn), lambda l, j: (l, 0, j)),
        out_shape=jax.ShapeDtypeStruct((depth, m, n), F32),
        compiler_params=_params(2),
        name="adaln_mod",
    )(c_all, ada_w, ada_b.reshape(depth, 1, n))


def _first_row_tile():
    return (pl.program_id(0) == 0) & (pl.program_id(1) == 0)


def _nmm_kernel(*refs, swiglu, tail, w_rows_out, ride):
    refs = list(refs)
    x_ref, nw_ref, sc_ref, sh_ref = refs[:4]
    del refs[:4]
    if ride:
        xr_ref, scr_ref, shr_ref = refs[:3]
        del refs[:3]
    n_w = 2 if swiglu or tail else 1
    w_refs = refs[:n_w]
    del refs[:n_w]
    if ride:
        o_ref, or_ref, h_ref, hr_ref = refs
    else:
        o_ref, h_ref = refs
    k = pl.program_id(2)

    def matmul(h, w):
        dims = (((1,), (1,)), ((), ())) if w_rows_out else (((1,), (0,)), ((), ()))
        return lax.dot_general(h, w, dims, preferred_element_type=F32)

    def finish(accs, out_ref):
        o = _silu(accs[0]) * accs[1] if swiglu else accs[0]
        out_ref[...] = o.reshape(out_ref.shape).astype(out_ref.dtype)

    def project(ws, rows_ref, out_ref):
        h = rows_ref[...]
        finish([matmul(h, w) for w in ws], out_ref)

    def norm_project(ws, src_ref, scale_ref, shift_ref, rows_ref, out_ref):
        x = src_ref[...]
        d = x.shape[-1]
        rs = lax.rsqrt(jnp.mean(x * x, axis=-1, keepdims=True) + EPS)
        kc = d // NORM_CHUNKS
        accs = None
        for c in range(NORM_CHUNKS):
            cols = slice(c * kc, (c + 1) * kc)
            hc = (x[..., cols] * rs * nw_ref[..., cols] * (1.0 + scale_ref[..., cols])
                  + shift_ref[..., cols])
            hc = hc.reshape(rows_ref.shape[0], kc).astype(BF16)
            rows_ref[:, cols] = hc
            part = [matmul(hc, w[:, cols] if w_rows_out else w[cols, :]) for w in ws]
            accs = part if accs is None else [a + b for a, b in zip(accs, part)]
        finish(accs, out_ref)

    def step(tile_refs, first):
        ws = [w[...].astype(BF16) for w in tile_refs]
        if first:
            norm_project(ws, x_ref, sc_ref, sh_ref, h_ref, o_ref)
            if ride:
                pl.when(_first_row_tile())(
                    lambda: norm_project(ws, xr_ref, scr_ref, shr_ref, hr_ref, or_ref))
        else:
            project(ws, h_ref, o_ref)
            if ride:
                pl.when(_first_row_tile())(lambda: project(ws, hr_ref, or_ref))

    last = pl.num_programs(2) - 1 if tail else pl.num_programs(2)
    pl.when(k == 0)(lambda: step(w_refs[:n_w - tail], True))
    pl.when((k > 0) & (k < last))(lambda: step(w_refs[:n_w - tail], False))
    if tail:
        pl.when(k == last)(lambda: step(w_refs[1:], False))


def _row_block_mode(rows):
    return dict(pipeline_mode=pl.Buffered(1)) if rows >= SINGLE_BUFFER_ROWS else {}


def _ride_col_tile(n_tiles):
    return lambda i, j, k: (0, 0, jnp.where((i == 0) & (j == 0), k, n_tiles - 1))


def _norm_mod_matmul(x, nw, sc, sh, ws, layer, *, bb, tt, tn, out_dtype, name, w_tail=None,
                     w_rows_out=False, ride=None):
    b, t, d = x.shape
    swiglu = len(ws) == 2
    n_main = ws[0].shape[1 if w_rows_out else 2] // tn
    n_tiles = n_main + (w_tail is not None)
    n = n_tiles * tn
    xmap = lambda i, j, k: (i, j, 0)
    mmap = lambda i, j, k: (i, 0, 0)
    const3 = lambda i, j, k: (0, 0, 0)
    tile = lambda k: jnp.minimum(k, n_main - 1)
    if w_rows_out:
        w_specs = [pl.BlockSpec((None, tn, d), lambda i, j, k: (layer, tile(k), 0))] * len(ws)
    else:
        w_specs = [pl.BlockSpec((None, d, tn), lambda i, j, k: (layer, 0, tile(k)))] * len(ws)
    operands = list(ws)
    if w_tail is not None:
        w_specs.append(pl.BlockSpec(w_tail.shape, lambda i, j, k: (0, 0)))
        operands.append(w_tail)
    in_specs = [pl.BlockSpec((bb, tt, d), xmap, **_row_block_mode(bb * tt)),
                pl.BlockSpec((1, 1, d), const3),
                pl.BlockSpec((bb, 1, d), mmap),
                pl.BlockSpec((bb, 1, d), mmap)]
    out_specs = [pl.BlockSpec((bb, tt, tn), lambda i, j, k: (i, j, k))]
    out_shape = [jax.ShapeDtypeStruct((b, t, n), out_dtype)]
    scratch = [pltpu.VMEM((bb * tt, d), BF16)]
    ride_ops = ()
    if ride is not None:
        ride_ops = ride
        rb, rt, _ = ride[0].shape
        in_specs += [pl.BlockSpec((rb, rt, d), const3, pipeline_mode=pl.Buffered(1)),
                     pl.BlockSpec((rb, 1, d), const3), pl.BlockSpec((rb, 1, d), const3)]
        out_specs.append(pl.BlockSpec((rb, rt, tn), _ride_col_tile(n_tiles)))
        out_shape.append(jax.ShapeDtypeStruct((rb, rt, n), out_dtype))
        scratch.append(pltpu.VMEM((rb * rt, d), BF16))
    outs = pl.pallas_call(
        functools.partial(_nmm_kernel, swiglu=swiglu, tail=w_tail is not None,
                          w_rows_out=w_rows_out, ride=ride is not None),
        grid=(b // bb, t // tt, n_tiles),
        in_specs=in_specs + w_specs,
        out_specs=out_specs,
        out_shape=out_shape,
        scratch_shapes=scratch,
        compiler_params=_params(3),
        name=name,
    )(x, nw.reshape(1, 1, d), sc, sh, *ride_ops, *operands)
    return outs if ride is not None else outs[0]


def _mmr_kernel(*refs, k_sizes, ride):
    refs = list(refs)
    n_a = len(k_sizes)
    groups = [refs[:n_a + 2]]
    del refs[:n_a + 2]
    if ride:
        groups.append(refs[:n_a + 2])
        del refs[:n_a + 2]
    w_ref = refs[0]
    out_refs = refs[1:]
    offs = np.cumsum((0,) + tuple(k_sizes))
    ws = [w_ref[offs[i]:offs[i + 1], :].astype(BF16) for i in range(n_a)]

    def apply(group, o_ref):
        *a_refs, x_ref, g_ref = group
        bb, tt, tn = o_ref.shape
        acc = jnp.zeros((bb * tt, tn), F32)
        for a_ref, w, ks in zip(a_refs, ws, k_sizes):
            acc = acc + jnp.dot(a_ref[...].reshape(bb * tt, ks), w, preferred_element_type=F32)
        o_ref[...] = x_ref[...] + g_ref[...] * acc.reshape(bb, tt, tn)

    apply(groups[0], out_refs[0])
    if ride:
        pl.when(_first_row_tile())(lambda: apply(groups[1], out_refs[1]))


def _matmul_residual(a_list, w, layer, x, g, *, bb, tt, tn, name, ride=None):
    b, t, d = x.shape
    k_sizes = tuple(a.shape[-1] for a in a_list)
    k_total = sum(k_sizes)
    n_tiles = d // tn
    in_specs = ([pl.BlockSpec((bb, tt, ks), lambda i, j, k: (i, j, 0), **_row_block_mode(bb * tt))
                 for ks in k_sizes]
                + [pl.BlockSpec((bb, tt, tn), lambda i, j, k: (i, j, k)),
                   pl.BlockSpec((bb, 1, tn), lambda i, j, k: (i, 0, k))])
    out_specs = [pl.BlockSpec((bb, tt, tn), lambda i, j, k: (i, j, k))]
    out_shape = [jax.ShapeDtypeStruct((b, t, d), F32)]
    ride_ops = ()
    if ride is not None:
        ra_list, rx, rg = ride
        rb, rt, _ = rx.shape
        ride_ops = (*ra_list, rx, rg)
        col = _ride_col_tile(n_tiles)
        in_specs += ([pl.BlockSpec((rb, rt, ks), lambda i, j, k: (0, 0, 0),
                                   pipeline_mode=pl.Buffered(1)) for ks in k_sizes]
                     + [pl.BlockSpec((rb, rt, tn), col), pl.BlockSpec((rb, 1, tn), col)])
        out_specs.append(pl.BlockSpec((rb, rt, tn), col))
        out_shape.append(jax.ShapeDtypeStruct(rx.shape, F32))
    outs = pl.pallas_call(
        functools.partial(_mmr_kernel, k_sizes=k_sizes, ride=ride is not None),
        grid=(b // bb, t // tt, n_tiles),
        in_specs=in_specs + [pl.BlockSpec((None, k_total, tn), lambda i, j, k: (layer, 0, k))],
        out_specs=out_specs,
        out_shape=out_shape,
        compiler_params=_params(3),
        name=name,
    )(*a_list, x, g, *ride_ops, w)
    return outs if ride is not None else outs[0]


def _swap_halves(x, half):
    w = x.shape[-1]
    if w > LANES:
        return jnp.concatenate([_swap_halves(x[..., c * LANES:(c + 1) * LANES], half)
                                for c in range(w // LANES)], axis=-1)
    if w == 2 * half:
        return pltpu.roll(x, half, x.ndim - 1)
    lane = lax.broadcasted_iota(jnp.int32, x.shape, x.ndim - 1)
    first = (lane & half) == 0
    return jnp.where(first, pltpu.roll(x, w - half, x.ndim - 1), pltpu.roll(x, half, x.ndim - 1))


def _lane_aligned(slab, offset, width):
    return pltpu.roll(slab, slab.shape[-1] - offset, slab.ndim - 1)[:, :width]


def _rope(x, cos, sin_signed, half):
    reps = x.shape[-1] // cos.shape[-1]
    if reps > 1:
        cos = jnp.concatenate([cos] * reps, axis=-1)
        sin_signed = jnp.concatenate([sin_signed] * reps, axis=-1)
    return x * cos + _swap_halves(x, half) * sin_signed


def _causal_conv(u, state_ref, w_ref, b_ref, cbuf_ref, new_ref, first, last):
    tt = u.shape[0]

    @pl.when(first)
    def _():
        cbuf_ref[0:SUBLANES, :] = jnp.zeros((SUBLANES, u.shape[1]), F32)
        cbuf_ref[SUBLANES - (CONV_W - 1):SUBLANES, :] = state_ref[0]

    cbuf_ref[SUBLANES:SUBLANES + tt, :] = u
    full = cbuf_ref[...]
    out = b_ref[...]
    for j in range(CONV_W):
        shift = CONV_W - 1 - j
        rows = pltpu.roll(full, shift, 0) if shift else full
        out = out + rows[SUBLANES:SUBLANES + tt, :] * w_ref[j:j + 1, :]
    tail = cbuf_ref[tt:tt + SUBLANES, :]
    cbuf_ref[0:SUBLANES, :] = tail

    @pl.when(last)
    def _():
        new_ref[0] = tail[SUBLANES - (CONV_W - 1):, :]

    return out


def _post0_kernel(p_ref, pdt_ref, cq_ref, sq_ref, ci_ref, si_ref, qn_ref, kn_ref, dtb_ref,
                  q_ref, qi_ref, k_ref, v_ref, ki_ref, sm_ref, kb_ref, vb_ref, kib_ref):
    cq, sq, ci, si = cq_ref[...], sq_ref[...], ci_ref[...], si_ref[...]

    def head_norm_rope(x, w):
        ms = jnp.mean(x * x, axis=-1, keepdims=True)
        return _rope(x * lax.rsqrt(ms + EPS) * w, cq, sq, HEAD_DIM // 2)

    for h in range(ATT_HEADS):
        xh = p_ref[0, :, OFF_Q + h * HEAD_DIM:OFF_Q + (h + 1) * HEAD_DIM]
        q_ref[0, :, h * HEAD_DIM:(h + 1) * HEAD_DIM] = (head_norm_rope(xh, qn_ref[...])
                                                         * Q_SCALE).astype(BF16)
    ones = jnp.ones((p_ref.shape[1], HEAD_DIM), BF16)
    for h in range(KV_HEADS):
        xh = p_ref[0, :, OFF_K + h * HEAD_DIM:OFF_K + (h + 1) * HEAD_DIM]
        kh = head_norm_rope(xh, kn_ref[...])
        vh = p_ref[0, :, OFF_V + h * HEAD_DIM:OFF_V + (h + 1) * HEAD_DIM]
        k_ref[0, pl.ds(h, xh.shape[0], stride=KV_HEADS), :] = kh
        v_ref[0, pl.ds(h, xh.shape[0], stride=KV_HEADS), :] = vh
        kb_ref[0, :, h * HEAD_DIM:(h + 1) * HEAD_DIM] = kh.astype(BF16)
        vb_ref[0, :, 2 * h * HEAD_DIM:(2 * h + 1) * HEAD_DIM] = vh.astype(BF16)
        vb_ref[0, :, (2 * h + 1) * HEAD_DIM:(2 * h + 2) * HEAD_DIM] = ones

    qi = p_ref[0, :, OFF_QI:OFF_QI + QI_WIDTH]
    qi_ref[0] = _rope(qi, ci, si, IDX_DIM // 2).astype(BF16)

    sm = p_ref[0, :, OFF_SMALL:OFF_SMALL + LANES]
    lane = lax.broadcasted_iota(jnp.int32, sm.shape, 1)
    ki_part = _rope(sm, ci, si, IDX_DIM // 2)
    wi_part = sm * (IDX_HEADS ** -0.5 * IDX_DIM ** -0.5)
    dt_part = _softplus(pdt_ref[0] + dtb_ref[...])
    sm_out = jnp.where(lane < SM_WI, ki_part,
                       jnp.where(lane < SM_DT, wi_part,
                                 jnp.where(lane < SM_DT + SSD_HEADS, dt_part, 0.0)))
    sm_ref[0] = sm_out
    ki_ref[0] = sm_out[:, :IDX_DIM]
    kib_ref[0] = sm_out[:, :IDX_DIM].astype(BF16)


def _post0(proj, tabs, q_norm_w, k_norm_w, dt_bias_pad, *, tt):
    b, t, n = proj.shape
    tok = lambda w, col=0: pl.BlockSpec((1, tt, w), lambda i, j: (i, j, col))
    tab = pl.BlockSpec((tt, LANES), lambda i, j: (j, 0))
    full2 = lambda a: pl.BlockSpec(a.shape, lambda i, j: (0, 0))
    out_shapes = [
        jax.ShapeDtypeStruct((b, t, ATT_WIDTH), BF16),
        jax.ShapeDtypeStruct((b, t, QI_WIDTH), BF16),
        jax.ShapeDtypeStruct((b, t * KV_HEADS, HEAD_DIM), F32),
        jax.ShapeDtypeStruct((b, t * KV_HEADS, HEAD_DIM), F32),
        jax.ShapeDtypeStruct((b, t, IDX_DIM), F32),
        jax.ShapeDtypeStruct((b, t, LANES), F32),
        jax.ShapeDtypeStruct((b, t, KV_WIDTH), BF16),
        jax.ShapeDtypeStruct((b, t, VEXT_WIDTH), BF16),
        jax.ShapeDtypeStruct((b, t, IDX_DIM), BF16),
    ]
    kv_rows = pl.BlockSpec((1, tt * KV_HEADS, HEAD_DIM), lambda i, j: (i, j, 0))
    out_specs = [tok(ATT_WIDTH), tok(QI_WIDTH), kv_rows, kv_rows, tok(IDX_DIM),
                 tok(LANES), tok(KV_WIDTH), tok(VEXT_WIDTH), tok(IDX_DIM)]
    qn = q_norm_w.reshape(1, HEAD_DIM)
    kn = k_norm_w.reshape(1, HEAD_DIM)
    head = OFF_SMALL + LANES
    return pl.pallas_call(
        _post0_kernel,
        grid=(b, t // tt),
        in_specs=[tok(head), tok(LANES, (OFF_DT - SM_DT) // LANES), tab, tab, tab, tab, full2(qn),
                  full2(kn), full2(dt_bias_pad)],
        out_specs=out_specs,
        out_shape=out_shapes,
        compiler_params=_params(2),
        name="post_proj0",
    )(proj, proj, *tabs, qn, kn, dt_bias_pad)


def _dsa_kernel(q_ref, qi_ref, sm_ref, k_ref, v_ref, ki_ref, *rest, qb, tk, n_keys, past, topk,
                n_bisect, keys_on_sublanes):
    if past:
        pk_ref, pv_ref, pki_ref = rest[:3]
        rest = rest[3:]
    (o_ref, sc_ref, bias_ref, acc_ref, qs_ref, s0_ref, s1_ref, mt0_ref, mt1_ref, m_ref) = rest[:10]
    if past:
        kall_ref, vall_ref, kiall_ref = rest[10:]
        t_new = k_ref.shape[1]
        pad = kall_ref.shape[0] - n_keys
        for g in range(KV_HEADS):
            head_rows = pl.ds(g, past, stride=KV_HEADS)
            kall_ref[0:past, g * HEAD_DIM:(g + 1) * HEAD_DIM] = pk_ref[0, head_rows, :].astype(BF16)
            vall_ref[0:past, 2 * g * HEAD_DIM:(2 * g + 1) * HEAD_DIM] = pv_ref[0, head_rows, :].astype(BF16)
            vall_ref[0:past, (2 * g + 1) * HEAD_DIM:(2 * g + 2) * HEAD_DIM] = jnp.ones(
                (past, HEAD_DIM), BF16)
        kiall_ref[0:past, :] = pki_ref[0].astype(BF16)
        for dst, src in ((kall_ref, k_ref), (vall_ref, v_ref), (kiall_ref, ki_ref)):
            dst[past:past + t_new, :] = src[0]
            dst[past + t_new:, :] = jnp.zeros((pad, dst.shape[1]), BF16)
        k_src, v_src, ki_src = kall_ref, vall_ref, kiall_ref
    else:
        k_src, v_src, ki_src = k_ref.at[0], v_ref.at[0], ki_ref.at[0]
    kax = 0 if keys_on_sublanes else 1
    tile_shape = (tk, qb) if keys_on_sublanes else (qb, tk)
    vec_shape = (1, qb) if keys_on_sublanes else (qb, 1)
    n_acc = 8
    part_shape = (n_acc, SUBLANES, qb) if keys_on_sublanes else (qb, LANES)
    j = pl.program_id(1)
    pos0 = past + j * qb
    q_chunk = lax.shift_right_logical(pos0 + lax.broadcasted_iota(jnp.int32, vec_shape, 1 - kax),
                                      CHUNK_SHIFT)
    n_valid = jnp.minimum((q_chunk + 1) * CHUNK, n_keys)
    max_valid = jnp.minimum(((pos0 + qb - 1) // CHUNK + 1) * CHUNK, n_keys)
    nkt = (max_valid + tk - 1) // tk
    kf = float(topk)
    active_f = jnp.where(n_valid > topk, 1.0, 0.0)

    def key_index(kt):
        return kt * tk + lax.broadcasted_iota(jnp.int32, tile_shape, kax)

    nt_dims = (((1,), (1,)), ((), ()))
    for h in range(IDX_HEADS):
        qs_ref[h * qb:(h + 1) * qb, :] = qi_ref[0, :, h * IDX_DIM:(h + 1) * IDX_DIM]
    if keys_on_sublanes:
        wi_t = sm_ref[0].T
        head_w = lambda h: wi_t[SM_WI + h:SM_WI + h + 1, :]
    else:
        wis = sm_ref[0][:, SM_WI:SM_WI + IDX_HEADS]
        head_w = lambda h: wis[:, h:h + 1]

    def score_tile(kt, carry):
        kit = ki_src[pl.ds(pl.multiple_of(kt * tk, tk), tk), :]
        acc = jnp.zeros(tile_shape, F32)
        if keys_on_sublanes:
            for pr in range(IDX_HEADS // 2):
                s2 = lax.dot_general(kit, qs_ref[2 * pr * qb:(2 * pr + 2) * qb, :], nt_dims,
                                     preferred_element_type=F32)
                for e in range(2):
                    acc = acc + head_w(2 * pr + e) * jnp.maximum(s2[:, e * qb:(e + 1) * qb], 0.0)
        else:
            s_all = lax.dot_general(qs_ref[...], kit, nt_dims, preferred_element_type=F32)
            for h in range(IDX_HEADS):
                acc = acc + head_w(h) * jnp.maximum(s_all[h * qb:(h + 1) * qb, :], 0.0)
        sc_ref[kt] = jnp.where(key_index(kt) < n_valid, acc, -jnp.inf)
        return carry

    lax.fori_loop(0, nkt, score_tile, 0)

    def fold(m, op2, red):
        if keys_on_sublanes:
            return red(m.reshape(tk // (n_acc * SUBLANES), n_acc, SUBLANES, qb), axis=0)
        f = m[:, 0:LANES]
        for c in range(1, tk // LANES):
            f = op2(f, m[:, c * LANES:(c + 1) * LANES])
        return f

    def reduce_tiles(make, op2, red, init):
        def body(kt, part):
            return op2(part, fold(make(sc_ref[kt], kt), op2, red))
        part = lax.fori_loop(0, nkt, body, jnp.full(part_shape, init, F32))
        if keys_on_sublanes:
            part = red(part, axis=0)
        return red(part, axis=kax, keepdims=True)

    def count(pred):
        return reduce_tiles(lambda t, kt: jnp.where(pred(t, kt), 1.0, 0.0), jnp.add, jnp.sum, 0.0)

    def row_max(pred):
        return reduce_tiles(lambda t, kt: jnp.where(pred(t, kt), t, -jnp.inf), jnp.maximum, jnp.max,
                            -jnp.inf)

    def row_min_valid():
        return reduce_tiles(lambda t, kt: jnp.where(key_index(kt) < n_valid, t, jnp.inf), jnp.minimum,
                            jnp.min, jnp.inf)

    def any_set(flag_f):
        return jnp.max(flag_f) > 0.0

    def bisect(_, state):
        lo, hi, c_lo = state
        mid = lo + (hi - lo) * 0.5
        c = count(lambda t, kt: t >= mid)
        ge = c >= kf
        return jnp.where(ge, mid, lo), jnp.where(ge, hi, mid), jnp.where(ge, c, c_lo)

    lo0 = row_min_valid()
    hi0 = row_max(lambda t, kt: t == t)
    lo, hi, c_lo = lax.fori_loop(0, n_bisect, bisect, (lo0, hi0, n_valid.astype(F32)))
    found_f = jnp.where(c_lo == kf, active_f, 0.0)
    v_found = reduce_tiles(lambda t, kt: jnp.where(t >= lo, t, jnp.inf), jnp.minimum, jnp.min, jnp.inf)

    def walk_down():
        w0 = row_max(lambda t, kt: t <= hi)
        c0 = count(lambda t, kt: t >= w0)

        def walk_cond(state):
            w, c = state
            return any_set(jnp.where(c < kf, active_f, 0.0))

        def walk_body(state):
            w, c = state
            w2 = row_max(lambda t, kt: t < w)
            c2 = count(lambda t, kt: t >= w2)
            upd = c < kf
            return jnp.where(upd, w2, w), jnp.where(upd, c2, c)

        return lax.while_loop(walk_cond, walk_body,
                              (jnp.where(found_f > 0.5, v_found, w0),
                               jnp.where(found_f > 0.5, kf, c0)))

    w, c_ge = lax.cond(any_set(active_f - found_f), walk_down,
                       lambda: (v_found, jnp.full(vec_shape, kf, F32)))
    thr = jnp.where(active_f > 0.5, w, -jnp.inf)
    tied_f = jnp.where(c_ge > kf, active_f, 0.0)

    def write_bias(sel_fn):
        def body(kt, carry):
            kidx = key_index(kt)
            sel = sel_fn(sc_ref[kt], kidx) & (kidx < n_valid)
            bias = jnp.where(sel, 0.0, NEG_BIG)
            bias_ref[kt] = bias.T if keys_on_sublanes else bias
            return carry
        lax.fori_loop(0, nkt, body, 0)

    def no_ties():
        write_bias(lambda t, kidx: t >= thr)

    def with_ties():
        need = kf - count(lambda t, kt: t > thr)
        n_steps = max(1, int(np.ceil(np.log2(sc_ref.shape[0] * tk))) + 1)

        def step(_, lh):
            lo_i, hi_i = lh
            mid = lax.shift_right_arithmetic(lo_i + hi_i, 1)
            ge = count(lambda t, kt: (t == thr) & (key_index(kt) <= mid)) >= need
            return jnp.where(ge, lo_i, mid), jnp.where(ge, mid, hi_i)

        lo_i = jnp.full(vec_shape, -1, jnp.int32)
        hi_i = jnp.full(vec_shape, sc_ref.shape[0] * tk - 1, jnp.int32)
        _, last_tie = lax.fori_loop(0, n_steps, step, (lo_i, hi_i))
        write_bias(lambda t, kidx: (t > thr) | ((t == thr)
                                                & ((kidx <= last_tie) | (tied_f < 0.5))))

    any_tie = jnp.max(tied_f) > 0.0
    lax.cond(any_tie, with_ties, no_ties)

    rep = ATT_HEADS // KV_HEADS
    qgs = [jnp.concatenate(
        [q_ref[0, :, (g * rep + r) * HEAD_DIM:(g * rep + r + 1) * HEAD_DIM] for r in range(rep)],
        axis=0) for g in range(KV_HEADS)]
    acc_ref[...] = jnp.zeros(acc_ref.shape, F32)
    bufs = ((s0_ref, mt0_ref), (s1_ref, mt1_ref))

    def qk_tile(kt, buf):
        s_ref, mt_ref = bufs[buf]
        row0 = pl.multiple_of(kt * tk, tk)
        bias = bias_ref[kt]
        for g in range(KV_HEADS):
            kg = k_src[pl.ds(row0, tk), g * HEAD_DIM:(g + 1) * HEAD_DIM]
            s = lax.dot_general(qgs[g], kg, nt_dims, preferred_element_type=F32)
            s = (s.reshape(rep, qb, tk) + bias[None]).reshape(rep * qb, tk)
            s_ref[g] = s
            mt_ref[g] = jnp.broadcast_to(jnp.max(s, axis=-1, keepdims=True), (rep * qb, LANES))

    def lane_tile(x, width):
        return jnp.concatenate([x] * (width // LANES), axis=1)

    def att_tile(kt, buf, prefetch):
        if prefetch:
            qk_tile(kt + 1, 1 - buf)
        s_ref, mt_ref = bufs[buf]
        row0 = pl.multiple_of(kt * tk, tk)
        for g in range(KV_HEADS):
            vg = v_src[pl.ds(row0, tk), 2 * g * HEAD_DIM:(2 * g + 2) * HEAD_DIM]
            m_old = m_ref[g]
            m_new = jnp.maximum(m_old, mt_ref[g])
            m_ref[g] = m_new
            alpha = jnp.exp2(m_old - m_new)
            p = jnp.exp2(s_ref[g] - lane_tile(m_new, tk)).astype(BF16)
            acc_ref[g] = (lane_tile(alpha, 2 * HEAD_DIM) * acc_ref[g]
                          + jnp.dot(p, vg, preferred_element_type=F32))

    def tile_pair(pi, carry):
        att_tile(2 * pi, 0, True)
        att_tile(2 * pi + 1, 1, True)
        return carry

    def last_two():
        att_tile(nkt - 2, 0, True)
        att_tile(nkt - 1, 1, False)

    def last_one():
        att_tile(nkt - 1, 0, False)

    m_ref[...] = jnp.full(m_ref.shape, NEG_BIG, F32)
    qk_tile(0, 0)
    lax.fori_loop(0, (nkt - 1) // 2, tile_pair, 0)
    lax.cond((nkt & 1) == 0, last_two, last_one)
    for g in range(KV_HEADS):
        acc = acc_ref[g]
        o = acc[:, :HEAD_DIM] / acc[:, HEAD_DIM:]
        for r in range(rep):
            h = g * rep + r
            o_ref[0, :, h * HEAD_DIM:(h + 1) * HEAD_DIM] = o[r * qb:(r + 1) * qb].astype(BF16)


def _dsa_attention(q, qi, sm, k_new, v_new, ki_new, cache, *, qb, tk):
    b, t, _ = q.shape
    past = 0 if cache is None else cache[2].shape[1]
    n_keys = past + t
    nk_pad = -(-n_keys // tk) * tk
    topk = min(TOPK_MAX, n_keys // 4)
    keys_on_sublanes = qb == LANES
    rep = ATT_HEADS // KV_HEADS
    tokq = lambda w: pl.BlockSpec((1, qb, w), lambda i, j: (i, j, 0))
    keys = lambda w: pl.BlockSpec((1, t, w), lambda i, j: (i, 0, 0))
    tile_shape = (tk, qb) if keys_on_sublanes else (qb, tk)
    cache_specs, cache_scratch = [], []
    if cache is not None:
        kv = pl.BlockSpec((1, past * KV_HEADS, HEAD_DIM), lambda i, j: (i, 0, 0))
        cache_specs = [kv, kv, pl.BlockSpec((1, past, IDX_DIM), lambda i, j: (i, 0, 0))]
        cache_scratch = [pltpu.VMEM((nk_pad, w), BF16) for w in (KV_WIDTH, VEXT_WIDTH, IDX_DIM)]
    return pl.pallas_call(
        functools.partial(_dsa_kernel, qb=qb, tk=tk, n_keys=n_keys, past=past, topk=topk,
                          n_bisect=20, keys_on_sublanes=keys_on_sublanes),
        grid=(b, t // qb),
        in_specs=[tokq(ATT_WIDTH), tokq(QI_WIDTH), tokq(LANES), keys(KV_WIDTH), keys(VEXT_WIDTH),
                  keys(IDX_DIM)] + cache_specs,
        out_specs=tokq(ATT_WIDTH),
        out_shape=jax.ShapeDtypeStruct((b, t, ATT_WIDTH), BF16),
        scratch_shapes=[pltpu.VMEM((nk_pad // tk,) + tile_shape, F32),
                        pltpu.VMEM((nk_pad // tk, qb, tk), F32),
                        pltpu.VMEM((KV_HEADS, rep * qb, 2 * HEAD_DIM), F32),
                        pltpu.VMEM((IDX_HEADS * qb, IDX_DIM), BF16),
                        pltpu.VMEM((KV_HEADS, rep * qb, tk), F32),
                        pltpu.VMEM((KV_HEADS, rep * qb, tk), F32),
                        pltpu.VMEM((KV_HEADS, rep * qb, LANES), F32),
                        pltpu.VMEM((KV_HEADS, rep * qb, LANES), F32),
                        pltpu.VMEM((KV_HEADS, rep * qb, LANES), F32)] + cache_scratch,
        compiler_params=_params(2),
        name="dsa_attention",
    )(q, qi, sm, k_new, v_new, ki_new, *(cache or ()))


def _ssd_kernel(pxbc_ref, z_ref, sm_ref, st_ref, cs_ref, cw_ref, cb_ref, alog_ref, dexp_ref, nw_ref,
                e_ref, y_ref, stout_ref, cnew_ref, ht_ref, yi_ref, cbuf_ref, xbc_ref):
    c = pl.program_id(1)
    nc = pl.num_programs(1)
    L = pxbc_ref.shape[1]
    hp = SSD_INNER // SSD_GROUPS

    @pl.when(c == 0)
    def _():
        ht_ref[...] = st_ref[0].reshape(SSD_INNER, SSD_STATE).T

    raw = _lane_aligned(pxbc_ref[0, :, :OFF_DT - OFF_XBC + LANES], SM_DT, SSD_CONV_DIM)
    xbc_ref[...] = _silu(_causal_conv(raw, cs_ref, cw_ref, cb_ref, cbuf_ref, cnew_ref, c == 0,
                                      c == nc - 1))
    xs = xbc_ref[:, 0:SSD_INNER]
    sm = sm_ref[0]
    lane = lax.broadcasted_iota(jnp.int32, (1, LANES), 1)
    is_dt = (lane >= SM_DT) & (lane < SM_DT + SSD_HEADS)
    a_neg = jnp.where(is_dt, -jnp.exp(alog_ref[...]), 0.0)
    dt = jnp.where(is_dt, sm, 0.0)
    rows = lax.broadcasted_iota(jnp.int32, (L, L), 0)
    cols = lax.broadcasted_iota(jnp.int32, (L, L), 1)
    tri = cols <= rows
    e = e_ref[...]
    cum = jnp.dot(tri.astype(F32), dt * a_neg, precision=HIGHEST, preferred_element_type=F32)
    eye = (lax.broadcasted_iota(jnp.int32, (LANES, LANES), 0)
           == lax.broadcasted_iota(jnp.int32, (LANES, LANES), 1)).astype(F32)
    cum_t = lax.dot_general(eye, cum, (((1,), (1,)), ((), ())), precision=HIGHEST,
                            preferred_element_type=F32)
    both = jnp.concatenate([cum, dt], axis=0)
    hi = both.astype(BF16)
    rest = both - hi.astype(F32)
    mid = rest.astype(BF16)
    low = (rest - mid.astype(F32)).astype(BF16)
    both_e = (jnp.dot(hi, e, preferred_element_type=F32) + jnp.dot(mid, e, preferred_element_type=F32)
              + jnp.dot(low, e, preferred_element_type=F32))
    cum_e, dt_e = both_e[:L], both_e[L:]
    last = cum_e[L - 1:L, :]
    xdt = (xs * dt_e).astype(BF16)
    xw = (xs * (jnp.exp(last - cum_e) * dt_e)).astype(BF16)
    ht_old = ht_ref[...]

    for g in range(SSD_GROUPS):
        bg = xbc_ref[:, SSD_INNER + g * SSD_STATE:SSD_INNER + (g + 1) * SSD_STATE]
        cg = xbc_ref[:, SSD_INNER + (SSD_GROUPS + g) * SSD_STATE:
                     SSD_INNER + (SSD_GROUPS + g + 1) * SSD_STATE]
        bgb, cgb = bg.astype(BF16), cg.astype(BF16)
        cb = lax.dot_general(cgb, bgb, (((1,), (1,)), ((), ())), preferred_element_type=F32)
        for hh in range(SSD_HEADS // SSD_GROUPS):
            h = g * (SSD_HEADS // SSD_GROUPS) + hh
            seg = cum[:, SM_DT + h:SM_DT + h + 1] - cum_t[SM_DT + h:SM_DT + h + 1, :]
            decay = jnp.exp(jnp.where(tri, seg, -jnp.inf))
            wts = (cb * decay).astype(BF16)
            yi_ref[:, h * SSD_HEADDIM:(h + 1) * SSD_HEADDIM] = jnp.dot(
                wts, xdt[:, h * SSD_HEADDIM:(h + 1) * SSD_HEADDIM], preferred_element_type=F32)
        ht_g = ht_old[:, g * hp:(g + 1) * hp]
        y_inter = jnp.dot(cgb, ht_g.astype(BF16), preferred_element_type=F32)
        yi_ref[:, g * hp:(g + 1) * hp] = (yi_ref[:, g * hp:(g + 1) * hp]
                                          + y_inter * jnp.exp(cum_e[:, g * hp:(g + 1) * hp]))
        upd = jnp.dot(bg.T.astype(BF16), xw[:, g * hp:(g + 1) * hp],
                      preferred_element_type=F32)
        ht_ref[:, g * hp:(g + 1) * hp] = ht_g * jnp.exp(last[:, g * hp:(g + 1) * hp]) + upd

    y = yi_ref[...] + dexp_ref[...] * xs
    y = y * _silu(_lane_aligned(z_ref[0], SM_DT, SSD_INNER))
    for g in range(SSD_GROUPS):
        yg = y[:, g * hp:(g + 1) * hp]
        ms = jnp.mean(yg * yg, axis=-1, keepdims=True)
        y_ref[0, :, g * hp:(g + 1) * hp] = (yg * lax.rsqrt(ms + EPS)
                                            * nw_ref[:, g * hp:(g + 1) * hp]).astype(BF16)

    @pl.when(c == nc - 1)
    def _():
        stout_ref[0] = ht_ref[...].T.reshape(SSD_HEADS, SSD_HEADDIM, SSD_STATE)


def _ssd(proj, sm, state, conv_state, conv_w, conv_b, a_log_pad, d_exp, norm_w, e_mat, *, chunk):
    b, t, _ = proj.shape
    tok = lambda w, col=0: pl.BlockSpec((1, chunk, w), lambda i, j: (i, j, col))
    full2 = lambda a: pl.BlockSpec(a.shape, lambda i, j: (0, 0))
    st = pl.BlockSpec((1, SSD_HEADS, SSD_HEADDIM, SSD_STATE), lambda i, j: (i, 0, 0, 0))
    cst = pl.BlockSpec((1, CONV_W - 1, SSD_CONV_DIM), lambda i, j: (i, 0, 0))
    nw = norm_w.reshape(1, SSD_INNER)
    cb = conv_b.reshape(1, SSD_CONV_DIM)
    return pl.pallas_call(
        _ssd_kernel,
        grid=(b, t // chunk),
        in_specs=[tok(XBC_BLOCK, (OFF_XBC - SM_DT) // XBC_BLOCK),
                  tok(Z_BLOCK, (OFF_Z - SM_DT) // Z_BLOCK), tok(LANES), st, cst, full2(conv_w),
                  full2(cb), full2(a_log_pad), full2(d_exp), full2(nw), full2(e_mat)],
        out_specs=[tok(SSD_INNER), st, cst],
        out_shape=[jax.ShapeDtypeStruct((b, t, SSD_INNER), BF16),
                   jax.ShapeDtypeStruct(state.shape, F32),
                   jax.ShapeDtypeStruct((b, CONV_W - 1, SSD_CONV_DIM), F32)],
        scratch_shapes=[pltpu.VMEM((SSD_STATE, SSD_INNER), F32),
                        pltpu.VMEM((chunk, SSD_INNER), F32),
                        pltpu.VMEM((chunk + SUBLANES, SSD_CONV_DIM), F32),
                        pltpu.VMEM((chunk, SSD_CONV_DIM), F32)],
        compiler_params=_params(2),
        name="ssd_scan",
    )(proj, proj, sm, state, conv_state, conv_w, cb, a_log_pad, d_exp, nw, e_mat)


def _lru_kernel(gate_ref, xin_ref, cw_ref, cb_ref, wa_ref, wx_ref, ba_ref, bx_ref, lam_ref,
                h0_ref, cs_ref, y_ref, hout_ref, cnew_ref, cbuf_ref, a_ref, u_ref, hp_ref):
    t = pl.program_id(1)
    nt = pl.num_programs(1)
    tt = xin_ref.shape[1]
    xc = _causal_conv(xin_ref[0], cs_ref, cw_ref, cb_ref, cbuf_ref, cnew_ref, t == 0, t == nt - 1)

    @pl.when(t == 0)
    def _():
        hp_ref[...] = h0_ref[0]

    for kb in range(LRU_BLOCKS):
        sl = slice(kb * LRU_BLOCK_DIM, (kb + 1) * LRU_BLOCK_DIM)
        xb = xc[:, sl].astype(BF16)
        a_ref[:, sl] = jnp.dot(xb, wa_ref[kb].astype(BF16), preferred_element_type=F32)
        u_ref[:, sl] = jnp.dot(xb, wx_ref[kb].astype(BF16), preferred_element_type=F32)
    r = jax.nn.sigmoid(a_ref[...] + ba_ref[...])
    i = jax.nn.sigmoid(u_ref[...] + bx_ref[...])
    log_a = -LRU_C * r * _softplus(-lam_ref[...])
    a = jnp.exp(log_a)
    u = jnp.sqrt(-jnp.tanh(log_a) * (1.0 + a * a)) * (i * xc)

    width = a.shape[1]
    a = a.reshape(tt // SUBLANES, SUBLANES, width)
    u = u.reshape(tt // SUBLANES, SUBLANES, width)
    sub = lax.broadcasted_iota(jnp.int32, (1, SUBLANES, width), 1)
    for s in (1, 2, 4):
        m = sub >= s
        a_sh = pltpu.roll(a, s, 1)
        u_sh = pltpu.roll(u, s, 1)
        u = jnp.where(m, a * u_sh + u, u)
        a = jnp.where(m, a * a_sh, a)
    a_ref[...] = a.reshape(tt, width)
    u_ref[...] = u.reshape(tt, width)

    def group(gi, hprev):
        r0 = pl.multiple_of(gi * SUBLANES, SUBLANES)
        hs = a_ref[pl.ds(r0, SUBLANES), :] * hprev + u_ref[pl.ds(r0, SUBLANES), :]
        u_ref[pl.ds(r0, SUBLANES), :] = hs
        return hs[SUBLANES - 1:SUBLANES, :]

    h_last = lax.fori_loop(0, tt // SUBLANES, group, hp_ref[...])
    hp_ref[...] = h_last
    gate = gate_ref[0]
    gelu = 0.5 * gate * (1.0 + jnp.tanh(np.sqrt(2.0 / np.pi) * (gate + 0.044715 * (gate * gate * gate))))
    y_ref[0] = (u_ref[...] * gelu).astype(BF16)

    @pl.when(t == nt - 1)
    def _():
        hout_ref[0] = h_last


def _lru(proj, conv_w, conv_b, w_a, b_a, w_x, b_x, lam, h0, conv_state, *, tt):
    b, t, _ = proj.shape
    w = LRU_WIDTH
    tok = lambda col: pl.BlockSpec((1, tt, w), lambda i, j: (i, j, col))
    full = lambda a: pl.BlockSpec(a.shape, lambda i, j: (0,) * a.ndim)
    row = lambda v: v.reshape(1, w)
    state = pl.BlockSpec((1, CONV_W - 1, w), lambda i, j: (i, 0, 0))
    hspec = pl.BlockSpec((1, 1, w), lambda i, j: (i, 0, 0))
    args = (proj, proj, conv_w, row(conv_b), w_a, w_x, row(b_a), row(b_x), row(lam), h0, conv_state)
    return pl.pallas_call(
        _lru_kernel,
        grid=(b, t // tt),
        in_specs=[tok(0), tok(1)] + [full(a) for a in args[2:9]] + [hspec, state],
        out_specs=[pl.BlockSpec((1, tt, w), lambda i, j: (i, j, 0)), hspec, state],
        out_shape=[jax.ShapeDtypeStruct((b, t, w), BF16),
                   jax.ShapeDtypeStruct((b, 1, w), F32),
                   jax.ShapeDtypeStruct((b, CONV_W - 1, w), F32)],
        scratch_shapes=[pltpu.VMEM((tt + SUBLANES, w), F32), pltpu.VMEM((tt, w), F32),
                        pltpu.VMEM((tt, w), F32), pltpu.VMEM((1, w), F32)],
        compiler_params=_params(2),
        name="rg_lru",
    )(*args)


def _rope_tables(pos, dim):
    half = dim // 2
    inv = np.power(ROPE_THETA, -np.arange(half, dtype=np.float64) / half)
    ang = pos.astype(np.float64)[:, None] * inv[None, :]
    cos = np.concatenate([np.cos(ang), np.cos(ang)], axis=-1)
    sin = np.concatenate([-np.sin(ang), np.sin(ang)], axis=-1)
    reps = LANES // dim
    return (jnp.asarray(np.tile(cos, (1, reps)), F32), jnp.asarray(np.tile(sin, (1, reps)), F32))


def _head_expansion():
    e = np.zeros((LANES, SSD_INNER), np.float32)
    for h in range(SSD_HEADS):
        e[SM_DT + h, h * SSD_HEADDIM:(h + 1) * SSD_HEADDIM] = 1.0
    return jnp.asarray(e, BF16)


def _pad_lanes(v, offset):
    out = jnp.zeros((1, LANES), F32)
    return out.at[0, offset:offset + v.shape[0]].set(v)


def _w_in_ab_tail(w_t):
    whole = AB_WIDTH // AB_TILE_N * AB_TILE_N
    return jnp.pad(w_t[whole:], ((0, AB_PAD_N - AB_WIDTH), (0, 0)))


def _layer0_mixers(proj, st, p, cfg):
    b, t, _ = proj.shape
    past = 0 if st["past_k"] is None else st["past_k"].shape[1]
    pos = np.arange(past, past + t)
    tabs = _rope_tables(pos, HEAD_DIM) + _rope_tables(pos, IDX_DIM)
    q, qi, k, v, ki, sm, kb, vbe, kib = _post0(proj, tabs, p["q_norm_w"][0], p["k_norm_w"][0],
                                               p["dt_bias_pad"], tt=cfg["post_tt"])
    cache = None if not past else (st["past_k"].reshape(b, past * KV_HEADS, HEAD_DIM),
                                   st["past_v"].reshape(b, past * KV_HEADS, HEAD_DIM), st["past_ki"])
    att = _dsa_attention(q, qi, sm, kb, vbe, kib, cache, qb=cfg["qb"], tk=cfg["tk"])
    y_ssd, ssm_new, ssm_conv_new = _ssd(
        proj, sm, st["ssm_h"], st["ssm_conv"], p["ssd_conv_w"][0], p["ssd_conv_b"][0], p["a_log_pad"],
        p["d_exp"], p["ssd_norm_w"][0], p["e_mat"], chunk=cfg["ssd_chunk"])
    kv_shape = (1, b, t, KV_HEADS, HEAD_DIM)
    return att, y_ssd, (k.reshape(kv_shape), v.reshape(kv_shape), ki[None], ssm_new[None],
                        ssm_conv_new[None])


def _layer1_mixer(proj, st, p, cfg):
    b = proj.shape[0]
    y_lru, lru_new, lru_conv_new = _lru(
        proj, p["lru_conv_w"][0], p["lru_conv_b"][0], p["lru_w_a"][0], p["lru_b_a"][0],
        p["lru_w_x"][0], p["lru_b_x"][0], p["lru_lambda"][0], st["lru_h"].reshape(b, 1, LRU_WIDTH),
        st["lru_conv"], tt=cfg["lru_tt"])
    return y_lru, (lru_new.reshape(1, b, LRU_WIDTH), lru_conv_new[None])


def _forward(xp, xs, mods_p, mods_s, st_p, st_s, p, cfg_p, cfg_s):
    bb, tt = cfg_p["bb"], cfg_p["tt"]

    def ffn(xp, xs, l):
        (_, _, _, sh_p, sc_p, g_p), (_, _, _, sh_s, sc_s, g_s) = mods_p[l], mods_s[l]
        gu_p, gu_s = _norm_mod_matmul(
            xp, p["norm_ffn_w"][l], sc_p, sh_p, [p["ffn_w_gate"], p["ffn_w_up"]], l, bb=bb,
            tt=cfg_p["tt_ffn"], tn=512, out_dtype=BF16, name="ffn_gate_up", ride=(xs, sc_s, sh_s))
        return _matmul_residual([gu_p], p["ffn_w_down"], l, xp, g_p, bb=bb, tt=cfg_p["tt_ffn"],
                                tn=256, name="ffn_down", ride=([gu_s], xs, g_s))

    (sh_p, sc_p, g_p, *_), (sh_s, sc_s, g_s, *_) = mods_p[0], mods_s[0]
    proj_p, proj_s = _norm_mod_matmul(
        xp, p["norm_mix_w"][0], sc_p, sh_p, [p["w_in_ab_t"]], 0, bb=bb, tt=cfg_p["tt_ab"],
        tn=AB_TILE_N, out_dtype=F32, name="in_proj_ab", w_tail=p["w_in_ab_tail"], w_rows_out=True,
        ride=(xs, sc_s, sh_s))
    att_p, y_p, outs0_p = _layer0_mixers(proj_p, st_p, p, cfg_p)
    att_s, y_s, outs0_s = _layer0_mixers(proj_s, st_s, p, cfg_s)
    xp, xs = _matmul_residual([att_p, y_p], p["w_out_ab"], 0, xp, g_p, bb=bb, tt=tt, tn=512,
                              name="out_proj_ab", ride=([att_s, y_s], xs, g_s))
    xp, xs = ffn(xp, xs, 0)

    (sh_p, sc_p, g_p, *_), (sh_s, sc_s, g_s, *_) = mods_p[1], mods_s[1]
    proj_p, proj_s = _norm_mod_matmul(
        xp, p["norm_mix_w"][1], sc_p, sh_p, [p["w_in_c"]], 0, bb=bb, tt=cfg_p["tt_ab"], tn=512,
        out_dtype=F32, name="in_proj_c", ride=(xs, sc_s, sh_s))
    y_p, outs1_p = _layer1_mixer(proj_p, st_p, p, cfg_p)
    y_s, outs1_s = _layer1_mixer(proj_s, st_s, p, cfg_s)
    xp, xs = _matmul_residual([y_p], p["w_out_c"], 0, xp, g_p, bb=bb, tt=tt, tn=512,
                              name="out_proj_c", ride=([y_s], xs, g_s))
    xp, xs = ffn(xp, xs, 1)
    return (xp,) + outs0_p + outs1_p, (xs,) + outs0_s + outs1_s


PROMPT_CFG = dict(bb=1, tt=2048, tt_ab=1024, tt_ffn=1024, post_tt=1024, qb=128, tk=512,
                  ssd_chunk=128, lru_tt=512)
SAMPLE_CFG = dict(post_tt=32, qb=32, tk=384, ssd_chunk=32, lru_tt=32)


def kernel(x_prompt, x_sample, cache_attn_k, cache_attn_v, cache_idx_k, state_ssm, state_ssm_conv,
           state_lru, state_lru_conv, c_prompt, c_sample, ada_w, ada_b, norm_mix_w, norm_ffn_w,
           w_in_ab, q_norm_w, k_norm_w, ssd_conv_w, ssd_conv_b, ssd_dt_bias, ssd_a_log, ssd_d,
           ssd_norm_w, w_out_ab, w_in_c, lru_conv_w, lru_conv_b, lru_w_a, lru_b_a, lru_w_x, lru_b_x,
           lru_lambda, w_out_c, ffn_w_gate, ffn_w_up, ffn_w_down):
    bp, bs = x_prompt.shape[0], x_sample.shape[0]
    p = dict(norm_mix_w=norm_mix_w, norm_ffn_w=norm_ffn_w, q_norm_w=q_norm_w, k_norm_w=k_norm_w,
             ssd_conv_w=ssd_conv_w, ssd_conv_b=ssd_conv_b, ssd_norm_w=ssd_norm_w, w_out_ab=w_out_ab,
             w_in_c=w_in_c, lru_conv_w=lru_conv_w, lru_conv_b=lru_conv_b, lru_w_a=lru_w_a,
             lru_b_a=lru_b_a, lru_w_x=lru_w_x, lru_b_x=lru_b_x, lru_lambda=lru_lambda,
             w_out_c=w_out_c, ffn_w_gate=ffn_w_gate, ffn_w_up=ffn_w_up, ffn_w_down=ffn_w_down)
    p["w_in_ab_t"] = jnp.swapaxes(w_in_ab, 1, 2)
    p["w_in_ab_tail"] = _w_in_ab_tail(p["w_in_ab_t"][0])
    p["dt_bias_pad"] = _pad_lanes(ssd_dt_bias[0], SM_DT)
    p["a_log_pad"] = _pad_lanes(ssd_a_log[0], SM_DT)
    p["d_exp"] = jnp.repeat(ssd_d[0], SSD_HEADDIM).reshape(1, SSD_INNER)
    p["e_mat"] = _head_expansion()

    m_rows = 32
    c_all = jnp.concatenate([c_prompt, c_sample, jnp.zeros((m_rows - bp - bs, D_MODEL), F32)], axis=0)
    mod = _modulation(c_all, ada_w, ada_b)

    def group_mods(r0, nb):
        return [[mod[l, r0:r0 + nb, i * D_MODEL:(i + 1) * D_MODEL].reshape(nb, 1, D_MODEL)
                 for i in range(6)] for l in range(mod.shape[0])]

    zeros = lambda *s: jnp.zeros(s, F32)
    st_p = dict(past_k=None, past_v=None, past_ki=None,
                ssm_h=zeros(bp, SSD_HEADS, SSD_HEADDIM, SSD_STATE),
                ssm_conv=zeros(bp, CONV_W - 1, SSD_CONV_DIM), lru_h=zeros(bp, LRU_WIDTH),
                lru_conv=zeros(bp, CONV_W - 1, LRU_WIDTH))
    st_s = dict(past_k=cache_attn_k[0], past_v=cache_attn_v[0], past_ki=cache_idx_k[0],
                ssm_h=state_ssm[0], ssm_conv=state_ssm_conv[0], lru_h=state_lru[0],
                lru_conv=state_lru_conv[0])
    out_p, out_s = _forward(x_prompt, x_sample, group_mods(0, bp), group_mods(bp, bs), st_p, st_s, p,
                            PROMPT_CFG, SAMPLE_CFG)
    return (out_p[0], out_s[0]) + out_p[1:] + out_s[1:]
```

```python
import functools

import numpy as np
import jax
import jax.numpy as jnp
from jax import lax
from jax.experimental import pallas as pl
from jax.experimental.pallas import tpu as pltpu

F32 = jnp.float32
BF16 = jnp.bfloat16
HIGHEST = lax.Precision.HIGHEST

D_MODEL = 2048
CHUNK = 64
CHUNK_SHIFT = 6
ATT_HEADS = 8
KV_HEADS = 2
HEAD_DIM = 128
IDX_HEADS = 16
IDX_DIM = 64
TOPK_MAX = 256
ROPE_THETA = 10000.0
SSD_HEADS = 16
SSD_HEADDIM = 64
SSD_GROUPS = 2
SSD_STATE = 128
SSD_INNER = SSD_HEADS * SSD_HEADDIM
SSD_CONV_DIM = SSD_INNER + 2 * SSD_GROUPS * SSD_STATE
CONV_W = 4
LRU_WIDTH = D_MODEL
LRU_BLOCKS = 16
LRU_BLOCK_DIM = LRU_WIDTH // LRU_BLOCKS
LRU_C = 8.0
EPS = 1e-6
ATT_WIDTH = ATT_HEADS * HEAD_DIM
KV_WIDTH = KV_HEADS * HEAD_DIM
VEXT_WIDTH = 2 * KV_WIDTH
QI_WIDTH = IDX_HEADS * IDX_DIM
Q_SCALE = HEAD_DIM ** -0.5 * float(np.log2(np.e))

LANES = 128
SUBLANES = 8
VMEM_LIMIT_BYTES = 62 * 1024 * 1024
SINGLE_BUFFER_ROWS = 2048
NORM_CHUNKS = 4

OFF_Q = 0
OFF_K = OFF_Q + ATT_WIDTH
OFF_V = OFF_K + KV_WIDTH
OFF_QI = OFF_V + KV_WIDTH
OFF_SMALL = OFF_QI + QI_WIDTH
SM_WI = IDX_DIM
SM_DT = IDX_DIM + IDX_HEADS
OFF_Z = OFF_SMALL + SM_DT
OFF_XBC = OFF_Z + SSD_INNER
OFF_DT = OFF_XBC + SSD_CONV_DIM
AB_WIDTH = OFF_DT + SSD_HEADS
Z_BLOCK = 1280
XBC_BLOCK = 1792
AB_TILE_N = 512
AB_PAD_N = 5632
assert OFF_Z % LANES == SM_DT and OFF_XBC % LANES == SM_DT and OFF_DT % LANES == SM_DT
assert (OFF_Z - SM_DT) % Z_BLOCK == 0 and OFF_Z - SM_DT + Z_BLOCK >= OFF_Z + SSD_INNER
assert (OFF_XBC - SM_DT) % XBC_BLOCK == 0 and XBC_BLOCK >= OFF_DT - OFF_XBC + LANES
assert OFF_XBC - SM_DT + XBC_BLOCK <= AB_PAD_N

NEG_BIG = -1e30


def _params(n_axes):
    return pltpu.CompilerParams(dimension_semantics=("arbitrary",) * n_axes,
                                vmem_limit_bytes=VMEM_LIMIT_BYTES)


def _silu(x):
    return x * jax.nn.sigmoid(x)


def _softplus(x):
    return jnp.maximum(x, 0.0) + jnp.log1p(jnp.exp(-jnp.abs(x)))


def _mod_kernel(c_ref, w_ref, b_ref, o_ref):
    a = _silu(c_ref[...]).astype(BF16)
    o_ref[0] = jnp.dot(a, w_ref[0].astype(BF16), preferred_element_type=F32) + b_ref[0]


def _modulation(c_all, ada_w, ada_b):
    depth, d, n = ada_w.shape
    m = c_all.shape[0]
    tn = 1024
    return pl.pallas_call(
        _mod_kernel,
        grid=(depth, n // tn),
        in_specs=[pl.BlockSpec((m, d), lambda l, j: (0, 0)),
                  pl.BlockSpec((1, d, tn), lambda l, j: (l, 0, j)),
                  pl.BlockSpec((1, 1, tn), lambda l, j: (l, 0, j))],
        out_specs=pl.BlockSpec((1, m, tn), lambda l, j: (l, 0, j)),
        out_shape=jax.ShapeDtypeStruct((depth, m, n), F32),
        compiler_params=_params(2),
        name="adaln_mod",
    )(c_all, ada_w, ada_b.reshape(depth, 1, n))


def _first_row_tile():
    return (pl.program_id(0) == 0) & (pl.program_id(1) == 0)


def _nmm_kernel(*refs, swiglu, tail, w_rows_out, ride):
    refs = list(refs)
    x_ref, nw_ref, sc_ref, sh_ref = refs[:4]
    del refs[:4]
    if ride:
        xr_ref, scr_ref, shr_ref = refs[:3]
        del refs[:3]
    n_w = 2 if swiglu or tail else 1
    w_refs = refs[:n_w]
    del refs[:n_w]
    if ride:
        o_ref, or_ref, h_ref, hr_ref = refs
    else:
        o_ref, h_ref = refs
    k = pl.program_id(2)

    def matmul(h, w):
        dims = (((1,), (1,)), ((), ())) if w_rows_out else (((1,), (0,)), ((), ()))
        return lax.dot_general(h, w, dims, preferred_element_type=F32)

    def finish(accs, out_ref):
        o = _silu(accs[0]) * accs[1] if swiglu else accs[0]
        out_ref[...] = o.reshape(out_ref.shape).astype(out_ref.dtype)

    def project(ws, rows_ref, out_ref):
        h = rows_ref[...]
        finish([matmul(h, w) for w in ws], out_ref)

    def norm_project(ws, src_ref, scale_ref, shift_ref, rows_ref, out_ref):
        x = src_ref[...]
        d = x.shape[-1]
        rs = lax.rsqrt(jnp.mean(x * x, axis=-1, keepdims=True) + EPS)
        kc = d // NORM_CHUNKS
        accs = None
        for c in range(NORM_CHUNKS):
            cols = slice(c * kc, (c + 1) * kc)
            hc = (x[..., cols] * rs * nw_ref[..., cols] * (1.0 + scale_ref[..., cols])
                  + shift_ref[..., cols])
            hc = hc.reshape(rows_ref.shape[0], kc).astype(BF16)
            rows_ref[:, cols] = hc
            part = [matmul(hc, w[:, cols] if w_rows_out else w[cols, :]) for w in ws]
            accs = part if accs is None else [a + b for a, b in zip(accs, part)]
        finish(accs, out_ref)

    def step(tile_refs, first):
        ws = [w[...].astype(BF16) for w in tile_refs]
        if first:
            norm_project(ws, x_ref, sc_ref, sh_ref, h_ref, o_ref)
            if ride:
                pl.when(_first_row_tile())(
                    lambda: norm_project(ws, xr_ref, scr_ref, shr_ref, hr_ref, or_ref))
        else:
            project(ws, h_ref, o_ref)
            if ride:
                pl.when(_first_row_tile())(lambda: project(ws, hr_ref, or_ref))

    last = pl.num_programs(2) - 1 if tail else pl.num_programs(2)
    pl.when(k == 0)(lambda: step(w_refs[:n_w - tail], True))
    pl.when((k > 0) & (k < last))(lambda: step(w_refs[:n_w - tail], False))
    if tail:
        pl.when(k == last)(lambda: step(w_refs[1:], False))


def _row_block_mode(rows):
    return dict(pipeline_mode=pl.Buffered(1)) if rows >= SINGLE_BUFFER_ROWS else {}


def _ride_col_tile(n_tiles):
    return lambda i, j, k: (0, 0, jnp.where((i == 0) & (j == 0), k, n_tiles - 1))


def _norm_mod_matmul(x, nw, sc, sh, ws, layer, *, bb, tt, tn, out_dtype, name, w_tail=None,
                     w_rows_out=False, ride=None):
    b, t, d = x.shape
    swiglu = len(ws) == 2
    n_main = ws[0].shape[1 if w_rows_out else 2] // tn
    n_tiles = n_main + (w_tail is not None)
    n = n_tiles * tn
    xmap = lambda i, j, k: (i, j, 0)
    mmap = lambda i, j, k: (i, 0, 0)
    const3 = lambda i, j, k: (0, 0, 0)
    tile = lambda k: jnp.minimum(k, n_main - 1)
    if w_rows_out:
        w_specs = [pl.BlockSpec((None, tn, d), lambda i, j, k: (layer, tile(k), 0))] * len(ws)
    else:
        w_specs = [pl.BlockSpec((None, d, tn), lambda i, j, k: (layer, 0, tile(k)))] * len(ws)
    operands = list(ws)
    if w_tail is not None:
        w_specs.append(pl.BlockSpec(w_tail.shape, lambda i, j, k: (0, 0)))
        operands.append(w_tail)
    in_specs = [pl.BlockSpec((bb, tt, d), xmap, **_row_block_mode(bb * tt)),
                pl.BlockSpec((1, 1, d), const3),
                pl.BlockSpec((bb, 1, d), mmap),
                pl.BlockSpec((bb, 1, d), mmap)]
    out_specs = [pl.BlockSpec((bb, tt, tn), lambda i, j, k: (i, j, k))]
    out_shape = [jax.ShapeDtypeStruct((b, t, n), out_dtype)]
    scratch = [pltpu.VMEM((bb * tt, d), BF16)]
    ride_ops = ()
    if ride is not None:
        ride_ops = ride
        rb, rt, _ = ride[0].shape
        in_specs += [pl.BlockSpec((rb, rt, d), const3, pipeline_mode=pl.Buffered(1)),
                     pl.BlockSpec((rb, 1, d), const3), pl.BlockSpec((rb, 1, d), const3)]
        out_specs.append(pl.BlockSpec((rb, rt, tn), _ride_col_tile(n_tiles)))
        out_shape.append(jax.ShapeDtypeStruct((rb, rt, n), out_dtype))
        scratch.append(pltpu.VMEM((rb * rt, d), BF16))
    outs = pl.pallas_call(
        functools.partial(_nmm_kernel, swiglu=swiglu, tail=w_tail is not None,
                          w_rows_out=w_rows_out, ride=ride is not None),
        grid=(b // bb, t // tt, n_tiles),
        in_specs=in_specs + w_specs,
        out_specs=out_specs,
        out_shape=out_shape,
        scratch_shapes=scratch,
        compiler_params=_params(3),
        name=name,
    )(x, nw.reshape(1, 1, d), sc, sh, *ride_ops, *operands)
    return outs if ride is not None else outs[0]


def _mmr_kernel(*refs, k_sizes, ride):
    refs = list(refs)
    n_a = len(k_sizes)
    groups = [refs[:n_a + 2]]
    del refs[:n_a + 2]
    if ride:
        groups.append(refs[:n_a + 2])
        del refs[:n_a + 2]
    w_ref = refs[0]
    out_refs = refs[1:]
    offs = np.cumsum((0,) + tuple(k_sizes))
    ws = [w_ref[offs[i]:offs[i + 1], :].astype(BF16) for i in range(n_a)]

    def apply(group, o_ref):
        *a_refs, x_ref, g_ref = group
        bb, tt, tn = o_ref.shape
        acc = jnp.zeros((bb * tt, tn), F32)
        for a_ref, w, ks in zip(a_refs, ws, k_sizes):
            acc = acc + jnp.dot(a_ref[...].reshape(bb * tt, ks), w, preferred_element_type=F32)
        o_ref[...] = x_ref[...] + g_ref[...] * acc.reshape(bb, tt, tn)

    apply(groups[0], out_refs[0])
    if ride:
        pl.when(_first_row_tile())(lambda: apply(groups[1], out_refs[1]))


def _matmul_residual(a_list, w, layer, x, g, *, bb, tt, tn, name, ride=None):
    b, t, d = x.shape
    k_sizes = tuple(a.shape[-1] for a in a_list)
    k_total = sum(k_sizes)
    n_tiles = d // tn
    in_specs = ([pl.BlockSpec((bb, tt, ks), lambda i, j, k: (i, j, 0), **_row_block_mode(bb * tt))
                 for ks in k_sizes]
                + [pl.BlockSpec((bb, tt, tn), lambda i, j, k: (i, j, k)),
                   pl.BlockSpec((bb, 1, tn), lambda i, j, k: (i, 0, k))])
    out_specs = [pl.BlockSpec((bb, tt, tn), lambda i, j, k: (i, j, k))]
    out_shape = [jax.ShapeDtypeStruct((b, t, d), F32)]
    ride_ops = ()
    if ride is not None:
        ra_list, rx, rg = ride
        rb, rt, _ = rx.shape
        ride_ops = (*ra_list, rx, rg)
        col = _ride_col_tile(n_tiles)
        in_specs += ([pl.BlockSpec((rb, rt, ks), lambda i, j, k: (0, 0, 0),
                                   pipeline_mode=pl.Buffered(1)) for ks in k_sizes]
                     + [pl.BlockSpec((rb, rt, tn), col), pl.BlockSpec((rb, 1, tn), col)])
        out_specs.append(pl.BlockSpec((rb, rt, tn), col))
        out_shape.append(jax.ShapeDtypeStruct(rx.shape, F32))
    outs = pl.pallas_call(
        functools.partial(_mmr_kernel, k_sizes=k_sizes, ride=ride is not None),
        grid=(b // bb, t // tt, n_tiles),
        in_specs=in_specs + [pl.BlockSpec((None, k_total, tn), lambda i, j, k: (layer, 0, k))],
        out_specs=out_specs,
        out_shape=out_shape,
        compiler_params=_params(3),
        name=name,
    )(*a_list, x, g, *ride_ops, w)
    return outs if ride is not None else outs[0]


def _swap_halves(x, half):
    w = x.shape[-1]
    if w > LANES:
        return jnp.concatenate([_swap_halves(x[..., c * LANES:(c + 1) * LANES], half)
                                for c in range(w // LANES)], axis=-1)
    if w == 2 * half:
        return pltpu.roll(x, half, x.ndim - 1)
    lane = lax.broadcasted_iota(jnp.int32, x.shape, x.ndim - 1)
    first = (lane & half) == 0
    return jnp.where(first, pltpu.roll(x, w - half, x.ndim - 1), pltpu.roll(x, half, x.ndim - 1))


def _lane_aligned(slab, offset, width):
    return pltpu.roll(slab, slab.shape[-1] - offset, slab.ndim - 1)[:, :width]


def _rope(x, cos, sin_signed, half):
    reps = x.shape[-1] // cos.shape[-1]
    if reps > 1:
        cos = jnp.concatenate([cos] * reps, axis=-1)
        sin_signed = jnp.concatenate([sin_signed] * reps, axis=-1)
    return x * cos + _swap_halves(x, half) * sin_signed


def _conv_carry_init(state_ref, cbuf_ref):
    cbuf_ref[0:SUBLANES, :] = jnp.zeros((SUBLANES, cbuf_ref.shape[1]), F32)
    cbuf_ref[SUBLANES - (CONV_W - 1):SUBLANES, :] = state_ref[0]


def _conv_state(tail):
    return tail[SUBLANES - (CONV_W - 1):, :]


def _causal_conv_tile(u, w_ref, b_ref, cbuf_ref):
    tt = u.shape[0]
    cbuf_ref[SUBLANES:SUBLANES + tt, :] = u
    full = cbuf_ref[...]
    out = b_ref[...]
    for j in range(CONV_W):
        shift = CONV_W - 1 - j
        rows = pltpu.roll(full, shift, 0) if shift else full
        out = out + rows[SUBLANES:SUBLANES + tt, :] * w_ref[j:j + 1, :]
    tail = cbuf_ref[tt:tt + SUBLANES, :]
    cbuf_ref[0:SUBLANES, :] = tail
    return out, tail


def _causal_conv(u, state_ref, w_ref, b_ref, cbuf_ref, new_ref, first, last):
    pl.when(first)(lambda: _conv_carry_init(state_ref, cbuf_ref))
    out, tail = _causal_conv_tile(u, w_ref, b_ref, cbuf_ref)

    @pl.when(last)
    def _():
        new_ref[0] = _conv_state(tail)

    return out


def _post0_kernel(p_ref, pdt_ref, cq_ref, sq_ref, ci_ref, si_ref, qn_ref, kn_ref, dtb_ref,
                  q_ref, qi_ref, k_ref, v_ref, ki_ref, sm_ref, kb_ref, vb_ref, kib_ref):
    cq, sq, ci, si = cq_ref[...], sq_ref[...], ci_ref[...], si_ref[...]

    def head_norm_rope(x, w):
        ms = jnp.mean(x * x, axis=-1, keepdims=True)
        return _rope(x * lax.rsqrt(ms + EPS) * w, cq, sq, HEAD_DIM // 2)

    for h in range(ATT_HEADS):
        xh = p_ref[0, :, OFF_Q + h * HEAD_DIM:OFF_Q + (h + 1) * HEAD_DIM]
        q_ref[0, :, h * HEAD_DIM:(h + 1) * HEAD_DIM] = (head_norm_rope(xh, qn_ref[...])
                                                         * Q_SCALE).astype(BF16)
    ones = jnp.ones((p_ref.shape[1], HEAD_DIM), BF16)
    for h in range(KV_HEADS):
        xh = p_ref[0, :, OFF_K + h * HEAD_DIM:OFF_K + (h + 1) * HEAD_DIM]
        kh = head_norm_rope(xh, kn_ref[...])
        vh = p_ref[0, :, OFF_V + h * HEAD_DIM:OFF_V + (h + 1) * HEAD_DIM]
        k_ref[0, pl.ds(h, xh.shape[0], stride=KV_HEADS), :] = kh
        v_ref[0, pl.ds(h, xh.shape[0], stride=KV_HEADS), :] = vh
        kb_ref[0, :, h * HEAD_DIM:(h + 1) * HEAD_DIM] = kh.astype(BF16)
        vb_ref[0, :, 2 * h * HEAD_DIM:(2 * h + 1) * HEAD_DIM] = vh.astype(BF16)
        vb_ref[0, :, (2 * h + 1) * HEAD_DIM:(2 * h + 2) * HEAD_DIM] = ones

    qi = p_ref[0, :, OFF_QI:OFF_QI + QI_WIDTH]
    qi_ref[0] = _rope(qi, ci, si, IDX_DIM // 2).astype(BF16)

    sm = p_ref[0, :, OFF_SMALL:OFF_SMALL + LANES]
    lane = lax.broadcasted_iota(jnp.int32, sm.shape, 1)
    ki_part = _rope(sm, ci, si, IDX_DIM // 2)
    wi_part = sm * (IDX_HEADS ** -0.5 * IDX_DIM ** -0.5)
    dt_part = _softplus(pdt_ref[0] + dtb_ref[...])
    sm_out = jnp.where(lane < SM_WI, ki_part,
                       jnp.where(lane < SM_DT, wi_part,
                                 jnp.where(lane < SM_DT + SSD_HEADS, dt_part, 0.0)))
    sm_ref[0] = sm_out
    ki_ref[0] = sm_out[:, :IDX_DIM]
    kib_ref[0] = sm_out[:, :IDX_DIM].astype(BF16)


def _post0(proj, tabs, q_norm_w, k_norm_w, dt_bias_pad, *, tt):
    b, t, n = proj.shape
    tok = lambda w, col=0: pl.BlockSpec((1, tt, w), lambda i, j: (i, j, col))
    tab = pl.BlockSpec((tt, LANES), lambda i, j: (j, 0))
    full2 = lambda a: pl.BlockSpec(a.shape, lambda i, j: (0, 0))
    out_shapes = [
        jax.ShapeDtypeStruct((b, t, ATT_WIDTH), BF16),
        jax.ShapeDtypeStruct((b, t, QI_WIDTH), BF16),
        jax.ShapeDtypeStruct((b, t * KV_HEADS, HEAD_DIM), F32),
        jax.ShapeDtypeStruct((b, t * KV_HEADS, HEAD_DIM), F32),
        jax.ShapeDtypeStruct((b, t, IDX_DIM), F32),
        jax.ShapeDtypeStruct((b, t, LANES), F32),
        jax.ShapeDtypeStruct((b, t, KV_WIDTH), BF16),
        jax.ShapeDtypeStruct((b, t, VEXT_WIDTH), BF16),
        jax.ShapeDtypeStruct((b, t, IDX_DIM), BF16),
    ]
    kv_rows = pl.BlockSpec((1, tt * KV_HEADS, HEAD_DIM), lambda i, j: (i, j, 0))
    out_specs = [tok(ATT_WIDTH), tok(QI_WIDTH), kv_rows, kv_rows, tok(IDX_DIM),
                 tok(LANES), tok(KV_WIDTH), tok(VEXT_WIDTH), tok(IDX_DIM)]
    qn = q_norm_w.reshape(1, HEAD_DIM)
    kn = k_norm_w.reshape(1, HEAD_DIM)
    head = OFF_SMALL + LANES
    return pl.pallas_call(
        _post0_kernel,
        grid=(b, t // tt),
        in_specs=[tok(head), tok(LANES, (OFF_DT - SM_DT) // LANES), tab, tab, tab, tab, full2(qn),
                  full2(kn), full2(dt_bias_pad)],
        out_specs=out_specs,
        out_shape=out_shapes,
        compiler_params=_params(2),
        name="post_proj0",
    )(proj, proj, *tabs, qn, kn, dt_bias_pad)


def _dsa_kernel(q_ref, qi_ref, sm_ref, k_ref, v_ref, ki_ref, *rest, qb, tk, n_keys, past, topk,
                n_bisect, keys_on_sublanes):
    if past:
        pk_ref, pv_ref, pki_ref = rest[:3]
        rest = rest[3:]
    (o_ref, sc_ref, bias_ref, acc_ref, qs_ref, s0_ref, s1_ref, mt0_ref, mt1_ref, m_ref) = rest[:10]
    if past:
        kall_ref, vall_ref, kiall_ref = rest[10:]
        t_new = k_ref.shape[1]
        pad = kall_ref.shape[0] - n_keys
        for g in range(KV_HEADS):
            head_rows = pl.ds(g, past, stride=KV_HEADS)
            kall_ref[0:past, g * HEAD_DIM:(g + 1) * HEAD_DIM] = pk_ref[0, head_rows, :].astype(BF16)
            vall_ref[0:past, 2 * g * HEAD_DIM:(2 * g + 1) * HEAD_DIM] = pv_ref[0, head_rows, :].astype(BF16)
            vall_ref[0:past, (2 * g + 1) * HEAD_DIM:(2 * g + 2) * HEAD_DIM] = jnp.ones(
                (past, HEAD_DIM), BF16)
        kiall_ref[0:past, :] = pki_ref[0].astype(BF16)
        for dst, src in ((kall_ref, k_ref), (vall_ref, v_ref), (kiall_ref, ki_ref)):
            dst[past:past + t_new, :] = src[0]
            dst[past + t_new:, :] = jnp.zeros((pad, dst.shape[1]), BF16)
        k_src, v_src, ki_src = kall_ref, vall_ref, kiall_ref
    else:
        k_src, v_src, ki_src = k_ref.at[0], v_ref.at[0], ki_ref.at[0]
    kax = 0 if keys_on_sublanes else 1
    tile_shape = (tk, qb) if keys_on_sublanes else (qb, tk)
    vec_shape = (1, qb) if keys_on_sublanes else (qb, 1)
    n_acc = 8
    part_shape = (n_acc, SUBLANES, qb) if keys_on_sublanes else (qb, LANES)
    j = pl.program_id(1)
    pos0 = past + j * qb
    q_chunk = lax.shift_right_logical(pos0 + lax.broadcasted_iota(jnp.int32, vec_shape, 1 - kax),
                                      CHUNK_SHIFT)
    n_valid = jnp.minimum((q_chunk + 1) * CHUNK, n_keys)
    max_valid = jnp.minimum(((pos0 + qb - 1) // CHUNK + 1) * CHUNK, n_keys)
    nkt = (max_valid + tk - 1) // tk
    kf = float(topk)
    active_f = jnp.where(n_valid > topk, 1.0, 0.0)

    def key_index(kt):
        return kt * tk + lax.broadcasted_iota(jnp.int32, tile_shape, kax)

    nt_dims = (((1,), (1,)), ((), ()))
    for h in range(IDX_HEADS):
        qs_ref[h * qb:(h + 1) * qb, :] = qi_ref[0, :, h * IDX_DIM:(h + 1) * IDX_DIM]
    if keys_on_sublanes:
        wi_t = sm_ref[0].T
        head_w = lambda h: wi_t[SM_WI + h:SM_WI + h + 1, :]
    else:
        wis = sm_ref[0][:, SM_WI:SM_WI + IDX_HEADS]
        head_w = lambda h: wis[:, h:h + 1]

    def score_tile(kt, carry):
        kit = ki_src[pl.ds(pl.multiple_of(kt * tk, tk), tk), :]
        acc = jnp.zeros(tile_shape, F32)
        if keys_on_sublanes:
            for pr in range(IDX_HEADS // 2):
                s2 = lax.dot_general(kit, qs_ref[2 * pr * qb:(2 * pr + 2) * qb, :], nt_dims,
                                     preferred_element_type=F32)
                for e in range(2):
                    acc = acc + head_w(2 * pr + e) * jnp.maximum(s2[:, e * qb:(e + 1) * qb], 0.0)
        else:
            s_all = lax.dot_general(qs_ref[...], kit, nt_dims, preferred_element_type=F32)
            for h in range(IDX_HEADS):
                acc = acc + head_w(h) * jnp.maximum(s_all[h * qb:(h + 1) * qb, :], 0.0)
        sc_ref[kt] = jnp.where(key_index(kt) < n_valid, acc, -jnp.inf)
        return carry

    lax.fori_loop(0, nkt, score_tile, 0)

    def fold(m, op2, red):
        if keys_on_sublanes:
            return red(m.reshape(tk // (n_acc * SUBLANES), n_acc, SUBLANES, qb), axis=0)
        f = m[:, 0:LANES]
        for c in range(1, tk // LANES):
            f = op2(f, m[:, c * LANES:(c + 1) * LANES])
        return f

    def reduce_tiles(make, op2, red, init):
        def body(kt, part):
            return op2(part, fold(make(sc_ref[kt], kt), op2, red))
        part = lax.fori_loop(0, nkt, body, jnp.full(part_shape, init, F32))
        if keys_on_sublanes:
            part = red(part, axis=0)
        return red(part, axis=kax, keepdims=True)

    def count(pred):
        return reduce_tiles(lambda t, kt: jnp.where(pred(t, kt), 1.0, 0.0), jnp.add, jnp.sum, 0.0)

    def row_max(pred):
        return reduce_tiles(lambda t, kt: jnp.where(pred(t, kt), t, -jnp.inf), jnp.maximum, jnp.max,
                            -jnp.inf)

    def row_min_valid():
        return reduce_tiles(lambda t, kt: jnp.where(key_index(kt) < n_valid, t, jnp.inf), jnp.minimum,
                            jnp.min, jnp.inf)

    def any_set(flag_f):
        return jnp.max(flag_f) > 0.0

    def bisect(_, state):
        lo, hi, c_lo = state
        mid = lo + (hi - lo) * 0.5
        c = count(lambda t, kt: t >= mid)
        ge = c >= kf
        return jnp.where(ge, mid, lo), jnp.where(ge, hi, mid), jnp.where(ge, c, c_lo)

    lo0 = row_min_valid()
    hi0 = row_max(lambda t, kt: t == t)
    lo, hi, c_lo = lax.fori_loop(0, n_bisect, bisect, (lo0, hi0, n_valid.astype(F32)))
    found_f = jnp.where(c_lo == kf, active_f, 0.0)
    v_found = reduce_tiles(lambda t, kt: jnp.where(t >= lo, t, jnp.inf), jnp.minimum, jnp.min, jnp.inf)

    def walk_down():
        w0 = row_max(lambda t, kt: t <= hi)
        c0 = count(lambda t, kt: t >= w0)

        def walk_cond(state):
            w, c = state
            return any_set(jnp.where(c < kf, active_f, 0.0))

        def walk_body(state):
            w, c = state
            w2 = row_max(lambda t, kt: t < w)
            c2 = count(lambda t, kt: t >= w2)
            upd = c < kf
            return jnp.where(upd, w2, w), jnp.where(upd, c2, c)

        return lax.while_loop(walk_cond, walk_body,
                              (jnp.where(found_f > 0.5, v_found, w0),
                               jnp.where(found_f > 0.5, kf, c0)))

    w, c_ge = lax.cond(any_set(active_f - found_f), walk_down,
                       lambda: (v_found, jnp.full(vec_shape, kf, F32)))
    thr = jnp.where(active_f > 0.5, w, -jnp.inf)
    tied_f = jnp.where(c_ge > kf, active_f, 0.0)

    def write_bias(sel_fn):
        def body(kt, carry):
            kidx = key_index(kt)
            sel = sel_fn(sc_ref[kt], kidx) & (kidx < n_valid)
            bias = jnp.where(sel, 0.0, NEG_BIG)
            bias_ref[kt] = bias.T if keys_on_sublanes else bias
            return carry
        lax.fori_loop(0, nkt, body, 0)

    def no_ties():
        write_bias(lambda t, kidx: t >= thr)

    def with_ties():
        need = kf - count(lambda t, kt: t > thr)
        n_steps = max(1, int(np.ceil(np.log2(sc_ref.shape[0] * tk))) + 1)

        def step(_, lh):
            lo_i, hi_i = lh
            mid = lax.shift_right_arithmetic(lo_i + hi_i, 1)
            ge = count(lambda t, kt: (t == thr) & (key_index(kt) <= mid)) >= need
            return jnp.where(ge, lo_i, mid), jnp.where(ge, mid, hi_i)

        lo_i = jnp.full(vec_shape, -1, jnp.int32)
        hi_i = jnp.full(vec_shape, sc_ref.shape[0] * tk - 1, jnp.int32)
        _, last_tie = lax.fori_loop(0, n_steps, step, (lo_i, hi_i))
        write_bias(lambda t, kidx: (t > thr) | ((t == thr)
                                                & ((kidx <= last_tie) | (tied_f < 0.5))))

    any_tie = jnp.max(tied_f) > 0.0
    lax.cond(any_tie, with_ties, no_ties)

    rep = ATT_HEADS // KV_HEADS
    qgs = [jnp.concatenate(
        [q_ref[0, :, (g * rep + r) * HEAD_DIM:(g * rep + r + 1) * HEAD_DIM] for r in range(rep)],
        axis=0) for g in range(KV_HEADS)]
    acc_ref[...] = jnp.zeros(acc_ref.shape, F32)
    bufs = ((s0_ref, mt0_ref), (s1_ref, mt1_ref))

    def qk_tile(kt, buf):
        s_ref, mt_ref = bufs[buf]
        row0 = pl.multiple_of(kt * tk, tk)
        bias = bias_ref[kt]
        for g in range(KV_HEADS):
            kg = k_src[pl.ds(row0, tk), g * HEAD_DIM:(g + 1) * HEAD_DIM]
            s = lax.dot_general(qgs[g], kg, nt_dims, preferred_element_type=F32)
            s = (s.reshape(rep, qb, tk) + bias[None]).reshape(rep * qb, tk)
            s_ref[g] = s
            mt_ref[g] = jnp.broadcast_to(jnp.max(s, axis=-1, keepdims=True), (rep * qb, LANES))

    def lane_tile(x, width):
        return jnp.concatenate([x] * (width // LANES), axis=1)

    def att_tile(kt, buf, prefetch):
        if prefetch:
            qk_tile(kt + 1, 1 - buf)
        s_ref, mt_ref = bufs[buf]
        row0 = pl.multiple_of(kt * tk, tk)
        for g in range(KV_HEADS):
            vg = v_src[pl.ds(row0, tk), 2 * g * HEAD_DIM:(2 * g + 2) * HEAD_DIM]
            m_old = m_ref[g]
            m_new = jnp.maximum(m_old, mt_ref[g])
            m_ref[g] = m_new
            alpha = jnp.exp2(m_old - m_new)
            p = jnp.exp2(s_ref[g] - lane_tile(m_new, tk)).astype(BF16)
            acc_ref[g] = (lane_tile(alpha, 2 * HEAD_DIM) * acc_ref[g]
                          + jnp.dot(p, vg, preferred_element_type=F32))

    def tile_pair(pi, carry):
        att_tile(2 * pi, 0, True)
        att_tile(2 * pi + 1, 1, True)
        return carry

    def last_two():
        att_tile(nkt - 2, 0, True)
        att_tile(nkt - 1, 1, False)

    def last_one():
        att_tile(nkt - 1, 0, False)

    m_ref[...] = jnp.full(m_ref.shape, NEG_BIG, F32)
    qk_tile(0, 0)
    lax.fori_loop(0, (nkt - 1) // 2, tile_pair, 0)
    lax.cond((nkt & 1) == 0, last_two, last_one)
    for g in range(KV_HEADS):
        acc = acc_ref[g]
        o = acc[:, :HEAD_DIM] / acc[:, HEAD_DIM:]
        for r in range(rep):
            h = g * rep + r
            o_ref[0, :, h * HEAD_DIM:(h + 1) * HEAD_DIM] = o[r * qb:(r + 1) * qb].astype(BF16)


def _dsa_attention(q, qi, sm, k_new, v_new, ki_new, cache, *, qb, tk):
    b, t, _ = q.shape
    past = 0 if cache is None else cache[2].shape[1]
    n_keys = past + t
    nk_pad = -(-n_keys // tk) * tk
    topk = min(TOPK_MAX, n_keys // 4)
    keys_on_sublanes = qb == LANES
    rep = ATT_HEADS // KV_HEADS
    tokq = lambda w: pl.BlockSpec((1, qb, w), lambda i, j: (i, j, 0))
    keys = lambda w: pl.BlockSpec((1, t, w), lambda i, j: (i, 0, 0))
    tile_shape = (tk, qb) if keys_on_sublanes else (qb, tk)
    cache_specs, cache_scratch = [], []
    if cache is not None:
        kv = pl.BlockSpec((1, past * KV_HEADS, HEAD_DIM), lambda i, j: (i, 0, 0))
        cache_specs = [kv, kv, pl.BlockSpec((1, past, IDX_DIM), lambda i, j: (i, 0, 0))]
        cache_scratch = [pltpu.VMEM((nk_pad, w), BF16) for w in (KV_WIDTH, VEXT_WIDTH, IDX_DIM)]
    return pl.pallas_call(
        functools.partial(_dsa_kernel, qb=qb, tk=tk, n_keys=n_keys, past=past, topk=topk,
                          n_bisect=20, keys_on_sublanes=keys_on_sublanes),
        grid=(b, t // qb),
        in_specs=[tokq(ATT_WIDTH), tokq(QI_WIDTH), tokq(LANES), keys(KV_WIDTH), keys(VEXT_WIDTH),
                  keys(IDX_DIM)] + cache_specs,
        out_specs=tokq(ATT_WIDTH),
        out_shape=jax.ShapeDtypeStruct((b, t, ATT_WIDTH), BF16),
        scratch_shapes=[pltpu.VMEM((nk_pad // tk,) + tile_shape, F32),
                        pltpu.VMEM((nk_pad // tk, qb, tk), F32),
                        pltpu.VMEM((KV_HEADS, rep * qb, 2 * HEAD_DIM), F32),
                        pltpu.VMEM((IDX_HEADS * qb, IDX_DIM), BF16),
                        pltpu.VMEM((KV_HEADS, rep * qb, tk), F32),
                        pltpu.VMEM((KV_HEADS, rep * qb, tk), F32),
                        pltpu.VMEM((KV_HEADS, rep * qb, LANES), F32),
                        pltpu.VMEM((KV_HEADS, rep * qb, LANES), F32),
                        pltpu.VMEM((KV_HEADS, rep * qb, LANES), F32)] + cache_scratch,
        compiler_params=_params(2),
        name="dsa_attention",
    )(q, qi, sm, k_new, v_new, ki_new, *(cache or ()))


def _ssd_kernel(pfirst_ref, pnext_ref, z_ref, sm_ref, st_ref, cs_ref, cw_ref, cb_ref, alog_ref,
                dexp_ref, nw_ref, e_ref, y_ref, stout_ref, cnew_ref, ht_ref, yi_ref, cbuf_ref, xa_ref,
                xb_ref):
    c = pl.program_id(1)
    nc = pl.num_programs(1)
    L = pfirst_ref.shape[1]
    hp = SSD_INNER // SSD_GROUPS

    def conv_act(p_ref):
        raw = _lane_aligned(p_ref[0, :, :OFF_DT - OFF_XBC + LANES], SM_DT, SSD_CONV_DIM)
        out, tail = _causal_conv_tile(raw, cw_ref, cb_ref, cbuf_ref)
        return _silu(out), tail

    @pl.when(c == 0)
    def _():
        ht_ref[...] = st_ref[0].reshape(SSD_INNER, SSD_STATE).T
        _conv_carry_init(cs_ref, cbuf_ref)
        act, tail = conv_act(pfirst_ref)
        xa_ref[...] = act

        @pl.when(nc == 1)
        def _():
            cnew_ref[0] = _conv_state(tail)

    def chunk(xbc_ref, next_ref):
        _ssd_chunk(xbc_ref, z_ref, sm_ref, alog_ref, dexp_ref, nw_ref, e_ref, y_ref, ht_ref, yi_ref)
        act, tail = conv_act(pnext_ref)
        next_ref[...] = act
        return tail

    tail = lax.cond((c & 1) == 0, lambda: chunk(xa_ref, xb_ref), lambda: chunk(xb_ref, xa_ref))

    @pl.when(c + 1 == nc - 1)
    def _():
        cnew_ref[0] = _conv_state(tail)

    @pl.when(c == nc - 1)
    def _():
        stout_ref[0] = ht_ref[...].T.reshape(SSD_HEADS, SSD_HEADDIM, SSD_STATE)


def _ssd_chunk(xbc_ref, z_ref, sm_ref, alog_ref, dexp_ref, nw_ref, e_ref, y_ref, ht_ref, yi_ref):
    L = xbc_ref.shape[0]
    hp = SSD_INNER // SSD_GROUPS
    xs = xbc_ref[:, 0:SSD_INNER]
    sm = sm_ref[0]
    lane = lax.broadcasted_iota(jnp.int32, (1, LANES), 1)
    is_dt = (lane >= SM_DT) & (lane < SM_DT + SSD_HEADS)
    a_neg = jnp.where(is_dt, -jnp.exp(alog_ref[...]), 0.0)
    dt = jnp.where(is_dt, sm, 0.0)
    rows = lax.broadcasted_iota(jnp.int32, (L, L), 0)
    cols = lax.broadcasted_iota(jnp.int32, (L, L), 1)
    tri = cols <= rows
    e = e_ref[...]
    cum = jnp.dot(tri.astype(F32), dt * a_neg, precision=HIGHEST, preferred_element_type=F32)
    eye = (lax.broadcasted_iota(jnp.int32, (LANES, LANES), 0)
           == lax.broadcasted_iota(jnp.int32, (LANES, LANES), 1)).astype(F32)
    cum_t = lax.dot_general(eye, cum, (((1,), (1,)), ((), ())), precision=HIGHEST,
                            preferred_element_type=F32)
    both = jnp.concatenate([cum, dt], axis=0)
    hi = both.astype(BF16)
    rest = both - hi.astype(F32)
    mid = rest.astype(BF16)
    low = (rest - mid.astype(F32)).astype(BF16)
    both_e = (jnp.dot(hi, e, preferred_element_type=F32) + jnp.dot(mid, e, preferred_element_type=F32)
              + jnp.dot(low, e, preferred_element_type=F32))
    cum_e, dt_e = both_e[:L], both_e[L:]
    last = cum_e[L - 1:L, :]
    xdt = (xs * dt_e).astype(BF16)
    xw = (xs * (jnp.exp(last - cum_e) * dt_e)).astype(BF16)
    ht_old = ht_ref[...]

    for g in range(SSD_GROUPS):
        bg = xbc_ref[:, SSD_INNER + g * SSD_STATE:SSD_INNER + (g + 1) * SSD_STATE]
        cg = xbc_ref[:, SSD_INNER + (SSD_GROUPS + g) * SSD_STATE:
                     SSD_INNER + (SSD_GROUPS + g + 1) * SSD_STATE]
        bgb, cgb = bg.astype(BF16), cg.astype(BF16)
        cb = lax.dot_general(cgb, bgb, (((1,), (1,)), ((), ())), preferred_element_type=F32)
        for hh in range(SSD_HEADS // SSD_GROUPS):
            h = g * (SSD_HEADS // SSD_GROUPS) + hh
            seg = cum[:, SM_DT + h:SM_DT + h + 1] - cum_t[SM_DT + h:SM_DT + h + 1, :]
            decay = jnp.exp(jnp.where(tri, seg, -jnp.inf))
            wts = (cb * decay).astype(BF16)
            yi_ref[:, h * SSD_HEADDIM:(h + 1) * SSD_HEADDIM] = jnp.dot(
                wts, xdt[:, h * SSD_HEADDIM:(h + 1) * SSD_HEADDIM], preferred_element_type=F32)
        ht_g = ht_old[:, g * hp:(g + 1) * hp]
        y_inter = jnp.dot(cgb, ht_g.astype(BF16), preferred_element_type=F32)
        yi_ref[:, g * hp:(g + 1) * hp] = (yi_ref[:, g * hp:(g + 1) * hp]
                                          + y_inter * jnp.exp(cum_e[:, g * hp:(g + 1) * hp]))
        upd = jnp.dot(bg.T.astype(BF16), xw[:, g * hp:(g + 1) * hp],
                      preferred_element_type=F32)
        ht_ref[:, g * hp:(g + 1) * hp] = ht_g * jnp.exp(last[:, g * hp:(g + 1) * hp]) + upd

    y = yi_ref[...] + dexp_ref[...] * xs
    y = y * _silu(_lane_aligned(z_ref[0], SM_DT, SSD_INNER))
    for g in range(SSD_GROUPS):
        yg = y[:, g * hp:(g + 1) * hp]
        ms = jnp.mean(yg * yg, axis=-1, keepdims=True)
        y_ref[0, :, g * hp:(g + 1) * hp] = (yg * lax.rsqrt(ms + EPS)
                                            * nw_ref[:, g * hp:(g + 1) * hp]).astype(BF16)


def _ssd(proj, sm, state, conv_state, conv_w, conv_b, a_log_pad, d_exp, norm_w, e_mat, *, chunk):
    b, t, _ = proj.shape
    n_chunks = t // chunk
    xbc_col = (OFF_XBC - SM_DT) // XBC_BLOCK
    tok = lambda w, col=0: pl.BlockSpec((1, chunk, w), lambda i, j: (i, j, col))
    xbc_first = pl.BlockSpec((1, chunk, XBC_BLOCK), lambda i, j: (i, 0, xbc_col))
    xbc_next = pl.BlockSpec((1, chunk, XBC_BLOCK),
                            lambda i, j: (i, jnp.minimum(j + 1, n_chunks - 1), xbc_col))
    full2 = lambda a: pl.BlockSpec(a.shape, lambda i, j: (0, 0))
    st = pl.BlockSpec((1, SSD_HEADS, SSD_HEADDIM, SSD_STATE), lambda i, j: (i, 0, 0, 0))
    cst = pl.BlockSpec((1, CONV_W - 1, SSD_CONV_DIM), lambda i, j: (i, 0, 0))
    nw = norm_w.reshape(1, SSD_INNER)
    cb = conv_b.reshape(1, SSD_CONV_DIM)
    return pl.pallas_call(
        _ssd_kernel,
        grid=(b, n_chunks),
        in_specs=[xbc_first, xbc_next,
                  tok(Z_BLOCK, (OFF_Z - SM_DT) // Z_BLOCK), tok(LANES), st, cst, full2(conv_w),
                  full2(cb), full2(a_log_pad), full2(d_exp), full2(nw), full2(e_mat)],
        out_specs=[tok(SSD_INNER), st, cst],
        out_shape=[jax.ShapeDtypeStruct((b, t, SSD_INNER), BF16),
                   jax.ShapeDtypeStruct(state.shape, F32),
                   jax.ShapeDtypeStruct((b, CONV_W - 1, SSD_CONV_DIM), F32)],
        scratch_shapes=[pltpu.VMEM((SSD_STATE, SSD_INNER), F32),
                        pltpu.VMEM((chunk, SSD_INNER), F32),
                        pltpu.VMEM((chunk + SUBLANES, SSD_CONV_DIM), F32),
                        pltpu.VMEM((chunk, SSD_CONV_DIM), F32),
                        pltpu.VMEM((chunk, SSD_CONV_DIM), F32)],
        compiler_params=_params(2),
        name="ssd_scan",
    )(proj, proj, proj, sm, state, conv_state, conv_w, cb, a_log_pad, d_exp, nw, e_mat)


def _lru_kernel(gate_ref, xin_ref, cw_ref, cb_ref, wa_ref, wx_ref, ba_ref, bx_ref, lam_ref,
                h0_ref, cs_ref, y_ref, hout_ref, cnew_ref, cbuf_ref, a_ref, u_ref, hp_ref):
    t = pl.program_id(1)
    nt = pl.num_programs(1)
    tt = xin_ref.shape[1]
    xc = _causal_conv(xin_ref[0], cs_ref, cw_ref, cb_ref, cbuf_ref, cnew_ref, t == 0, t == nt - 1)

    @pl.when(t == 0)
    def _():
        hp_ref[...] = h0_ref[0]

    for kb in range(LRU_BLOCKS):
        sl = slice(kb * LRU_BLOCK_DIM, (kb + 1) * LRU_BLOCK_DIM)
        xb = xc[:, sl].astype(BF16)
        a_ref[:, sl] = jnp.dot(xb, wa_ref[kb].astype(BF16), preferred_element_type=F32)
        u_ref[:, sl] = jnp.dot(xb, wx_ref[kb].astype(BF16), preferred_element_type=F32)
    r = jax.nn.sigmoid(a_ref[...] + ba_ref[...])
    i = jax.nn.sigmoid(u_ref[...] + bx_ref[...])
    log_a = -LRU_C * r * _softplus(-lam_ref[...])
    a = jnp.exp(log_a)
    u = jnp.sqrt(-jnp.tanh(log_a) * (1.0 + a * a)) * (i * xc)

    width = a.shape[1]
    a = a.reshape(tt // SUBLANES, SUBLANES, width)
    u = u.reshape(tt // SUBLANES, SUBLANES, width)
    sub = lax.broadcasted_iota(jnp.int32, (1, SUBLANES, width), 1)
    for s in (1, 2, 4):
        m = sub >= s
        a_sh = pltpu.roll(a, s, 1)
        u_sh = pltpu.roll(u, s, 1)
        u = jnp.where(m, a * u_sh + u, u)
        a = jnp.where(m, a * a_sh, a)
    a_ref[...] = a.reshape(tt, width)
    u_ref[...] = u.reshape(tt, width)

    def group(gi, hprev):
        r0 = pl.multiple_of(gi * SUBLANES, SUBLANES)
        hs = a_ref[pl.ds(r0, SUBLANES), :] * hprev + u_ref[pl.ds(r0, SUBLANES), :]
        u_ref[pl.ds(r0, SUBLANES), :] = hs
        return hs[SUBLANES - 1:SUBLANES, :]

    h_last = lax.fori_loop(0, tt // SUBLANES, group, hp_ref[...])
    hp_ref[...] = h_last
    gate = gate_ref[0]
    gelu = 0.5 * gate * (1.0 + jnp.tanh(np.sqrt(2.0 / np.pi) * (gate + 0.044715 * (gate * gate * gate))))
    y_ref[0] = (u_ref[...] * gelu).astype(BF16)

    @pl.when(t == nt - 1)
    def _():
        hout_ref[0] = h_last


def _lru(proj, conv_w, conv_b, w_a, b_a, w_x, b_x, lam, h0, conv_state, *, tt):
    b, t, _ = proj.shape
    w = LRU_WIDTH
    tok = lambda col: pl.BlockSpec((1, tt, w), lambda i, j: (i, j, col))
    full = lambda a: pl.BlockSpec(a.shape, lambda i, j: (0,) * a.ndim)
    row = lambda v: v.reshape(1, w)
    state = pl.BlockSpec((1, CONV_W - 1, w), lambda i, j: (i, 0, 0))
    hspec = pl.BlockSpec((1, 1, w), lambda i, j: (i, 0, 0))
    args = (proj, proj, conv_w, row(conv_b), w_a, w_x, row(b_a), row(b_x), row(lam), h0, conv_state)
    return pl.pallas_call(
        _lru_kernel,
        grid=(b, t // tt),
        in_specs=[tok(0), tok(1)] + [full(a) for a in args[2:9]] + [hspec, state],
        out_specs=[pl.BlockSpec((1, tt, w), lambda i, j: (i, j, 0)), hspec, state],
        out_shape=[jax.ShapeDtypeStruct((b, t, w), BF16),
                   jax.ShapeDtypeStruct((b, 1, w), F32),
                   jax.ShapeDtypeStruct((b, CONV_W - 1, w), F32)],
        scratch_shapes=[pltpu.VMEM((tt + SUBLANES, w), F32), pltpu.VMEM((tt, w), F32),
                        pltpu.VMEM((tt, w), F32), pltpu.VMEM((1, w), F32)],
        compiler_params=_params(2),
        name="rg_lru",
    )(*args)


def _rope_tables(pos, dim):
    half = dim // 2
    inv = np.power(ROPE_THETA, -np.arange(half, dtype=np.float64) / half)
    ang = pos.astype(np.float64)[:, None] * inv[None, :]
    cos = np.concatenate([np.cos(ang), np.cos(ang)], axis=-1)
    sin = np.concatenate([-np.sin(ang), np.sin(ang)], axis=-1)
    reps = LANES // dim
    return (jnp.asarray(np.tile(cos, (1, reps)), F32), jnp.asarray(np.tile(sin, (1, reps)), F32))


def _head_expansion():
    e = np.zeros((LANES, SSD_INNER), np.float32)
    for h in range(SSD_HEADS):
        e[SM_DT + h, h * SSD_HEADDIM:(h + 1) * SSD_HEADDIM] = 1.0
    return jnp.asarray(e, BF16)


def _pad_lanes(v, offset):
    out = jnp.zeros((1, LANES), F32)
    return out.at[0, offset:offset + v.shape[0]].set(v)


def _w_in_ab_tail(w_t):
    whole = AB_WIDTH // AB_TILE_N * AB_TILE_N
    return jnp.pad(w_t[whole:], ((0, AB_PAD_N - AB_WIDTH), (0, 0)))


def _layer0_mixers(proj, st, p, cfg):
    b, t, _ = proj.shape
    past = 0 if st["past_k"] is None else st["past_k"].shape[1]
    pos = np.arange(past, past + t)
    tabs = _rope_tables(pos, HEAD_DIM) + _rope_tables(pos, IDX_DIM)
    q, qi, k, v, ki, sm, kb, vbe, kib = _post0(proj, tabs, p["q_norm_w"][0], p["k_norm_w"][0],
                                               p["dt_bias_pad"], tt=cfg["post_tt"])
    cache = None if not past else (st["past_k"].reshape(b, past * KV_HEADS, HEAD_DIM),
                                   st["past_v"].reshape(b, past * KV_HEADS, HEAD_DIM), st["past_ki"])
    att = _dsa_attention(q, qi, sm, kb, vbe, kib, cache, qb=cfg["qb"], tk=cfg["tk"])
    y_ssd, ssm_new, ssm_conv_new = _ssd(
        proj, sm, st["ssm_h"], st["ssm_conv"], p["ssd_conv_w"][0], p["ssd_conv_b"][0], p["a_log_pad"],
        p["d_exp"], p["ssd_norm_w"][0], p["e_mat"], chunk=cfg["ssd_chunk"])
    kv_shape = (1, b, t, KV_HEADS, HEAD_DIM)
    return att, y_ssd, (k.reshape(kv_shape), v.reshape(kv_shape), ki[None], ssm_new[None],
                        ssm_conv_new[None])


def _layer1_mixer(proj, st, p, cfg):
    b = proj.shape[0]
    y_lru, lru_new, lru_conv_new = _lru(
        proj, p["lru_conv_w"][0], p["lru_conv_b"][0], p["lru_w_a"][0], p["lru_b_a"][0],
        p["lru_w_x"][0], p["lru_b_x"][0], p["lru_lambda"][0], st["lru_h"].reshape(b, 1, LRU_WIDTH),
        st["lru_conv"], tt=cfg["lru_tt"])
    return y_lru, (lru_new.reshape(1, b, LRU_WIDTH), lru_conv_new[None])


def _forward(xp, xs, mods_p, mods_s, st_p, st_s, p, cfg_p, cfg_s):
    bb, tt = cfg_p["bb"], cfg_p["tt"]

    def ffn(xp, xs, l):
        (_, _, _, sh_p, sc_p, g_p), (_, _, _, sh_s, sc_s, g_s) = mods_p[l], mods_s[l]
        gu_p, gu_s = _norm_mod_matmul(
            xp, p["norm_ffn_w"][l], sc_p, sh_p, [p["ffn_w_gate"], p["ffn_w_up"]], l, bb=bb,
            tt=cfg_p["tt_ffn"], tn=512, out_dtype=BF16, name="ffn_gate_up", ride=(xs, sc_s, sh_s))
        return _matmul_residual([gu_p], p["ffn_w_down"], l, xp, g_p, bb=bb, tt=cfg_p["tt_ffn"],
                                tn=256, name="ffn_down", ride=([gu_s], xs, g_s))

    (sh_p, sc_p, g_p, *_), (sh_s, sc_s, g_s, *_) = mods_p[0], mods_s[0]
    proj_p, proj_s = _norm_mod_matmul(
        xp, p["norm_mix_w"][0], sc_p, sh_p, [p["w_in_ab_t"]], 0, bb=bb, tt=cfg_p["tt_ab"],
        tn=AB_TILE_N, out_dtype=F32, name="in_proj_ab", w_tail=p["w_in_ab_tail"], w_rows_out=True,
        ride=(xs, sc_s, sh_s))
    att_p, y_p, outs0_p = _layer0_mixers(proj_p, st_p, p, cfg_p)
    att_s, y_s, outs0_s = _layer0_mixers(proj_s, st_s, p, cfg_s)
    xp, xs = _matmul_residual([att_p, y_p], p["w_out_ab"], 0, xp, g_p, bb=bb, tt=tt, tn=512,
                              name="out_proj_ab", ride=([att_s, y_s], xs, g_s))
    xp, xs = ffn(xp, xs, 0)

    (sh_p, sc_p, g_p, *_), (sh_s, sc_s, g_s, *_) = mods_p[1], mods_s[1]
    proj_p, proj_s = _norm_mod_matmul(
        xp, p["norm_mix_w"][1], sc_p, sh_p, [p["w_in_c"]], 0, bb=bb, tt=cfg_p["tt_ab"], tn=512,
        out_dtype=F32, name="in_proj_c", ride=(xs, sc_s, sh_s))
    y_p, outs1_p = _layer1_mixer(proj_p, st_p, p, cfg_p)
    y_s, outs1_s = _layer1_mixer(proj_s, st_s, p, cfg_s)
    xp, xs = _matmul_residual([y_p], p["w_out_c"], 0, xp, g_p, bb=bb, tt=tt, tn=512,
                              name="out_proj_c", ride=([y_s], xs, g_s))
    xp, xs = ffn(xp, xs, 1)
    return (xp,) + outs0_p + outs1_p, (xs,) + outs0_s + outs1_s


PROMPT_CFG = dict(bb=1, tt=2048, tt_ab=1024, tt_ffn=1024, post_tt=1024, qb=128, tk=512,
                  ssd_chunk=128, lru_tt=512)
SAMPLE_CFG = dict(post_tt=32, qb=32, tk=384, ssd_chunk=32, lru_tt=32)


def kernel(x_prompt, x_sample, cache_attn_k, cache_attn_v, cache_idx_k, state_ssm, state_ssm_conv,
           state_lru, state_lru_conv, c_prompt, c_sample, ada_w, ada_b, norm_mix_w, norm_ffn_w,
           w_in_ab, q_norm_w, k_norm_w, ssd_conv_w, ssd_conv_b, ssd_dt_bias, ssd_a_log, ssd_d,
           ssd_norm_w, w_out_ab, w_in_c, lru_conv_w, lru_conv_b, lru_w_a, lru_b_a, lru_w_x, lru_b_x,
           lru_lambda, w_out_c, ffn_w_gate, ffn_w_up, ffn_w_down):
    bp, bs = x_prompt.shape[0], x_sample.shape[0]
    p = dict(norm_mix_w=norm_mix_w, norm_ffn_w=norm_ffn_w, q_norm_w=q_norm_w, k_norm_w=k_norm_w,
             ssd_conv_w=ssd_conv_w, ssd_conv_b=ssd_conv_b, ssd_norm_w=ssd_norm_w, w_out_ab=w_out_ab,
             w_in_c=w_in_c, lru_conv_w=lru_conv_w, lru_conv_b=lru_conv_b, lru_w_a=lru_w_a,
             lru_b_a=lru_b_a, lru_w_x=lru_w_x, lru_b_x=lru_b_x, lru_lambda=lru_lambda,
             w_out_c=w_out_c, ffn_w_gate=ffn_w_gate, ffn_w_up=ffn_w_up, ffn_w_down=ffn_w_down)
    p["w_in_ab_t"] = jnp.swapaxes(w_in_ab, 1, 2)
    p["w_in_ab_tail"] = _w_in_ab_tail(p["w_in_ab_t"][0])
    p["dt_bias_pad"] = _pad_lanes(ssd_dt_bias[0], SM_DT)
    p["a_log_pad"] = _pad_lanes(ssd_a_log[0], SM_DT)
    p["d_exp"] = jnp.repeat(ssd_d[0], SSD_HEADDIM).reshape(1, SSD_INNER)
    p["e_mat"] = _head_expansion()

    m_rows = 32
    c_all = jnp.concatenate([c_prompt, c_sample, jnp.zeros((m_rows - bp - bs, D_MODEL), F32)], axis=0)
    mod = _modulation(c_all, ada_w, ada_b)

    def group_mods(r0, nb):
        return [[mod[l, r0:r0 + nb, i * D_MODEL:(i + 1) * D_MODEL].reshape(nb, 1, D_MODEL)
                 for i in range(6)] for l in range(mod.shape[0])]

    zeros = lambda *s: jnp.zeros(s, F32)
    st_p = dict(past_k=None, past_v=None, past_ki=None,
                ssm_h=zeros(bp, SSD_HEADS, SSD_HEADDIM, SSD_STATE),
                ssm_conv=zeros(bp, CONV_W - 1, SSD_CONV_DIM), lru_h=zeros(bp, LRU_WIDTH),
                lru_conv=zeros(bp, CONV_W - 1, LRU_WIDTH))
    st_s = dict(past_k=cache_attn_k[0], past_v=cache_attn_v[0], past_ki=cache_idx_k[0],
                ssm_h=state_ssm[0], ssm_conv=state_ssm_conv[0], lru_h=state_lru[0],
                lru_conv=state_lru_conv[0])
    out_p, out_s = _forward(x_prompt, x_sample, group_mods(0, bp), group_mods(bp, bs), st_p, st_s, p,
                            PROMPT_CFG, SAMPLE_CFG)
    return (out_p[0], out_s[0]) + out_p[1:] + out_s[1:]
```

```python
import functools

import numpy as np
import jax
import jax.numpy as jnp
from jax import lax
from jax.experimental import pallas as pl
from jax.experimental.pallas import tpu as pltpu

F32 = jnp.float32
BF16 = jnp.bfloat16
HIGHEST = lax.Precision.HIGHEST

D_MODEL = 2048
CHUNK = 64
CHUNK_SHIFT = 6
ATT_HEADS = 8
KV_HEADS = 2
HEAD_DIM = 128
IDX_HEADS = 16
IDX_DIM = 64
TOPK_MAX = 256
ROPE_THETA = 10000.0
SSD_HEADS = 16
SSD_HEADDIM = 64
SSD_GROUPS = 2
SSD_STATE = 128
SSD_INNER = SSD_HEADS * SSD_HEADDIM
SSD_CONV_DIM = SSD_INNER + 2 * SSD_GROUPS * SSD_STATE
CONV_W = 4
LRU_WIDTH = D_MODEL
LRU_BLOCKS = 16
LRU_BLOCK_DIM = LRU_WIDTH // LRU_BLOCKS
LRU_C = 8.0
EPS = 1e-6
ATT_WIDTH = ATT_HEADS * HEAD_DIM
KV_WIDTH = KV_HEADS * HEAD_DIM
VEXT_WIDTH = 2 * KV_WIDTH
QI_WIDTH = IDX_HEADS * IDX_DIM
Q_SCALE = HEAD_DIM ** -0.5 * float(np.log2(np.e))

LANES = 128
SUBLANES = 8
VMEM_LIMIT_BYTES = 62 * 1024 * 1024
SINGLE_BUFFER_ROWS = 2048
NORM_CHUNKS = 4

OFF_Q = 0
OFF_K = OFF_Q + ATT_WIDTH
OFF_V = OFF_K + KV_WIDTH
OFF_QI = OFF_V + KV_WIDTH
OFF_SMALL = OFF_QI + QI_WIDTH
SM_WI = IDX_DIM
SM_DT = IDX_DIM + IDX_HEADS
OFF_Z = OFF_SMALL + SM_DT
OFF_XBC = OFF_Z + SSD_INNER
OFF_DT = OFF_XBC + SSD_CONV_DIM
AB_WIDTH = OFF_DT + SSD_HEADS
Z_BLOCK = 1280
XBC_BLOCK = 1792
AB_TILE_N = 512
AB_PAD_N = 5632
assert OFF_Z % LANES == SM_DT and OFF_XBC % LANES == SM_DT and OFF_DT % LANES == SM_DT
assert (OFF_Z - SM_DT) % Z_BLOCK == 0 and OFF_Z - SM_DT + Z_BLOCK >= OFF_Z + SSD_INNER
assert (OFF_XBC - SM_DT) % XBC_BLOCK == 0 and XBC_BLOCK >= OFF_DT - OFF_XBC + LANES
assert OFF_XBC - SM_DT + XBC_BLOCK <= AB_PAD_N

NEG_BIG = -1e30


def _params(n_axes):
    return pltpu.CompilerParams(dimension_semantics=("arbitrary",) * n_axes,
                                vmem_limit_bytes=VMEM_LIMIT_BYTES)


def _silu(x):
    return x * jax.nn.sigmoid(x)


def _softplus(x):
    return jnp.maximum(x, 0.0) + jnp.log1p(jnp.exp(-jnp.abs(x)))


def _mod_kernel(c_ref, w_ref, b_ref, o_ref):
    a = _silu(c_ref[...]).astype(BF16)
    o_ref[0] = jnp.dot(a, w_ref[0].astype(BF16), preferred_element_type=F32) + b_ref[0]


def _modulation(c_all, ada_w, ada_b):
    depth, d, n = ada_w.shape
    m = c_all.shape[0]
    tn = 1024
    return pl.pallas_call(
        _mod_kernel,
        grid=(depth, n // tn),
        in_specs=[pl.BlockSpec((m, d), lambda l, j: (0, 0)),
                  pl.BlockSpec((1, d, tn), lambda l, j: (l, 0, j)),
                  pl.BlockSpec((1, 1, tn), lambda l, j: (l, 0, j))],
        out_specs=pl.BlockSpec((1, m, tn), lambda l, j: (l, 0, j)),
        out_shape=jax.ShapeDtypeStruct((depth, m, n), F32),
        compiler_params=_params(2),
        name="adaln_mod",
    )(c_all, ada_w, ada_b.reshape(depth, 1, n))


def _first_row_tile():
    return (pl.program_id(0) == 0) & (pl.program_id(1) == 0)


def _nmm_kernel(*refs, swiglu, tail, w_rows_out, ride):
    refs = list(refs)
    x_ref, nw_ref, sc_ref, sh_ref = refs[:4]
    del refs[:4]
    if ride:
        xr_ref, scr_ref, shr_ref = refs[:3]
        del refs[:3]
    n_w = 2 if swiglu or tail else 1
    w_refs = refs[:n_w]
    del refs[:n_w]
    if ride:
        o_ref, or_ref, h_ref, hr_ref = refs
    else:
        o_ref, h_ref = refs
    k = pl.program_id(2)

    def matmul(h, w):
        dims = (((1,), (1,)), ((), ())) if w_rows_out else (((1,), (0,)), ((), ()))
        return lax.dot_general(h, w, dims, preferred_element_type=F32)

    def finish(accs, out_ref):
        o = _silu(accs[0]) * accs[1] if swiglu else accs[0]
        out_ref[...] = o.reshape(out_ref.shape).astype(out_ref.dtype)

    def project(ws, rows_ref, out_ref):
        h = rows_ref[...]
        finish([matmul(h, w) for w in ws], out_ref)

    def norm_project(ws, src_ref, scale_ref, shift_ref, rows_ref, out_ref):
        x = src_ref[...]
        d = x.shape[-1]
        rs = lax.rsqrt(jnp.mean(x * x, axis=-1, keepdims=True) + EPS)
        kc = d // NORM_CHUNKS
        accs = None
        for c in range(NORM_CHUNKS):
            cols = slice(c * kc, (c + 1) * kc)
            hc = (x[..., cols] * rs * nw_ref[..., cols] * (1.0 + scale_ref[..., cols])
                  + shift_ref[..., cols])
            hc = hc.reshape(rows_ref.shape[0], kc).astype(BF16)
            rows_ref[:, cols] = hc
            part = [matmul(hc, w[:, cols] if w_rows_out else w[cols, :]) for w in ws]
            accs = part if accs is None else [a + b for a, b in zip(accs, part)]
        finish(accs, out_ref)

    def step(tile_refs, first):
        ws = [w[...].astype(BF16) for w in tile_refs]
        if first:
            norm_project(ws, x_ref, sc_ref, sh_ref, h_ref, o_ref)
            if ride:
                pl.when(_first_row_tile())(
                    lambda: norm_project(ws, xr_ref, scr_ref, shr_ref, hr_ref, or_ref))
        else:
            project(ws, h_ref, o_ref)
            if ride:
                pl.when(_first_row_tile())(lambda: project(ws, hr_ref, or_ref))

    last = pl.num_programs(2) - 1 if tail else pl.num_programs(2)
    pl.when(k == 0)(lambda: step(w_refs[:n_w - tail], True))
    pl.when((k > 0) & (k < last))(lambda: step(w_refs[:n_w - tail], False))
    if tail:
        pl.when(k == last)(lambda: step(w_refs[1:], False))


def _row_block_mode(rows):
    return dict(pipeline_mode=pl.Buffered(1)) if rows >= SINGLE_BUFFER_ROWS else {}


def _ride_col_tile(n_tiles):
    return lambda i, j, k: (0, 0, jnp.where((i == 0) & (j == 0), k, n_tiles - 1))


def _norm_mod_matmul(x, nw, sc, sh, ws, layer, *, bb, tt, tn, out_dtype, name, w_tail=None,
                     w_rows_out=False, ride=None):
    b, t, d = x.shape
    swiglu = len(ws) == 2
    n_main = ws[0].shape[1 if w_rows_out else 2] // tn
    n_tiles = n_main + (w_tail is not None)
    n = n_tiles * tn
    xmap = lambda i, j, k: (i, j, 0)
    mmap = lambda i, j, k: (i, 0, 0)
    const3 = lambda i, j, k: (0, 0, 0)
    tile = lambda k: jnp.minimum(k, n_main - 1)
    if w_rows_out:
        w_specs = [pl.BlockSpec((None, tn, d), lambda i, j, k: (layer, tile(k), 0))] * len(ws)
    else:
        w_specs = [pl.BlockSpec((None, d, tn), lambda i, j, k: (layer, 0, tile(k)))] * len(ws)
    operands = list(ws)
    if w_tail is not None:
        w_specs.append(pl.BlockSpec(w_tail.shape, lambda i, j, k: (0, 0)))
        operands.append(w_tail)
    in_specs = [pl.BlockSpec((bb, tt, d), xmap, **_row_block_mode(bb * tt)),
                pl.BlockSpec((1, 1, d), const3),
                pl.BlockSpec((bb, 1, d), mmap),
                pl.BlockSpec((bb, 1, d), mmap)]
    out_specs = [pl.BlockSpec((bb, tt, tn), lambda i, j, k: (i, j, k))]
    out_shape = [jax.ShapeDtypeStruct((b, t, n), out_dtype)]
    scratch = [pltpu.VMEM((bb * tt, d), BF16)]
    ride_ops = ()
    if ride is not None:
        ride_ops = ride
        rb, rt, _ = ride[0].shape
        in_specs += [pl.BlockSpec((rb, rt, d), const3, pipeline_mode=pl.Buffered(1)),
                     pl.BlockSpec((rb, 1, d), const3), pl.BlockSpec((rb, 1, d), const3)]
        out_specs.append(pl.BlockSpec((rb, rt, tn), _ride_col_tile(n_tiles)))
        out_shape.append(jax.ShapeDtypeStruct((rb, rt, n), out_dtype))
        scratch.append(pltpu.VMEM((rb * rt, d), BF16))
    outs = pl.pallas_call(
        functools.partial(_nmm_kernel, swiglu=swiglu, tail=w_tail is not None,
                          w_rows_out=w_rows_out, ride=ride is not None),
        grid=(b // bb, t // tt, n_tiles),
        in_specs=in_specs + w_specs,
        out_specs=out_specs,
        out_shape=out_shape,
        scratch_shapes=scratch,
        compiler_params=_params(3),
        name=name,
    )(x, nw.reshape(1, 1, d), sc, sh, *ride_ops, *operands)
    return outs if ride is not None else outs[0]


def _mmr_kernel(*refs, k_sizes, ride):
    refs = list(refs)
    n_a = len(k_sizes)
    groups = [refs[:n_a + 2]]
    del refs[:n_a + 2]
    if ride:
        groups.append(refs[:n_a + 2])
        del refs[:n_a + 2]
    w_ref = refs[0]
    out_refs = refs[1:]
    offs = np.cumsum((0,) + tuple(k_sizes))
    ws = [w_ref[offs[i]:offs[i + 1], :].astype(BF16) for i in range(n_a)]

    def apply(group, o_ref):
        *a_refs, x_ref, g_ref = group
        bb, tt, tn = o_ref.shape
        acc = jnp.zeros((bb * tt, tn), F32)
        for a_ref, w, ks in zip(a_refs, ws, k_sizes):
            acc = acc + jnp.dot(a_ref[...].reshape(bb * tt, ks), w, preferred_element_type=F32)
        o_ref[...] = x_ref[...] + g_ref[...] * acc.reshape(bb, tt, tn)

    apply(groups[0], out_refs[0])
    if ride:
        pl.when(_first_row_tile())(lambda: apply(groups[1], out_refs[1]))


def _matmul_residual(a_list, w, layer, x, g, *, bb, tt, tn, name, ride=None):
    b, t, d = x.shape
    k_sizes = tuple(a.shape[-1] for a in a_list)
    k_total = sum(k_sizes)
    n_tiles = d // tn
    in_specs = ([pl.BlockSpec((bb, tt, ks), lambda i, j, k: (i, j, 0), **_row_block_mode(bb * tt))
                 for ks in k_sizes]
                + [pl.BlockSpec((bb, tt, tn), lambda i, j, k: (i, j, k)),
                   pl.BlockSpec((bb, 1, tn), lambda i, j, k: (i, 0, k))])
    out_specs = [pl.BlockSpec((bb, tt, tn), lambda i, j, k: (i, j, k))]
    out_shape = [jax.ShapeDtypeStruct((b, t, d), F32)]
    ride_ops = ()
    if ride is not None:
        ra_list, rx, rg = ride
        rb, rt, _ = rx.shape
        ride_ops = (*ra_list, rx, rg)
        col = _ride_col_tile(n_tiles)
        in_specs += ([pl.BlockSpec((rb, rt, ks), lambda i, j, k: (0, 0, 0),
                                   pipeline_mode=pl.Buffered(1)) for ks in k_sizes]
                     + [pl.BlockSpec((rb, rt, tn), col), pl.BlockSpec((rb, 1, tn), col)])
        out_specs.append(pl.BlockSpec((rb, rt, tn), col))
        out_shape.append(jax.ShapeDtypeStruct(rx.shape, F32))
    outs = pl.pallas_call(
        functools.partial(_mmr_kernel, k_sizes=k_sizes, ride=ride is not None),
        grid=(b // bb, t // tt, n_tiles),
        in_specs=in_specs + [pl.BlockSpec((None, k_total, tn), lambda i, j, k: (layer, 0, k))],
        out_specs=out_specs,
        out_shape=out_shape,
        compiler_params=_params(3),
        name=name,
    )(*a_list, x, g, *ride_ops, w)
    return outs if ride is not None else outs[0]


def _swap_halves(x, half):
    w = x.shape[-1]
    if w > LANES:
        return jnp.concatenate([_swap_halves(x[..., c * LANES:(c + 1) * LANES], half)
                                for c in range(w // LANES)], axis=-1)
    if w == 2 * half:
        return pltpu.roll(x, half, x.ndim - 1)
    lane = lax.broadcasted_iota(jnp.int32, x.shape, x.ndim - 1)
    first = (lane & half) == 0
    return jnp.where(first, pltpu.roll(x, w - half, x.ndim - 1), pltpu.roll(x, half, x.ndim - 1))


def _lane_aligned(slab, offset, width):
    return pltpu.roll(slab, slab.shape[-1] - offset, slab.ndim - 1)[:, :width]


def _rope(x, cos, sin_signed, half):
    reps = x.shape[-1] // cos.shape[-1]
    if reps > 1:
        cos = jnp.concatenate([cos] * reps, axis=-1)
        sin_signed = jnp.concatenate([sin_signed] * reps, axis=-1)
    return x * cos + _swap_halves(x, half) * sin_signed


def _conv_carry_init(state_ref, cbuf_ref):
    cbuf_ref[0:SUBLANES, :] = jnp.zeros((SUBLANES, cbuf_ref.shape[1]), F32)
    cbuf_ref[SUBLANES - (CONV_W - 1):SUBLANES, :] = state_ref[0]


def _conv_state(tail):
    return tail[SUBLANES - (CONV_W - 1):, :]


def _causal_conv_tile(u, w_ref, b_ref, cbuf_ref):
    tt = u.shape[0]
    cbuf_ref[SUBLANES:SUBLANES + tt, :] = u
    full = cbuf_ref[...]
    out = b_ref[...]
    for j in range(CONV_W):
        shift = CONV_W - 1 - j
        rows = pltpu.roll(full, shift, 0) if shift else full
        out = out + rows[SUBLANES:SUBLANES + tt, :] * w_ref[j:j + 1, :]
    tail = cbuf_ref[tt:tt + SUBLANES, :]
    cbuf_ref[0:SUBLANES, :] = tail
    return out, tail


def _causal_conv(u, state_ref, w_ref, b_ref, cbuf_ref, new_ref, first, last):
    pl.when(first)(lambda: _conv_carry_init(state_ref, cbuf_ref))
    out, tail = _causal_conv_tile(u, w_ref, b_ref, cbuf_ref)

    @pl.when(last)
    def _():
        new_ref[0] = _conv_state(tail)

    return out


def _post0_kernel(p_ref, pdt_ref, cq_ref, sq_ref, ci_ref, si_ref, qn_ref, kn_ref, dtb_ref,
                  q_ref, qi_ref, k_ref, v_ref, ki_ref, sm_ref, kb_ref, vb_ref, kib_ref):
    cq, sq, ci, si = cq_ref[...], sq_ref[...], ci_ref[...], si_ref[...]

    def head_norm_rope(x, w):
        ms = jnp.mean(x * x, axis=-1, keepdims=True)
        return _rope(x * lax.rsqrt(ms + EPS) * w, cq, sq, HEAD_DIM // 2)

    for h in range(ATT_HEADS):
        xh = p_ref[0, :, OFF_Q + h * HEAD_DIM:OFF_Q + (h + 1) * HEAD_DIM]
        q_ref[0, :, h * HEAD_DIM:(h + 1) * HEAD_DIM] = (head_norm_rope(xh, qn_ref[...])
                                                         * Q_SCALE).astype(BF16)
    ones = jnp.ones((p_ref.shape[1], HEAD_DIM), BF16)
    for h in range(KV_HEADS):
        xh = p_ref[0, :, OFF_K + h * HEAD_DIM:OFF_K + (h + 1) * HEAD_DIM]
        kh = head_norm_rope(xh, kn_ref[...])
        vh = p_ref[0, :, OFF_V + h * HEAD_DIM:OFF_V + (h + 1) * HEAD_DIM]
        k_ref[0, pl.ds(h, xh.shape[0], stride=KV_HEADS), :] = kh
        v_ref[0, pl.ds(h, xh.shape[0], stride=KV_HEADS), :] = vh
        kb_ref[0, :, h * HEAD_DIM:(h + 1) * HEAD_DIM] = kh.astype(BF16)
        vb_ref[0, :, 2 * h * HEAD_DIM:(2 * h + 1) * HEAD_DIM] = vh.astype(BF16)
        vb_ref[0, :, (2 * h + 1) * HEAD_DIM:(2 * h + 2) * HEAD_DIM] = ones

    qi = p_ref[0, :, OFF_QI:OFF_QI + QI_WIDTH]
    qi_ref[0] = _rope(qi, ci, si, IDX_DIM // 2).astype(BF16)

    sm = p_ref[0, :, OFF_SMALL:OFF_SMALL + LANES]
    lane = lax.broadcasted_iota(jnp.int32, sm.shape, 1)
    ki_part = _rope(sm, ci, si, IDX_DIM // 2)
    wi_part = sm * (IDX_HEADS ** -0.5 * IDX_DIM ** -0.5)
    dt_part = _softplus(pdt_ref[0] + dtb_ref[...])
    sm_out = jnp.where(lane < SM_WI, ki_part,
                       jnp.where(lane < SM_DT, wi_part,
                                 jnp.where(lane < SM_DT + SSD_HEADS, dt_part, 0.0)))
    sm_ref[0] = sm_out
    ki_ref[0] = sm_out[:, :IDX_DIM]
    kib_ref[0] = sm_out[:, :IDX_DIM].astype(BF16)


def _post0(proj, tabs, q_norm_w, k_norm_w, dt_bias_pad, *, tt):
    b, t, n = proj.shape
    tok = lambda w, col=0: pl.BlockSpec((1, tt, w), lambda i, j: (i, j, col))
    tab = pl.BlockSpec((tt, LANES), lambda i, j: (j, 0))
    full2 = lambda a: pl.BlockSpec(a.shape, lambda i, j: (0, 0))
    out_shapes = [
        jax.ShapeDtypeStruct((b, t, ATT_WIDTH), BF16),
        jax.ShapeDtypeStruct((b, t, QI_WIDTH), BF16),
        jax.ShapeDtypeStruct((b, t * KV_HEADS, HEAD_DIM), F32),
        jax.ShapeDtypeStruct((b, t * KV_HEADS, HEAD_DIM), F32),
        jax.ShapeDtypeStruct((b, t, IDX_DIM), F32),
        jax.ShapeDtypeStruct((b, t, LANES), F32),
        jax.ShapeDtypeStruct((b, t, KV_WIDTH), BF16),
        jax.ShapeDtypeStruct((b, t, VEXT_WIDTH), BF16),
        jax.ShapeDtypeStruct((b, t, IDX_DIM), BF16),
    ]
    kv_rows = pl.BlockSpec((1, tt * KV_HEADS, HEAD_DIM), lambda i, j: (i, j, 0))
    out_specs = [tok(ATT_WIDTH), tok(QI_WIDTH), kv_rows, kv_rows, tok(IDX_DIM),
                 tok(LANES), tok(KV_WIDTH), tok(VEXT_WIDTH), tok(IDX_DIM)]
    qn = q_norm_w.reshape(1, HEAD_DIM)
    kn = k_norm_w.reshape(1, HEAD_DIM)
    head = OFF_SMALL + LANES
    return pl.pallas_call(
        _post0_kernel,
        grid=(b, t // tt),
        in_specs=[tok(head), tok(LANES, (OFF_DT - SM_DT) // LANES), tab, tab, tab, tab, full2(qn),
                  full2(kn), full2(dt_bias_pad)],
        out_specs=out_specs,
        out_shape=out_shapes,
        compiler_params=_params(2),
        name="post_proj0",
    )(proj, proj, *tabs, qn, kn, dt_bias_pad)


def _dsa_kernel(q_ref, qi_ref, sm_ref, k_ref, v_ref, ki_ref, *rest, qb, tk, n_keys, past, topk,
                n_bisect, keys_on_sublanes):
    if past:
        pk_ref, pv_ref, pki_ref = rest[:3]
        rest = rest[3:]
    (o_ref, sc_ref, acc_ref, qs_ref, s0_ref, s1_ref, mt0_ref, mt1_ref, m_ref) = rest[:9]
    if past:
        kall_ref, vall_ref, kiall_ref = rest[9:]
        t_new = k_ref.shape[1]
        pad = kall_ref.shape[0] - n_keys
        for g in range(KV_HEADS):
            head_rows = pl.ds(g, past, stride=KV_HEADS)
            kall_ref[0:past, g * HEAD_DIM:(g + 1) * HEAD_DIM] = pk_ref[0, head_rows, :].astype(BF16)
            vall_ref[0:past, 2 * g * HEAD_DIM:(2 * g + 1) * HEAD_DIM] = pv_ref[0, head_rows, :].astype(BF16)
            vall_ref[0:past, (2 * g + 1) * HEAD_DIM:(2 * g + 2) * HEAD_DIM] = jnp.ones(
                (past, HEAD_DIM), BF16)
        kiall_ref[0:past, :] = pki_ref[0].astype(BF16)
        for dst, src in ((kall_ref, k_ref), (vall_ref, v_ref), (kiall_ref, ki_ref)):
            dst[past:past + t_new, :] = src[0]
            dst[past + t_new:, :] = jnp.zeros((pad, dst.shape[1]), BF16)
        k_src, v_src, ki_src = kall_ref, vall_ref, kiall_ref
    else:
        k_src, v_src, ki_src = k_ref.at[0], v_ref.at[0], ki_ref.at[0]
    kax = 0 if keys_on_sublanes else 1
    tile_shape = (tk, qb) if keys_on_sublanes else (qb, tk)
    vec_shape = (1, qb) if keys_on_sublanes else (qb, 1)
    n_acc = 8
    part_shape = (n_acc, SUBLANES, qb) if keys_on_sublanes else (qb, LANES)
    j = pl.program_id(1)
    pos0 = past + j * qb
    q_chunk = lax.shift_right_logical(pos0 + lax.broadcasted_iota(jnp.int32, vec_shape, 1 - kax),
                                      CHUNK_SHIFT)
    n_valid = jnp.minimum((q_chunk + 1) * CHUNK, n_keys)
    max_valid = jnp.minimum(((pos0 + qb - 1) // CHUNK + 1) * CHUNK, n_keys)
    nkt = (max_valid + tk - 1) // tk
    kf = float(topk)
    active_f = jnp.where(n_valid > topk, 1.0, 0.0)

    def key_index(kt):
        return kt * tk + lax.broadcasted_iota(jnp.int32, tile_shape, kax)

    nt_dims = (((1,), (1,)), ((), ()))
    for h in range(IDX_HEADS):
        qs_ref[h * qb:(h + 1) * qb, :] = qi_ref[0, :, h * IDX_DIM:(h + 1) * IDX_DIM]
    if keys_on_sublanes:
        wi_t = sm_ref[0].T
        head_w = lambda h: wi_t[SM_WI + h:SM_WI + h + 1, :]
    else:
        wis = sm_ref[0][:, SM_WI:SM_WI + IDX_HEADS]
        head_w = lambda h: wis[:, h:h + 1]

    def score_tile(kt, carry):
        kit = ki_src[pl.ds(pl.multiple_of(kt * tk, tk), tk), :]
        acc = jnp.zeros(tile_shape, F32)
        if keys_on_sublanes:
            for pr in range(IDX_HEADS // 2):
                s2 = lax.dot_general(kit, qs_ref[2 * pr * qb:(2 * pr + 2) * qb, :], nt_dims,
                                     preferred_element_type=F32)
                for e in range(2):
                    acc = acc + head_w(2 * pr + e) * jnp.maximum(s2[:, e * qb:(e + 1) * qb], 0.0)
        else:
            s_all = lax.dot_general(qs_ref[...], kit, nt_dims, preferred_element_type=F32)
            for h in range(IDX_HEADS):
                acc = acc + head_w(h) * jnp.maximum(s_all[h * qb:(h + 1) * qb, :], 0.0)
        valid = key_index(kt) < n_valid
        sc_ref[kt] = jnp.where(valid, acc, -jnp.inf)
        mn, mx = carry
        return (jnp.minimum(mn, fold(jnp.where(valid, acc, jnp.inf), jnp.minimum, jnp.min)),
                jnp.maximum(mx, fold(jnp.where(valid, acc, -jnp.inf), jnp.maximum, jnp.max)))

    def fold(m, op2, red):
        if keys_on_sublanes:
            return red(m.reshape(tk // (n_acc * SUBLANES), n_acc, SUBLANES, qb), axis=0)
        f = m[:, 0:LANES]
        for c in range(1, tk // LANES):
            f = op2(f, m[:, c * LANES:(c + 1) * LANES])
        return f

    def finish(part, red):
        if keys_on_sublanes:
            part = red(part, axis=0)
        return red(part, axis=kax, keepdims=True)

    def reduce_tiles(make, op2, red, init):
        def body(kt, part):
            return op2(part, fold(make(sc_ref[kt], kt), op2, red))
        return finish(lax.fori_loop(0, nkt, body, jnp.full(part_shape, init, F32)), red)

    def count(pred):
        return reduce_tiles(lambda t, kt: jnp.where(pred(t, kt), 1.0, 0.0), jnp.add, jnp.sum, 0.0)

    def row_max(pred):
        return reduce_tiles(lambda t, kt: jnp.where(pred(t, kt), t, -jnp.inf), jnp.maximum, jnp.max,
                            -jnp.inf)

    mn_part, mx_part = lax.fori_loop(0, nkt, score_tile, (jnp.full(part_shape, jnp.inf, F32),
                                                          jnp.full(part_shape, -jnp.inf, F32)))
    lo0, hi0 = finish(mn_part, jnp.min), finish(mx_part, jnp.max)

    def any_set(flag_f):
        return jnp.max(flag_f) > 0.0

    def bisect(_, state):
        lo, hi, c_lo = state
        mid = lo + (hi - lo) * 0.5
        c = count(lambda t, kt: t >= mid)
        ge = c >= kf
        return jnp.where(ge, mid, lo), jnp.where(ge, hi, mid), jnp.where(ge, c, c_lo)

    lo, hi, c_lo = lax.fori_loop(0, n_bisect, bisect, (lo0, hi0, n_valid.astype(F32)))
    found_f = jnp.where(c_lo == kf, active_f, 0.0)
    v_found = reduce_tiles(lambda t, kt: jnp.where(t >= lo, t, jnp.inf), jnp.minimum, jnp.min, jnp.inf)

    def walk_down():
        w0 = row_max(lambda t, kt: t <= hi)
        c0 = count(lambda t, kt: t >= w0)

        def walk_cond(state):
            w, c = state
            return any_set(jnp.where(c < kf, active_f, 0.0))

        def walk_body(state):
            w, c = state
            w2 = row_max(lambda t, kt: t < w)
            c2 = count(lambda t, kt: t >= w2)
            upd = c < kf
            return jnp.where(upd, w2, w), jnp.where(upd, c2, c)

        return lax.while_loop(walk_cond, walk_body,
                              (jnp.where(found_f > 0.5, v_found, w0),
                               jnp.where(found_f > 0.5, kf, c0)))

    w, c_ge = lax.cond(any_set(active_f - found_f), walk_down,
                       lambda: (v_found, jnp.full(vec_shape, kf, F32)))
    thr = jnp.where(active_f > 0.5, w, -jnp.inf)
    tied_f = jnp.where(c_ge > kf, active_f, 0.0)

    def last_kept_tie():
        need = kf - count(lambda t, kt: t > thr)
        n_steps = max(1, int(np.ceil(np.log2(sc_ref.shape[0] * tk))) + 1)

        def step(_, lh):
            lo_i, hi_i = lh
            mid = lax.shift_right_arithmetic(lo_i + hi_i, 1)
            ge = count(lambda t, kt: (t == thr) & (key_index(kt) <= mid)) >= need
            return jnp.where(ge, lo_i, mid), jnp.where(ge, mid, hi_i)

        lo_i = jnp.full(vec_shape, -1, jnp.int32)
        hi_i = jnp.full(vec_shape, sc_ref.shape[0] * tk - 1, jnp.int32)
        return lax.fori_loop(0, n_steps, step, (lo_i, hi_i))[1]

    last_tie = lax.cond(any_set(tied_f), last_kept_tie, lambda: jnp.zeros(vec_shape, jnp.int32))

    def bias_tile(kt):
        t, kidx = sc_ref[kt], key_index(kt)
        keep_tie = (kidx <= last_tie) | (tied_f < 0.5)
        sel = ((t > thr) | ((t == thr) & keep_tie)) & (kidx < n_valid)
        bias = jnp.where(sel, 0.0, NEG_BIG)
        return bias.T if keys_on_sublanes else bias

    rep = ATT_HEADS // KV_HEADS
    qgs = [jnp.concatenate(
        [q_ref[0, :, (g * rep + r) * HEAD_DIM:(g * rep + r + 1) * HEAD_DIM] for r in range(rep)],
        axis=0) for g in range(KV_HEADS)]
    acc_ref[...] = jnp.zeros(acc_ref.shape, F32)
    bufs = ((s0_ref, mt0_ref), (s1_ref, mt1_ref))

    def qk_tile(kt, buf):
        s_ref, mt_ref = bufs[buf]
        row0 = pl.multiple_of(kt * tk, tk)
        bias = bias_tile(kt)
        for g in range(KV_HEADS):
            kg = k_src[pl.ds(row0, tk), g * HEAD_DIM:(g + 1) * HEAD_DIM]
            s = lax.dot_general(qgs[g], kg, nt_dims, preferred_element_type=F32)
            s = (s.reshape(rep, qb, tk) + bias[None]).reshape(rep * qb, tk)
            s_ref[g] = s
            mt_ref[g] = jnp.broadcast_to(jnp.max(s, axis=-1, keepdims=True), (rep * qb, LANES))

    def lane_tile(x, width):
        return jnp.concatenate([x] * (width // LANES), axis=1)

    def att_tile(kt, buf, prefetch):
        if prefetch:
            qk_tile(kt + 1, 1 - buf)
        s_ref, mt_ref = bufs[buf]
        row0 = pl.multiple_of(kt * tk, tk)
        for g in range(KV_HEADS):
            vg = v_src[pl.ds(row0, tk), 2 * g * HEAD_DIM:(2 * g + 2) * HEAD_DIM]
            m_old = m_ref[g]
            m_new = jnp.maximum(m_old, mt_ref[g])
            m_ref[g] = m_new
            alpha = jnp.exp2(m_old - m_new)
            p = jnp.exp2(s_ref[g] - lane_tile(m_new, tk)).astype(BF16)
            acc_ref[g] = (lane_tile(alpha, 2 * HEAD_DIM) * acc_ref[g]
                          + jnp.dot(p, vg, preferred_element_type=F32))

    def tile_pair(pi, carry):
        att_tile(2 * pi, 0, True)
        att_tile(2 * pi + 1, 1, True)
        return carry

    def last_two():
        att_tile(nkt - 2, 0, True)
        att_tile(nkt - 1, 1, False)

    def last_one():
        att_tile(nkt - 1, 0, False)

    m_ref[...] = jnp.full(m_ref.shape, NEG_BIG, F32)
    qk_tile(0, 0)
    lax.fori_loop(0, (nkt - 1) // 2, tile_pair, 0)
    lax.cond((nkt & 1) == 0, last_two, last_one)
    for g in range(KV_HEADS):
        acc = acc_ref[g]
        o = acc[:, :HEAD_DIM] / acc[:, HEAD_DIM:]
        for r in range(rep):
            h = g * rep + r
            o_ref[0, :, h * HEAD_DIM:(h + 1) * HEAD_DIM] = o[r * qb:(r + 1) * qb].astype(BF16)


def _dsa_attention(q, qi, sm, k_new, v_new, ki_new, cache, *, qb, tk):
    b, t, _ = q.shape
    past = 0 if cache is None else cache[2].shape[1]
    n_keys = past + t
    nk_pad = -(-n_keys // tk) * tk
    topk = min(TOPK_MAX, n_keys // 4)
    keys_on_sublanes = qb == LANES
    rep = ATT_HEADS // KV_HEADS
    tokq = lambda w: pl.BlockSpec((1, qb, w), lambda i, j: (i, j, 0))
    keys = lambda w: pl.BlockSpec((1, t, w), lambda i, j: (i, 0, 0))
    tile_shape = (tk, qb) if keys_on_sublanes else (qb, tk)
    cache_specs, cache_scratch = [], []
    if cache is not None:
        kv = pl.BlockSpec((1, past * KV_HEADS, HEAD_DIM), lambda i, j: (i, 0, 0))
        cache_specs = [kv, kv, pl.BlockSpec((1, past, IDX_DIM), lambda i, j: (i, 0, 0))]
        cache_scratch = [pltpu.VMEM((nk_pad, w), BF16) for w in (KV_WIDTH, VEXT_WIDTH, IDX_DIM)]
    return pl.pallas_call(
        functools.partial(_dsa_kernel, qb=qb, tk=tk, n_keys=n_keys, past=past, topk=topk,
                          n_bisect=20, keys_on_sublanes=keys_on_sublanes),
        grid=(b, t // qb),
        in_specs=[tokq(ATT_WIDTH), tokq(QI_WIDTH), tokq(LANES), keys(KV_WIDTH), keys(VEXT_WIDTH),
                  keys(IDX_DIM)] + cache_specs,
        out_specs=tokq(ATT_WIDTH),
        out_shape=jax.ShapeDtypeStruct((b, t, ATT_WIDTH), BF16),
        scratch_shapes=[pltpu.VMEM((nk_pad // tk,) + tile_shape, F32),
                        pltpu.VMEM((KV_HEADS, rep * qb, 2 * HEAD_DIM), F32),
                        pltpu.VMEM((IDX_HEADS * qb, IDX_DIM), BF16),
                        pltpu.VMEM((KV_HEADS, rep * qb, tk), F32),
                        pltpu.VMEM((KV_HEADS, rep * qb, tk), F32),
                        pltpu.VMEM((KV_HEADS, rep * qb, LANES), F32),
                        pltpu.VMEM((KV_HEADS, rep * qb, LANES), F32),
                        pltpu.VMEM((KV_HEADS, rep * qb, LANES), F32)] + cache_scratch,
        compiler_params=_params(2),
        name="dsa_attention",
    )(q, qi, sm, k_new, v_new, ki_new, *(cache or ()))


def _ssd_kernel(pfirst_ref, pnext_ref, z_ref, sm_ref, st_ref, cs_ref, cw_ref, cb_ref, alog_ref,
                dexp_ref, nw_ref, e_ref, y_ref, stout_ref, cnew_ref, ht_ref, yi_ref, cbuf_ref, xa_ref,
                xb_ref):
    c = pl.program_id(1)
    nc = pl.num_programs(1)
    L = pfirst_ref.shape[1]
    hp = SSD_INNER // SSD_GROUPS

    def conv_act(p_ref):
        raw = _lane_aligned(p_ref[0, :, :OFF_DT - OFF_XBC + LANES], SM_DT, SSD_CONV_DIM)
        out, tail = _causal_conv_tile(raw, cw_ref, cb_ref, cbuf_ref)
        return _silu(out), tail

    @pl.when(c == 0)
    def _():
        ht_ref[...] = st_ref[0].reshape(SSD_INNER, SSD_STATE).T
        _conv_carry_init(cs_ref, cbuf_ref)
        act, tail = conv_act(pfirst_ref)
        xa_ref[...] = act

        @pl.when(nc == 1)
        def _():
            cnew_ref[0] = _conv_state(tail)

    def chunk(xbc_ref, next_ref):
        _ssd_chunk(xbc_ref, z_ref, sm_ref, alog_ref, dexp_ref, nw_ref, e_ref, y_ref, ht_ref, yi_ref)
        act, tail = conv_act(pnext_ref)
        next_ref[...] = act
        return tail

    tail = lax.cond((c & 1) == 0, lambda: chunk(xa_ref, xb_ref), lambda: chunk(xb_ref, xa_ref))

    @pl.when(c + 1 == nc - 1)
    def _():
        cnew_ref[0] = _conv_state(tail)

    @pl.when(c == nc - 1)
    def _():
        stout_ref[0] = ht_ref[...].T.reshape(SSD_HEADS, SSD_HEADDIM, SSD_STATE)


def _ssd_chunk(xbc_ref, z_ref, sm_ref, alog_ref, dexp_ref, nw_ref, e_ref, y_ref, ht_ref, yi_ref):
    L = xbc_ref.shape[0]
    hp = SSD_INNER // SSD_GROUPS
    xs = xbc_ref[:, 0:SSD_INNER]
    sm = sm_ref[0]
    lane = lax.broadcasted_iota(jnp.int32, (1, LANES), 1)
    is_dt = (lane >= SM_DT) & (lane < SM_DT + SSD_HEADS)
    a_neg = jnp.where(is_dt, -jnp.exp(alog_ref[...]), 0.0)
    dt = jnp.where(is_dt, sm, 0.0)
    rows = lax.broadcasted_iota(jnp.int32, (L, L), 0)
    cols = lax.broadcasted_iota(jnp.int32, (L, L), 1)
    tri = cols <= rows
    e = e_ref[...]
    cum = jnp.dot(tri.astype(F32), dt * a_neg, precision=HIGHEST, preferred_element_type=F32)
    eye = (lax.broadcasted_iota(jnp.int32, (LANES, LANES), 0)
           == lax.broadcasted_iota(jnp.int32, (LANES, LANES), 1)).astype(F32)
    cum_t = lax.dot_general(eye, cum, (((1,), (1,)), ((), ())), precision=HIGHEST,
                            preferred_element_type=F32)
    both = jnp.concatenate([cum, dt], axis=0)
    hi = both.astype(BF16)
    rest = both - hi.astype(F32)
    mid = rest.astype(BF16)
    low = (rest - mid.astype(F32)).astype(BF16)
    both_e = (jnp.dot(hi, e, preferred_element_type=F32) + jnp.dot(mid, e, preferred_element_type=F32)
              + jnp.dot(low, e, preferred_element_type=F32))
    cum_e, dt_e = both_e[:L], both_e[L:]
    last = cum_e[L - 1:L, :]
    xdt = (xs * dt_e).astype(BF16)
    xw = (xs * (jnp.exp(last - cum_e) * dt_e)).astype(BF16)
    ht_old = ht_ref[...]

    for g in range(SSD_GROUPS):
        bg = xbc_ref[:, SSD_INNER + g * SSD_STATE:SSD_INNER + (g + 1) * SSD_STATE]
        cg = xbc_ref[:, SSD_INNER + (SSD_GROUPS + g) * SSD_STATE:
                     SSD_INNER + (SSD_GROUPS + g + 1) * SSD_STATE]
        bgb, cgb = bg.astype(BF16), cg.astype(BF16)
        cb = lax.dot_general(cgb, bgb, (((1,), (1,)), ((), ())), preferred_element_type=F32)
        for hh in range(SSD_HEADS // SSD_GROUPS):
            h = g * (SSD_HEADS // SSD_GROUPS) + hh
            seg = cum[:, SM_DT + h:SM_DT + h + 1] - cum_t[SM_DT + h:SM_DT + h + 1, :]
            decay = jnp.exp(jnp.where(tri, seg, -jnp.inf))
            wts = (cb * decay).astype(BF16)
            yi_ref[:, h * SSD_HEADDIM:(h + 1) * SSD_HEADDIM] = jnp.dot(
                wts, xdt[:, h * SSD_HEADDIM:(h + 1) * SSD_HEADDIM], preferred_element_type=F32)
        ht_g = ht_old[:, g * hp:(g + 1) * hp]
        y_inter = jnp.dot(cgb, ht_g.astype(BF16), preferred_element_type=F32)
        yi_ref[:, g * hp:(g + 1) * hp] = (yi_ref[:, g * hp:(g + 1) * hp]
                                          + y_inter * jnp.exp(cum_e[:, g * hp:(g + 1) * hp]))
        upd = jnp.dot(bg.T.astype(BF16), xw[:, g * hp:(g + 1) * hp],
                      preferred_element_type=F32)
        ht_ref[:, g * hp:(g + 1) * hp] = ht_g * jnp.exp(last[:, g * hp:(g + 1) * hp]) + upd

    y = yi_ref[...] + dexp_ref[...] * xs
    y = y * _silu(_lane_aligned(z_ref[0], SM_DT, SSD_INNER))
    for g in range(SSD_GROUPS):
        yg = y[:, g * hp:(g + 1) * hp]
        ms = jnp.mean(yg * yg, axis=-1, keepdims=True)
        y_ref[0, :, g * hp:(g + 1) * hp] = (yg * lax.rsqrt(ms + EPS)
                                            * nw_ref[:, g * hp:(g + 1) * hp]).astype(BF16)


def _ssd(proj, sm, state, conv_state, conv_w, conv_b, a_log_pad, d_exp, norm_w, e_mat, *, chunk):
    b, t, _ = proj.shape
    n_chunks = t // chunk
    xbc_col = (OFF_XBC - SM_DT) // XBC_BLOCK
    tok = lambda w, col=0: pl.BlockSpec((1, chunk, w), lambda i, j: (i, j, col))
    xbc_first = pl.BlockSpec((1, chunk, XBC_BLOCK), lambda i, j: (i, 0, xbc_col))
    xbc_next = pl.BlockSpec((1, chunk, XBC_BLOCK),
                            lambda i, j: (i, jnp.minimum(j + 1, n_chunks - 1), xbc_col))
    full2 = lambda a: pl.BlockSpec(a.shape, lambda i, j: (0, 0))
    st = pl.BlockSpec((1, SSD_HEADS, SSD_HEADDIM, SSD_STATE), lambda i, j: (i, 0, 0, 0))
    cst = pl.BlockSpec((1, CONV_W - 1, SSD_CONV_DIM), lambda i, j: (i, 0, 0))
    nw = norm_w.reshape(1, SSD_INNER)
    cb = conv_b.reshape(1, SSD_CONV_DIM)
    return pl.pallas_call(
        _ssd_kernel,
        grid=(b, n_chunks),
        in_specs=[xbc_first, xbc_next,
                  tok(Z_BLOCK, (OFF_Z - SM_DT) // Z_BLOCK), tok(LANES), st, cst, full2(conv_w),
                  full2(cb), full2(a_log_pad), full2(d_exp), full2(nw), full2(e_mat)],
        out_specs=[tok(SSD_INNER), st, cst],
        out_shape=[jax.ShapeDtypeStruct((b, t, SSD_INNER), BF16),
                   jax.ShapeDtypeStruct(state.shape, F32),
                   jax.ShapeDtypeStruct((b, CONV_W - 1, SSD_CONV_DIM), F32)],
        scratch_shapes=[pltpu.VMEM((SSD_STATE, SSD_INNER), F32),
                        pltpu.VMEM((chunk, SSD_INNER), F32),
                        pltpu.VMEM((chunk + SUBLANES, SSD_CONV_DIM), F32),
                        pltpu.VMEM((chunk, SSD_CONV_DIM), F32),
                        pltpu.VMEM((chunk, SSD_CONV_DIM), F32)],
        compiler_params=_params(2),
        name="ssd_scan",
    )(proj, proj, proj, sm, state, conv_state, conv_w, cb, a_log_pad, d_exp, nw, e_mat)


def _lru_kernel(gate_ref, xin_ref, cw_ref, cb_ref, wa_ref, wx_ref, ba_ref, bx_ref, lam_ref,
                h0_ref, cs_ref, y_ref, hout_ref, cnew_ref, cbuf_ref, a_ref, u_ref, hp_ref):
    t = pl.program_id(1)
    nt = pl.num_programs(1)
    tt = xin_ref.shape[1]
    xc = _causal_conv(xin_ref[0], cs_ref, cw_ref, cb_ref, cbuf_ref, cnew_ref, t == 0, t == nt - 1)

    @pl.when(t == 0)
    def _():
        hp_ref[...] = h0_ref[0]

    for kb in range(LRU_BLOCKS):
        sl = slice(kb * LRU_BLOCK_DIM, (kb + 1) * LRU_BLOCK_DIM)
        xb = xc[:, sl].astype(BF16)
        a_ref[:, sl] = jnp.dot(xb, wa_ref[kb].astype(BF16), preferred_element_type=F32)
        u_ref[:, sl] = jnp.dot(xb, wx_ref[kb].astype(BF16), preferred_element_type=F32)
    r = jax.nn.sigmoid(a_ref[...] + ba_ref[...])
    i = jax.nn.sigmoid(u_ref[...] + bx_ref[...])
    log_a = -LRU_C * r * _softplus(-lam_ref[...])
    a = jnp.exp(log_a)
    u = jnp.sqrt(-jnp.tanh(log_a) * (1.0 + a * a)) * (i * xc)

    width = a.shape[1]
    a = a.reshape(tt // SUBLANES, SUBLANES, width)
    u = u.reshape(tt // SUBLANES, SUBLANES, width)
    sub = lax.broadcasted_iota(jnp.int32, (1, SUBLANES, width), 1)
    for s in (1, 2, 4):
        m = sub >= s
        a_sh = pltpu.roll(a, s, 1)
        u_sh = pltpu.roll(u, s, 1)
        u = jnp.where(m, a * u_sh + u, u)
        a = jnp.where(m, a * a_sh, a)
    a_ref[...] = a.reshape(tt, width)
    u_ref[...] = u.reshape(tt, width)

    def group(gi, hprev):
        r0 = pl.multiple_of(gi * SUBLANES, SUBLANES)
        hs = a_ref[pl.ds(r0, SUBLANES), :] * hprev + u_ref[pl.ds(r0, SUBLANES), :]
        u_ref[pl.ds(r0, SUBLANES), :] = hs
        return hs[SUBLANES - 1:SUBLANES, :]

    h_last = lax.fori_loop(0, tt // SUBLANES, group, hp_ref[...])
    hp_ref[...] = h_last
    gate = gate_ref[0]
    gelu = 0.5 * gate * (1.0 + jnp.tanh(np.sqrt(2.0 / np.pi) * (gate + 0.044715 * (gate * gate * gate))))
    y_ref[0] = (u_ref[...] * gelu).astype(BF16)

    @pl.when(t == nt - 1)
    def _():
        hout_ref[0] = h_last


def _lru(proj, conv_w, conv_b, w_a, b_a, w_x, b_x, lam, h0, conv_state, *, tt):
    b, t, _ = proj.shape
    w = LRU_WIDTH
    tok = lambda col: pl.BlockSpec((1, tt, w), lambda i, j: (i, j, col))
    full = lambda a: pl.BlockSpec(a.shape, lambda i, j: (0,) * a.ndim)
    row = lambda v: v.reshape(1, w)
    state = pl.BlockSpec((1, CONV_W - 1, w), lambda i, j: (i, 0, 0))
    hspec = pl.BlockSpec((1, 1, w), lambda i, j: (i, 0, 0))
    args = (proj, proj, conv_w, row(conv_b), w_a, w_x, row(b_a), row(b_x), row(lam), h0, conv_state)
    return pl.pallas_call(
        _lru_kernel,
        grid=(b, t // tt),
        in_specs=[tok(0), tok(1)] + [full(a) for a in args[2:9]] + [hspec, state],
        out_specs=[pl.BlockSpec((1, tt, w), lambda i, j: (i, j, 0)), hspec, state],
        out_shape=[jax.ShapeDtypeStruct((b, t, w), BF16),
                   jax.ShapeDtypeStruct((b, 1, w), F32),
                   jax.ShapeDtypeStruct((b, CONV_W - 1, w), F32)],
        scratch_shapes=[pltpu.VMEM((tt + SUBLANES, w), F32), pltpu.VMEM((tt, w), F32),
                        pltpu.VMEM((tt, w), F32), pltpu.VMEM((1, w), F32)],
        compiler_params=_params(2),
        name="rg_lru",
    )(*args)


def _rope_tables(pos, dim):
    half = dim // 2
    inv = np.power(ROPE_THETA, -np.arange(half, dtype=np.float64) / half)
    ang = pos.astype(np.float64)[:, None] * inv[None, :]
    cos = np.concatenate([np.cos(ang), np.cos(ang)], axis=-1)
    sin = np.concatenate([-np.sin(ang), np.sin(ang)], axis=-1)
    reps = LANES // dim
    return (jnp.asarray(np.tile(cos, (1, reps)), F32), jnp.asarray(np.tile(sin, (1, reps)), F32))


def _head_expansion():
    e = np.zeros((LANES, SSD_INNER), np.float32)
    for h in range(SSD_HEADS):
        e[SM_DT + h, h * SSD_HEADDIM:(h + 1) * SSD_HEADDIM] = 1.0
    return jnp.asarray(e, BF16)


def _pad_lanes(v, offset):
    out = jnp.zeros((1, LANES), F32)
    return out.at[0, offset:offset + v.shape[0]].set(v)


def _w_in_ab_tail(w_t):
    whole = AB_WIDTH // AB_TILE_N * AB_TILE_N
    return jnp.pad(w_t[whole:], ((0, AB_PAD_N - AB_WIDTH), (0, 0)))


def _layer0_mixers(proj, st, p, cfg):
    b, t, _ = proj.shape
    past = 0 if st["past_k"] is None else st["past_k"].shape[1]
    pos = np.arange(past, past + t)
    tabs = _rope_tables(pos, HEAD_DIM) + _rope_tables(pos, IDX_DIM)
    q, qi, k, v, ki, sm, kb, vbe, kib = _post0(proj, tabs, p["q_norm_w"][0], p["k_norm_w"][0],
                                               p["dt_bias_pad"], tt=cfg["post_tt"])
    cache = None if not past else (st["past_k"].reshape(b, past * KV_HEADS, HEAD_DIM),
                                   st["past_v"].reshape(b, past * KV_HEADS, HEAD_DIM), st["past_ki"])
    att = _dsa_attention(q, qi, sm, kb, vbe, kib, cache, qb=cfg["qb"], tk=cfg["tk"])
    y_ssd, ssm_new, ssm_conv_new = _ssd(
        proj, sm, st["ssm_h"], st["ssm_conv"], p["ssd_conv_w"][0], p["ssd_conv_b"][0], p["a_log_pad"],
        p["d_exp"], p["ssd_norm_w"][0], p["e_mat"], chunk=cfg["ssd_chunk"])
    kv_shape = (1, b, t, KV_HEADS, HEAD_DIM)
    return att, y_ssd, (k.reshape(kv_shape), v.reshape(kv_shape), ki[None], ssm_new[None],
                        ssm_conv_new[None])


def _layer1_mixer(proj, st, p, cfg):
    b = proj.shape[0]
    y_lru, lru_new, lru_conv_new = _lru(
        proj, p["lru_conv_w"][0], p["lru_conv_b"][0], p["lru_w_a"][0], p["lru_b_a"][0],
        p["lru_w_x"][0], p["lru_b_x"][0], p["lru_lambda"][0], st["lru_h"].reshape(b, 1, LRU_WIDTH),
        st["lru_conv"], tt=cfg["lru_tt"])
    return y_lru, (lru_new.reshape(1, b, LRU_WIDTH), lru_conv_new[None])


def _forward(xp, xs, mods_p, mods_s, st_p, st_s, p, cfg_p, cfg_s):
    bb, tt = cfg_p["bb"], cfg_p["tt"]

    def ffn(xp, xs, l):
        (_, _, _, sh_p, sc_p, g_p), (_, _, _, sh_s, sc_s, g_s) = mods_p[l], mods_s[l]
        gu_p, gu_s = _norm_mod_matmul(
            xp, p["norm_ffn_w"][l], sc_p, sh_p, [p["ffn_w_gate"], p["ffn_w_up"]], l, bb=bb,
            tt=cfg_p["tt_ffn"], tn=512, out_dtype=BF16, name="ffn_gate_up", ride=(xs, sc_s, sh_s))
        return _matmul_residual([gu_p], p["ffn_w_down"], l, xp, g_p, bb=bb, tt=cfg_p["tt_ffn"],
                                tn=256, name="ffn_down", ride=([gu_s], xs, g_s))

    (sh_p, sc_p, g_p, *_), (sh_s, sc_s, g_s, *_) = mods_p[0], mods_s[0]
    proj_p, proj_s = _norm_mod_matmul(
        xp, p["norm_mix_w"][0], sc_p, sh_p, [p["w_in_ab_t"]], 0, bb=bb, tt=cfg_p["tt_ab"],
        tn=AB_TILE_N, out_dtype=F32, name="in_proj_ab", w_tail=p["w_in_ab_tail"], w_rows_out=True,
        ride=(xs, sc_s, sh_s))
    att_p, y_p, outs0_p = _layer0_mixers(proj_p, st_p, p, cfg_p)
    att_s, y_s, outs0_s = _layer0_mixers(proj_s, st_s, p, cfg_s)
    xp, xs = _matmul_residual([att_p, y_p], p["w_out_ab"], 0, xp, g_p, bb=bb, tt=tt, tn=512,
                              name="out_proj_ab", ride=([att_s, y_s], xs, g_s))
    xp, xs = ffn(xp, xs, 0)

    (sh_p, sc_p, g_p, *_), (sh_s, sc_s, g_s, *_) = mods_p[1], mods_s[1]
    proj_p, proj_s = _norm_mod_matmul(
        xp, p["norm_mix_w"][1], sc_p, sh_p, [p["w_in_c"]], 0, bb=bb, tt=cfg_p["tt_ab"], tn=512,
        out_dtype=F32, name="in_proj_c", ride=(xs, sc_s, sh_s))
    y_p, outs1_p = _layer1_mixer(proj_p, st_p, p, cfg_p)
    y_s, outs1_s = _layer1_mixer(proj_s, st_s, p, cfg_s)
    xp, xs = _matmul_residual([y_p], p["w_out_c"], 0, xp, g_p, bb=bb, tt=tt, tn=512,
                              name="out_proj_c", ride=([y_s], xs, g_s))
    xp, xs = ffn(xp, xs, 1)
    return (xp,) + outs0_p + outs1_p, (xs,) + outs0_s + outs1_s


PROMPT_CFG = dict(bb=1, tt=2048, tt_ab=1024, tt_ffn=1024, post_tt=1024, qb=128, tk=512,
                  ssd_chunk=128, lru_tt=512)
SAMPLE_CFG = dict(post_tt=32, qb=32, tk=384, ssd_chunk=32, lru_tt=32)


def kernel(x_prompt, x_sample, cache_attn_k, cache_attn_v, cache_idx_k, state_ssm, state_ssm_conv,
           state_lru, state_lru_conv, c_prompt, c_sample, ada_w, ada_b, norm_mix_w, norm_ffn_w,
           w_in_ab, q_norm_w, k_norm_w, ssd_conv_w, ssd_conv_b, ssd_dt_bias, ssd_a_log, ssd_d,
           ssd_norm_w, w_out_ab, w_in_c, lru_conv_w, lru_conv_b, lru_w_a, lru_b_a, lru_w_x, lru_b_x,
           lru_lambda, w_out_c, ffn_w_gate, ffn_w_up, ffn_w_down):
    bp, bs = x_prompt.shape[0], x_sample.shape[0]
    p = dict(norm_mix_w=norm_mix_w, norm_ffn_w=norm_ffn_w, q_norm_w=q_norm_w, k_norm_w=k_norm_w,
             ssd_conv_w=ssd_conv_w, ssd_conv_b=ssd_conv_b, ssd_norm_w=ssd_norm_w, w_out_ab=w_out_ab,
             w_in_c=w_in_c, lru_conv_w=lru_conv_w, lru_conv_b=lru_conv_b, lru_w_a=lru_w_a,
             lru_b_a=lru_b_a, lru_w_x=lru_w_x, lru_b_x=lru_b_x, lru_lambda=lru_lambda,
             w_out_c=w_out_c, ffn_w_gate=ffn_w_gate, ffn_w_up=ffn_w_up, ffn_w_down=ffn_w_down)
    p["w_in_ab_t"] = jnp.swapaxes(w_in_ab, 1, 2)
    p["w_in_ab_tail"] = _w_in_ab_tail(p["w_in_ab_t"][0])
    p["dt_bias_pad"] = _pad_lanes(ssd_dt_bias[0], SM_DT)
    p["a_log_pad"] = _pad_lanes(ssd_a_log[0], SM_DT)
    p["d_exp"] = jnp.repeat(ssd_d[0], SSD_HEADDIM).reshape(1, SSD_INNER)
    p["e_mat"] = _head_expansion()

    m_rows = 32
    c_all = jnp.concatenate([c_prompt, c_sample, jnp.zeros((m_rows - bp - bs, D_MODEL), F32)], axis=0)
    mod = _modulation(c_all, ada_w, ada_b)

    def group_mods(r0, nb):
        return [[mod[l, r0:r0 + nb, i * D_MODEL:(i + 1) * D_MODEL].reshape(nb, 1, D_MODEL)
                 for i in range(6)] for l in range(mod.shape[0])]

    zeros = lambda *s: jnp.zeros(s, F32)
    st_p = dict(past_k=None, past_v=None, past_ki=None,
                ssm_h=zeros(bp, SSD_HEADS, SSD_HEADDIM, SSD_STATE),
                ssm_conv=zeros(bp, CONV_W - 1, SSD_CONV_DIM), lru_h=zeros(bp, LRU_WIDTH),
                lru_conv=zeros(bp, CONV_W - 1, LRU_WIDTH))
    st_s = dict(past_k=cache_attn_k[0], past_v=cache_attn_v[0], past_ki=cache_idx_k[0],
                ssm_h=state_ssm[0], ssm_conv=state_ssm_conv[0], lru_h=state_lru[0],
                lru_conv=state_lru_conv[0])
    out_p, out_s = _forward(x_prompt, x_sample, group_mods(0, bp), group_mods(bp, bs), st_p, st_s, p,
                            PROMPT_CFG, SAMPLE_CFG)
    return (out_p[0], out_s[0]) + out_p[1:] + out_s[1:]
```

```python
import functools

import numpy as np
import jax
import jax.numpy as jnp
from jax import lax
from jax.experimental import pallas as pl
from jax.experimental.pallas import tpu as pltpu

F32 = jnp.float32
BF16 = jnp.bfloat16
HIGHEST = lax.Precision.HIGHEST

D_MODEL = 2048
CHUNK = 64
CHUNK_SHIFT = 6
ATT_HEADS = 8
KV_HEADS = 2
HEAD_DIM = 128
IDX_HEADS = 16
IDX_DIM = 64
TOPK_MAX = 256
ROPE_THETA = 10000.0
SSD_HEADS = 16
SSD_HEADDIM = 64
SSD_GROUPS = 2
SSD_STATE = 128
SSD_INNER = SSD_HEADS * SSD_HEADDIM
SSD_CONV_DIM = SSD_INNER + 2 * SSD_GROUPS * SSD_STATE
CONV_W = 4
LRU_WIDTH = D_MODEL
LRU_BLOCKS = 16
LRU_BLOCK_DIM = LRU_WIDTH // LRU_BLOCKS
LRU_C = 8.0
EPS = 1e-6
ATT_WIDTH = ATT_HEADS * HEAD_DIM
KV_WIDTH = KV_HEADS * HEAD_DIM
VEXT_WIDTH = 2 * KV_WIDTH
QI_WIDTH = IDX_HEADS * IDX_DIM
Q_SCALE = HEAD_DIM ** -0.5 * float(np.log2(np.e))

LANES = 128
SUBLANES = 8
VMEM_LIMIT_BYTES = 62 * 1024 * 1024
SINGLE_BUFFER_ROWS = 2048
NORM_CHUNKS = 4

OFF_Q = 0
OFF_K = OFF_Q + ATT_WIDTH
OFF_V = OFF_K + KV_WIDTH
OFF_QI = OFF_V + KV_WIDTH
OFF_SMALL = OFF_QI + QI_WIDTH
SM_WI = IDX_DIM
SM_DT = IDX_DIM + IDX_HEADS
OFF_Z = OFF_SMALL + SM_DT
OFF_XBC = OFF_Z + SSD_INNER
OFF_DT = OFF_XBC + SSD_CONV_DIM
AB_WIDTH = OFF_DT + SSD_HEADS
Z_BLOCK = 1280
XBC_BLOCK = 1792
AB_TILE_N = 512
AB_PAD_N = 5632
assert OFF_Z % LANES == SM_DT and OFF_XBC % LANES == SM_DT and OFF_DT % LANES == SM_DT
assert (OFF_Z - SM_DT) % Z_BLOCK == 0 and OFF_Z - SM_DT + Z_BLOCK >= OFF_Z + SSD_INNER
assert (OFF_XBC - SM_DT) % XBC_BLOCK == 0 and XBC_BLOCK >= OFF_DT - OFF_XBC + LANES
assert OFF_XBC - SM_DT + XBC_BLOCK <= AB_PAD_N

NEG_BIG = -1e30


def _params(n_axes):
    return pltpu.CompilerParams(dimension_semantics=("arbitrary",) * n_axes,
                                vmem_limit_bytes=VMEM_LIMIT_BYTES)


def _silu(x):
    return x * jax.nn.sigmoid(x)


def _softplus(x):
    return jnp.maximum(x, 0.0) + jnp.log1p(jnp.exp(-jnp.abs(x)))


def _mod_kernel(c_ref, w_ref, b_ref, o_ref):
    a = _silu(c_ref[...]).astype(BF16)
    o_ref[0] = jnp.dot(a, w_ref[0].astype(BF16), preferred_element_type=F32) + b_ref[0]


def _modulation(c_all, ada_w, ada_b):
    depth, d, n = ada_w.shape
    m = c_all.shape[0]
    tn = 2048
    return pl.pallas_call(
        _mod_kernel,
        grid=(depth, n // tn),
        in_specs=[pl.BlockSpec((m, d), lambda l, j: (0, 0)),
                  pl.BlockSpec((1, d, tn), lambda l, j: (l, 0, j)),
                  pl.BlockSpec((1, 1, tn), lambda l, j: (l, 0, j))],
        out_specs=pl.BlockSpec((1, m, tn), lambda l, j: (l, 0, j)),
        out_shape=jax.ShapeDtypeStruct((depth, m, n), F32),
        compiler_params=_params(2),
        name="adaln_mod",
    )(c_all, ada_w, ada_b.reshape(depth, 1, n))


def _first_row_tile():
    return (pl.program_id(0) == 0) & (pl.program_id(1) == 0)


def _nmm_kernel(*refs, swiglu, tail, w_rows_out, ride):
    refs = list(refs)
    x_ref, nw_ref, sc_ref, sh_ref = refs[:4]
    del refs[:4]
    if ride:
        xr_ref, scr_ref, shr_ref = refs[:3]
        del refs[:3]
    n_w = 2 if swiglu or tail else 1
    w_refs = refs[:n_w]
    del refs[:n_w]
    if ride:
        o_ref, or_ref, h_ref, hr_ref = refs
    else:
        o_ref, h_ref = refs
    k = pl.program_id(2)

    def matmul(h, w):
        dims = (((1,), (1,)), ((), ())) if w_rows_out else (((1,), (0,)), ((), ()))
        return lax.dot_general(h, w, dims, preferred_element_type=F32)

    def finish(accs, out_ref):
        o = _silu(accs[0]) * accs[1] if swiglu else accs[0]
        out_ref[...] = o.reshape(out_ref.shape).astype(out_ref.dtype)

    def project(ws, rows_ref, out_ref):
        h = rows_ref[...]
        finish([matmul(h, w) for w in ws], out_ref)

    def norm_project(ws, src_ref, scale_ref, shift_ref, rows_ref, out_ref):
        x = src_ref[...]
        d = x.shape[-1]
        rs = lax.rsqrt(jnp.mean(x * x, axis=-1, keepdims=True) + EPS)
        kc = d // NORM_CHUNKS
        accs = None
        for c in range(NORM_CHUNKS):
            cols = slice(c * kc, (c + 1) * kc)
            hc = (x[..., cols] * rs * nw_ref[..., cols] * (1.0 + scale_ref[..., cols])
                  + shift_ref[..., cols])
            hc = hc.reshape(rows_ref.shape[0], kc).astype(BF16)
            rows_ref[:, cols] = hc
            part = [matmul(hc, w[:, cols] if w_rows_out else w[cols, :]) for w in ws]
            accs = part if accs is None else [a + b for a, b in zip(accs, part)]
        finish(accs, out_ref)

    def step(tile_refs, first):
        ws = [w[...].astype(BF16) for w in tile_refs]
        if first:
            norm_project(ws, x_ref, sc_ref, sh_ref, h_ref, o_ref)
            if ride:
                pl.when(_first_row_tile())(
                    lambda: norm_project(ws, xr_ref, scr_ref, shr_ref, hr_ref, or_ref))
        else:
            project(ws, h_ref, o_ref)
            if ride:
                pl.when(_first_row_tile())(lambda: project(ws, hr_ref, or_ref))

    last = pl.num_programs(2) - 1 if tail else pl.num_programs(2)
    pl.when(k == 0)(lambda: step(w_refs[:n_w - tail], True))
    pl.when((k > 0) & (k < last))(lambda: step(w_refs[:n_w - tail], False))
    if tail:
        pl.when(k == last)(lambda: step(w_refs[1:], False))


def _row_block_mode(rows):
    return dict(pipeline_mode=pl.Buffered(1)) if rows >= SINGLE_BUFFER_ROWS else {}


def _ride_col_tile(n_tiles):
    return lambda i, j, k: (0, 0, jnp.where((i == 0) & (j == 0), k, n_tiles - 1))


def _norm_mod_matmul(x, nw, sc, sh, ws, layer, *, bb, tt, tn, out_dtype, name, w_tail=None,
                     w_rows_out=False, ride=None):
    b, t, d = x.shape
    swiglu = len(ws) == 2
    n_main = ws[0].shape[1 if w_rows_out else 2] // tn
    n_tiles = n_main + (w_tail is not None)
    n = n_tiles * tn
    xmap = lambda i, j, k: (i, j, 0)
    mmap = lambda i, j, k: (i, 0, 0)
    const3 = lambda i, j, k: (0, 0, 0)
    tile = lambda k: jnp.minimum(k, n_main - 1)
    if w_rows_out:
        w_specs = [pl.BlockSpec((None, tn, d), lambda i, j, k: (layer, tile(k), 0))] * len(ws)
    else:
        w_specs = [pl.BlockSpec((None, d, tn), lambda i, j, k: (layer, 0, tile(k)))] * len(ws)
    operands = list(ws)
    if w_tail is not None:
        w_specs.append(pl.BlockSpec(w_tail.shape, lambda i, j, k: (0, 0)))
        operands.append(w_tail)
    in_specs = [pl.BlockSpec((bb, tt, d), xmap, **_row_block_mode(bb * tt)),
                pl.BlockSpec((1, 1, d), const3),
                pl.BlockSpec((bb, 1, d), mmap),
                pl.BlockSpec((bb, 1, d), mmap)]
    out_specs = [pl.BlockSpec((bb, tt, tn), lambda i, j, k: (i, j, k))]
    out_shape = [jax.ShapeDtypeStruct((b, t, n), out_dtype)]
    scratch = [pltpu.VMEM((bb * tt, d), BF16)]
    ride_ops = ()
    if ride is not None:
        ride_ops = ride
        rb, rt, _ = ride[0].shape
        in_specs += [pl.BlockSpec((rb, rt, d), const3, pipeline_mode=pl.Buffered(1)),
                     pl.BlockSpec((rb, 1, d), const3), pl.BlockSpec((rb, 1, d), const3)]
        out_specs.append(pl.BlockSpec((rb, rt, tn), _ride_col_tile(n_tiles)))
        out_shape.append(jax.ShapeDtypeStruct((rb, rt, n), out_dtype))
        scratch.append(pltpu.VMEM((rb * rt, d), BF16))
    outs = pl.pallas_call(
        functools.partial(_nmm_kernel, swiglu=swiglu, tail=w_tail is not None,
                          w_rows_out=w_rows_out, ride=ride is not None),
        grid=(b // bb, t // tt, n_tiles),
        in_specs=in_specs + w_specs,
        out_specs=out_specs,
        out_shape=out_shape,
        scratch_shapes=scratch,
        compiler_params=_params(3),
        name=name,
    )(x, nw.reshape(1, 1, d), sc, sh, *ride_ops, *operands)
    return outs if ride is not None else outs[0]


def _mmr_kernel(*refs, k_sizes, ride):
    refs = list(refs)
    n_a = len(k_sizes)
    groups = [refs[:n_a + 2]]
    del refs[:n_a + 2]
    if ride:
        groups.append(refs[:n_a + 2])
        del refs[:n_a + 2]
    w_ref = refs[0]
    out_refs = refs[1:]
    offs = np.cumsum((0,) + tuple(k_sizes))
    ws = [w_ref[offs[i]:offs[i + 1], :].astype(BF16) for i in range(n_a)]

    def apply(group, o_ref):
        *a_refs, x_ref, g_ref = group
        bb, tt, tn = o_ref.shape
        acc = jnp.zeros((bb * tt, tn), F32)
        for a_ref, w, ks in zip(a_refs, ws, k_sizes):
            acc = acc + jnp.dot(a_ref[...].reshape(bb * tt, ks), w, preferred_element_type=F32)
        o_ref[...] = x_ref[...] + g_ref[...] * acc.reshape(bb, tt, tn)

    apply(groups[0], out_refs[0])
    if ride:
        pl.when(_first_row_tile())(lambda: apply(groups[1], out_refs[1]))


def _matmul_residual(a_list, w, layer, x, g, *, bb, tt, tn, name, ride=None):
    b, t, d = x.shape
    k_sizes = tuple(a.shape[-1] for a in a_list)
    k_total = sum(k_sizes)
    n_tiles = d // tn
    in_specs = ([pl.BlockSpec((bb, tt, ks), lambda i, j, k: (i, j, 0), **_row_block_mode(bb * tt))
                 for ks in k_sizes]
                + [pl.BlockSpec((bb, tt, tn), lambda i, j, k: (i, j, k)),
                   pl.BlockSpec((bb, 1, tn), lambda i, j, k: (i, 0, k))])
    out_specs = [pl.BlockSpec((bb, tt, tn), lambda i, j, k: (i, j, k))]
    out_shape = [jax.ShapeDtypeStruct((b, t, d), F32)]
    ride_ops = ()
    if ride is not None:
        ra_list, rx, rg = ride
        rb, rt, _ = rx.shape
        ride_ops = (*ra_list, rx, rg)
        col = _ride_col_tile(n_tiles)
        in_specs += ([pl.BlockSpec((rb, rt, ks), lambda i, j, k: (0, 0, 0),
                                   pipeline_mode=pl.Buffered(1)) for ks in k_sizes]
                     + [pl.BlockSpec((rb, rt, tn), col), pl.BlockSpec((rb, 1, tn), col)])
        out_specs.append(pl.BlockSpec((rb, rt, tn), col))
        out_shape.append(jax.ShapeDtypeStruct(rx.shape, F32))
    outs = pl.pallas_call(
        functools.partial(_mmr_kernel, k_sizes=k_sizes, ride=ride is not None),
        grid=(b // bb, t // tt, n_tiles),
        in_specs=in_specs + [pl.BlockSpec((None, k_total, tn), lambda i, j, k: (layer, 0, k))],
        out_specs=out_specs,
        out_shape=out_shape,
        compiler_params=_params(3),
        name=name,
    )(*a_list, x, g, *ride_ops, w)
    return outs if ride is not None else outs[0]


def _swap_halves(x, half):
    w = x.shape[-1]
    if w > LANES:
        return jnp.concatenate([_swap_halves(x[..., c * LANES:(c + 1) * LANES], half)
                                for c in range(w // LANES)], axis=-1)
    if w == 2 * half:
        return pltpu.roll(x, half, x.ndim - 1)
    lane = lax.broadcasted_iota(jnp.int32, x.shape, x.ndim - 1)
    first = (lane & half) == 0
    return jnp.where(first, pltpu.roll(x, w - half, x.ndim - 1), pltpu.roll(x, half, x.ndim - 1))


def _lane_aligned(slab, offset, width):
    return pltpu.roll(slab, slab.shape[-1] - offset, slab.ndim - 1)[:, :width]


def _rope(x, cos, sin_signed, half):
    reps = x.shape[-1] // cos.shape[-1]
    if reps > 1:
        cos = jnp.concatenate([cos] * reps, axis=-1)
        sin_signed = jnp.concatenate([sin_signed] * reps, axis=-1)
    return x * cos + _swap_halves(x, half) * sin_signed


def _conv_carry_init(state_ref, cbuf_ref):
    cbuf_ref[0:SUBLANES, :] = jnp.zeros((SUBLANES, cbuf_ref.shape[1]), F32)
    cbuf_ref[SUBLANES - (CONV_W - 1):SUBLANES, :] = state_ref[0]


def _conv_state(tail):
    return tail[SUBLANES - (CONV_W - 1):, :]


def _causal_conv_tile(u, w_ref, b_ref, cbuf_ref):
    tt = u.shape[0]
    cbuf_ref[SUBLANES:SUBLANES + tt, :] = u
    full = cbuf_ref[...]
    out = b_ref[...]
    for j in range(CONV_W):
        shift = CONV_W - 1 - j
        rows = pltpu.roll(full, shift, 0) if shift else full
        out = out + rows[SUBLANES:SUBLANES + tt, :] * w_ref[j:j + 1, :]
    tail = cbuf_ref[tt:tt + SUBLANES, :]
    cbuf_ref[0:SUBLANES, :] = tail
    return out, tail


def _causal_conv(u, state_ref, w_ref, b_ref, cbuf_ref, new_ref, first, last):
    pl.when(first)(lambda: _conv_carry_init(state_ref, cbuf_ref))
    out, tail = _causal_conv_tile(u, w_ref, b_ref, cbuf_ref)

    @pl.when(last)
    def _():
        new_ref[0] = _conv_state(tail)

    return out


def _post0_kernel(p_ref, pdt_ref, cq_ref, sq_ref, ci_ref, si_ref, qn_ref, kn_ref, dtb_ref,
                  q_ref, qi_ref, k_ref, v_ref, ki_ref, sm_ref, kb_ref, vb_ref, kib_ref):
    cq, sq, ci, si = cq_ref[...], sq_ref[...], ci_ref[...], si_ref[...]

    def head_norm_rope(x, w):
        ms = jnp.mean(x * x, axis=-1, keepdims=True)
        return _rope(x * lax.rsqrt(ms + EPS) * w, cq, sq, HEAD_DIM // 2)

    for h in range(ATT_HEADS):
        xh = p_ref[0, :, OFF_Q + h * HEAD_DIM:OFF_Q + (h + 1) * HEAD_DIM]
        q_ref[0, :, h * HEAD_DIM:(h + 1) * HEAD_DIM] = (head_norm_rope(xh, qn_ref[...])
                                                         * Q_SCALE).astype(BF16)
    ones = jnp.ones((p_ref.shape[1], HEAD_DIM), BF16)
    for h in range(KV_HEADS):
        xh = p_ref[0, :, OFF_K + h * HEAD_DIM:OFF_K + (h + 1) * HEAD_DIM]
        kh = head_norm_rope(xh, kn_ref[...])
        vh = p_ref[0, :, OFF_V + h * HEAD_DIM:OFF_V + (h + 1) * HEAD_DIM]
        k_ref[0, pl.ds(h, xh.shape[0], stride=KV_HEADS), :] = kh
        v_ref[0, pl.ds(h, xh.shape[0], stride=KV_HEADS), :] = vh
        kb_ref[0, :, h * HEAD_DIM:(h + 1) * HEAD_DIM] = kh.astype(BF16)
        vb_ref[0, :, 2 * h * HEAD_DIM:(2 * h + 1) * HEAD_DIM] = vh.astype(BF16)
        vb_ref[0, :, (2 * h + 1) * HEAD_DIM:(2 * h + 2) * HEAD_DIM] = ones

    qi = p_ref[0, :, OFF_QI:OFF_QI + QI_WIDTH]
    qi_ref[0] = _rope(qi, ci, si, IDX_DIM // 2).astype(BF16)

    sm = p_ref[0, :, OFF_SMALL:OFF_SMALL + LANES]
    lane = lax.broadcasted_iota(jnp.int32, sm.shape, 1)
    ki_part = _rope(sm, ci, si, IDX_DIM // 2)
    wi_part = sm * (IDX_HEADS ** -0.5 * IDX_DIM ** -0.5)
    dt_part = _softplus(pdt_ref[0] + dtb_ref[...])
    sm_out = jnp.where(lane < SM_WI, ki_part,
                       jnp.where(lane < SM_DT, wi_part,
                                 jnp.where(lane < SM_DT + SSD_HEADS, dt_part, 0.0)))
    sm_ref[0] = sm_out
    ki_ref[0] = sm_out[:, :IDX_DIM]
    kib_ref[0] = sm_out[:, :IDX_DIM].astype(BF16)


def _post0(proj, tabs, q_norm_w, k_norm_w, dt_bias_pad, *, tt):
    b, t, n = proj.shape
    tok = lambda w, col=0: pl.BlockSpec((1, tt, w), lambda i, j: (i, j, col))
    tab = pl.BlockSpec((tt, LANES), lambda i, j: (j, 0))
    full2 = lambda a: pl.BlockSpec(a.shape, lambda i, j: (0, 0))
    out_shapes = [
        jax.ShapeDtypeStruct((b, t, ATT_WIDTH), BF16),
        jax.ShapeDtypeStruct((b, t, QI_WIDTH), BF16),
        jax.ShapeDtypeStruct((b, t * KV_HEADS, HEAD_DIM), F32),
        jax.ShapeDtypeStruct((b, t * KV_HEADS, HEAD_DIM), F32),
        jax.ShapeDtypeStruct((b, t, IDX_DIM), F32),
        jax.ShapeDtypeStruct((b, t, LANES), F32),
        jax.ShapeDtypeStruct((b, t, KV_WIDTH), BF16),
        jax.ShapeDtypeStruct((b, t, VEXT_WIDTH), BF16),
        jax.ShapeDtypeStruct((b, t, IDX_DIM), BF16),
    ]
    kv_rows = pl.BlockSpec((1, tt * KV_HEADS, HEAD_DIM), lambda i, j: (i, j, 0))
    out_specs = [tok(ATT_WIDTH), tok(QI_WIDTH), kv_rows, kv_rows, tok(IDX_DIM),
                 tok(LANES), tok(KV_WIDTH), tok(VEXT_WIDTH), tok(IDX_DIM)]
    qn = q_norm_w.reshape(1, HEAD_DIM)
    kn = k_norm_w.reshape(1, HEAD_DIM)
    head = OFF_SMALL + LANES
    return pl.pallas_call(
        _post0_kernel,
        grid=(b, t // tt),
        in_specs=[tok(head), tok(LANES, (OFF_DT - SM_DT) // LANES), tab, tab, tab, tab, full2(qn),
                  full2(kn), full2(dt_bias_pad)],
        out_specs=out_specs,
        out_shape=out_shapes,
        compiler_params=_params(2),
        name="post_proj0",
    )(proj, proj, *tabs, qn, kn, dt_bias_pad)


def _dsa_kernel(q_ref, qi_ref, sm_ref, k_ref, v_ref, ki_ref, *rest, qb, tk, n_keys, past, topk,
                n_bisect, keys_on_sublanes):
    if past:
        pk_ref, pv_ref, pki_ref = rest[:3]
        rest = rest[3:]
    (o_ref, sc_ref, acc_ref, qs_ref, s0_ref, s1_ref, mt0_ref, mt1_ref, m_ref) = rest[:9]
    if past:
        kall_ref, vall_ref, kiall_ref = rest[9:]
        t_new = k_ref.shape[1]
        pad = kall_ref.shape[0] - n_keys
        for g in range(KV_HEADS):
            head_rows = pl.ds(g, past, stride=KV_HEADS)
            kall_ref[0:past, g * HEAD_DIM:(g + 1) * HEAD_DIM] = pk_ref[0, head_rows, :].astype(BF16)
            vall_ref[0:past, 2 * g * HEAD_DIM:(2 * g + 1) * HEAD_DIM] = pv_ref[0, head_rows, :].astype(BF16)
            vall_ref[0:past, (2 * g + 1) * HEAD_DIM:(2 * g + 2) * HEAD_DIM] = jnp.ones(
                (past, HEAD_DIM), BF16)
        kiall_ref[0:past, :] = pki_ref[0].astype(BF16)
        for dst, src in ((kall_ref, k_ref), (vall_ref, v_ref), (kiall_ref, ki_ref)):
            dst[past:past + t_new, :] = src[0]
            dst[past + t_new:, :] = jnp.zeros((pad, dst.shape[1]), BF16)
        k_src, v_src, ki_src = kall_ref, vall_ref, kiall_ref
    else:
        k_src, v_src, ki_src = k_ref.at[0], v_ref.at[0], ki_ref.at[0]
    kax = 0 if keys_on_sublanes else 1
    tile_shape = (tk, qb) if keys_on_sublanes else (qb, tk)
    vec_shape = (1, qb) if keys_on_sublanes else (qb, 1)
    n_acc = 8
    part_shape = (n_acc, SUBLANES, qb) if keys_on_sublanes else (qb, LANES)
    j = pl.program_id(1)
    pos0 = past + j * qb
    q_chunk = lax.shift_right_logical(pos0 + lax.broadcasted_iota(jnp.int32, vec_shape, 1 - kax),
                                      CHUNK_SHIFT)
    n_valid = jnp.minimum((q_chunk + 1) * CHUNK, n_keys)
    max_valid = jnp.minimum(((pos0 + qb - 1) // CHUNK + 1) * CHUNK, n_keys)
    nkt = (max_valid + tk - 1) // tk
    kf = float(topk)
    active_f = jnp.where(n_valid > topk, 1.0, 0.0)

    def key_index(kt):
        return kt * tk + lax.broadcasted_iota(jnp.int32, tile_shape, kax)

    nt_dims = (((1,), (1,)), ((), ()))
    for h in range(IDX_HEADS):
        qs_ref[h * qb:(h + 1) * qb, :] = qi_ref[0, :, h * IDX_DIM:(h + 1) * IDX_DIM]
    if keys_on_sublanes:
        wi_t = sm_ref[0].T
        head_w = lambda h: wi_t[SM_WI + h:SM_WI + h + 1, :]
    else:
        wis = sm_ref[0][:, SM_WI:SM_WI + IDX_HEADS]
        head_w = lambda h: wis[:, h:h + 1]

    def score_tile(kt, carry):
        kit = ki_src[pl.ds(pl.multiple_of(kt * tk, tk), tk), :]
        acc = jnp.zeros(tile_shape, F32)
        if keys_on_sublanes:
            for pr in range(IDX_HEADS // 2):
                s2 = lax.dot_general(kit, qs_ref[2 * pr * qb:(2 * pr + 2) * qb, :], nt_dims,
                                     preferred_element_type=F32)
                for e in range(2):
                    acc = acc + head_w(2 * pr + e) * jnp.maximum(s2[:, e * qb:(e + 1) * qb], 0.0)
        else:
            s_all = lax.dot_general(qs_ref[...], kit, nt_dims, preferred_element_type=F32)
            for h in range(IDX_HEADS):
                acc = acc + head_w(h) * jnp.maximum(s_all[h * qb:(h + 1) * qb, :], 0.0)
        valid = key_index(kt) < n_valid
        sc_ref[kt] = jnp.where(valid, acc, -jnp.inf)
        mn, mx = carry
        return (jnp.minimum(mn, fold(jnp.where(valid, acc, jnp.inf), jnp.minimum, jnp.min)),
                jnp.maximum(mx, fold(jnp.where(valid, acc, -jnp.inf), jnp.maximum, jnp.max)))

    def fold(m, op2, red):
        if keys_on_sublanes:
            return red(m.reshape(tk // (n_acc * SUBLANES), n_acc, SUBLANES, qb), axis=0)
        f = m[:, 0:LANES]
        for c in range(1, tk // LANES):
            f = op2(f, m[:, c * LANES:(c + 1) * LANES])
        return f

    def finish(part, red):
        if keys_on_sublanes:
            part = red(part, axis=0)
        return red(part, axis=kax, keepdims=True)

    def reduce_tiles(make, op2, red, init):
        def body(kt, part):
            return op2(part, fold(make(sc_ref[kt], kt), op2, red))
        return finish(lax.fori_loop(0, nkt, body, jnp.full(part_shape, init, F32)), red)

    def count(pred):
        return reduce_tiles(lambda t, kt: jnp.where(pred(t, kt), 1.0, 0.0), jnp.add, jnp.sum, 0.0)

    def row_max(pred):
        return reduce_tiles(lambda t, kt: jnp.where(pred(t, kt), t, -jnp.inf), jnp.maximum, jnp.max,
                            -jnp.inf)

    mn_part, mx_part = lax.fori_loop(0, nkt, score_tile, (jnp.full(part_shape, jnp.inf, F32),
                                                          jnp.full(part_shape, -jnp.inf, F32)))
    lo0, hi0 = finish(mn_part, jnp.min), finish(mx_part, jnp.max)

    def any_set(flag_f):
        return jnp.max(flag_f) > 0.0

    def bisect(_, state):
        lo, hi, c_lo = state
        mid = lo + (hi - lo) * 0.5
        c = count(lambda t, kt: t >= mid)
        ge = c >= kf
        return jnp.where(ge, mid, lo), jnp.where(ge, hi, mid), jnp.where(ge, c, c_lo)

    lo, hi, c_lo = lax.fori_loop(0, n_bisect, bisect, (lo0, hi0, n_valid.astype(F32)))
    found_f = jnp.where(c_lo == kf, active_f, 0.0)

    def walk_down():
        w0 = row_max(lambda t, kt: t <= hi)
        c0 = count(lambda t, kt: t >= w0)

        def walk_cond(state):
            w, c = state
            return any_set(jnp.where(c < kf, active_f, 0.0))

        def walk_body(state):
            w, c = state
            w2 = row_max(lambda t, kt: t < w)
            c2 = count(lambda t, kt: t >= w2)
            upd = c < kf
            return jnp.where(upd, w2, w), jnp.where(upd, c2, c)

        return lax.while_loop(walk_cond, walk_body,
                              (jnp.where(found_f > 0.5, lo, w0),
                               jnp.where(found_f > 0.5, kf, c0)))

    w, c_ge = lax.cond(any_set(active_f - found_f), walk_down,
                       lambda: (lo, jnp.full(vec_shape, kf, F32)))
    thr = jnp.where(active_f > 0.5, w, -jnp.inf)
    tied_f = jnp.where(c_ge > kf, active_f, 0.0)

    def last_kept_tie():
        need = kf - count(lambda t, kt: t > thr)
        n_steps = max(1, int(np.ceil(np.log2(sc_ref.shape[0] * tk))) + 1)

        def step(_, lh):
            lo_i, hi_i = lh
            mid = lax.shift_right_arithmetic(lo_i + hi_i, 1)
            ge = count(lambda t, kt: (t == thr) & (key_index(kt) <= mid)) >= need
            return jnp.where(ge, lo_i, mid), jnp.where(ge, mid, hi_i)

        lo_i = jnp.full(vec_shape, -1, jnp.int32)
        hi_i = jnp.full(vec_shape, sc_ref.shape[0] * tk - 1, jnp.int32)
        return lax.fori_loop(0, n_steps, step, (lo_i, hi_i))[1]

    last_tie = lax.cond(any_set(tied_f), last_kept_tie, lambda: jnp.zeros(vec_shape, jnp.int32))

    def bias_tile(kt):
        t, kidx = sc_ref[kt], key_index(kt)
        keep_tie = (kidx <= last_tie) | (tied_f < 0.5)
        sel = ((t > thr) | ((t == thr) & keep_tie)) & (kidx < n_valid)
        bias = jnp.where(sel, 0.0, NEG_BIG)
        return bias.T if keys_on_sublanes else bias

    rep = ATT_HEADS // KV_HEADS
    qgs = [jnp.concatenate(
        [q_ref[0, :, (g * rep + r) * HEAD_DIM:(g * rep + r + 1) * HEAD_DIM] for r in range(rep)],
        axis=0) for g in range(KV_HEADS)]
    acc_ref[...] = jnp.zeros(acc_ref.shape, F32)
    bufs = ((s0_ref, mt0_ref), (s1_ref, mt1_ref))

    def qk_tile(kt, buf):
        s_ref, mt_ref = bufs[buf]
        row0 = pl.multiple_of(kt * tk, tk)
        bias = bias_tile(kt)
        for g in range(KV_HEADS):
            kg = k_src[pl.ds(row0, tk), g * HEAD_DIM:(g + 1) * HEAD_DIM]
            s = lax.dot_general(qgs[g], kg, nt_dims, preferred_element_type=F32)
            s = (s.reshape(rep, qb, tk) + bias[None]).reshape(rep * qb, tk)
            s_ref[g] = s
            mt_ref[g] = jnp.broadcast_to(jnp.max(s, axis=-1, keepdims=True), (rep * qb, LANES))

    def lane_tile(x, width):
        return jnp.concatenate([x] * (width // LANES), axis=1)

    def att_tile(kt, buf, prefetch):
        if prefetch:
            qk_tile(kt + 1, 1 - buf)
        s_ref, mt_ref = bufs[buf]
        row0 = pl.multiple_of(kt * tk, tk)
        for g in range(KV_HEADS):
            vg = v_src[pl.ds(row0, tk), 2 * g * HEAD_DIM:(2 * g + 2) * HEAD_DIM]
            m_old = m_ref[g]
            m_new = jnp.maximum(m_old, mt_ref[g])
            m_ref[g] = m_new
            alpha = jnp.exp2(m_old - m_new)
            p = jnp.exp2(s_ref[g] - lane_tile(m_new, tk)).astype(BF16)
            acc_ref[g] = (lane_tile(alpha, 2 * HEAD_DIM) * acc_ref[g]
                          + jnp.dot(p, vg, preferred_element_type=F32))

    def tile_pair(pi, carry):
        att_tile(2 * pi, 0, True)
        att_tile(2 * pi + 1, 1, True)
        return carry

    def last_two():
        att_tile(nkt - 2, 0, True)
        att_tile(nkt - 1, 1, False)

    def last_one():
        att_tile(nkt - 1, 0, False)

    m_ref[...] = jnp.full(m_ref.shape, NEG_BIG, F32)
    qk_tile(0, 0)
    lax.fori_loop(0, (nkt - 1) // 2, tile_pair, 0)
    lax.cond((nkt & 1) == 0, last_two, last_one)
    for g in range(KV_HEADS):
        acc = acc_ref[g]
        o = acc[:, :HEAD_DIM] / acc[:, HEAD_DIM:]
        for r in range(rep):
            h = g * rep + r
            o_ref[0, :, h * HEAD_DIM:(h + 1) * HEAD_DIM] = o[r * qb:(r + 1) * qb].astype(BF16)


def _dsa_attention(q, qi, sm, k_new, v_new, ki_new, cache, *, qb, tk):
    b, t, _ = q.shape
    past = 0 if cache is None else cache[2].shape[1]
    n_keys = past + t
    nk_pad = -(-n_keys // tk) * tk
    topk = min(TOPK_MAX, n_keys // 4)
    keys_on_sublanes = qb == LANES
    rep = ATT_HEADS // KV_HEADS
    tokq = lambda w: pl.BlockSpec((1, qb, w), lambda i, j: (i, j, 0))
    keys = lambda w: pl.BlockSpec((1, t, w), lambda i, j: (i, 0, 0))
    tile_shape = (tk, qb) if keys_on_sublanes else (qb, tk)
    cache_specs, cache_scratch = [], []
    if cache is not None:
        kv = pl.BlockSpec((1, past * KV_HEADS, HEAD_DIM), lambda i, j: (i, 0, 0))
        cache_specs = [kv, kv, pl.BlockSpec((1, past, IDX_DIM), lambda i, j: (i, 0, 0))]
        cache_scratch = [pltpu.VMEM((nk_pad, w), BF16) for w in (KV_WIDTH, VEXT_WIDTH, IDX_DIM)]
    return pl.pallas_call(
        functools.partial(_dsa_kernel, qb=qb, tk=tk, n_keys=n_keys, past=past, topk=topk,
                          n_bisect=20, keys_on_sublanes=keys_on_sublanes),
        grid=(b, t // qb),
        in_specs=[tokq(ATT_WIDTH), tokq(QI_WIDTH), tokq(LANES), keys(KV_WIDTH), keys(VEXT_WIDTH),
                  keys(IDX_DIM)] + cache_specs,
        out_specs=tokq(ATT_WIDTH),
        out_shape=jax.ShapeDtypeStruct((b, t, ATT_WIDTH), BF16),
        scratch_shapes=[pltpu.VMEM((nk_pad // tk,) + tile_shape, F32),
                        pltpu.VMEM((KV_HEADS, rep * qb, 2 * HEAD_DIM), F32),
                        pltpu.VMEM((IDX_HEADS * qb, IDX_DIM), BF16),
                        pltpu.VMEM((KV_HEADS, rep * qb, tk), F32),
                        pltpu.VMEM((KV_HEADS, rep * qb, tk), F32),
                        pltpu.VMEM((KV_HEADS, rep * qb, LANES), F32),
                        pltpu.VMEM((KV_HEADS, rep * qb, LANES), F32),
                        pltpu.VMEM((KV_HEADS, rep * qb, LANES), F32)] + cache_scratch,
        compiler_params=_params(2),
        name="dsa_attention",
    )(q, qi, sm, k_new, v_new, ki_new, *(cache or ()))


def _ssd_kernel(pfirst_ref, pnext_ref, z_ref, sm_ref, st_ref, cs_ref, cw_ref, cb_ref, alog_ref,
                dexp_ref, nw_ref, e_ref, y_ref, stout_ref, cnew_ref, ht_ref, yi_ref, cbuf_ref, xa_ref,
                xb_ref):
    c = pl.program_id(1)
    nc = pl.num_programs(1)
    L = pfirst_ref.shape[1]
    hp = SSD_INNER // SSD_GROUPS

    def conv_act(p_ref):
        raw = _lane_aligned(p_ref[0, :, :OFF_DT - OFF_XBC + LANES], SM_DT, SSD_CONV_DIM)
        out, tail = _causal_conv_tile(raw, cw_ref, cb_ref, cbuf_ref)
        return _silu(out), tail

    @pl.when(c == 0)
    def _():
        ht_ref[...] = st_ref[0].reshape(SSD_INNER, SSD_STATE).T
        _conv_carry_init(cs_ref, cbuf_ref)
        act, tail = conv_act(pfirst_ref)
        xa_ref[...] = act

        @pl.when(nc == 1)
        def _():
            cnew_ref[0] = _conv_state(tail)

    def chunk(xbc_ref, next_ref):
        _ssd_chunk(xbc_ref, z_ref, sm_ref, alog_ref, dexp_ref, nw_ref, e_ref, y_ref, ht_ref, yi_ref)
        act, tail = conv_act(pnext_ref)
        next_ref[...] = act
        return tail

    tail = lax.cond((c & 1) == 0, lambda: chunk(xa_ref, xb_ref), lambda: chunk(xb_ref, xa_ref))

    @pl.when(c + 1 == nc - 1)
    def _():
        cnew_ref[0] = _conv_state(tail)

    @pl.when(c == nc - 1)
    def _():
        stout_ref[0] = ht_ref[...].T.reshape(SSD_HEADS, SSD_HEADDIM, SSD_STATE)


def _ssd_chunk(xbc_ref, z_ref, sm_ref, alog_ref, dexp_ref, nw_ref, e_ref, y_ref, ht_ref, yi_ref):
    L = xbc_ref.shape[0]
    hp = SSD_INNER // SSD_GROUPS
    xs = xbc_ref[:, 0:SSD_INNER]
    sm = sm_ref[0]
    lane = lax.broadcasted_iota(jnp.int32, (1, LANES), 1)
    is_dt = (lane >= SM_DT) & (lane < SM_DT + SSD_HEADS)
    a_neg = jnp.where(is_dt, -jnp.exp(alog_ref[...]), 0.0)
    dt = jnp.where(is_dt, sm, 0.0)
    rows = lax.broadcasted_iota(jnp.int32, (L, L), 0)
    cols = lax.broadcasted_iota(jnp.int32, (L, L), 1)
    tri = cols <= rows
    e = e_ref[...]
    cum = jnp.dot(tri.astype(F32), dt * a_neg, precision=HIGHEST, preferred_element_type=F32)
    eye = (lax.broadcasted_iota(jnp.int32, (LANES, LANES), 0)
           == lax.broadcasted_iota(jnp.int32, (LANES, LANES), 1)).astype(F32)
    cum_t = lax.dot_general(eye, cum, (((1,), (1,)), ((), ())), precision=HIGHEST,
                            preferred_element_type=F32)
    both = jnp.concatenate([cum, dt], axis=0)
    hi = both.astype(BF16)
    rest = both - hi.astype(F32)
    mid = rest.astype(BF16)
    low = (rest - mid.astype(F32)).astype(BF16)
    both_e = (jnp.dot(hi, e, preferred_element_type=F32) + jnp.dot(mid, e, preferred_element_type=F32)
              + jnp.dot(low, e, preferred_element_type=F32))
    cum_e, dt_e = both_e[:L], both_e[L:]
    last = cum_e[L - 1:L, :]
    xdt = (xs * dt_e).astype(BF16)
    xw = (xs * (jnp.exp(last - cum_e) * dt_e)).astype(BF16)
    ht_old = ht_ref[...]

    for g in range(SSD_GROUPS):
        bg = xbc_ref[:, SSD_INNER + g * SSD_STATE:SSD_INNER + (g + 1) * SSD_STATE]
        cg = xbc_ref[:, SSD_INNER + (SSD_GROUPS + g) * SSD_STATE:
                     SSD_INNER + (SSD_GROUPS + g + 1) * SSD_STATE]
        bgb, cgb = bg.astype(BF16), cg.astype(BF16)
        cb = lax.dot_general(cgb, bgb, (((1,), (1,)), ((), ())), preferred_element_type=F32)
        for hh in range(SSD_HEADS // SSD_GROUPS):
            h = g * (SSD_HEADS // SSD_GROUPS) + hh
            seg = cum[:, SM_DT + h:SM_DT + h + 1] - cum_t[SM_DT + h:SM_DT + h + 1, :]
            decay = jnp.exp(jnp.where(tri, seg, -jnp.inf))
            wts = (cb * decay).astype(BF16)
            yi_ref[:, h * SSD_HEADDIM:(h + 1) * SSD_HEADDIM] = jnp.dot(
                wts, xdt[:, h * SSD_HEADDIM:(h + 1) * SSD_HEADDIM], preferred_element_type=F32)
        ht_g = ht_old[:, g * hp:(g + 1) * hp]
        y_inter = jnp.dot(cgb, ht_g.astype(BF16), preferred_element_type=F32)
        yi_ref[:, g * hp:(g + 1) * hp] = (yi_ref[:, g * hp:(g + 1) * hp]
                                          + y_inter * jnp.exp(cum_e[:, g * hp:(g + 1) * hp]))
        upd = jnp.dot(bg.T.astype(BF16), xw[:, g * hp:(g + 1) * hp],
                      preferred_element_type=F32)
        ht_ref[:, g * hp:(g + 1) * hp] = ht_g * jnp.exp(last[:, g * hp:(g + 1) * hp]) + upd

    y = yi_ref[...] + dexp_ref[...] * xs
    y = y * _silu(_lane_aligned(z_ref[0], SM_DT, SSD_INNER))
    for g in range(SSD_GROUPS):
        yg = y[:, g * hp:(g + 1) * hp]
        ms = jnp.mean(yg * yg, axis=-1, keepdims=True)
        y_ref[0, :, g * hp:(g + 1) * hp] = (yg * lax.rsqrt(ms + EPS)
                                            * nw_ref[:, g * hp:(g + 1) * hp]).astype(BF16)


def _ssd(proj, sm, state, conv_state, conv_w, conv_b, a_log_pad, d_exp, norm_w, e_mat, *, chunk):
    b, t, _ = proj.shape
    n_chunks = t // chunk
    xbc_col = (OFF_XBC - SM_DT) // XBC_BLOCK
    tok = lambda w, col=0: pl.BlockSpec((1, chunk, w), lambda i, j: (i, j, col))
    xbc_first = pl.BlockSpec((1, chunk, XBC_BLOCK), lambda i, j: (i, 0, xbc_col))
    xbc_next = pl.BlockSpec((1, chunk, XBC_BLOCK),
                            lambda i, j: (i, jnp.minimum(j + 1, n_chunks - 1), xbc_col))
    full2 = lambda a: pl.BlockSpec(a.shape, lambda i, j: (0, 0))
    st = pl.BlockSpec((1, SSD_HEADS, SSD_HEADDIM, SSD_STATE), lambda i, j: (i, 0, 0, 0))
    cst = pl.BlockSpec((1, CONV_W - 1, SSD_CONV_DIM), lambda i, j: (i, 0, 0))
    nw = norm_w.reshape(1, SSD_INNER)
    cb = conv_b.reshape(1, SSD_CONV_DIM)
    return pl.pallas_call(
        _ssd_kernel,
        grid=(b, n_chunks),
        in_specs=[xbc_first, xbc_next,
                  tok(Z_BLOCK, (OFF_Z - SM_DT) // Z_BLOCK), tok(LANES), st, cst, full2(conv_w),
                  full2(cb), full2(a_log_pad), full2(d_exp), full2(nw), full2(e_mat)],
        out_specs=[tok(SSD_INNER), st, cst],
        out_shape=[jax.ShapeDtypeStruct((b, t, SSD_INNER), BF16),
                   jax.ShapeDtypeStruct(state.shape, F32),
                   jax.ShapeDtypeStruct((b, CONV_W - 1, SSD_CONV_DIM), F32)],
        scratch_shapes=[pltpu.VMEM((SSD_STATE, SSD_INNER), F32),
                        pltpu.VMEM((chunk, SSD_INNER), F32),
                        pltpu.VMEM((chunk + SUBLANES, SSD_CONV_DIM), F32),
                        pltpu.VMEM((chunk, SSD_CONV_DIM), F32),
                        pltpu.VMEM((chunk, SSD_CONV_DIM), F32)],
        compiler_params=_params(2),
        name="ssd_scan",
    )(proj, proj, proj, sm, state, conv_state, conv_w, cb, a_log_pad, d_exp, nw, e_mat)


def _lru_kernel(gate_ref, xin_ref, cw_ref, cb_ref, wa_ref, wx_ref, ba_ref, bx_ref, lam_ref,
                h0_ref, cs_ref, y_ref, hout_ref, cnew_ref, cbuf_ref, a_ref, u_ref, hp_ref):
    t = pl.program_id(1)
    nt = pl.num_programs(1)
    tt = xin_ref.shape[1]
    xc = _causal_conv(xin_ref[0], cs_ref, cw_ref, cb_ref, cbuf_ref, cnew_ref, t == 0, t == nt - 1)

    @pl.when(t == 0)
    def _():
        hp_ref[...] = h0_ref[0]

    for kb in range(LRU_BLOCKS):
        sl = slice(kb * LRU_BLOCK_DIM, (kb + 1) * LRU_BLOCK_DIM)
        xb = xc[:, sl].astype(BF16)
        a_ref[:, sl] = jnp.dot(xb, wa_ref[kb].astype(BF16), preferred_element_type=F32)
        u_ref[:, sl] = jnp.dot(xb, wx_ref[kb].astype(BF16), preferred_element_type=F32)
    r = jax.nn.sigmoid(a_ref[...] + ba_ref[...])
    i = jax.nn.sigmoid(u_ref[...] + bx_ref[...])
    log_a = -LRU_C * r * _softplus(-lam_ref[...])
    a = jnp.exp(log_a)
    u = jnp.sqrt(-jnp.tanh(log_a) * (1.0 + a * a)) * (i * xc)

    width = a.shape[1]
    a = a.reshape(tt // SUBLANES, SUBLANES, width)
    u = u.reshape(tt // SUBLANES, SUBLANES, width)
    sub = lax.broadcasted_iota(jnp.int32, (1, SUBLANES, width), 1)
    for s in (1, 2, 4):
        m = sub >= s
        a_sh = pltpu.roll(a, s, 1)
        u_sh = pltpu.roll(u, s, 1)
        u = jnp.where(m, a * u_sh + u, u)
        a = jnp.where(m, a * a_sh, a)
    a_ref[...] = a.reshape(tt, width)
    u_ref[...] = u.reshape(tt, width)

    def group(gi, hprev):
        r0 = pl.multiple_of(gi * SUBLANES, SUBLANES)
        hs = a_ref[pl.ds(r0, SUBLANES), :] * hprev + u_ref[pl.ds(r0, SUBLANES), :]
        u_ref[pl.ds(r0, SUBLANES), :] = hs
        return hs[SUBLANES - 1:SUBLANES, :]

    h_last = lax.fori_loop(0, tt // SUBLANES, group, hp_ref[...])
    hp_ref[...] = h_last
    gate = gate_ref[0]
    gelu = 0.5 * gate * (1.0 + jnp.tanh(np.sqrt(2.0 / np.pi) * (gate + 0.044715 * (gate * gate * gate))))
    y_ref[0] = (u_ref[...] * gelu).astype(BF16)

    @pl.when(t == nt - 1)
    def _():
        hout_ref[0] = h_last


def _lru(proj, conv_w, conv_b, w_a, b_a, w_x, b_x, lam, h0, conv_state, *, tt):
    b, t, _ = proj.shape
    w = LRU_WIDTH
    tok = lambda col: pl.BlockSpec((1, tt, w), lambda i, j: (i, j, col))
    full = lambda a: pl.BlockSpec(a.shape, lambda i, j: (0,) * a.ndim)
    row = lambda v: v.reshape(1, w)
    state = pl.BlockSpec((1, CONV_W - 1, w), lambda i, j: (i, 0, 0))
    hspec = pl.BlockSpec((1, 1, w), lambda i, j: (i, 0, 0))
    args = (proj, proj, conv_w, row(conv_b), w_a, w_x, row(b_a), row(b_x), row(lam), h0, conv_state)
    return pl.pallas_call(
        _lru_kernel,
        grid=(b, t // tt),
        in_specs=[tok(0), tok(1)] + [full(a) for a in args[2:9]] + [hspec, state],
        out_specs=[pl.BlockSpec((1, tt, w), lambda i, j: (i, j, 0)), hspec, state],
        out_shape=[jax.ShapeDtypeStruct((b, t, w), BF16),
                   jax.ShapeDtypeStruct((b, 1, w), F32),
                   jax.ShapeDtypeStruct((b, CONV_W - 1, w), F32)],
        scratch_shapes=[pltpu.VMEM((tt + SUBLANES, w), F32), pltpu.VMEM((tt, w), F32),
                        pltpu.VMEM((tt, w), F32), pltpu.VMEM((1, w), F32)],
        compiler_params=_params(2),
        name="rg_lru",
    )(*args)


def _rope_tables(pos, dim):
    half = dim // 2
    inv = np.power(ROPE_THETA, -np.arange(half, dtype=np.float64) / half)
    ang = pos.astype(np.float64)[:, None] * inv[None, :]
    cos = np.concatenate([np.cos(ang), np.cos(ang)], axis=-1)
    sin = np.concatenate([-np.sin(ang), np.sin(ang)], axis=-1)
    reps = LANES // dim
    return (jnp.asarray(np.tile(cos, (1, reps)), F32), jnp.asarray(np.tile(sin, (1, reps)), F32))


def _head_expansion():
    e = np.zeros((LANES, SSD_INNER), np.float32)
    for h in range(SSD_HEADS):
        e[SM_DT + h, h * SSD_HEADDIM:(h + 1) * SSD_HEADDIM] = 1.0
    return jnp.asarray(e, BF16)


def _pad_lanes(v, offset):
    out = jnp.zeros((1, LANES), F32)
    return out.at[0, offset:offset + v.shape[0]].set(v)


def _w_in_ab_tail(w_t):
    whole = AB_WIDTH // AB_TILE_N * AB_TILE_N
    return jnp.pad(w_t[whole:], ((0, AB_PAD_N - AB_WIDTH), (0, 0)))


def _layer0_mixers(proj, st, p, cfg):
    b, t, _ = proj.shape
    past = 0 if st["past_k"] is None else st["past_k"].shape[1]
    pos = np.arange(past, past + t)
    tabs = _rope_tables(pos, HEAD_DIM) + _rope_tables(pos, IDX_DIM)
    q, qi, k, v, ki, sm, kb, vbe, kib = _post0(proj, tabs, p["q_norm_w"][0], p["k_norm_w"][0],
                                               p["dt_bias_pad"], tt=cfg["post_tt"])
    cache = None if not past else (st["past_k"].reshape(b, past * KV_HEADS, HEAD_DIM),
                                   st["past_v"].reshape(b, past * KV_HEADS, HEAD_DIM), st["past_ki"])
    att = _dsa_attention(q, qi, sm, kb, vbe, kib, cache, qb=cfg["qb"], tk=cfg["tk"])
    y_ssd, ssm_new, ssm_conv_new = _ssd(
        proj, sm, st["ssm_h"], st["ssm_conv"], p["ssd_conv_w"][0], p["ssd_conv_b"][0], p["a_log_pad"],
        p["d_exp"], p["ssd_norm_w"][0], p["e_mat"], chunk=cfg["ssd_chunk"])
    kv_shape = (1, b, t, KV_HEADS, HEAD_DIM)
    return att, y_ssd, (k.reshape(kv_shape), v.reshape(kv_shape), ki[None], ssm_new[None],
                        ssm_conv_new[None])


def _layer1_mixer(proj, st, p, cfg):
    b = proj.shape[0]
    y_lru, lru_new, lru_conv_new = _lru(
        proj, p["lru_conv_w"][0], p["lru_conv_b"][0], p["lru_w_a"][0], p["lru_b_a"][0],
        p["lru_w_x"][0], p["lru_b_x"][0], p["lru_lambda"][0], st["lru_h"].reshape(b, 1, LRU_WIDTH),
        st["lru_conv"], tt=cfg["lru_tt"])
    return y_lru, (lru_new.reshape(1, b, LRU_WIDTH), lru_conv_new[None])


def _forward(xp, xs, mods_p, mods_s, st_p, st_s, p, cfg_p, cfg_s):
    bb, tt = cfg_p["bb"], cfg_p["tt"]

    def ffn(xp, xs, l):
        (_, _, _, sh_p, sc_p, g_p), (_, _, _, sh_s, sc_s, g_s) = mods_p[l], mods_s[l]
        gu_p, gu_s = _norm_mod_matmul(
            xp, p["norm_ffn_w"][l], sc_p, sh_p, [p["ffn_w_gate"], p["ffn_w_up"]], l, bb=bb,
            tt=cfg_p["tt_ffn"], tn=512, out_dtype=BF16, name="ffn_gate_up", ride=(xs, sc_s, sh_s))
        return _matmul_residual([gu_p], p["ffn_w_down"], l, xp, g_p, bb=bb, tt=cfg_p["tt_ffn"],
                                tn=256, name="ffn_down", ride=([gu_s], xs, g_s))

    (sh_p, sc_p, g_p, *_), (sh_s, sc_s, g_s, *_) = mods_p[0], mods_s[0]
    proj_p, proj_s = _norm_mod_matmul(
        xp, p["norm_mix_w"][0], sc_p, sh_p, [p["w_in_ab_t"]], 0, bb=bb, tt=cfg_p["tt_ab"],
        tn=AB_TILE_N, out_dtype=F32, name="in_proj_ab", w_tail=p["w_in_ab_tail"], w_rows_out=True,
        ride=(xs, sc_s, sh_s))
    att_p, y_p, outs0_p = _layer0_mixers(proj_p, st_p, p, cfg_p)
    att_s, y_s, outs0_s = _layer0_mixers(proj_s, st_s, p, cfg_s)
    xp, xs = _matmul_residual([att_p, y_p], p["w_out_ab"], 0, xp, g_p, bb=bb, tt=tt, tn=512,
                              name="out_proj_ab", ride=([att_s, y_s], xs, g_s))
    xp, xs = ffn(xp, xs, 0)

    (sh_p, sc_p, g_p, *_), (sh_s, sc_s, g_s, *_) = mods_p[1], mods_s[1]
    proj_p, proj_s = _norm_mod_matmul(
        xp, p["norm_mix_w"][1], sc_p, sh_p, [p["w_in_c"]], 0, bb=bb, tt=cfg_p["tt_ab"], tn=512,
        out_dtype=F32, name="in_proj_c", ride=(xs, sc_s, sh_s))
    y_p, outs1_p = _layer1_mixer(proj_p, st_p, p, cfg_p)
    y_s, outs1_s = _layer1_mixer(proj_s, st_s, p, cfg_s)
    xp, xs = _matmul_residual([y_p], p["w_out_c"], 0, xp, g_p, bb=bb, tt=tt, tn=512,
                              name="out_proj_c", ride=([y_s], xs, g_s))
    xp, xs = ffn(xp, xs, 1)
    return (xp,) + outs0_p + outs1_p, (xs,) + outs0_s + outs1_s


PROMPT_CFG = dict(bb=1, tt=2048, tt_ab=1024, tt_ffn=1024, post_tt=1024, qb=128, tk=512,
                  ssd_chunk=128, lru_tt=512)
SAMPLE_CFG = dict(post_tt=32, qb=32, tk=384, ssd_chunk=32, lru_tt=32)


def kernel(x_prompt, x_sample, cache_attn_k, cache_attn_v, cache_idx_k, state_ssm, state_ssm_conv,
           state_lru, state_lru_conv, c_prompt, c_sample, ada_w, ada_b, norm_mix_w, norm_ffn_w,
           w_in_ab, q_norm_w, k_norm_w, ssd_conv_w, ssd_conv_b, ssd_dt_bias, ssd_a_log, ssd_d,
           ssd_norm_w, w_out_ab, w_in_c, lru_conv_w, lru_conv_b, lru_w_a, lru_b_a, lru_w_x, lru_b_x,
           lru_lambda, w_out_c, ffn_w_gate, ffn_w_up, ffn_w_down):
    bp, bs = x_prompt.shape[0], x_sample.shape[0]
    p = dict(norm_mix_w=norm_mix_w, norm_ffn_w=norm_ffn_w, q_norm_w=q_norm_w, k_norm_w=k_norm_w,
             ssd_conv_w=ssd_conv_w, ssd_conv_b=ssd_conv_b, ssd_norm_w=ssd_norm_w, w_out_ab=w_out_ab,
             w_in_c=w_in_c, lru_conv_w=lru_conv_w, lru_conv_b=lru_conv_b, lru_w_a=lru_w_a,
             lru_b_a=lru_b_a, lru_w_x=lru_w_x, lru_b_x=lru_b_x, lru_lambda=lru_lambda,
             w_out_c=w_out_c, ffn_w_gate=ffn_w_gate, ffn_w_up=ffn_w_up, ffn_w_down=ffn_w_down)
    p["w_in_ab_t"] = jnp.swapaxes(w_in_ab, 1, 2)
    p["w_in_ab_tail"] = _w_in_ab_tail(p["w_in_ab_t"][0])
    p["dt_bias_pad"] = _pad_lanes(ssd_dt_bias[0], SM_DT)
    p["a_log_pad"] = _pad_lanes(ssd_a_log[0], SM_DT)
    p["d_exp"] = jnp.repeat(ssd_d[0], SSD_HEADDIM).reshape(1, SSD_INNER)
    p["e_mat"] = _head_expansion()

    m_rows = 32
    c_all = jnp.concatenate([c_prompt, c_sample, jnp.zeros((m_rows - bp - bs, D_MODEL), F32)], axis=0)
    mod = _modulation(c_all, ada_w, ada_b)

    def group_mods(r0, nb):
        return [[mod[l, r0:r0 + nb, i * D_MODEL:(i + 1) * D_MODEL].reshape(nb, 1, D_MODEL)
                 for i in range(6)] for l in range(mod.shape[0])]

    zeros = lambda *s: jnp.zeros(s, F32)
    st_p = dict(past_k=None, past_v=None, past_ki=None,
                ssm_h=zeros(bp, SSD_HEADS, SSD_HEADDIM, SSD_STATE),
                ssm_conv=zeros(bp, CONV_W - 1, SSD_CONV_DIM), lru_h=zeros(bp, LRU_WIDTH),
                lru_conv=zeros(bp, CONV_W - 1, LRU_WIDTH))
    st_s = dict(past_k=cache_attn_k[0], past_v=cache_attn_v[0], past_ki=cache_idx_k[0],
                ssm_h=state_ssm[0], ssm_conv=state_ssm_conv[0], lru_h=state_lru[0],
                lru_conv=state_lru_conv[0])
    out_p, out_s = _forward(x_prompt, x_sample, group_mods(0, bp), group_mods(bp, bs), st_p, st_s, p,
                            PROMPT_CFG, SAMPLE_CFG)
    return (out_p[0], out_s[0]) + out_p[1:] + out_s[1:]
```

```python
import functools

import numpy as np
import jax
import jax.numpy as jnp
from jax import lax
from jax.experimental import pallas as pl
from jax.experimental.pallas import tpu as pltpu

F32 = jnp.float32
BF16 = jnp.bfloat16
HIGHEST = lax.Precision.HIGHEST

D_MODEL = 2048
CHUNK = 64
CHUNK_SHIFT = 6
ATT_HEADS = 8
KV_HEADS = 2
HEAD_DIM = 128
IDX_HEADS = 16
IDX_DIM = 64
TOPK_MAX = 256
ROPE_THETA = 10000.0
SSD_HEADS = 16
SSD_HEADDIM = 64
SSD_GROUPS = 2
SSD_STATE = 128
SSD_INNER = SSD_HEADS * SSD_HEADDIM
SSD_CONV_DIM = SSD_INNER + 2 * SSD_GROUPS * SSD_STATE
CONV_W = 4
LRU_WIDTH = D_MODEL
LRU_BLOCKS = 16
LRU_BLOCK_DIM = LRU_WIDTH // LRU_BLOCKS
LRU_C = 8.0
EPS = 1e-6
ATT_WIDTH = ATT_HEADS * HEAD_DIM
KV_WIDTH = KV_HEADS * HEAD_DIM
VEXT_WIDTH = 2 * KV_WIDTH
QI_WIDTH = IDX_HEADS * IDX_DIM
Q_SCALE = HEAD_DIM ** -0.5 * float(np.log2(np.e))

LANES = 128
SUBLANES = 8
VMEM_LIMIT_BYTES = 62 * 1024 * 1024
SINGLE_BUFFER_ROWS = 2048
NORM_CHUNKS = 4

OFF_Q = 0
OFF_K = OFF_Q + ATT_WIDTH
OFF_V = OFF_K + KV_WIDTH
OFF_QI = OFF_V + KV_WIDTH
OFF_SMALL = OFF_QI + QI_WIDTH
SM_WI = IDX_DIM
SM_DT = IDX_DIM + IDX_HEADS
OFF_Z = OFF_SMALL + SM_DT
OFF_XBC = OFF_Z + SSD_INNER
OFF_DT = OFF_XBC + SSD_CONV_DIM
AB_WIDTH = OFF_DT + SSD_HEADS
Z_BLOCK = 1280
XBC_BLOCK = 1792
AB_TILE_N = 512
AB_PAD_N = 5632
assert OFF_Z % LANES == SM_DT and OFF_XBC % LANES == SM_DT and OFF_DT % LANES == SM_DT
assert (OFF_Z - SM_DT) % Z_BLOCK == 0 and OFF_Z - SM_DT + Z_BLOCK >= OFF_Z + SSD_INNER
assert (OFF_XBC - SM_DT) % XBC_BLOCK == 0 and XBC_BLOCK >= OFF_DT - OFF_XBC + LANES
assert OFF_XBC - SM_DT + XBC_BLOCK <= AB_PAD_N

NEG_BIG = -1e30


def _params(n_axes):
    return pltpu.CompilerParams(dimension_semantics=("arbitrary",) * n_axes,
                                vmem_limit_bytes=VMEM_LIMIT_BYTES)


def _silu(x):
    return x * jax.nn.sigmoid(x)


def _softplus(x):
    return jnp.maximum(x, 0.0) + jnp.log1p(jnp.exp(-jnp.abs(x)))


def _mod_kernel(c_ref, w_ref, b_ref, o_ref):
    a = _silu(c_ref[...]).astype(BF16)
    o_ref[0] = jnp.dot(a, w_ref[0].astype(BF16), preferred_element_type=F32) + b_ref[0]


def _modulation(c_all, ada_w, ada_b):
    depth, d, n = ada_w.shape
    m = c_all.shape[0]
    tn = 1024
    return pl.pallas_call(
        _mod_kernel,
        grid=(depth, n // tn),
        in_specs=[pl.BlockSpec((m, d), lambda l, j: (0, 0)),
                  pl.BlockSpec((1, d, tn), lambda l, j: (l, 0, j)),
                  pl.BlockSpec((1, 1, tn), lambda l, j: (l, 0, j))],
        out_specs=pl.BlockSpec((1, m, tn), lambda l, j: (l, 0, j)),
        out_shape=jax.ShapeDtypeStruct((depth, m, n), F32),
        compiler_params=_params(2),
        name="adaln_mod",
    )(c_all, ada_w, ada_b.reshape(depth, 1, n))


def _first_row_tile():
    return (pl.program_id(0) == 0) & (pl.program_id(1) == 0)


def _nmm_kernel(*refs, swiglu, tail, w_rows_out, ride):
    refs = list(refs)
    x_ref, nw_ref, sc_ref, sh_ref = refs[:4]
    del refs[:4]
    if ride:
        xr_ref, scr_ref, shr_ref = refs[:3]
        del refs[:3]
    n_w = 2 if swiglu or tail else 1
    w_refs = refs[:n_w]
    del refs[:n_w]
    if ride:
        o_ref, or_ref, h_ref, hr_ref = refs
    else:
        o_ref, h_ref = refs
    k = pl.program_id(2)

    def matmul(h, w):
        dims = (((1,), (1,)), ((), ())) if w_rows_out else (((1,), (0,)), ((), ()))
        return lax.dot_general(h, w, dims, preferred_element_type=F32)

    def finish(accs, out_ref):
        o = _silu(accs[0]) * accs[1] if swiglu else accs[0]
        out_ref[...] = o.reshape(out_ref.shape).astype(out_ref.dtype)

    def project(ws, rows_ref, out_ref):
        h = rows_ref[...]
        finish([matmul(h, w) for w in ws], out_ref)

    def norm_project(ws, src_ref, scale_ref, shift_ref, rows_ref, out_ref):
        x = src_ref[...]
        d = x.shape[-1]
        rs = lax.rsqrt(jnp.mean(x * x, axis=-1, keepdims=True) + EPS)
        kc = d // NORM_CHUNKS
        accs = None
        for c in range(NORM_CHUNKS):
            cols = slice(c * kc, (c + 1) * kc)
            hc = (x[..., cols] * rs * nw_ref[..., cols] * (1.0 + scale_ref[..., cols])
                  + shift_ref[..., cols])
            hc = hc.reshape(rows_ref.shape[0], kc).astype(BF16)
            rows_ref[:, cols] = hc
            part = [matmul(hc, w[:, cols] if w_rows_out else w[cols, :]) for w in ws]
            accs = part if accs is None else [a + b for a, b in zip(accs, part)]
        finish(accs, out_ref)

    def step(tile_refs, first):
        ws = [w[...].astype(BF16) for w in tile_refs]
        if first:
            norm_project(ws, x_ref, sc_ref, sh_ref, h_ref, o_ref)
            if ride:
                pl.when(_first_row_tile())(
                    lambda: norm_project(ws, xr_ref, scr_ref, shr_ref, hr_ref, or_ref))
        else:
            project(ws, h_ref, o_ref)
            if ride:
                pl.when(_first_row_tile())(lambda: project(ws, hr_ref, or_ref))

    last = pl.num_programs(2) - 1 if tail else pl.num_programs(2)
    pl.when(k == 0)(lambda: step(w_refs[:n_w - tail], True))
    pl.when((k > 0) & (k < last))(lambda: step(w_refs[:n_w - tail], False))
    if tail:
        pl.when(k == last)(lambda: step(w_refs[1:], False))


def _row_block_mode(rows):
    return dict(pipeline_mode=pl.Buffered(1)) if rows >= SINGLE_BUFFER_ROWS else {}


def _ride_col_tile(n_tiles):
    return lambda i, j, k: (0, 0, jnp.where((i == 0) & (j == 0), k, n_tiles - 1))


def _norm_mod_matmul(x, nw, sc, sh, ws, layer, *, bb, tt, tn, out_dtype, name, w_tail=None,
                     w_rows_out=False, ride=None):
    b, t, d = x.shape
    swiglu = len(ws) == 2
    n_main = ws[0].shape[1 if w_rows_out else 2] // tn
    n_tiles = n_main + (w_tail is not None)
    n = n_tiles * tn
    xmap = lambda i, j, k: (i, j, 0)
    mmap = lambda i, j, k: (i, 0, 0)
    const3 = lambda i, j, k: (0, 0, 0)
    tile = lambda k: jnp.minimum(k, n_main - 1)
    if w_rows_out:
        w_specs = [pl.BlockSpec((None, tn, d), lambda i, j, k: (layer, tile(k), 0))] * len(ws)
    else:
        w_specs = [pl.BlockSpec((None, d, tn), lambda i, j, k: (layer, 0, tile(k)))] * len(ws)
    operands = list(ws)
    if w_tail is not None:
        w_specs.append(pl.BlockSpec(w_tail.shape, lambda i, j, k: (0, 0)))
        operands.append(w_tail)
    in_specs = [pl.BlockSpec((bb, tt, d), xmap, **_row_block_mode(bb * tt)),
                pl.BlockSpec((1, 1, d), const3),
                pl.BlockSpec((bb, 1, d), mmap),
                pl.BlockSpec((bb, 1, d), mmap)]
    out_specs = [pl.BlockSpec((bb, tt, tn), lambda i, j, k: (i, j, k))]
    out_shape = [jax.ShapeDtypeStruct((b, t, n), out_dtype)]
    scratch = [pltpu.VMEM((bb * tt, d), BF16)]
    ride_ops = ()
    if ride is not None:
        ride_ops = ride
        rb, rt, _ = ride[0].shape
        in_specs += [pl.BlockSpec((rb, rt, d), const3, pipeline_mode=pl.Buffered(1)),
                     pl.BlockSpec((rb, 1, d), const3), pl.BlockSpec((rb, 1, d), const3)]
        out_specs.append(pl.BlockSpec((rb, rt, tn), _ride_col_tile(n_tiles)))
        out_shape.append(jax.ShapeDtypeStruct((rb, rt, n), out_dtype))
        scratch.append(pltpu.VMEM((rb * rt, d), BF16))
    outs = pl.pallas_call(
        functools.partial(_nmm_kernel, swiglu=swiglu, tail=w_tail is not None,
                          w_rows_out=w_rows_out, ride=ride is not None),
        grid=(b // bb, t // tt, n_tiles),
        in_specs=in_specs + w_specs,
        out_specs=out_specs,
        out_shape=out_shape,
        scratch_shapes=scratch,
        compiler_params=_params(3),
        name=name,
    )(x, nw.reshape(1, 1, d), sc, sh, *ride_ops, *operands)
    return outs if ride is not None else outs[0]


def _mmr_kernel(*refs, k_sizes, ride):
    refs = list(refs)
    n_a = len(k_sizes)
    groups = [refs[:n_a + 2]]
    del refs[:n_a + 2]
    if ride:
        groups.append(refs[:n_a + 2])
        del refs[:n_a + 2]
    w_ref = refs[0]
    out_refs = refs[1:]
    offs = np.cumsum((0,) + tuple(k_sizes))
    ws = [w_ref[offs[i]:offs[i + 1], :].astype(BF16) for i in range(n_a)]

    def apply(group, o_ref):
        *a_refs, x_ref, g_ref = group
        bb, tt, tn = o_ref.shape
        acc = jnp.zeros((bb * tt, tn), F32)
        for a_ref, w, ks in zip(a_refs, ws, k_sizes):
            acc = acc + jnp.dot(a_ref[...].reshape(bb * tt, ks), w, preferred_element_type=F32)
        o_ref[...] = x_ref[...] + g_ref[...] * acc.reshape(bb, tt, tn)

    apply(groups[0], out_refs[0])
    if ride:
        pl.when(_first_row_tile())(lambda: apply(groups[1], out_refs[1]))


def _matmul_residual(a_list, w, layer, x, g, *, bb, tt, tn, name, ride=None):
    b, t, d = x.shape
    k_sizes = tuple(a.shape[-1] for a in a_list)
    k_total = sum(k_sizes)
    n_tiles = d // tn
    in_specs = ([pl.BlockSpec((bb, tt, ks), lambda i, j, k: (i, j, 0), **_row_block_mode(bb * tt))
                 for ks in k_sizes]
                + [pl.BlockSpec((bb, tt, tn), lambda i, j, k: (i, j, k)),
                   pl.BlockSpec((bb, 1, tn), lambda i, j, k: (i, 0, k))])
    out_specs = [pl.BlockSpec((bb, tt, tn), lambda i, j, k: (i, j, k))]
    out_shape = [jax.ShapeDtypeStruct((b, t, d), F32)]
    ride_ops = ()
    if ride is not None:
        ra_list, rx, rg = ride
        rb, rt, _ = rx.shape
        ride_ops = (*ra_list, rx, rg)
        col = _ride_col_tile(n_tiles)
        in_specs += ([pl.BlockSpec((rb, rt, ks), lambda i, j, k: (0, 0, 0),
                                   pipeline_mode=pl.Buffered(1)) for ks in k_sizes]
                     + [pl.BlockSpec((rb, rt, tn), col), pl.BlockSpec((rb, 1, tn), col)])
        out_specs.append(pl.BlockSpec((rb, rt, tn), col))
        out_shape.append(jax.ShapeDtypeStruct(rx.shape, F32))
    outs = pl.pallas_call(
        functools.partial(_mmr_kernel, k_sizes=k_sizes, ride=ride is not None),
        grid=(b // bb, t // tt, n_tiles),
        in_specs=in_specs + [pl.BlockSpec((None, k_total, tn), lambda i, j, k: (layer, 0, k))],
        out_specs=out_specs,
        out_shape=out_shape,
        compiler_params=_params(3),
        name=name,
    )(*a_list, x, g, *ride_ops, w)
    return outs if ride is not None else outs[0]


def _swap_halves(x, half):
    w = x.shape[-1]
    if w > LANES:
        return jnp.concatenate([_swap_halves(x[..., c * LANES:(c + 1) * LANES], half)
                                for c in range(w // LANES)], axis=-1)
    if w == 2 * half:
        return pltpu.roll(x, half, x.ndim - 1)
    lane = lax.broadcasted_iota(jnp.int32, x.shape, x.ndim - 1)
    first = (lane & half) == 0
    return jnp.where(first, pltpu.roll(x, w - half, x.ndim - 1), pltpu.roll(x, half, x.ndim - 1))


def _lane_aligned(slab, offset, width):
    return pltpu.roll(slab, slab.shape[-1] - offset, slab.ndim - 1)[:, :width]


def _rope(x, cos, sin_signed, half):
    reps = x.shape[-1] // cos.shape[-1]
    if reps > 1:
        cos = jnp.concatenate([cos] * reps, axis=-1)
        sin_signed = jnp.concatenate([sin_signed] * reps, axis=-1)
    return x * cos + _swap_halves(x, half) * sin_signed


def _conv_carry_init(state_ref, cbuf_ref):
    cbuf_ref[0:SUBLANES, :] = jnp.zeros((SUBLANES, cbuf_ref.shape[1]), F32)
    cbuf_ref[SUBLANES - (CONV_W - 1):SUBLANES, :] = state_ref[0]


def _conv_state(tail):
    return tail[SUBLANES - (CONV_W - 1):, :]


def _causal_conv_tile(u, w_ref, b_ref, cbuf_ref):
    tt = u.shape[0]
    cbuf_ref[SUBLANES:SUBLANES + tt, :] = u
    full = cbuf_ref[...]
    out = b_ref[...]
    for j in range(CONV_W):
        shift = CONV_W - 1 - j
        rows = pltpu.roll(full, shift, 0) if shift else full
        out = out + rows[SUBLANES:SUBLANES + tt, :] * w_ref[j:j + 1, :]
    tail = cbuf_ref[tt:tt + SUBLANES, :]
    cbuf_ref[0:SUBLANES, :] = tail
    return out, tail


def _causal_conv(u, state_ref, w_ref, b_ref, cbuf_ref, new_ref, first, last):
    pl.when(first)(lambda: _conv_carry_init(state_ref, cbuf_ref))
    out, tail = _causal_conv_tile(u, w_ref, b_ref, cbuf_ref)

    @pl.when(last)
    def _():
        new_ref[0] = _conv_state(tail)

    return out


def _post0_kernel(p_ref, pdt_ref, cq_ref, sq_ref, ci_ref, si_ref, qn_ref, kn_ref, dtb_ref,
                  q_ref, qi_ref, k_ref, v_ref, ki_ref, sm_ref, kb_ref, vb_ref, kib_ref):
    cq, sq, ci, si = cq_ref[...], sq_ref[...], ci_ref[...], si_ref[...]

    def head_norm_rope(x, w):
        ms = jnp.mean(x * x, axis=-1, keepdims=True)
        return _rope(x * lax.rsqrt(ms + EPS) * w, cq, sq, HEAD_DIM // 2)

    for h in range(ATT_HEADS):
        xh = p_ref[0, :, OFF_Q + h * HEAD_DIM:OFF_Q + (h + 1) * HEAD_DIM]
        q_ref[0, :, h * HEAD_DIM:(h + 1) * HEAD_DIM] = (head_norm_rope(xh, qn_ref[...])
                                                         * Q_SCALE).astype(BF16)
    ones = jnp.ones((p_ref.shape[1], HEAD_DIM), BF16)
    for h in range(KV_HEADS):
        xh = p_ref[0, :, OFF_K + h * HEAD_DIM:OFF_K + (h + 1) * HEAD_DIM]
        kh = head_norm_rope(xh, kn_ref[...])
        vh = p_ref[0, :, OFF_V + h * HEAD_DIM:OFF_V + (h + 1) * HEAD_DIM]
        k_ref[0, pl.ds(h, xh.shape[0], stride=KV_HEADS), :] = kh
        v_ref[0, pl.ds(h, xh.shape[0], stride=KV_HEADS), :] = vh
        kb_ref[0, :, h * HEAD_DIM:(h + 1) * HEAD_DIM] = kh.astype(BF16)
        vb_ref[0, :, 2 * h * HEAD_DIM:(2 * h + 1) * HEAD_DIM] = vh.astype(BF16)
        vb_ref[0, :, (2 * h + 1) * HEAD_DIM:(2 * h + 2) * HEAD_DIM] = ones

    qi = p_ref[0, :, OFF_QI:OFF_QI + QI_WIDTH]
    qi_ref[0] = _rope(qi, ci, si, IDX_DIM // 2).astype(BF16)

    sm = p_ref[0, :, OFF_SMALL:OFF_SMALL + LANES]
    lane = lax.broadcasted_iota(jnp.int32, sm.shape, 1)
    ki_part = _rope(sm, ci, si, IDX_DIM // 2)
    wi_part = sm * (IDX_HEADS ** -0.5 * IDX_DIM ** -0.5)
    dt_part = _softplus(pdt_ref[0] + dtb_ref[...])
    sm_out = jnp.where(lane < SM_WI, ki_part,
                       jnp.where(lane < SM_DT, wi_part,
                                 jnp.where(lane < SM_DT + SSD_HEADS, dt_part, 0.0)))
    sm_ref[0] = sm_out
    ki_ref[0] = sm_out[:, :IDX_DIM]
    kib_ref[0] = sm_out[:, :IDX_DIM].astype(BF16)


def _post0(proj, tabs, q_norm_w, k_norm_w, dt_bias_pad, *, tt):
    b, t, n = proj.shape
    tok = lambda w, col=0: pl.BlockSpec((1, tt, w), lambda i, j: (i, j, col))
    tab = pl.BlockSpec((tt, LANES), lambda i, j: (j, 0))
    full2 = lambda a: pl.BlockSpec(a.shape, lambda i, j: (0, 0))
    out_shapes = [
        jax.ShapeDtypeStruct((b, t, ATT_WIDTH), BF16),
        jax.ShapeDtypeStruct((b, t, QI_WIDTH), BF16),
        jax.ShapeDtypeStruct((b, t * KV_HEADS, HEAD_DIM), F32),
        jax.ShapeDtypeStruct((b, t * KV_HEADS, HEAD_DIM), F32),
        jax.ShapeDtypeStruct((b, t, IDX_DIM), F32),
        jax.ShapeDtypeStruct((b, t, LANES), F32),
        jax.ShapeDtypeStruct((b, t, KV_WIDTH), BF16),
        jax.ShapeDtypeStruct((b, t, VEXT_WIDTH), BF16),
        jax.ShapeDtypeStruct((b, t, IDX_DIM), BF16),
    ]
    kv_rows = pl.BlockSpec((1, tt * KV_HEADS, HEAD_DIM), lambda i, j: (i, j, 0))
    out_specs = [tok(ATT_WIDTH), tok(QI_WIDTH), kv_rows, kv_rows, tok(IDX_DIM),
                 tok(LANES), tok(KV_WIDTH), tok(VEXT_WIDTH), tok(IDX_DIM)]
    qn = q_norm_w.reshape(1, HEAD_DIM)
    kn = k_norm_w.reshape(1, HEAD_DIM)
    head = OFF_SMALL + LANES
    return pl.pallas_call(
        _post0_kernel,
        grid=(b, t // tt),
        in_specs=[tok(head), tok(LANES, (OFF_DT - SM_DT) // LANES), tab, tab, tab, tab, full2(qn),
                  full2(kn), full2(dt_bias_pad)],
        out_specs=out_specs,
        out_shape=out_shapes,
        compiler_params=_params(2),
        name="post_proj0",
    )(proj, proj, *tabs, qn, kn, dt_bias_pad)


def _dsa_kernel(q_ref, qi_ref, sm_ref, k_ref, v_ref, ki_ref, *rest, qb, tk, n_keys, past, topk,
                n_bisect, keys_on_sublanes):
    if past:
        pk_ref, pv_ref, pki_ref = rest[:3]
        rest = rest[3:]
    (o_ref, sc_ref, acc_ref, qs_ref, s0_ref, s1_ref, mt0_ref, mt1_ref, m_ref) = rest[:9]
    if past:
        kall_ref, vall_ref, kiall_ref = rest[9:]
        t_new = k_ref.shape[1]
        pad = kall_ref.shape[0] - n_keys
        for g in range(KV_HEADS):
            head_rows = pl.ds(g, past, stride=KV_HEADS)
            kall_ref[0:past, g * HEAD_DIM:(g + 1) * HEAD_DIM] = pk_ref[0, head_rows, :].astype(BF16)
            vall_ref[0:past, 2 * g * HEAD_DIM:(2 * g + 1) * HEAD_DIM] = pv_ref[0, head_rows, :].astype(BF16)
            vall_ref[0:past, (2 * g + 1) * HEAD_DIM:(2 * g + 2) * HEAD_DIM] = jnp.ones(
                (past, HEAD_DIM), BF16)
        kiall_ref[0:past, :] = pki_ref[0].astype(BF16)
        for dst, src in ((kall_ref, k_ref), (vall_ref, v_ref), (kiall_ref, ki_ref)):
            dst[past:past + t_new, :] = src[0]
            dst[past + t_new:, :] = jnp.zeros((pad, dst.shape[1]), BF16)
        k_src, v_src, ki_src = kall_ref, vall_ref, kiall_ref
    else:
        k_src, v_src, ki_src = k_ref.at[0], v_ref.at[0], ki_ref.at[0]
    kax = 0 if keys_on_sublanes else 1
    tile_shape = (tk, qb) if keys_on_sublanes else (qb, tk)
    vec_shape = (1, qb) if keys_on_sublanes else (qb, 1)
    n_acc = 8
    part_shape = (n_acc, SUBLANES, qb) if keys_on_sublanes else (qb, LANES)
    j = pl.program_id(1)
    pos0 = past + j * qb
    q_chunk = lax.shift_right_logical(pos0 + lax.broadcasted_iota(jnp.int32, vec_shape, 1 - kax),
                                      CHUNK_SHIFT)
    n_valid = jnp.minimum((q_chunk + 1) * CHUNK, n_keys)
    max_valid = jnp.minimum(((pos0 + qb - 1) // CHUNK + 1) * CHUNK, n_keys)
    nkt = (max_valid + tk - 1) // tk
    kf = float(topk)
    active_f = jnp.where(n_valid > topk, 1.0, 0.0)

    def key_index(kt):
        return kt * tk + lax.broadcasted_iota(jnp.int32, tile_shape, kax)

    nt_dims = (((1,), (1,)), ((), ()))
    for h in range(IDX_HEADS):
        qs_ref[h * qb:(h + 1) * qb, :] = qi_ref[0, :, h * IDX_DIM:(h + 1) * IDX_DIM]
    if keys_on_sublanes:
        wi_t = sm_ref[0].T
        head_w = lambda h: wi_t[SM_WI + h:SM_WI + h + 1, :]
    else:
        wis = sm_ref[0][:, SM_WI:SM_WI + IDX_HEADS]
        head_w = lambda h: wis[:, h:h + 1]

    def score_tile(kt, carry):
        kit = ki_src[pl.ds(pl.multiple_of(kt * tk, tk), tk), :]
        acc = jnp.zeros(tile_shape, F32)
        if keys_on_sublanes:
            for pr in range(IDX_HEADS // 2):
                s2 = lax.dot_general(kit, qs_ref[2 * pr * qb:(2 * pr + 2) * qb, :], nt_dims,
                                     preferred_element_type=F32)
                for e in range(2):
                    acc = acc + head_w(2 * pr + e) * jnp.maximum(s2[:, e * qb:(e + 1) * qb], 0.0)
        else:
            s_all = lax.dot_general(qs_ref[...], kit, nt_dims, preferred_element_type=F32)
            for h in range(IDX_HEADS):
                acc = acc + head_w(h) * jnp.maximum(s_all[h * qb:(h + 1) * qb, :], 0.0)
        valid = key_index(kt) < n_valid
        sc_ref[kt] = jnp.where(valid, acc, -jnp.inf)
        mn, mx = carry
        return (jnp.minimum(mn, fold(jnp.where(valid, acc, jnp.inf), jnp.minimum, jnp.min)),
                jnp.maximum(mx, fold(jnp.where(valid, acc, -jnp.inf), jnp.maximum, jnp.max)))

    def fold(m, op2, red):
        if keys_on_sublanes:
            return red(m.reshape(tk // (n_acc * SUBLANES), n_acc, SUBLANES, qb), axis=0)
        f = m[:, 0:LANES]
        for c in range(1, tk // LANES):
            f = op2(f, m[:, c * LANES:(c + 1) * LANES])
        return f

    def finish(part, red):
        if keys_on_sublanes:
            part = red(part, axis=0)
        return red(part, axis=kax, keepdims=True)

    def reduce_tiles(make, op2, red, init):
        def body(kt, part):
            return op2(part, fold(make(sc_ref[kt], kt), op2, red))
        return finish(lax.fori_loop(0, nkt, body, jnp.full(part_shape, init, F32)), red)

    def count(pred):
        return reduce_tiles(lambda t, kt: jnp.where(pred(t, kt), 1.0, 0.0), jnp.add, jnp.sum, 0.0)

    def row_max(pred):
        return reduce_tiles(lambda t, kt: jnp.where(pred(t, kt), t, -jnp.inf), jnp.maximum, jnp.max,
                            -jnp.inf)

    mn_part, mx_part = lax.fori_loop(0, nkt, score_tile, (jnp.full(part_shape, jnp.inf, F32),
                                                          jnp.full(part_shape, -jnp.inf, F32)))
    lo0, hi0 = finish(mn_part, jnp.min), finish(mx_part, jnp.max)

    def any_set(flag_f):
        return jnp.max(flag_f) > 0.0

    def bisect(_, state):
        lo, hi, c_lo = state
        mid = lo + (hi - lo) * 0.5
        c = count(lambda t, kt: t >= mid)
        ge = c >= kf
        return jnp.where(ge, mid, lo), jnp.where(ge, hi, mid), jnp.where(ge, c, c_lo)

    lo, hi, c_lo = lax.fori_loop(0, n_bisect, bisect, (lo0, hi0, n_valid.astype(F32)))
    found_f = jnp.where(c_lo == kf, active_f, 0.0)

    def walk_down():
        w0 = row_max(lambda t, kt: t <= hi)
        c0 = count(lambda t, kt: t >= w0)

        def walk_cond(state):
            w, c = state
            return any_set(jnp.where(c < kf, active_f, 0.0))

        def walk_body(state):
            w, c = state
            w2 = row_max(lambda t, kt: t < w)
            c2 = count(lambda t, kt: t >= w2)
            upd = c < kf
            return jnp.where(upd, w2, w), jnp.where(upd, c2, c)

        return lax.while_loop(walk_cond, walk_body,
                              (jnp.where(found_f > 0.5, lo, w0),
                               jnp.where(found_f > 0.5, kf, c0)))

    w, c_ge = lax.cond(any_set(active_f - found_f), walk_down,
                       lambda: (lo, jnp.full(vec_shape, kf, F32)))
    thr = jnp.where(active_f > 0.5, w, -jnp.inf)
    tied_f = jnp.where(c_ge > kf, active_f, 0.0)

    def last_kept_tie():
        need = kf - count(lambda t, kt: t > thr)
        n_steps = max(1, int(np.ceil(np.log2(sc_ref.shape[0] * tk))) + 1)

        def step(_, lh):
            lo_i, hi_i = lh
            mid = lax.shift_right_arithmetic(lo_i + hi_i, 1)
            ge = count(lambda t, kt: (t == thr) & (key_index(kt) <= mid)) >= need
            return jnp.where(ge, lo_i, mid), jnp.where(ge, mid, hi_i)

        lo_i = jnp.full(vec_shape, -1, jnp.int32)
        hi_i = jnp.full(vec_shape, sc_ref.shape[0] * tk - 1, jnp.int32)
        return lax.fori_loop(0, n_steps, step, (lo_i, hi_i))[1]

    last_tie = lax.cond(any_set(tied_f), last_kept_tie, lambda: jnp.zeros(vec_shape, jnp.int32))

    def bias_tile(kt):
        t, kidx = sc_ref[kt], key_index(kt)
        keep_tie = (kidx <= last_tie) | (tied_f < 0.5)
        sel = ((t > thr) | ((t == thr) & keep_tie)) & (kidx < n_valid)
        bias = jnp.where(sel, 0.0, NEG_BIG)
        return bias.T if keys_on_sublanes else bias

    rep = ATT_HEADS // KV_HEADS
    qgs = [jnp.concatenate(
        [q_ref[0, :, (g * rep + r) * HEAD_DIM:(g * rep + r + 1) * HEAD_DIM] for r in range(rep)],
        axis=0) for g in range(KV_HEADS)]
    acc_ref[...] = jnp.zeros(acc_ref.shape, F32)
    bufs = ((s0_ref, mt0_ref), (s1_ref, mt1_ref))

    def qk_tile(kt, buf):
        s_ref, mt_ref = bufs[buf]
        row0 = pl.multiple_of(kt * tk, tk)
        bias = bias_tile(kt)
        for g in range(KV_HEADS):
            kg = k_src[pl.ds(row0, tk), g * HEAD_DIM:(g + 1) * HEAD_DIM]
            s = lax.dot_general(qgs[g], kg, nt_dims, preferred_element_type=F32)
            s = (s.reshape(rep, qb, tk) + bias[None]).reshape(rep * qb, tk)
            s_ref[g] = s
            mt_ref[g] = jnp.broadcast_to(jnp.max(s, axis=-1, keepdims=True), (rep * qb, LANES))

    def lane_tile(x, width):
        return jnp.concatenate([x] * (width // LANES), axis=1)

    def att_tile(kt, buf, prefetch):
        if prefetch:
            qk_tile(kt + 1, 1 - buf)
        s_ref, mt_ref = bufs[buf]
        row0 = pl.multiple_of(kt * tk, tk)
        for g in range(KV_HEADS):
            vg = v_src[pl.ds(row0, tk), 2 * g * HEAD_DIM:(2 * g + 2) * HEAD_DIM]
            m_old = m_ref[g]
            m_new = jnp.maximum(m_old, mt_ref[g])
            m_ref[g] = m_new
            alpha = jnp.exp2(m_old - m_new)
            p = jnp.exp2(s_ref[g] - lane_tile(m_new, tk)).astype(BF16)
            acc_ref[g] = (lane_tile(alpha, 2 * HEAD_DIM) * acc_ref[g]
                          + jnp.dot(p, vg, preferred_element_type=F32))

    def tile_pair(pi, carry):
        att_tile(2 * pi, 0, True)
        att_tile(2 * pi + 1, 1, True)
        return carry

    def last_two():
        att_tile(nkt - 2, 0, True)
        att_tile(nkt - 1, 1, False)

    def last_one():
        att_tile(nkt - 1, 0, False)

    m_ref[...] = jnp.full(m_ref.shape, NEG_BIG, F32)
    qk_tile(0, 0)
    lax.fori_loop(0, (nkt - 1) // 2, tile_pair, 0)
    lax.cond((nkt & 1) == 0, last_two, last_one)
    for g in range(KV_HEADS):
        acc = acc_ref[g]
        o = acc[:, :HEAD_DIM] / acc[:, HEAD_DIM:]
        for r in range(rep):
            h = g * rep + r
            o_ref[0, :, h * HEAD_DIM:(h + 1) * HEAD_DIM] = o[r * qb:(r + 1) * qb].astype(BF16)


def _dsa_attention(q, qi, sm, k_new, v_new, ki_new, cache, *, qb, tk):
    b, t, _ = q.shape
    past = 0 if cache is None else cache[2].shape[1]
    n_keys = past + t
    nk_pad = -(-n_keys // tk) * tk
    topk = min(TOPK_MAX, n_keys // 4)
    keys_on_sublanes = qb == LANES
    rep = ATT_HEADS // KV_HEADS
    tokq = lambda w: pl.BlockSpec((1, qb, w), lambda i, j: (i, j, 0))
    keys = lambda w: pl.BlockSpec((1, t, w), lambda i, j: (i, 0, 0))
    tile_shape = (tk, qb) if keys_on_sublanes else (qb, tk)
    cache_specs, cache_scratch = [], []
    if cache is not None:
        kv = pl.BlockSpec((1, past * KV_HEADS, HEAD_DIM), lambda i, j: (i, 0, 0))
        cache_specs = [kv, kv, pl.BlockSpec((1, past, IDX_DIM), lambda i, j: (i, 0, 0))]
        cache_scratch = [pltpu.VMEM((nk_pad, w), BF16) for w in (KV_WIDTH, VEXT_WIDTH, IDX_DIM)]
    return pl.pallas_call(
        functools.partial(_dsa_kernel, qb=qb, tk=tk, n_keys=n_keys, past=past, topk=topk,
                          n_bisect=20, keys_on_sublanes=keys_on_sublanes),
        grid=(b, t // qb),
        in_specs=[tokq(ATT_WIDTH), tokq(QI_WIDTH), tokq(LANES), keys(KV_WIDTH), keys(VEXT_WIDTH),
                  keys(IDX_DIM)] + cache_specs,
        out_specs=tokq(ATT_WIDTH),
        out_shape=jax.ShapeDtypeStruct((b, t, ATT_WIDTH), BF16),
        scratch_shapes=[pltpu.VMEM((nk_pad // tk,) + tile_shape, F32),
                        pltpu.VMEM((KV_HEADS, rep * qb, 2 * HEAD_DIM), F32),
                        pltpu.VMEM((IDX_HEADS * qb, IDX_DIM), BF16),
                        pltpu.VMEM((KV_HEADS, rep * qb, tk), F32),
                        pltpu.VMEM((KV_HEADS, rep * qb, tk), F32),
                        pltpu.VMEM((KV_HEADS, rep * qb, LANES), F32),
                        pltpu.VMEM((KV_HEADS, rep * qb, LANES), F32),
                        pltpu.VMEM((KV_HEADS, rep * qb, LANES), F32)] + cache_scratch,
        compiler_params=_params(2),
        name="dsa_attention",
    )(q, qi, sm, k_new, v_new, ki_new, *(cache or ()))


def _ssd_kernel(pfirst_ref, pnext_ref, z_ref, sm_ref, st_ref, cs_ref, cw_ref, cb_ref, alog_ref,
                dexp_ref, nw_ref, e_ref, y_ref, stout_ref, cnew_ref, ht_ref, yi_ref, cbuf_ref, xa_ref,
                xb_ref):
    c = pl.program_id(1)
    nc = pl.num_programs(1)
    L = pfirst_ref.shape[1]
    hp = SSD_INNER // SSD_GROUPS

    def conv_act(p_ref):
        raw = _lane_aligned(p_ref[0, :, :OFF_DT - OFF_XBC + LANES], SM_DT, SSD_CONV_DIM)
        out, tail = _causal_conv_tile(raw, cw_ref, cb_ref, cbuf_ref)
        return _silu(out), tail

    @pl.when(c == 0)
    def _():
        ht_ref[...] = st_ref[0].reshape(SSD_INNER, SSD_STATE).T
        _conv_carry_init(cs_ref, cbuf_ref)
        act, tail = conv_act(pfirst_ref)
        xa_ref[...] = act

        @pl.when(nc == 1)
        def _():
            cnew_ref[0] = _conv_state(tail)

    def chunk(xbc_ref, next_ref):
        _ssd_chunk(xbc_ref, z_ref, sm_ref, alog_ref, dexp_ref, nw_ref, e_ref, y_ref, ht_ref, yi_ref)
        act, tail = conv_act(pnext_ref)
        next_ref[...] = act
        return tail

    tail = lax.cond((c & 1) == 0, lambda: chunk(xa_ref, xb_ref), lambda: chunk(xb_ref, xa_ref))

    @pl.when(c + 1 == nc - 1)
    def _():
        cnew_ref[0] = _conv_state(tail)

    @pl.when(c == nc - 1)
    def _():
        stout_ref[0] = ht_ref[...].T.reshape(SSD_HEADS, SSD_HEADDIM, SSD_STATE)


def _ssd_chunk(xbc_ref, z_ref, sm_ref, alog_ref, dexp_ref, nw_ref, e_ref, y_ref, ht_ref, yi_ref):
    L = xbc_ref.shape[0]
    hp = SSD_INNER // SSD_GROUPS
    xs = xbc_ref[:, 0:SSD_INNER]
    sm = sm_ref[0]
    lane = lax.broadcasted_iota(jnp.int32, (1, LANES), 1)
    is_dt = (lane >= SM_DT) & (lane < SM_DT + SSD_HEADS)
    a_neg = jnp.where(is_dt, -jnp.exp(alog_ref[...]), 0.0)
    dt = jnp.where(is_dt, sm, 0.0)
    rows = lax.broadcasted_iota(jnp.int32, (L, L), 0)
    cols = lax.broadcasted_iota(jnp.int32, (L, L), 1)
    tri = cols <= rows
    e = e_ref[...]
    cum = jnp.dot(tri.astype(F32), dt * a_neg, precision=HIGHEST, preferred_element_type=F32)
    eye = (lax.broadcasted_iota(jnp.int32, (LANES, LANES), 0)
           == lax.broadcasted_iota(jnp.int32, (LANES, LANES), 1)).astype(F32)
    cum_t = lax.dot_general(eye, cum, (((1,), (1,)), ((), ())), precision=HIGHEST,
                            preferred_element_type=F32)
    both = jnp.concatenate([cum, dt], axis=0)
    hi = both.astype(BF16)
    rest = both - hi.astype(F32)
    mid = rest.astype(BF16)
    low = (rest - mid.astype(F32)).astype(BF16)
    both_e = (jnp.dot(hi, e, preferred_element_type=F32) + jnp.dot(mid, e, preferred_element_type=F32)
              + jnp.dot(low, e, preferred_element_type=F32))
    cum_e, dt_e = both_e[:L], both_e[L:]
    last = cum_e[L - 1:L, :]
    xdt = (xs * dt_e).astype(BF16)
    xw = (xs * (jnp.exp(last - cum_e) * dt_e)).astype(BF16)
    ht_old = ht_ref[...]

    for g in range(SSD_GROUPS):
        bg = xbc_ref[:, SSD_INNER + g * SSD_STATE:SSD_INNER + (g + 1) * SSD_STATE]
        cg = xbc_ref[:, SSD_INNER + (SSD_GROUPS + g) * SSD_STATE:
                     SSD_INNER + (SSD_GROUPS + g + 1) * SSD_STATE]
        bgb, cgb = bg.astype(BF16), cg.astype(BF16)
        cb = lax.dot_general(cgb, bgb, (((1,), (1,)), ((), ())), preferred_element_type=F32)
        for hh in range(SSD_HEADS // SSD_GROUPS):
            h = g * (SSD_HEADS // SSD_GROUPS) + hh
            seg = cum[:, SM_DT + h:SM_DT + h + 1] - cum_t[SM_DT + h:SM_DT + h + 1, :]
            decay = jnp.exp(jnp.where(tri, seg, -jnp.inf))
            wts = (cb * decay).astype(BF16)
            yi_ref[:, h * SSD_HEADDIM:(h + 1) * SSD_HEADDIM] = jnp.dot(
                wts, xdt[:, h * SSD_HEADDIM:(h + 1) * SSD_HEADDIM], preferred_element_type=F32)
        ht_g = ht_old[:, g * hp:(g + 1) * hp]
        y_inter = jnp.dot(cgb, ht_g.astype(BF16), preferred_element_type=F32)
        yi_ref[:, g * hp:(g + 1) * hp] = (yi_ref[:, g * hp:(g + 1) * hp]
                                          + y_inter * jnp.exp(cum_e[:, g * hp:(g + 1) * hp]))
        upd = jnp.dot(bg.T.astype(BF16), xw[:, g * hp:(g + 1) * hp],
                      preferred_element_type=F32)
        ht_ref[:, g * hp:(g + 1) * hp] = ht_g * jnp.exp(last[:, g * hp:(g + 1) * hp]) + upd

    y = yi_ref[...] + dexp_ref[...] * xs
    y = y * _silu(_lane_aligned(z_ref[0], SM_DT, SSD_INNER))
    for g in range(SSD_GROUPS):
        yg = y[:, g * hp:(g + 1) * hp]
        ms = jnp.mean(yg * yg, axis=-1, keepdims=True)
        y_ref[0, :, g * hp:(g + 1) * hp] = (yg * lax.rsqrt(ms + EPS)
                                            * nw_ref[:, g * hp:(g + 1) * hp]).astype(BF16)


def _ssd(proj, sm, state, conv_state, conv_w, conv_b, a_log_pad, d_exp, norm_w, e_mat, *, chunk):
    b, t, _ = proj.shape
    n_chunks = t // chunk
    xbc_col = (OFF_XBC - SM_DT) // XBC_BLOCK
    tok = lambda w, col=0: pl.BlockSpec((1, chunk, w), lambda i, j: (i, j, col))
    xbc_first = pl.BlockSpec((1, chunk, XBC_BLOCK), lambda i, j: (i, 0, xbc_col))
    xbc_next = pl.BlockSpec((1, chunk, XBC_BLOCK),
                            lambda i, j: (i, jnp.minimum(j + 1, n_chunks - 1), xbc_col))
    full2 = lambda a: pl.BlockSpec(a.shape, lambda i, j: (0, 0))
    st = pl.BlockSpec((1, SSD_HEADS, SSD_HEADDIM, SSD_STATE), lambda i, j: (i, 0, 0, 0))
    cst = pl.BlockSpec((1, CONV_W - 1, SSD_CONV_DIM), lambda i, j: (i, 0, 0))
    nw = norm_w.reshape(1, SSD_INNER)
    cb = conv_b.reshape(1, SSD_CONV_DIM)
    return pl.pallas_call(
        _ssd_kernel,
        grid=(b, n_chunks),
        in_specs=[xbc_first, xbc_next,
                  tok(Z_BLOCK, (OFF_Z - SM_DT) // Z_BLOCK), tok(LANES), st, cst, full2(conv_w),
                  full2(cb), full2(a_log_pad), full2(d_exp), full2(nw), full2(e_mat)],
        out_specs=[tok(SSD_INNER), st, cst],
        out_shape=[jax.ShapeDtypeStruct((b, t, SSD_INNER), BF16),
                   jax.ShapeDtypeStruct(state.shape, F32),
                   jax.ShapeDtypeStruct((b, CONV_W - 1, SSD_CONV_DIM), F32)],
        scratch_shapes=[pltpu.VMEM((SSD_STATE, SSD_INNER), F32),
                        pltpu.VMEM((chunk, SSD_INNER), F32),
                        pltpu.VMEM((chunk + SUBLANES, SSD_CONV_DIM), F32),
                        pltpu.VMEM((chunk, SSD_CONV_DIM), F32),
                        pltpu.VMEM((chunk, SSD_CONV_DIM), F32)],
        compiler_params=_params(2),
        name="ssd_scan",
    )(proj, proj, proj, sm, state, conv_state, conv_w, cb, a_log_pad, d_exp, nw, e_mat)


def _lru_kernel(gate_ref, xin_ref, cw_ref, cb_ref, wa_ref, wx_ref, ba_ref, bx_ref, lam_ref,
                h0_ref, cs_ref, y_ref, hout_ref, cnew_ref, cbuf_ref, a_ref, u_ref, hp_ref):
    t = pl.program_id(1)
    nt = pl.num_programs(1)
    tt = xin_ref.shape[1]
    xc = _causal_conv(xin_ref[0], cs_ref, cw_ref, cb_ref, cbuf_ref, cnew_ref, t == 0, t == nt - 1)

    @pl.when(t == 0)
    def _():
        hp_ref[...] = h0_ref[0]

    for kb in range(LRU_BLOCKS):
        sl = slice(kb * LRU_BLOCK_DIM, (kb + 1) * LRU_BLOCK_DIM)
        xb = xc[:, sl].astype(BF16)
        a_ref[:, sl] = jnp.dot(xb, wa_ref[kb].astype(BF16), preferred_element_type=F32)
        u_ref[:, sl] = jnp.dot(xb, wx_ref[kb].astype(BF16), preferred_element_type=F32)
    r = jax.nn.sigmoid(a_ref[...] + ba_ref[...])
    i = jax.nn.sigmoid(u_ref[...] + bx_ref[...])
    log_a = -LRU_C * r * _softplus(-lam_ref[...])
    a = jnp.exp(log_a)
    u = jnp.sqrt(-jnp.tanh(log_a) * (1.0 + a * a)) * (i * xc)

    width = a.shape[1]
    a = a.reshape(tt // SUBLANES, SUBLANES, width)
    u = u.reshape(tt // SUBLANES, SUBLANES, width)
    sub = lax.broadcasted_iota(jnp.int32, (1, SUBLANES, width), 1)
    for s in (1, 2, 4):
        m = sub >= s
        a_sh = pltpu.roll(a, s, 1)
        u_sh = pltpu.roll(u, s, 1)
        u = jnp.where(m, a * u_sh + u, u)
        a = jnp.where(m, a * a_sh, a)
    a_ref[...] = a.reshape(tt, width)
    u_ref[...] = u.reshape(tt, width)

    def group(gi, hprev):
        r0 = pl.multiple_of(gi * SUBLANES, SUBLANES)
        hs = a_ref[pl.ds(r0, SUBLANES), :] * hprev + u_ref[pl.ds(r0, SUBLANES), :]
        u_ref[pl.ds(r0, SUBLANES), :] = hs
        return hs[SUBLANES - 1:SUBLANES, :]

    h_last = lax.fori_loop(0, tt // SUBLANES, group, hp_ref[...])
    hp_ref[...] = h_last
    gate = gate_ref[0]
    gelu = 0.5 * gate * (1.0 + jnp.tanh(np.sqrt(2.0 / np.pi) * (gate + 0.044715 * (gate * gate * gate))))
    y_ref[0] = (u_ref[...] * gelu).astype(BF16)

    @pl.when(t == nt - 1)
    def _():
        hout_ref[0] = h_last


def _lru(proj, conv_w, conv_b, w_a, b_a, w_x, b_x, lam, h0, conv_state, *, tt):
    b, t, _ = proj.shape
    w = LRU_WIDTH
    tok = lambda col: pl.BlockSpec((1, tt, w), lambda i, j: (i, j, col))
    full = lambda a: pl.BlockSpec(a.shape, lambda i, j: (0,) * a.ndim)
    row = lambda v: v.reshape(1, w)
    state = pl.BlockSpec((1, CONV_W - 1, w), lambda i, j: (i, 0, 0))
    hspec = pl.BlockSpec((1, 1, w), lambda i, j: (i, 0, 0))
    args = (proj, proj, conv_w, row(conv_b), w_a, w_x, row(b_a), row(b_x), row(lam), h0, conv_state)
    return pl.pallas_call(
        _lru_kernel,
        grid=(b, t // tt),
        in_specs=[tok(0), tok(1)] + [full(a) for a in args[2:9]] + [hspec, state],
        out_specs=[pl.BlockSpec((1, tt, w), lambda i, j: (i, j, 0)), hspec, state],
        out_shape=[jax.ShapeDtypeStruct((b, t, w), BF16),
                   jax.ShapeDtypeStruct((b, 1, w), F32),
                   jax.ShapeDtypeStruct((b, CONV_W - 1, w), F32)],
        scratch_shapes=[pltpu.VMEM((tt + SUBLANES, w), F32), pltpu.VMEM((tt, w), F32),
                        pltpu.VMEM((tt, w), F32), pltpu.VMEM((1, w), F32)],
        compiler_params=_params(2),
        name="rg_lru",
    )(*args)


def _rope_tables(pos, dim):
    half = dim // 2
    inv = np.power(ROPE_THETA, -np.arange(half, dtype=np.float64) / half)
    ang = pos.astype(np.float64)[:, None] * inv[None, :]
    cos = np.concatenate([np.cos(ang), np.cos(ang)], axis=-1)
    sin = np.concatenate([-np.sin(ang), np.sin(ang)], axis=-1)
    reps = LANES // dim
    return (jnp.asarray(np.tile(cos, (1, reps)), F32), jnp.asarray(np.tile(sin, (1, reps)), F32))


def _head_expansion():
    e = np.zeros((LANES, SSD_INNER), np.float32)
    for h in range(SSD_HEADS):
        e[SM_DT + h, h * SSD_HEADDIM:(h + 1) * SSD_HEADDIM] = 1.0
    return jnp.asarray(e, BF16)


def _pad_lanes(v, offset):
    out = jnp.zeros((1, LANES), F32)
    return out.at[0, offset:offset + v.shape[0]].set(v)


def _w_in_ab_tail(w_t):
    whole = AB_WIDTH // AB_TILE_N * AB_TILE_N
    return jnp.pad(w_t[whole:], ((0, AB_PAD_N - AB_WIDTH), (0, 0)))


def _layer0_mixers(proj, st, p, cfg):
    b, t, _ = proj.shape
    past = 0 if st["past_k"] is None else st["past_k"].shape[1]
    pos = np.arange(past, past + t)
    tabs = _rope_tables(pos, HEAD_DIM) + _rope_tables(pos, IDX_DIM)
    q, qi, k, v, ki, sm, kb, vbe, kib = _post0(proj, tabs, p["q_norm_w"][0], p["k_norm_w"][0],
                                               p["dt_bias_pad"], tt=cfg["post_tt"])
    cache = None if not past else (st["past_k"].reshape(b, past * KV_HEADS, HEAD_DIM),
                                   st["past_v"].reshape(b, past * KV_HEADS, HEAD_DIM), st["past_ki"])
    att = _dsa_attention(q, qi, sm, kb, vbe, kib, cache, qb=cfg["qb"], tk=cfg["tk"])
    y_ssd, ssm_new, ssm_conv_new = _ssd(
        proj, sm, st["ssm_h"], st["ssm_conv"], p["ssd_conv_w"][0], p["ssd_conv_b"][0], p["a_log_pad"],
        p["d_exp"], p["ssd_norm_w"][0], p["e_mat"], chunk=cfg["ssd_chunk"])
    kv_shape = (1, b, t, KV_HEADS, HEAD_DIM)
    return att, y_ssd, (k.reshape(kv_shape), v.reshape(kv_shape), ki[None], ssm_new[None],
                        ssm_conv_new[None])


def _layer1_mixer(proj, st, p, cfg):
    b = proj.shape[0]
    y_lru, lru_new, lru_conv_new = _lru(
        proj, p["lru_conv_w"][0], p["lru_conv_b"][0], p["lru_w_a"][0], p["lru_b_a"][0],
        p["lru_w_x"][0], p["lru_b_x"][0], p["lru_lambda"][0], st["lru_h"].reshape(b, 1, LRU_WIDTH),
        st["lru_conv"], tt=cfg["lru_tt"])
    return y_lru, (lru_new.reshape(1, b, LRU_WIDTH), lru_conv_new[None])


def _forward(xp, xs, mods_p, mods_s, st_p, st_s, p, cfg_p, cfg_s):
    bb, tt = cfg_p["bb"], cfg_p["tt"]

    def ffn(xp, xs, l):
        (_, _, _, sh_p, sc_p, g_p), (_, _, _, sh_s, sc_s, g_s) = mods_p[l], mods_s[l]
        gu_p, gu_s = _norm_mod_matmul(
            xp, p["norm_ffn_w"][l], sc_p, sh_p, [p["ffn_w_gate"], p["ffn_w_up"]], l, bb=bb,
            tt=cfg_p["tt_ffn"], tn=512, out_dtype=BF16, name="ffn_gate_up", ride=(xs, sc_s, sh_s))
        return _matmul_residual([gu_p], p["ffn_w_down"], l, xp, g_p, bb=bb, tt=cfg_p["tt_ffn"],
                                tn=256, name="ffn_down", ride=([gu_s], xs, g_s))

    (sh_p, sc_p, g_p, *_), (sh_s, sc_s, g_s, *_) = mods_p[0], mods_s[0]
    proj_p, proj_s = _norm_mod_matmul(
        xp, p["norm_mix_w"][0], sc_p, sh_p, [p["w_in_ab_t"]], 0, bb=bb, tt=cfg_p["tt_ab"],
        tn=AB_TILE_N, out_dtype=F32, name="in_proj_ab", w_tail=p["w_in_ab_tail"], w_rows_out=True,
        ride=(xs, sc_s, sh_s))
    att_p, y_p, outs0_p = _layer0_mixers(proj_p, st_p, p, cfg_p)
    att_s, y_s, outs0_s = _layer0_mixers(proj_s, st_s, p, cfg_s)
    xp, xs = _matmul_residual([att_p, y_p], p["w_out_ab"], 0, xp, g_p, bb=bb, tt=tt, tn=512,
                              name="out_proj_ab", ride=([att_s, y_s], xs, g_s))
    xp, xs = ffn(xp, xs, 0)

    (sh_p, sc_p, g_p, *_), (sh_s, sc_s, g_s, *_) = mods_p[1], mods_s[1]
    proj_p, proj_s = _norm_mod_matmul(
        xp, p["norm_mix_w"][1], sc_p, sh_p, [p["w_in_c"]], 0, bb=bb, tt=cfg_p["tt_ab"], tn=512,
        out_dtype=F32, name="in_proj_c", ride=(xs, sc_s, sh_s))
    y_p, outs1_p = _layer1_mixer(proj_p, st_p, p, cfg_p)
    y_s, outs1_s = _layer1_mixer(proj_s, st_s, p, cfg_s)
    xp, xs = _matmul_residual([y_p], p["w_out_c"], 0, xp, g_p, bb=bb, tt=tt, tn=512,
                              name="out_proj_c", ride=([y_s], xs, g_s))
    xp, xs = ffn(xp, xs, 1)
    return (xp,) + outs0_p + outs1_p, (xs,) + outs0_s + outs1_s


PROMPT_CFG = dict(bb=1, tt=2048, tt_ab=1024, tt_ffn=1024, post_tt=1024, qb=128, tk=512,
                  ssd_chunk=128, lru_tt=512)
SAMPLE_CFG = dict(post_tt=32, qb=32, tk=384, ssd_chunk=32, lru_tt=32)


def kernel(x_prompt, x_sample, cache_attn_k, cache_attn_v, cache_idx_k, state_ssm, state_ssm_conv,
           state_lru, state_lru_conv, c_prompt, c_sample, ada_w, ada_b, norm_mix_w, norm_ffn_w,
           w_in_ab, q_norm_w, k_norm_w, ssd_conv_w, ssd_conv_b, ssd_dt_bias, ssd_a_log, ssd_d,
           ssd_norm_w, w_out_ab, w_in_c, lru_conv_w, lru_conv_b, lru_w_a, lru_b_a, lru_w_x, lru_b_x,
           lru_lambda, w_out_c, ffn_w_gate, ffn_w_up, ffn_w_down):
    bp, bs = x_prompt.shape[0], x_sample.shape[0]
    p = dict(norm_mix_w=norm_mix_w, norm_ffn_w=norm_ffn_w, q_norm_w=q_norm_w, k_norm_w=k_norm_w,
             ssd_conv_w=ssd_conv_w, ssd_conv_b=ssd_conv_b, ssd_norm_w=ssd_norm_w, w_out_ab=w_out_ab,
             w_in_c=w_in_c, lru_conv_w=lru_conv_w, lru_conv_b=lru_conv_b, lru_w_a=lru_w_a,
             lru_b_a=lru_b_a, lru_w_x=lru_w_x, lru_b_x=lru_b_x, lru_lambda=lru_lambda,
             w_out_c=w_out_c, ffn_w_gate=ffn_w_gate, ffn_w_up=ffn_w_up, ffn_w_down=ffn_w_down)
    p["w_in_ab_t"] = jnp.swapaxes(w_in_ab, 1, 2)
    p["w_in_ab_tail"] = _w_in_ab_tail(p["w_in_ab_t"][0])
    p["dt_bias_pad"] = _pad_lanes(ssd_dt_bias[0], SM_DT)
    p["a_log_pad"] = _pad_lanes(ssd_a_log[0], SM_DT)
    p["d_exp"] = jnp.repeat(ssd_d[0], SSD_HEADDIM).reshape(1, SSD_INNER)
    p["e_mat"] = _head_expansion()

    m_rows = 32
    c_all = jnp.concatenate([c_prompt, c_sample, jnp.zeros((m_rows - bp - bs, D_MODEL), F32)], axis=0)
    mod = _modulation(c_all, ada_w, ada_b)

    def group_mods(r0, nb):
        return [[mod[l, r0:r0 + nb, i * D_MODEL:(i + 1) * D_MODEL].reshape(nb, 1, D_MODEL)
                 for i in range(6)] for l in range(mod.shape[0])]

    zeros = lambda *s: jnp.zeros(s, F32)
    st_p = dict(past_k=None, past_v=None, past_ki=None,
                ssm_h=zeros(bp, SSD_HEADS, SSD_HEADDIM, SSD_STATE),
                ssm_conv=zeros(bp, CONV_W - 1, SSD_CONV_DIM), lru_h=zeros(bp, LRU_WIDTH),
                lru_conv=zeros(bp, CONV_W - 1, LRU_WIDTH))
    st_s = dict(past_k=cache_attn_k[0], past_v=cache_attn_v[0], past_ki=cache_idx_k[0],
                ssm_h=state_ssm[0], ssm_conv=state_ssm_conv[0], lru_h=state_lru[0],
                lru_conv=state_lru_conv[0])
    out_p, out_s = _forward(x_prompt, x_sample, group_mods(0, bp), group_mods(bp, bs), st_p, st_s, p,
                            PROMPT_CFG, SAMPLE_CFG)
    return (out_p[0], out_s[0]) + out_p[1:] + out_s[1:]
```

```python
import functools

import numpy as np
import jax
import jax.numpy as jnp
from jax import lax
from jax.experimental import pallas as pl
from jax.experimental.pallas import tpu as pltpu

F32 = jnp.float32
BF16 = jnp.bfloat16
HIGHEST = lax.Precision.HIGHEST

D_MODEL = 2048
CHUNK = 64
CHUNK_SHIFT = 6
ATT_HEADS = 8
KV_HEADS = 2
HEAD_DIM = 128
IDX_HEADS = 16
IDX_DIM = 64
TOPK_MAX = 256
ROPE_THETA = 10000.0
SSD_HEADS = 16
SSD_HEADDIM = 64
SSD_GROUPS = 2
SSD_STATE = 128
SSD_INNER = SSD_HEADS * SSD_HEADDIM
SSD_CONV_DIM = SSD_INNER + 2 * SSD_GROUPS * SSD_STATE
CONV_W = 4
LRU_WIDTH = D_MODEL
LRU_BLOCKS = 16
LRU_BLOCK_DIM = LRU_WIDTH // LRU_BLOCKS
LRU_C = 8.0
EPS = 1e-6
ATT_WIDTH = ATT_HEADS * HEAD_DIM
KV_WIDTH = KV_HEADS * HEAD_DIM
VEXT_WIDTH = 2 * KV_WIDTH
QI_WIDTH = IDX_HEADS * IDX_DIM
Q_SCALE = HEAD_DIM ** -0.5 * float(np.log2(np.e))

LANES = 128
SUBLANES = 8
VMEM_LIMIT_BYTES = 62 * 1024 * 1024
SINGLE_BUFFER_ROWS = 2048
NORM_CHUNKS = 4

OFF_Q = 0
OFF_K = OFF_Q + ATT_WIDTH
OFF_V = OFF_K + KV_WIDTH
OFF_QI = OFF_V + KV_WIDTH
OFF_SMALL = OFF_QI + QI_WIDTH
SM_WI = IDX_DIM
SM_DT = IDX_DIM + IDX_HEADS
OFF_Z = OFF_SMALL + SM_DT
OFF_XBC = OFF_Z + SSD_INNER
OFF_DT = OFF_XBC + SSD_CONV_DIM
AB_WIDTH = OFF_DT + SSD_HEADS
Z_BLOCK = 1280
XBC_BLOCK = 1792
AB_TILE_N = 512
AB_PAD_N = 5632
assert OFF_Z % LANES == SM_DT and OFF_XBC % LANES == SM_DT and OFF_DT % LANES == SM_DT
assert (OFF_Z - SM_DT) % Z_BLOCK == 0 and OFF_Z - SM_DT + Z_BLOCK >= OFF_Z + SSD_INNER
assert (OFF_XBC - SM_DT) % XBC_BLOCK == 0 and XBC_BLOCK >= OFF_DT - OFF_XBC + LANES
assert OFF_XBC - SM_DT + XBC_BLOCK <= AB_PAD_N

NEG_BIG = -1e30


def _params(n_axes):
    return pltpu.CompilerParams(dimension_semantics=("arbitrary",) * n_axes,
                                vmem_limit_bytes=VMEM_LIMIT_BYTES)


def _silu(x):
    return x * jax.nn.sigmoid(x)


def _softplus(x):
    return jnp.maximum(x, 0.0) + jnp.log1p(jnp.exp(-jnp.abs(x)))


def _mod_kernel(c_ref, w_ref, b_ref, o_ref):
    a = _silu(c_ref[...]).astype(BF16)
    o_ref[0] = jnp.dot(a, w_ref[0].astype(BF16), preferred_element_type=F32) + b_ref[0]


def _modulation(c_all, ada_w, ada_b):
    depth, d, n = ada_w.shape
    m = c_all.shape[0]
    tn = 1024
    return pl.pallas_call(
        _mod_kernel,
        grid=(depth, n // tn),
        in_specs=[pl.BlockSpec((m, d), lambda l, j: (0, 0)),
                  pl.BlockSpec((1, d, tn), lambda l, j: (l, 0, j)),
                  pl.BlockSpec((1, 1, tn), lambda l, j: (l, 0, j))],
        out_specs=pl.BlockSpec((1, m, tn), lambda l, j: (l, 0, j)),
        out_shape=jax.ShapeDtypeStruct((depth, m, n), F32),
        compiler_params=_params(2),
        name="adaln_mod",
    )(c_all, ada_w, ada_b.reshape(depth, 1, n))


def _first_row_tile():
    return (pl.program_id(0) == 0) & (pl.program_id(1) == 0)


def _nmm_kernel(*refs, swiglu, tail, w_rows_out, ride):
    refs = list(refs)
    x_ref, nw_ref, sc_ref, sh_ref = refs[:4]
    del refs[:4]
    if ride:
        xr_ref, scr_ref, shr_ref = refs[:3]
        del refs[:3]
    n_w = 2 if swiglu or tail else 1
    w_refs = refs[:n_w]
    del refs[:n_w]
    if ride:
        o_ref, or_ref, h_ref, hr_ref = refs
    else:
        o_ref, h_ref = refs
    k = pl.program_id(2)

    def matmul(h, w):
        dims = (((1,), (1,)), ((), ())) if w_rows_out else (((1,), (0,)), ((), ()))
        return lax.dot_general(h, w, dims, preferred_element_type=F32)

    def finish(accs, out_ref):
        o = _silu(accs[0]) * accs[1] if swiglu else accs[0]
        out_ref[...] = o.reshape(out_ref.shape).astype(out_ref.dtype)

    def project(ws, rows_ref, out_ref):
        h = rows_ref[...]
        finish([matmul(h, w) for w in ws], out_ref)

    def norm_project(ws, src_ref, scale_ref, shift_ref, rows_ref, out_ref):
        x = src_ref[...]
        d = x.shape[-1]
        rs = lax.rsqrt(jnp.mean(x * x, axis=-1, keepdims=True) + EPS)
        kc = d // NORM_CHUNKS
        accs = None
        for c in range(NORM_CHUNKS):
            cols = slice(c * kc, (c + 1) * kc)
            hc = (x[..., cols] * rs * nw_ref[..., cols] * (1.0 + scale_ref[..., cols])
                  + shift_ref[..., cols])
            hc = hc.reshape(rows_ref.shape[0], kc).astype(BF16)
            rows_ref[:, cols] = hc
            part = [matmul(hc, w[:, cols] if w_rows_out else w[cols, :]) for w in ws]
            accs = part if accs is None else [a + b for a, b in zip(accs, part)]
        finish(accs, out_ref)

    def step(tile_refs, first):
        ws = [w[...].astype(BF16) for w in tile_refs]
        if first:
            norm_project(ws, x_ref, sc_ref, sh_ref, h_ref, o_ref)
            if ride:
                pl.when(_first_row_tile())(
                    lambda: norm_project(ws, xr_ref, scr_ref, shr_ref, hr_ref, or_ref))
        else:
            project(ws, h_ref, o_ref)
            if ride:
                pl.when(_first_row_tile())(lambda: project(ws, hr_ref, or_ref))

    last = pl.num_programs(2) - 1 if tail else pl.num_programs(2)
    pl.when(k == 0)(lambda: step(w_refs[:n_w - tail], True))
    pl.when((k > 0) & (k < last))(lambda: step(w_refs[:n_w - tail], False))
    if tail:
        pl.when(k == last)(lambda: step(w_refs[1:], False))


def _row_block_mode(rows):
    return dict(pipeline_mode=pl.Buffered(1)) if rows >= SINGLE_BUFFER_ROWS else {}


def _ride_col_tile(n_tiles):
    return lambda i, j, k: (0, 0, jnp.where((i == 0) & (j == 0), k, n_tiles - 1))


def _norm_mod_matmul(x, nw, sc, sh, ws, layer, *, bb, tt, tn, out_dtype, name, w_tail=None,
                     w_rows_out=False, ride=None):
    b, t, d = x.shape
    swiglu = len(ws) == 2
    n_main = ws[0].shape[1 if w_rows_out else 2] // tn
    n_tiles = n_main + (w_tail is not None)
    n = n_tiles * tn
    xmap = lambda i, j, k: (i, j, 0)
    mmap = lambda i, j, k: (i, 0, 0)
    const3 = lambda i, j, k: (0, 0, 0)
    tile = lambda k: jnp.minimum(k, n_main - 1)
    if w_rows_out:
        w_specs = [pl.BlockSpec((None, tn, d), lambda i, j, k: (layer, tile(k), 0))] * len(ws)
    else:
        w_specs = [pl.BlockSpec((None, d, tn), lambda i, j, k: (layer, 0, tile(k)))] * len(ws)
    operands = list(ws)
    if w_tail is not None:
        w_specs.append(pl.BlockSpec(w_tail.shape, lambda i, j, k: (0, 0)))
        operands.append(w_tail)
    in_specs = [pl.BlockSpec((bb, tt, d), xmap, **_row_block_mode(bb * tt)),
                pl.BlockSpec((1, 1, d), const3),
                pl.BlockSpec((bb, 1, d), mmap),
                pl.BlockSpec((bb, 1, d), mmap)]
    out_specs = [pl.BlockSpec((bb, tt, tn), lambda i, j, k: (i, j, k))]
    out_shape = [jax.ShapeDtypeStruct((b, t, n), out_dtype)]
    scratch = [pltpu.VMEM((bb * tt, d), BF16)]
    ride_ops = ()
    if ride is not None:
        ride_ops = ride
        rb, rt, _ = ride[0].shape
        in_specs += [pl.BlockSpec((rb, rt, d), const3, pipeline_mode=pl.Buffered(1)),
                     pl.BlockSpec((rb, 1, d), const3), pl.BlockSpec((rb, 1, d), const3)]
        out_specs.append(pl.BlockSpec((rb, rt, tn), _ride_col_tile(n_tiles)))
        out_shape.append(jax.ShapeDtypeStruct((rb, rt, n), out_dtype))
        scratch.append(pltpu.VMEM((rb * rt, d), BF16))
    outs = pl.pallas_call(
        functools.partial(_nmm_kernel, swiglu=swiglu, tail=w_tail is not None,
                          w_rows_out=w_rows_out, ride=ride is not None),
        grid=(b // bb, t // tt, n_tiles),
        in_specs=in_specs + w_specs,
        out_specs=out_specs,
        out_shape=out_shape,
        scratch_shapes=scratch,
        compiler_params=_params(3),
        name=name,
    )(x, nw.reshape(1, 1, d), sc, sh, *ride_ops, *operands)
    return outs if ride is not None else outs[0]


def _mmr_kernel(*refs, k_sizes, ride):
    refs = list(refs)
    n_a = len(k_sizes)
    groups = [refs[:n_a + 2]]
    del refs[:n_a + 2]
    if ride:
        groups.append(refs[:n_a + 2])
        del refs[:n_a + 2]
    w_ref = refs[0]
    out_refs = refs[1:]
    offs = np.cumsum((0,) + tuple(k_sizes))
    ws = [w_ref[offs[i]:offs[i + 1], :].astype(BF16) for i in range(n_a)]

    def apply(group, o_ref):
        *a_refs, x_ref, g_ref = group
        bb, tt, tn = o_ref.shape
        acc = jnp.zeros((bb * tt, tn), F32)
        for a_ref, w, ks in zip(a_refs, ws, k_sizes):
            acc = acc + jnp.dot(a_ref[...].reshape(bb * tt, ks), w, preferred_element_type=F32)
        o_ref[...] = x_ref[...] + g_ref[...] * acc.reshape(bb, tt, tn)

    apply(groups[0], out_refs[0])
    if ride:
        pl.when(_first_row_tile())(lambda: apply(groups[1], out_refs[1]))


def _matmul_residual(a_list, w, layer, x, g, *, bb, tt, tn, name, ride=None):
    b, t, d = x.shape
    k_sizes = tuple(a.shape[-1] for a in a_list)
    k_total = sum(k_sizes)
    n_tiles = d // tn
    in_specs = ([pl.BlockSpec((bb, tt, ks), lambda i, j, k: (i, j, 0), **_row_block_mode(bb * tt))
                 for ks in k_sizes]
                + [pl.BlockSpec((bb, tt, tn), lambda i, j, k: (i, j, k)),
                   pl.BlockSpec((bb, 1, tn), lambda i, j, k: (i, 0, k))])
    out_specs = [pl.BlockSpec((bb, tt, tn), lambda i, j, k: (i, j, k))]
    out_shape = [jax.ShapeDtypeStruct((b, t, d), F32)]
    ride_ops = ()
    if ride is not None:
        ra_list, rx, rg = ride
        rb, rt, _ = rx.shape
        ride_ops = (*ra_list, rx, rg)
        col = _ride_col_tile(n_tiles)
        in_specs += ([pl.BlockSpec((rb, rt, ks), lambda i, j, k: (0, 0, 0),
                                   pipeline_mode=pl.Buffered(1)) for ks in k_sizes]
                     + [pl.BlockSpec((rb, rt, tn), col), pl.BlockSpec((rb, 1, tn), col)])
        out_specs.append(pl.BlockSpec((rb, rt, tn), col))
        out_shape.append(jax.ShapeDtypeStruct(rx.shape, F32))
    outs = pl.pallas_call(
        functools.partial(_mmr_kernel, k_sizes=k_sizes, ride=ride is not None),
        grid=(b // bb, t // tt, n_tiles),
        in_specs=in_specs + [pl.BlockSpec((None, k_total, tn), lambda i, j, k: (layer, 0, k))],
        out_specs=out_specs,
        out_shape=out_shape,
        compiler_params=_params(3),
        name=name,
    )(*a_list, x, g, *ride_ops, w)
    return outs if ride is not None else outs[0]


def _swap_halves(x, half):
    w = x.shape[-1]
    if w > LANES:
        return jnp.concatenate([_swap_halves(x[..., c * LANES:(c + 1) * LANES], half)
                                for c in range(w // LANES)], axis=-1)
    if w == 2 * half:
        return pltpu.roll(x, half, x.ndim - 1)
    lane = lax.broadcasted_iota(jnp.int32, x.shape, x.ndim - 1)
    first = (lane & half) == 0
    return jnp.where(first, pltpu.roll(x, w - half, x.ndim - 1), pltpu.roll(x, half, x.ndim - 1))


def _lane_aligned(slab, offset, width):
    return pltpu.roll(slab, slab.shape[-1] - offset, slab.ndim - 1)[:, :width]


def _rope(x, cos, sin_signed, half):
    reps = x.shape[-1] // cos.shape[-1]
    if reps > 1:
        cos = jnp.concatenate([cos] * reps, axis=-1)
        sin_signed = jnp.concatenate([sin_signed] * reps, axis=-1)
    return x * cos + _swap_halves(x, half) * sin_signed


def _conv_carry_init(state_ref, cbuf_ref):
    cbuf_ref[0:SUBLANES, :] = jnp.zeros((SUBLANES, cbuf_ref.shape[1]), F32)
    cbuf_ref[SUBLANES - (CONV_W - 1):SUBLANES, :] = state_ref[0]


def _conv_state(tail):
    return tail[SUBLANES - (CONV_W - 1):, :]


def _causal_conv_tile(u, w_ref, b_ref, cbuf_ref):
    tt = u.shape[0]
    cbuf_ref[SUBLANES:SUBLANES + tt, :] = u
    full = cbuf_ref[...]
    out = b_ref[...]
    for j in range(CONV_W):
        shift = CONV_W - 1 - j
        rows = pltpu.roll(full, shift, 0) if shift else full
        out = out + rows[SUBLANES:SUBLANES + tt, :] * w_ref[j:j + 1, :]
    tail = cbuf_ref[tt:tt + SUBLANES, :]
    cbuf_ref[0:SUBLANES, :] = tail
    return out, tail


def _causal_conv(u, state_ref, w_ref, b_ref, cbuf_ref, new_ref, first, last):
    pl.when(first)(lambda: _conv_carry_init(state_ref, cbuf_ref))
    out, tail = _causal_conv_tile(u, w_ref, b_ref, cbuf_ref)

    @pl.when(last)
    def _():
        new_ref[0] = _conv_state(tail)

    return out


def _post0_kernel(p_ref, pdt_ref, cq_ref, sq_ref, ci_ref, si_ref, qn_ref, kn_ref, dtb_ref,
                  q_ref, qi_ref, k_ref, v_ref, ki_ref, sm_ref, kb_ref, vb_ref, kib_ref):
    cq, sq, ci, si = cq_ref[...], sq_ref[...], ci_ref[...], si_ref[...]

    def head_norm_rope(x, w):
        ms = jnp.mean(x * x, axis=-1, keepdims=True)
        return _rope(x * lax.rsqrt(ms + EPS) * w, cq, sq, HEAD_DIM // 2)

    for h in range(ATT_HEADS):
        xh = p_ref[0, :, OFF_Q + h * HEAD_DIM:OFF_Q + (h + 1) * HEAD_DIM]
        q_ref[0, :, h * HEAD_DIM:(h + 1) * HEAD_DIM] = (head_norm_rope(xh, qn_ref[...])
                                                         * Q_SCALE).astype(BF16)
    ones = jnp.ones((p_ref.shape[1], HEAD_DIM), BF16)
    for h in range(KV_HEADS):
        xh = p_ref[0, :, OFF_K + h * HEAD_DIM:OFF_K + (h + 1) * HEAD_DIM]
        kh = head_norm_rope(xh, kn_ref[...])
        vh = p_ref[0, :, OFF_V + h * HEAD_DIM:OFF_V + (h + 1) * HEAD_DIM]
        k_ref[0, pl.ds(h, xh.shape[0], stride=KV_HEADS), :] = kh
        v_ref[0, pl.ds(h, xh.shape[0], stride=KV_HEADS), :] = vh
        kb_ref[0, :, h * HEAD_DIM:(h + 1) * HEAD_DIM] = kh.astype(BF16)
        vb_ref[0, :, 2 * h * HEAD_DIM:(2 * h + 1) * HEAD_DIM] = vh.astype(BF16)
        vb_ref[0, :, (2 * h + 1) * HEAD_DIM:(2 * h + 2) * HEAD_DIM] = ones

    qi = p_ref[0, :, OFF_QI:OFF_QI + QI_WIDTH]
    qi_ref[0] = _rope(qi, ci, si, IDX_DIM // 2).astype(BF16)

    sm = p_ref[0, :, OFF_SMALL:OFF_SMALL + LANES]
    lane = lax.broadcasted_iota(jnp.int32, sm.shape, 1)
    ki_part = _rope(sm, ci, si, IDX_DIM // 2)
    wi_part = sm * (IDX_HEADS ** -0.5 * IDX_DIM ** -0.5)
    dt_part = _softplus(pdt_ref[0] + dtb_ref[...])
    sm_out = jnp.where(lane < SM_WI, ki_part,
                       jnp.where(lane < SM_DT, wi_part,
                                 jnp.where(lane < SM_DT + SSD_HEADS, dt_part, 0.0)))
    sm_ref[0] = sm_out
    ki_ref[0] = sm_out[:, :IDX_DIM]
    kib_ref[0] = sm_out[:, :IDX_DIM].astype(BF16)


def _post0(proj, tabs, q_norm_w, k_norm_w, dt_bias_pad, *, tt):
    b, t, n = proj.shape
    tok = lambda w, col=0: pl.BlockSpec((1, tt, w), lambda i, j: (i, j, col))
    tab = pl.BlockSpec((tt, LANES), lambda i, j: (j, 0))
    full2 = lambda a: pl.BlockSpec(a.shape, lambda i, j: (0, 0))
    out_shapes = [
        jax.ShapeDtypeStruct((b, t, ATT_WIDTH), BF16),
        jax.ShapeDtypeStruct((b, t, QI_WIDTH), BF16),
        jax.ShapeDtypeStruct((b, t * KV_HEADS, HEAD_DIM), F32),
        jax.ShapeDtypeStruct((b, t * KV_HEADS, HEAD_DIM), F32),
        jax.ShapeDtypeStruct((b, t, IDX_DIM), F32),
        jax.ShapeDtypeStruct((b, t, LANES), F32),
        jax.ShapeDtypeStruct((b, t, KV_WIDTH), BF16),
        jax.ShapeDtypeStruct((b, t, VEXT_WIDTH), BF16),
        jax.ShapeDtypeStruct((b, t, IDX_DIM), BF16),
    ]
    kv_rows = pl.BlockSpec((1, tt * KV_HEADS, HEAD_DIM), lambda i, j: (i, j, 0))
    out_specs = [tok(ATT_WIDTH), tok(QI_WIDTH), kv_rows, kv_rows, tok(IDX_DIM),
                 tok(LANES), tok(KV_WIDTH), tok(VEXT_WIDTH), tok(IDX_DIM)]
    qn = q_norm_w.reshape(1, HEAD_DIM)
    kn = k_norm_w.reshape(1, HEAD_DIM)
    head = OFF_SMALL + LANES
    return pl.pallas_call(
        _post0_kernel,
        grid=(b, t // tt),
        in_specs=[tok(head), tok(LANES, (OFF_DT - SM_DT) // LANES), tab, tab, tab, tab, full2(qn),
                  full2(kn), full2(dt_bias_pad)],
        out_specs=out_specs,
        out_shape=out_shapes,
        compiler_params=_params(2),
        name="post_proj0",
    )(proj, proj, *tabs, qn, kn, dt_bias_pad)


def _dsa_kernel(q_ref, qi_ref, sm_ref, k_ref, v_ref, ki_ref, *rest, qb, tk, n_keys, past, topk,
                n_bisect, keys_on_sublanes):
    if past:
        pk_ref, pv_ref, pki_ref = rest[:3]
        rest = rest[3:]
    (o_ref, sc_ref, acc_ref, qs_ref, s0_ref, s1_ref, mt0_ref, mt1_ref, m_ref) = rest[:9]
    if past:
        kall_ref, vall_ref, kiall_ref = rest[9:]
        t_new = k_ref.shape[1]
        pad = kall_ref.shape[0] - n_keys
        for g in range(KV_HEADS):
            head_rows = pl.ds(g, past, stride=KV_HEADS)
            kall_ref[0:past, g * HEAD_DIM:(g + 1) * HEAD_DIM] = pk_ref[0, head_rows, :].astype(BF16)
            vall_ref[0:past, 2 * g * HEAD_DIM:(2 * g + 1) * HEAD_DIM] = pv_ref[0, head_rows, :].astype(BF16)
            vall_ref[0:past, (2 * g + 1) * HEAD_DIM:(2 * g + 2) * HEAD_DIM] = jnp.ones(
                (past, HEAD_DIM), BF16)
        kiall_ref[0:past, :] = pki_ref[0].astype(BF16)
        for dst, src in ((kall_ref, k_ref), (vall_ref, v_ref), (kiall_ref, ki_ref)):
            dst[past:past + t_new, :] = src[0]
            dst[past + t_new:, :] = jnp.zeros((pad, dst.shape[1]), BF16)
        k_src, v_src, ki_src = kall_ref, vall_ref, kiall_ref
    else:
        k_src, v_src, ki_src = k_ref.at[0], v_ref.at[0], ki_ref.at[0]
    kax = 0 if keys_on_sublanes else 1
    tile_shape = (tk, qb) if keys_on_sublanes else (qb, tk)
    vec_shape = (1, qb) if keys_on_sublanes else (qb, 1)
    n_acc = 4
    part_shape = (n_acc, SUBLANES, qb) if keys_on_sublanes else (qb, LANES)
    j = pl.program_id(1)
    pos0 = past + j * qb
    q_chunk = lax.shift_right_logical(pos0 + lax.broadcasted_iota(jnp.int32, vec_shape, 1 - kax),
                                      CHUNK_SHIFT)
    n_valid = jnp.minimum((q_chunk + 1) * CHUNK, n_keys)
    max_valid = jnp.minimum(((pos0 + qb - 1) // CHUNK + 1) * CHUNK, n_keys)
    nkt = (max_valid + tk - 1) // tk
    kf = float(topk)
    active_f = jnp.where(n_valid > topk, 1.0, 0.0)

    def key_index(kt):
        return kt * tk + lax.broadcasted_iota(jnp.int32, tile_shape, kax)

    nt_dims = (((1,), (1,)), ((), ()))
    for h in range(IDX_HEADS):
        qs_ref[h * qb:(h + 1) * qb, :] = qi_ref[0, :, h * IDX_DIM:(h + 1) * IDX_DIM]
    if keys_on_sublanes:
        wi_t = sm_ref[0].T
        head_w = lambda h: wi_t[SM_WI + h:SM_WI + h + 1, :]
    else:
        wis = sm_ref[0][:, SM_WI:SM_WI + IDX_HEADS]
        head_w = lambda h: wis[:, h:h + 1]

    def score_tile(kt, carry):
        kit = ki_src[pl.ds(pl.multiple_of(kt * tk, tk), tk), :]
        acc = jnp.zeros(tile_shape, F32)
        if keys_on_sublanes:
            for pr in range(IDX_HEADS // 2):
                s2 = lax.dot_general(kit, qs_ref[2 * pr * qb:(2 * pr + 2) * qb, :], nt_dims,
                                     preferred_element_type=F32)
                for e in range(2):
                    acc = acc + head_w(2 * pr + e) * jnp.maximum(s2[:, e * qb:(e + 1) * qb], 0.0)
        else:
            s_all = lax.dot_general(qs_ref[...], kit, nt_dims, preferred_element_type=F32)
            for h in range(IDX_HEADS):
                acc = acc + head_w(h) * jnp.maximum(s_all[h * qb:(h + 1) * qb, :], 0.0)
        valid = key_index(kt) < n_valid
        sc_ref[kt] = jnp.where(valid, acc, -jnp.inf)
        mn, mx = carry
        return (jnp.minimum(mn, fold(jnp.where(valid, acc, jnp.inf), jnp.minimum, jnp.min)),
                jnp.maximum(mx, fold(jnp.where(valid, acc, -jnp.inf), jnp.maximum, jnp.max)))

    def fold(m, op2, red):
        if keys_on_sublanes:
            return red(m.reshape(tk // (n_acc * SUBLANES), n_acc, SUBLANES, qb), axis=0)
        f = m[:, 0:LANES]
        for c in range(1, tk // LANES):
            f = op2(f, m[:, c * LANES:(c + 1) * LANES])
        return f

    def finish(part, red):
        if keys_on_sublanes:
            part = red(part, axis=0)
        return red(part, axis=kax, keepdims=True)

    def reduce_tiles(make, op2, red, init):
        def body(kt, part):
            return op2(part, fold(make(sc_ref[kt], kt), op2, red))
        return finish(lax.fori_loop(0, nkt, body, jnp.full(part_shape, init, F32)), red)

    def count(pred):
        return reduce_tiles(lambda t, kt: jnp.where(pred(t, kt), 1.0, 0.0), jnp.add, jnp.sum, 0.0)

    def row_max(pred):
        return reduce_tiles(lambda t, kt: jnp.where(pred(t, kt), t, -jnp.inf), jnp.maximum, jnp.max,
                            -jnp.inf)

    mn_part, mx_part = lax.fori_loop(0, nkt, score_tile, (jnp.full(part_shape, jnp.inf, F32),
                                                          jnp.full(part_shape, -jnp.inf, F32)))
    lo0, hi0 = finish(mn_part, jnp.min), finish(mx_part, jnp.max)

    def any_set(flag_f):
        return jnp.max(flag_f) > 0.0

    def bisect(_, state):
        lo, hi, c_lo = state
        mid = lo + (hi - lo) * 0.5
        c = count(lambda t, kt: t >= mid)
        ge = c >= kf
        return jnp.where(ge, mid, lo), jnp.where(ge, hi, mid), jnp.where(ge, c, c_lo)

    lo, hi, c_lo = lax.fori_loop(0, n_bisect, bisect, (lo0, hi0, n_valid.astype(F32)))
    found_f = jnp.where(c_lo == kf, active_f, 0.0)

    def walk_down():
        w0 = row_max(lambda t, kt: t <= hi)
        c0 = count(lambda t, kt: t >= w0)

        def walk_cond(state):
            w, c = state
            return any_set(jnp.where(c < kf, active_f, 0.0))

        def walk_body(state):
            w, c = state
            w2 = row_max(lambda t, kt: t < w)
            c2 = count(lambda t, kt: t >= w2)
            upd = c < kf
            return jnp.where(upd, w2, w), jnp.where(upd, c2, c)

        return lax.while_loop(walk_cond, walk_body,
                              (jnp.where(found_f > 0.5, lo, w0),
                               jnp.where(found_f > 0.5, kf, c0)))

    w, c_ge = lax.cond(any_set(active_f - found_f), walk_down,
                       lambda: (lo, jnp.full(vec_shape, kf, F32)))
    thr = jnp.where(active_f > 0.5, w, -jnp.inf)
    tied_f = jnp.where(c_ge > kf, active_f, 0.0)

    def last_kept_tie():
        need = kf - count(lambda t, kt: t > thr)
        n_steps = max(1, int(np.ceil(np.log2(sc_ref.shape[0] * tk))) + 1)

        def step(_, lh):
            lo_i, hi_i = lh
            mid = lax.shift_right_arithmetic(lo_i + hi_i, 1)
            ge = count(lambda t, kt: (t == thr) & (key_index(kt) <= mid)) >= need
            return jnp.where(ge, lo_i, mid), jnp.where(ge, mid, hi_i)

        lo_i = jnp.full(vec_shape, -1, jnp.int32)
        hi_i = jnp.full(vec_shape, sc_ref.shape[0] * tk - 1, jnp.int32)
        return lax.fori_loop(0, n_steps, step, (lo_i, hi_i))[1]

    last_tie = lax.cond(any_set(tied_f), last_kept_tie, lambda: jnp.zeros(vec_shape, jnp.int32))

    def bias_tile(kt):
        t, kidx = sc_ref[kt], key_index(kt)
        keep_tie = (kidx <= last_tie) | (tied_f < 0.5)
        sel = ((t > thr) | ((t == thr) & keep_tie)) & (kidx < n_valid)
        bias = jnp.where(sel, 0.0, NEG_BIG)
        return bias.T if keys_on_sublanes else bias

    rep = ATT_HEADS // KV_HEADS
    qgs = [jnp.concatenate(
        [q_ref[0, :, (g * rep + r) * HEAD_DIM:(g * rep + r + 1) * HEAD_DIM] for r in range(rep)],
        axis=0) for g in range(KV_HEADS)]
    acc_ref[...] = jnp.zeros(acc_ref.shape, F32)
    bufs = ((s0_ref, mt0_ref), (s1_ref, mt1_ref))

    def qk_tile(kt, buf):
        s_ref, mt_ref = bufs[buf]
        row0 = pl.multiple_of(kt * tk, tk)
        bias = bias_tile(kt)
        for g in range(KV_HEADS):
            kg = k_src[pl.ds(row0, tk), g * HEAD_DIM:(g + 1) * HEAD_DIM]
            s = lax.dot_general(qgs[g], kg, nt_dims, preferred_element_type=F32)
            s = (s.reshape(rep, qb, tk) + bias[None]).reshape(rep * qb, tk)
            s_ref[g] = s
            mt_ref[g] = jnp.broadcast_to(jnp.max(s, axis=-1, keepdims=True), (rep * qb, LANES))

    def lane_tile(x, width):
        return jnp.concatenate([x] * (width // LANES), axis=1)

    def att_tile(kt, buf, prefetch):
        if prefetch:
            qk_tile(kt + 1, 1 - buf)
        s_ref, mt_ref = bufs[buf]
        row0 = pl.multiple_of(kt * tk, tk)
        for g in range(KV_HEADS):
            vg = v_src[pl.ds(row0, tk), 2 * g * HEAD_DIM:(2 * g + 2) * HEAD_DIM]
            m_old = m_ref[g]
            m_new = jnp.maximum(m_old, mt_ref[g])
            m_ref[g] = m_new
            alpha = jnp.exp2(m_old - m_new)
            p = jnp.exp2(s_ref[g] - lane_tile(m_new, tk)).astype(BF16)
            acc_ref[g] = (lane_tile(alpha, 2 * HEAD_DIM) * acc_ref[g]
                          + jnp.dot(p, vg, preferred_element_type=F32))

    def tile_pair(pi, carry):
        att_tile(2 * pi, 0, True)
        att_tile(2 * pi + 1, 1, True)
        return carry

    def last_two():
        att_tile(nkt - 2, 0, True)
        att_tile(nkt - 1, 1, False)

    def last_one():
        att_tile(nkt - 1, 0, False)

    m_ref[...] = jnp.full(m_ref.shape, NEG_BIG, F32)
    qk_tile(0, 0)
    lax.fori_loop(0, (nkt - 1) // 2, tile_pair, 0)
    lax.cond((nkt & 1) == 0, last_two, last_one)
    for g in range(KV_HEADS):
        acc = acc_ref[g]
        o = acc[:, :HEAD_DIM] / acc[:, HEAD_DIM:]
        for r in range(rep):
            h = g * rep + r
            o_ref[0, :, h * HEAD_DIM:(h + 1) * HEAD_DIM] = o[r * qb:(r + 1) * qb].astype(BF16)


def _dsa_attention(q, qi, sm, k_new, v_new, ki_new, cache, *, qb, tk):
    b, t, _ = q.shape
    past = 0 if cache is None else cache[2].shape[1]
    n_keys = past + t
    nk_pad = -(-n_keys // tk) * tk
    topk = min(TOPK_MAX, n_keys // 4)
    keys_on_sublanes = qb == LANES
    rep = ATT_HEADS // KV_HEADS
    tokq = lambda w: pl.BlockSpec((1, qb, w), lambda i, j: (i, j, 0))
    keys = lambda w: pl.BlockSpec((1, t, w), lambda i, j: (i, 0, 0))
    tile_shape = (tk, qb) if keys_on_sublanes else (qb, tk)
    cache_specs, cache_scratch = [], []
    if cache is not None:
        kv = pl.BlockSpec((1, past * KV_HEADS, HEAD_DIM), lambda i, j: (i, 0, 0))
        cache_specs = [kv, kv, pl.BlockSpec((1, past, IDX_DIM), lambda i, j: (i, 0, 0))]
        cache_scratch = [pltpu.VMEM((nk_pad, w), BF16) for w in (KV_WIDTH, VEXT_WIDTH, IDX_DIM)]
    return pl.pallas_call(
        functools.partial(_dsa_kernel, qb=qb, tk=tk, n_keys=n_keys, past=past, topk=topk,
                          n_bisect=20, keys_on_sublanes=keys_on_sublanes),
        grid=(b, t // qb),
        in_specs=[tokq(ATT_WIDTH), tokq(QI_WIDTH), tokq(LANES), keys(KV_WIDTH), keys(VEXT_WIDTH),
                  keys(IDX_DIM)] + cache_specs,
        out_specs=tokq(ATT_WIDTH),
        out_shape=jax.ShapeDtypeStruct((b, t, ATT_WIDTH), BF16),
        scratch_shapes=[pltpu.VMEM((nk_pad // tk,) + tile_shape, F32),
                        pltpu.VMEM((KV_HEADS, rep * qb, 2 * HEAD_DIM), F32),
                        pltpu.VMEM((IDX_HEADS * qb, IDX_DIM), BF16),
                        pltpu.VMEM((KV_HEADS, rep * qb, tk), F32),
                        pltpu.VMEM((KV_HEADS, rep * qb, tk), F32),
                        pltpu.VMEM((KV_HEADS, rep * qb, LANES), F32),
                        pltpu.VMEM((KV_HEADS, rep * qb, LANES), F32),
                        pltpu.VMEM((KV_HEADS, rep * qb, LANES), F32)] + cache_scratch,
        compiler_params=_params(2),
        name="dsa_attention",
    )(q, qi, sm, k_new, v_new, ki_new, *(cache or ()))


def _ssd_kernel(pfirst_ref, pnext_ref, z_ref, sm_ref, st_ref, cs_ref, cw_ref, cb_ref, alog_ref,
                dexp_ref, nw_ref, e_ref, y_ref, stout_ref, cnew_ref, ht_ref, yi_ref, cbuf_ref, xa_ref,
                xb_ref):
    c = pl.program_id(1)
    nc = pl.num_programs(1)
    L = pfirst_ref.shape[1]
    hp = SSD_INNER // SSD_GROUPS

    def conv_act(p_ref):
        raw = _lane_aligned(p_ref[0, :, :OFF_DT - OFF_XBC + LANES], SM_DT, SSD_CONV_DIM)
        out, tail = _causal_conv_tile(raw, cw_ref, cb_ref, cbuf_ref)
        return _silu(out), tail

    @pl.when(c == 0)
    def _():
        ht_ref[...] = st_ref[0].reshape(SSD_INNER, SSD_STATE).T
        _conv_carry_init(cs_ref, cbuf_ref)
        act, tail = conv_act(pfirst_ref)
        xa_ref[...] = act

        @pl.when(nc == 1)
        def _():
            cnew_ref[0] = _conv_state(tail)

    def chunk(xbc_ref, next_ref):
        _ssd_chunk(xbc_ref, z_ref, sm_ref, alog_ref, dexp_ref, nw_ref, e_ref, y_ref, ht_ref, yi_ref)
        act, tail = conv_act(pnext_ref)
        next_ref[...] = act
        return tail

    tail = lax.cond((c & 1) == 0, lambda: chunk(xa_ref, xb_ref), lambda: chunk(xb_ref, xa_ref))

    @pl.when(c + 1 == nc - 1)
    def _():
        cnew_ref[0] = _conv_state(tail)

    @pl.when(c == nc - 1)
    def _():
        stout_ref[0] = ht_ref[...].T.reshape(SSD_HEADS, SSD_HEADDIM, SSD_STATE)


def _ssd_chunk(xbc_ref, z_ref, sm_ref, alog_ref, dexp_ref, nw_ref, e_ref, y_ref, ht_ref, yi_ref):
    L = xbc_ref.shape[0]
    hp = SSD_INNER // SSD_GROUPS
    xs = xbc_ref[:, 0:SSD_INNER]
    sm = sm_ref[0]
    lane = lax.broadcasted_iota(jnp.int32, (1, LANES), 1)
    is_dt = (lane >= SM_DT) & (lane < SM_DT + SSD_HEADS)
    a_neg = jnp.where(is_dt, -jnp.exp(alog_ref[...]), 0.0)
    dt = jnp.where(is_dt, sm, 0.0)
    rows = lax.broadcasted_iota(jnp.int32, (L, L), 0)
    cols = lax.broadcasted_iota(jnp.int32, (L, L), 1)
    tri = cols <= rows
    e = e_ref[...]
    cum = jnp.dot(tri.astype(F32), dt * a_neg, precision=HIGHEST, preferred_element_type=F32)
    eye = (lax.broadcasted_iota(jnp.int32, (LANES, LANES), 0)
           == lax.broadcasted_iota(jnp.int32, (LANES, LANES), 1)).astype(F32)
    cum_t = lax.dot_general(eye, cum, (((1,), (1,)), ((), ())), precision=HIGHEST,
                            preferred_element_type=F32)
    both = jnp.concatenate([cum, dt], axis=0)
    hi = both.astype(BF16)
    rest = both - hi.astype(F32)
    mid = rest.astype(BF16)
    low = (rest - mid.astype(F32)).astype(BF16)
    both_e = (jnp.dot(hi, e, preferred_element_type=F32) + jnp.dot(mid, e, preferred_element_type=F32)
              + jnp.dot(low, e, preferred_element_type=F32))
    cum_e, dt_e = both_e[:L], both_e[L:]
    last = cum_e[L - 1:L, :]
    xdt = (xs * dt_e).astype(BF16)
    xw = (xs * (jnp.exp(last - cum_e) * dt_e)).astype(BF16)
    ht_old = ht_ref[...]

    for g in range(SSD_GROUPS):
        bg = xbc_ref[:, SSD_INNER + g * SSD_STATE:SSD_INNER + (g + 1) * SSD_STATE]
        cg = xbc_ref[:, SSD_INNER + (SSD_GROUPS + g) * SSD_STATE:
                     SSD_INNER + (SSD_GROUPS + g + 1) * SSD_STATE]
        bgb, cgb = bg.astype(BF16), cg.astype(BF16)
        cb = lax.dot_general(cgb, bgb, (((1,), (1,)), ((), ())), preferred_element_type=F32)
        for hh in range(SSD_HEADS // SSD_GROUPS):
            h = g * (SSD_HEADS // SSD_GROUPS) + hh
            seg = cum[:, SM_DT + h:SM_DT + h + 1] - cum_t[SM_DT + h:SM_DT + h + 1, :]
            decay = jnp.exp(jnp.where(tri, seg, -jnp.inf))
            wts = (cb * decay).astype(BF16)
            yi_ref[:, h * SSD_HEADDIM:(h + 1) * SSD_HEADDIM] = jnp.dot(
                wts, xdt[:, h * SSD_HEADDIM:(h + 1) * SSD_HEADDIM], preferred_element_type=F32)
        ht_g = ht_old[:, g * hp:(g + 1) * hp]
        y_inter = jnp.dot(cgb, ht_g.astype(BF16), preferred_element_type=F32)
        yi_ref[:, g * hp:(g + 1) * hp] = (yi_ref[:, g * hp:(g + 1) * hp]
                                          + y_inter * jnp.exp(cum_e[:, g * hp:(g + 1) * hp]))
        upd = jnp.dot(bg.T.astype(BF16), xw[:, g * hp:(g + 1) * hp],
                      preferred_element_type=F32)
        ht_ref[:, g * hp:(g + 1) * hp] = ht_g * jnp.exp(last[:, g * hp:(g + 1) * hp]) + upd

    y = yi_ref[...] + dexp_ref[...] * xs
    y = y * _silu(_lane_aligned(z_ref[0], SM_DT, SSD_INNER))
    for g in range(SSD_GROUPS):
        yg = y[:, g * hp:(g + 1) * hp]
        ms = jnp.mean(yg * yg, axis=-1, keepdims=True)
        y_ref[0, :, g * hp:(g + 1) * hp] = (yg * lax.rsqrt(ms + EPS)
                                            * nw_ref[:, g * hp:(g + 1) * hp]).astype(BF16)


def _ssd(proj, sm, state, conv_state, conv_w, conv_b, a_log_pad, d_exp, norm_w, e_mat, *, chunk):
    b, t, _ = proj.shape
    n_chunks = t // chunk
    xbc_col = (OFF_XBC - SM_DT) // XBC_BLOCK
    tok = lambda w, col=0: pl.BlockSpec((1, chunk, w), lambda i, j: (i, j, col))
    xbc_first = pl.BlockSpec((1, chunk, XBC_BLOCK), lambda i, j: (i, 0, xbc_col))
    xbc_next = pl.BlockSpec((1, chunk, XBC_BLOCK),
                            lambda i, j: (i, jnp.minimum(j + 1, n_chunks - 1), xbc_col))
    full2 = lambda a: pl.BlockSpec(a.shape, lambda i, j: (0, 0))
    st = pl.BlockSpec((1, SSD_HEADS, SSD_HEADDIM, SSD_STATE), lambda i, j: (i, 0, 0, 0))
    cst = pl.BlockSpec((1, CONV_W - 1, SSD_CONV_DIM), lambda i, j: (i, 0, 0))
    nw = norm_w.reshape(1, SSD_INNER)
    cb = conv_b.reshape(1, SSD_CONV_DIM)
    return pl.pallas_call(
        _ssd_kernel,
        grid=(b, n_chunks),
        in_specs=[xbc_first, xbc_next,
                  tok(Z_BLOCK, (OFF_Z - SM_DT) // Z_BLOCK), tok(LANES), st, cst, full2(conv_w),
                  full2(cb), full2(a_log_pad), full2(d_exp), full2(nw), full2(e_mat)],
        out_specs=[tok(SSD_INNER), st, cst],
        out_shape=[jax.ShapeDtypeStruct((b, t, SSD_INNER), BF16),
                   jax.ShapeDtypeStruct(state.shape, F32),
                   jax.ShapeDtypeStruct((b, CONV_W - 1, SSD_CONV_DIM), F32)],
        scratch_shapes=[pltpu.VMEM((SSD_STATE, SSD_INNER), F32),
                        pltpu.VMEM((chunk, SSD_INNER), F32),
                        pltpu.VMEM((chunk + SUBLANES, SSD_CONV_DIM), F32),
                        pltpu.VMEM((chunk, SSD_CONV_DIM), F32),
                        pltpu.VMEM((chunk, SSD_CONV_DIM), F32)],
        compiler_params=_params(2),
        name="ssd_scan",
    )(proj, proj, proj, sm, state, conv_state, conv_w, cb, a_log_pad, d_exp, nw, e_mat)


def _lru_kernel(gate_ref, xin_ref, cw_ref, cb_ref, wa_ref, wx_ref, ba_ref, bx_ref, lam_ref,
                h0_ref, cs_ref, y_ref, hout_ref, cnew_ref, cbuf_ref, a_ref, u_ref, hp_ref):
    t = pl.program_id(1)
    nt = pl.num_programs(1)
    tt = xin_ref.shape[1]
    xc = _causal_conv(xin_ref[0], cs_ref, cw_ref, cb_ref, cbuf_ref, cnew_ref, t == 0, t == nt - 1)

    @pl.when(t == 0)
    def _():
        hp_ref[...] = h0_ref[0]

    for kb in range(LRU_BLOCKS):
        sl = slice(kb * LRU_BLOCK_DIM, (kb + 1) * LRU_BLOCK_DIM)
        xb = xc[:, sl].astype(BF16)
        a_ref[:, sl] = jnp.dot(xb, wa_ref[kb].astype(BF16), preferred_element_type=F32)
        u_ref[:, sl] = jnp.dot(xb, wx_ref[kb].astype(BF16), preferred_element_type=F32)
    r = jax.nn.sigmoid(a_ref[...] + ba_ref[...])
    i = jax.nn.sigmoid(u_ref[...] + bx_ref[...])
    log_a = -LRU_C * r * _softplus(-lam_ref[...])
    a = jnp.exp(log_a)
    u = jnp.sqrt(-jnp.tanh(log_a) * (1.0 + a * a)) * (i * xc)

    width = a.shape[1]
    a = a.reshape(tt // SUBLANES, SUBLANES, width)
    u = u.reshape(tt // SUBLANES, SUBLANES, width)
    sub = lax.broadcasted_iota(jnp.int32, (1, SUBLANES, width), 1)
    for s in (1, 2, 4):
        m = sub >= s
        a_sh = pltpu.roll(a, s, 1)
        u_sh = pltpu.roll(u, s, 1)
        u = jnp.where(m, a * u_sh + u, u)
        a = jnp.where(m, a * a_sh, a)
    a_ref[...] = a.reshape(tt, width)
    u_ref[...] = u.reshape(tt, width)

    def group(gi, hprev):
        r0 = pl.multiple_of(gi * SUBLANES, SUBLANES)
        hs = a_ref[pl.ds(r0, SUBLANES), :] * hprev + u_ref[pl.ds(r0, SUBLANES), :]
        u_ref[pl.ds(r0, SUBLANES), :] = hs
        return hs[SUBLANES - 1:SUBLANES, :]

    h_last = lax.fori_loop(0, tt // SUBLANES, group, hp_ref[...])
    hp_ref[...] = h_last
    gate = gate_ref[0]
    gelu = 0.5 * gate * (1.0 + jnp.tanh(np.sqrt(2.0 / np.pi) * (gate + 0.044715 * (gate * gate * gate))))
    y_ref[0] = (u_ref[...] * gelu).astype(BF16)

    @pl.when(t == nt - 1)
    def _():
        hout_ref[0] = h_last


def _lru(proj, conv_w, conv_b, w_a, b_a, w_x, b_x, lam, h0, conv_state, *, tt):
    b, t, _ = proj.shape
    w = LRU_WIDTH
    tok = lambda col: pl.BlockSpec((1, tt, w), lambda i, j: (i, j, col))
    full = lambda a: pl.BlockSpec(a.shape, lambda i, j: (0,) * a.ndim)
    row = lambda v: v.reshape(1, w)
    state = pl.BlockSpec((1, CONV_W - 1, w), lambda i, j: (i, 0, 0))
    hspec = pl.BlockSpec((1, 1, w), lambda i, j: (i, 0, 0))
    args = (proj, proj, conv_w, row(conv_b), w_a, w_x, row(b_a), row(b_x), row(lam), h0, conv_state)
    return pl.pallas_call(
        _lru_kernel,
        grid=(b, t // tt),
        in_specs=[tok(0), tok(1)] + [full(a) for a in args[2:9]] + [hspec, state],
        out_specs=[pl.BlockSpec((1, tt, w), lambda i, j: (i, j, 0)), hspec, state],
        out_shape=[jax.ShapeDtypeStruct((b, t, w), BF16),
                   jax.ShapeDtypeStruct((b, 1, w), F32),
                   jax.ShapeDtypeStruct((b, CONV_W - 1, w), F32)],
        scratch_shapes=[pltpu.VMEM((tt + SUBLANES, w), F32), pltpu.VMEM((tt, w), F32),
                        pltpu.VMEM((tt, w), F32), pltpu.VMEM((1, w), F32)],
        compiler_params=_params(2),
        name="rg_lru",
    )(*args)


def _rope_tables(pos, dim):
    half = dim // 2
    inv = np.power(ROPE_THETA, -np.arange(half, dtype=np.float64) / half)
    ang = pos.astype(np.float64)[:, None] * inv[None, :]
    cos = np.concatenate([np.cos(ang), np.cos(ang)], axis=-1)
    sin = np.concatenate([-np.sin(ang), np.sin(ang)], axis=-1)
    reps = LANES // dim
    return (jnp.asarray(np.tile(cos, (1, reps)), F32), jnp.asarray(np.tile(sin, (1, reps)), F32))


def _head_expansion():
    e = np.zeros((LANES, SSD_INNER), np.float32)
    for h in range(SSD_HEADS):
        e[SM_DT + h, h * SSD_HEADDIM:(h + 1) * SSD_HEADDIM] = 1.0
    return jnp.asarray(e, BF16)


def _pad_lanes(v, offset):
    out = jnp.zeros((1, LANES), F32)
    return out.at[0, offset:offset + v.shape[0]].set(v)


def _w_in_ab_tail(w_t):
    whole = AB_WIDTH // AB_TILE_N * AB_TILE_N
    return jnp.pad(w_t[whole:], ((0, AB_PAD_N - AB_WIDTH), (0, 0)))


def _layer0_mixers(proj, st, p, cfg):
    b, t, _ = proj.shape
    past = 0 if st["past_k"] is None else st["past_k"].shape[1]
    pos = np.arange(past, past + t)
    tabs = _rope_tables(pos, HEAD_DIM) + _rope_tables(pos, IDX_DIM)
    q, qi, k, v, ki, sm, kb, vbe, kib = _post0(proj, tabs, p["q_norm_w"][0], p["k_norm_w"][0],
                                               p["dt_bias_pad"], tt=cfg["post_tt"])
    cache = None if not past else (st["past_k"].reshape(b, past * KV_HEADS, HEAD_DIM),
                                   st["past_v"].reshape(b, past * KV_HEADS, HEAD_DIM), st["past_ki"])
    att = _dsa_attention(q, qi, sm, kb, vbe, kib, cache, qb=cfg["qb"], tk=cfg["tk"])
    y_ssd, ssm_new, ssm_conv_new = _ssd(
        proj, sm, st["ssm_h"], st["ssm_conv"], p["ssd_conv_w"][0], p["ssd_conv_b"][0], p["a_log_pad"],
        p["d_exp"], p["ssd_norm_w"][0], p["e_mat"], chunk=cfg["ssd_chunk"])
    kv_shape = (1, b, t, KV_HEADS, HEAD_DIM)
    return att, y_ssd, (k.reshape(kv_shape), v.reshape(kv_shape), ki[None], ssm_new[None],
                        ssm_conv_new[None])


def _layer1_mixer(proj, st, p, cfg):
    b = proj.shape[0]
    y_lru, lru_new, lru_conv_new = _lru(
        proj, p["lru_conv_w"][0], p["lru_conv_b"][0], p["lru_w_a"][0], p["lru_b_a"][0],
        p["lru_w_x"][0], p["lru_b_x"][0], p["lru_lambda"][0], st["lru_h"].reshape(b, 1, LRU_WIDTH),
        st["lru_conv"], tt=cfg["lru_tt"])
    return y_lru, (lru_new.reshape(1, b, LRU_WIDTH), lru_conv_new[None])


def _forward(xp, xs, mods_p, mods_s, st_p, st_s, p, cfg_p, cfg_s):
    bb, tt = cfg_p["bb"], cfg_p["tt"]

    def ffn(xp, xs, l):
        (_, _, _, sh_p, sc_p, g_p), (_, _, _, sh_s, sc_s, g_s) = mods_p[l], mods_s[l]
        gu_p, gu_s = _norm_mod_matmul(
            xp, p["norm_ffn_w"][l], sc_p, sh_p, [p["ffn_w_gate"], p["ffn_w_up"]], l, bb=bb,
            tt=cfg_p["tt_ffn"], tn=512, out_dtype=BF16, name="ffn_gate_up", ride=(xs, sc_s, sh_s))
        return _matmul_residual([gu_p], p["ffn_w_down"], l, xp, g_p, bb=bb, tt=cfg_p["tt_ffn"],
                                tn=256, name="ffn_down", ride=([gu_s], xs, g_s))

    (sh_p, sc_p, g_p, *_), (sh_s, sc_s, g_s, *_) = mods_p[0], mods_s[0]
    proj_p, proj_s = _norm_mod_matmul(
        xp, p["norm_mix_w"][0], sc_p, sh_p, [p["w_in_ab_t"]], 0, bb=bb, tt=cfg_p["tt_ab"],
        tn=AB_TILE_N, out_dtype=F32, name="in_proj_ab", w_tail=p["w_in_ab_tail"], w_rows_out=True,
        ride=(xs, sc_s, sh_s))
    att_p, y_p, outs0_p = _layer0_mixers(proj_p, st_p, p, cfg_p)
    att_s, y_s, outs0_s = _layer0_mixers(proj_s, st_s, p, cfg_s)
    xp, xs = _matmul_residual([att_p, y_p], p["w_out_ab"], 0, xp, g_p, bb=bb, tt=tt, tn=512,
                              name="out_proj_ab", ride=([att_s, y_s], xs, g_s))
    xp, xs = ffn(xp, xs, 0)

    (sh_p, sc_p, g_p, *_), (sh_s, sc_s, g_s, *_) = mods_p[1], mods_s[1]
    proj_p, proj_s = _norm_mod_matmul(
        xp, p["norm_mix_w"][1], sc_p, sh_p, [p["w_in_c"]], 0, bb=bb, tt=cfg_p["tt_ab"], tn=512,
        out_dtype=F32, name="in_proj_c", ride=(xs, sc_s, sh_s))
    y_p, outs1_p = _layer1_mixer(proj_p, st_p, p, cfg_p)
    y_s, outs1_s = _layer1_mixer(proj_s, st_s, p, cfg_s)
    xp, xs = _matmul_residual([y_p], p["w_out_c"], 0, xp, g_p, bb=bb, tt=tt, tn=512,
                              name="out_proj_c", ride=([y_s], xs, g_s))
    xp, xs = ffn(xp, xs, 1)
    return (xp,) + outs0_p + outs1_p, (xs,) + outs0_s + outs1_s


PROMPT_CFG = dict(bb=1, tt=2048, tt_ab=1024, tt_ffn=1024, post_tt=1024, qb=128, tk=512,
                  ssd_chunk=128, lru_tt=512)
SAMPLE_CFG = dict(post_tt=32, qb=32, tk=384, ssd_chunk=32, lru_tt=32)


def kernel(x_prompt, x_sample, cache_attn_k, cache_attn_v, cache_idx_k, state_ssm, state_ssm_conv,
           state_lru, state_lru_conv, c_prompt, c_sample, ada_w, ada_b, norm_mix_w, norm_ffn_w,
           w_in_ab, q_norm_w, k_norm_w, ssd_conv_w, ssd_conv_b, ssd_dt_bias, ssd_a_log, ssd_d,
           ssd_norm_w, w_out_ab, w_in_c, lru_conv_w, lru_conv_b, lru_w_a, lru_b_a, lru_w_x, lru_b_x,
           lru_lambda, w_out_c, ffn_w_gate, ffn_w_up, ffn_w_down):
    bp, bs = x_prompt.shape[0], x_sample.shape[0]
    p = dict(norm_mix_w=norm_mix_w, norm_ffn_w=norm_ffn_w, q_norm_w=q_norm_w, k_norm_w=k_norm_w,
             ssd_conv_w=ssd_conv_w, ssd_conv_b=ssd_conv_b, ssd_norm_w=ssd_norm_w, w_out_ab=w_out_ab,
             w_in_c=w_in_c, lru_conv_w=lru_conv_w, lru_conv_b=lru_conv_b, lru_w_a=lru_w_a,
             lru_b_a=lru_b_a, lru_w_x=lru_w_x, lru_b_x=lru_b_x, lru_lambda=lru_lambda,
             w_out_c=w_out_c, ffn_w_gate=ffn_w_gate, ffn_w_up=ffn_w_up, ffn_w_down=ffn_w_down)
    p["w_in_ab_t"] = jnp.swapaxes(w_in_ab, 1, 2)
    p["w_in_ab_tail"] = _w_in_ab_tail(p["w_in_ab_t"][0])
    p["dt_bias_pad"] = _pad_lanes(ssd_dt_bias[0], SM_DT)
    p["a_log_pad"] = _pad_lanes(ssd_a_log[0], SM_DT)
    p["d_exp"] = jnp.repeat(ssd_d[0], SSD_HEADDIM).reshape(1, SSD_INNER)
    p["e_mat"] = _head_expansion()

    m_rows = 32
    c_all = jnp.concatenate([c_prompt, c_sample, jnp.zeros((m_rows - bp - bs, D_MODEL), F32)], axis=0)
    mod = _modulation(c_all, ada_w, ada_b)

    def group_mods(r0, nb):
        return [[mod[l, r0:r0 + nb, i * D_MODEL:(i + 1) * D_MODEL].reshape(nb, 1, D_MODEL)
                 for i in range(6)] for l in range(mod.shape[0])]

    zeros = lambda *s: jnp.zeros(s, F32)
    st_p = dict(past_k=None, past_v=None, past_ki=None,
                ssm_h=zeros(bp, SSD_HEADS, SSD_HEADDIM, SSD_STATE),
                ssm_conv=zeros(bp, CONV_W - 1, SSD_CONV_DIM), lru_h=zeros(bp, LRU_WIDTH),
                lru_conv=zeros(bp, CONV_W - 1, LRU_WIDTH))
    st_s = dict(past_k=cache_attn_k[0], past_v=cache_attn_v[0], past_ki=cache_idx_k[0],
                ssm_h=state_ssm[0], ssm_conv=state_ssm_conv[0], lru_h=state_lru[0],
                lru_conv=state_lru_conv[0])
    out_p, out_s = _forward(x_prompt, x_sample, group_mods(0, bp), group_mods(bp, bs), st_p, st_s, p,
                            PROMPT_CFG, SAMPLE_CFG)
    return (out_p[0], out_s[0]) + out_p[1:] + out_s[1:]
```
